```python
import math
import jax
import jax.numpy as jnp
from jax import lax
import numpy as np

D_MODEL = 2048
BATCH = 8
SEQ = 4096
DEPTH = 4

N_MIXERS = 2
N_LAYERS_A = (DEPTH + 1) // 2
N_LAYERS_B = DEPTH // 2
CHUNK = 128
SGU_WIDTH = D_MODEL
SGU_GROUP = 128
SGU_HEADS = SGU_WIDTH // SGU_GROUP
SSM_WIDTH = D_MODEL
SSM_GROUP = 16
SSM_HEADS = SSM_WIDTH // SSM_GROUP
SSM_STATE = 64
DT_MIN = 1e-3
DT_MAX = 1e-1
FFN_HIDDEN = 5632
CONV_WIDTH = 3
EPS = 1e-6

kernel_name = 'hybrid_sgu_s5_convffn'


def rms_norm(x, g):
    xf = x.astype(jnp.float32)
    y = xf * lax.rsqrt(jnp.mean(xf * xf, axis=-1, keepdims=True) + EPS)
    return (y * g.astype(jnp.float32)).astype(x.dtype)


def chunked_sgu_mixer(h, w_in, g_v, w_s, b_s, w_out):
    bsz, seq, _ = h.shape
    z = jax.nn.gelu(h @ w_in)
    u, v = jnp.split(z, 2, axis=-1)
    v = rms_norm(v, g_v).reshape(bsz, seq // CHUNK, CHUNK, SGU_HEADS, SGU_GROUP)
    causal = jnp.tril(jnp.ones((CHUNK, CHUNK), dtype=bool))
    w = jnp.where(causal[None], w_s, jnp.zeros((), w_s.dtype))
    s = jnp.einsum('hts,bcshd->bcthd', w, v) + b_s.T[:, :, None]
    s = s.reshape(bsz, seq, SGU_WIDTH)
    return (u * s) @ w_out


def _cmul(ar, ai, br, bi):
    return ar * br - ai * bi, ar * bi + ai * br


def _scan_combine(earlier, later):
    a1r, a1i, b1r, b1i = earlier
    a2r, a2i, b2r, b2i = later
    ar, ai = _cmul(a2r, a2i, a1r, a1i)
    br, bi = _cmul(a2r, a2i, b1r, b1i)
    return ar, ai, br + b2r, bi + b2i


def s5_mixer(h, w_in, a_re, a_im, log_dt, b_re, b_im, c_re, c_im, d_skip, w_glu):
    f32 = jnp.float32
    bsz, seq, _ = h.shape
    n_chunks = seq // CHUNK
    u = (h @ w_in).astype(f32).reshape(bsz, n_chunks, CHUNK, SSM_HEADS, SSM_GROUP)
    u = u.transpose(1, 0, 2, 3, 4)
    dt = jnp.exp(log_dt.astype(f32))[:, None]
    lr, li = a_re.astype(f32), a_im.astype(f32)
    mag = jnp.exp(dt * lr)
    abar_r, abar_i = mag * jnp.cos(dt * li), mag * jnp.sin(dt * li)
    den = lr * lr + li * li
    qr = ((abar_r - 1.0) * lr + abar_i * li) / den
    qi = (abar_i * lr - (abar_r - 1.0) * li) / den
    bbar_r, bbar_i = _cmul(qr[..., None], qi[..., None], b_re.astype(f32), b_im.astype(f32))
    cr, ci = c_re.astype(f32), c_im.astype(f32)
    dd = d_skip.astype(f32).reshape(SSM_HEADS, SSM_GROUP)
    a_seq_r = jnp.broadcast_to(abar_r, (bsz, CHUNK, SSM_HEADS, SSM_STATE))
    a_seq_i = jnp.broadcast_to(abar_i, (bsz, CHUNK, SSM_HEADS, SSM_STATE))

    def chunk_step(carry, u_c):
        h0r, h0i = carry
        bur = jnp.einsum('gpc,btgc->btgp', bbar_r, u_c)
        bui = jnp.einsum('gpc,btgc->btgp', bbar_i, u_c)
        pr, pim, hr, hi = lax.associative_scan(
            _scan_combine, (a_seq_r, a_seq_i, bur, bui), axis=1)
        sr, si = _cmul(pr, pim, h0r[:, None], h0i[:, None])
        hr = hr + sr
        hi = hi + si
        y = (jnp.einsum('gcp,btgp->btgc', cr, hr)
             - jnp.einsum('gcp,btgp->btgc', ci, hi)
             + dd * u_c)
        return (hr[:, -1], hi[:, -1]), y

    init = (jnp.zeros((bsz, SSM_HEADS, SSM_STATE), f32),
            jnp.zeros((bsz, SSM_HEADS, SSM_STATE), f32))
    _, y = lax.scan(chunk_step, init, u)
    y = y.transpose(1, 0, 2, 3, 4).reshape(bsz, seq, SSM_WIDTH).astype(h.dtype)
    ga, gb = jnp.split(jax.nn.gelu(y) @ w_glu, 2, axis=-1)
    return ga * jax.nn.sigmoid(gb)


def conv_glu_ffn(h, w_up, conv_w, conv_b, w_down):
    seq = h.shape[1]
    z = h @ w_up
    zp = jnp.pad(z, ((0, 0), (CONV_WIDTH - 1, 0), (0, 0)))
    acc = conv_b + conv_w[CONV_WIDTH - 1] * zp[:, CONV_WIDTH - 1:CONV_WIDTH - 1 + seq]
    for k in range(CONV_WIDTH - 1):
        acc = acc + conv_w[k] * zp[:, k:k + seq]
    gate, val = jnp.split(acc, 2, axis=-1)
    return (jax.nn.silu(gate) * val) @ w_down


def _fwd_setup_inputs(seed: int = 0) -> dict:
    key = jax.random.key(seed)
    ks = jax.random.split(key, 24)
    f32 = jnp.float32
    d = D_MODEL
    na, nb = N_LAYERS_A, N_LAYERS_B

    def nrm(k, shape, scale):
        return jax.random.normal(k, shape, f32) * scale

    def gain(k, shape):
        return 1.0 + 0.01 * jax.random.normal(k, shape, f32)

    n_idx = jnp.arange(SSM_STATE, dtype=f32)
    return {
        'x': nrm(ks[0], (BATCH, SEQ, d), 1.0),
        'norm_mix_g': gain(ks[1], (DEPTH, d)),
        'norm_ffn_g': gain(ks[2], (DEPTH, d)),
        'a_w_in': nrm(ks[3], (na, d, 2 * SGU_WIDTH), d ** -0.5),
        'a_g_v': gain(ks[4], (na, SGU_WIDTH)),
        'a_w_s': nrm(ks[5], (na, SGU_HEADS, CHUNK, CHUNK), 0.5 * CHUNK ** -0.5),
        'a_b_s': gain(ks[6], (na, SGU_HEADS, CHUNK)),
        'a_w_out': nrm(ks[7], (na, SGU_WIDTH, d), SGU_WIDTH ** -0.5),
        'b_w_in': nrm(ks[8], (nb, d, SSM_WIDTH), d ** -0.5),
        'b_a_re': -0.5 + nrm(ks[9], (nb, SSM_HEADS, SSM_STATE), 0.01),
        'b_a_im': math.pi * n_idx + nrm(ks[10], (nb, SSM_HEADS, SSM_STATE), 0.01),
        'b_log_dt': jax.random.uniform(ks[11], (nb, SSM_HEADS), f32,
                                       minval=math.log(DT_MIN), maxval=math.log(DT_MAX)),
        'b_b_re': nrm(ks[12], (nb, SSM_HEADS, SSM_STATE, SSM_GROUP), (2 * SSM_GROUP) ** -0.5),
        'b_b_im': nrm(ks[13], (nb, SSM_HEADS, SSM_STATE, SSM_GROUP), (2 * SSM_GROUP) ** -0.5),
        'b_c_re': nrm(ks[14], (nb, SSM_HEADS, SSM_GROUP, SSM_STATE), (2 * SSM_STATE) ** -0.5),
        'b_c_im': nrm(ks[15], (nb, SSM_HEADS, SSM_GROUP, SSM_STATE), (2 * SSM_STATE) ** -0.5),
        'b_d': nrm(ks[16], (nb, SSM_WIDTH), 1.0),
        'b_w_glu': nrm(ks[17], (nb, SSM_WIDTH, 2 * d), SSM_WIDTH ** -0.5),
        'f_w_up': nrm(ks[18], (DEPTH, d, 2 * FFN_HIDDEN), d ** -0.5),
        'f_conv_w': nrm(ks[19], (DEPTH, CONV_WIDTH, 2 * FFN_HIDDEN), CONV_WIDTH ** -0.5),
        'f_conv_b': nrm(ks[20], (DEPTH, 2 * FFN_HIDDEN), 0.01),
        'f_w_down': nrm(ks[21], (DEPTH, FFN_HIDDEN, d), FFN_HIDDEN ** -0.5),
        'final_g': gain(ks[22], (d,)),
    }


def _fwd_reference(x, norm_mix_g, norm_ffn_g, a_w_in, a_g_v, a_w_s, a_b_s, a_w_out,
              b_w_in, b_a_re, b_a_im, b_log_dt, b_b_re, b_b_im, b_c_re, b_c_im, b_d, b_w_glu,
              f_w_up, f_conv_w, f_conv_b, f_w_down, final_g):
    h = x
    for i in range(DEPTH):
        j = i // N_MIXERS
        hn = rms_norm(h, norm_mix_g[i])
        if i % N_MIXERS == 0:
            h = h + chunked_sgu_mixer(hn, a_w_in[j], a_g_v[j], a_w_s[j], a_b_s[j], a_w_out[j])
        else:
            h = h + s5_mixer(hn, b_w_in[j], b_a_re[j], b_a_im[j], b_log_dt[j],
                             b_b_re[j], b_b_im[j], b_c_re[j], b_c_im[j], b_d[j], b_w_glu[j])
        h = h + conv_glu_ffn(rms_norm(h, norm_ffn_g[i]), f_w_up[i], f_conv_w[i],
                             f_conv_b[i], f_w_down[i])
    return rms_norm(h, final_g)


import jax as _jax
import jax.numpy as _jnp

TWIN_FORMAT = 'train_step'
FWD_PARAMS = ['x', 'norm_mix_g', 'norm_ffn_g', 'a_w_in', 'a_g_v', 'a_w_s', 'a_b_s', 'a_w_out', 'b_w_in', 'b_a_re', 'b_a_im', 'b_log_dt', 'b_b_re', 'b_b_im', 'b_c_re', 'b_c_im', 'b_d', 'b_w_glu', 'f_w_up', 'f_conv_w', 'f_conv_b', 'f_w_down', 'final_g']
TWIN_WEIGHTS = ['norm_mix_g', 'norm_ffn_g', 'a_w_in', 'a_g_v', 'a_w_s', 'a_b_s', 'a_w_out', 'b_w_in', 'b_a_re', 'b_a_im', 'b_log_dt', 'b_b_re', 'b_b_im', 'b_c_re', 'b_c_im', 'b_d', 'b_w_glu', 'f_w_up', 'f_conv_w', 'f_conv_b', 'f_w_down', 'final_g']
TWIN_DIFF_INPUT = 'x'
TWIN_INPUTS = ['x', 'norm_mix_g', 'norm_ffn_g', 'a_w_in', 'a_g_v', 'a_w_s', 'a_b_s', 'a_w_out', 'b_w_in', 'b_a_re', 'b_a_im', 'b_log_dt', 'b_b_re', 'b_b_im', 'b_c_re', 'b_c_im', 'b_d', 'b_w_glu', 'f_w_up', 'f_conv_w', 'f_conv_b', 'f_w_down', 'final_g', 'loss_target', 'm_norm_mix_g', 'm_norm_ffn_g', 'm_a_w_in', 'm_a_g_v', 'm_a_w_s', 'm_a_b_s', 'm_a_w_out', 'm_b_w_in', 'm_b_a_re', 'm_b_a_im', 'm_b_log_dt', 'm_b_b_re', 'm_b_b_im', 'm_b_c_re', 'm_b_c_im', 'm_b_d', 'm_b_w_glu', 'm_f_w_up', 'm_f_conv_w', 'm_f_conv_b', 'm_f_w_down', 'm_final_g', 'v_norm_mix_g', 'v_norm_ffn_g', 'v_a_w_in', 'v_a_g_v', 'v_a_w_s', 'v_a_b_s', 'v_a_w_out', 'v_b_w_in', 'v_b_a_re', 'v_b_a_im', 'v_b_log_dt', 'v_b_b_re', 'v_b_b_im', 'v_b_c_re', 'v_b_c_im', 'v_b_d', 'v_b_w_glu', 'v_f_w_up', 'v_f_conv_w', 'v_f_conv_b', 'v_f_w_down', 'v_final_g']
TWIN_OUTPUTS = ['loss', 'grad_x', 'grad_norm_mix_g', 'grad_norm_ffn_g', 'grad_a_w_in', 'grad_a_g_v', 'grad_a_w_s', 'grad_a_b_s', 'grad_a_w_out', 'grad_b_w_in', 'grad_b_a_re', 'grad_b_a_im', 'grad_b_log_dt', 'grad_b_b_re', 'grad_b_b_im', 'grad_b_c_re', 'grad_b_c_im', 'grad_b_d', 'grad_b_w_glu', 'grad_f_w_up', 'grad_f_conv_w', 'grad_f_conv_b', 'grad_f_w_down', 'grad_final_g', 'delta_norm_mix_g', 'delta_norm_ffn_g', 'delta_a_w_in', 'delta_a_g_v', 'delta_a_w_s', 'delta_a_b_s', 'delta_a_w_out', 'delta_b_w_in', 'delta_b_a_re', 'delta_b_a_im', 'delta_b_log_dt', 'delta_b_b_re', 'delta_b_b_im', 'delta_b_c_re', 'delta_b_c_im', 'delta_b_d', 'delta_b_w_glu', 'delta_f_w_up', 'delta_f_conv_w', 'delta_f_conv_b', 'delta_f_w_down', 'delta_final_g', 'new_m_norm_mix_g', 'new_m_norm_ffn_g', 'new_m_a_w_in', 'new_m_a_g_v', 'new_m_a_w_s', 'new_m_a_b_s', 'new_m_a_w_out', 'new_m_b_w_in', 'new_m_b_a_re', 'new_m_b_a_im', 'new_m_b_log_dt', 'new_m_b_b_re', 'new_m_b_b_im', 'new_m_b_c_re', 'new_m_b_c_im', 'new_m_b_d', 'new_m_b_w_glu', 'new_m_f_w_up', 'new_m_f_conv_w', 'new_m_f_conv_b', 'new_m_f_w_down', 'new_m_final_g', 'new_v_norm_mix_g', 'new_v_norm_ffn_g', 'new_v_a_w_in', 'new_v_a_g_v', 'new_v_a_w_s', 'new_v_a_b_s', 'new_v_a_w_out', 'new_v_b_w_in', 'new_v_b_a_re', 'new_v_b_a_im', 'new_v_b_log_dt', 'new_v_b_b_re', 'new_v_b_b_im', 'new_v_b_c_re', 'new_v_b_c_im', 'new_v_b_d', 'new_v_b_w_glu', 'new_v_f_w_up', 'new_v_f_conv_w', 'new_v_f_conv_b', 'new_v_f_w_down', 'new_v_final_g']
TWIN_LEAF_KINDS = {'loss': 'loss', 'grad_x': 'grad_x', 'grad_norm_mix_g': 'grad_w', 'grad_norm_ffn_g': 'grad_w', 'grad_a_w_in': 'grad_w', 'grad_a_g_v': 'grad_w', 'grad_a_w_s': 'grad_w', 'grad_a_b_s': 'grad_w', 'grad_a_w_out': 'grad_w', 'grad_b_w_in': 'grad_w', 'grad_b_a_re': 'grad_w', 'grad_b_a_im': 'grad_w', 'grad_b_log_dt': 'grad_w', 'grad_b_b_re': 'grad_w', 'grad_b_b_im': 'grad_w', 'grad_b_c_re': 'grad_w', 'grad_b_c_im': 'grad_w', 'grad_b_d': 'grad_w', 'grad_b_w_glu': 'grad_w', 'grad_f_w_up': 'grad_w', 'grad_f_conv_w': 'grad_w', 'grad_f_conv_b': 'grad_w', 'grad_f_w_down': 'grad_w', 'grad_final_g': 'grad_w', 'delta_norm_mix_g': 'delta_w', 'delta_norm_ffn_g': 'delta_w', 'delta_a_w_in': 'delta_w', 'delta_a_g_v': 'delta_w', 'delta_a_w_s': 'delta_w', 'delta_a_b_s': 'delta_w', 'delta_a_w_out': 'delta_w', 'delta_b_w_in': 'delta_w', 'delta_b_a_re': 'delta_w', 'delta_b_a_im': 'delta_w', 'delta_b_log_dt': 'delta_w', 'delta_b_b_re': 'delta_w', 'delta_b_b_im': 'delta_w', 'delta_b_c_re': 'delta_w', 'delta_b_c_im': 'delta_w', 'delta_b_d': 'delta_w', 'delta_b_w_glu': 'delta_w', 'delta_f_w_up': 'delta_w', 'delta_f_conv_w': 'delta_w', 'delta_f_conv_b': 'delta_w', 'delta_f_w_down': 'delta_w', 'delta_final_g': 'delta_w', 'new_m_norm_mix_g': 'new_m', 'new_m_norm_ffn_g': 'new_m', 'new_m_a_w_in': 'new_m', 'new_m_a_g_v': 'new_m', 'new_m_a_w_s': 'new_m', 'new_m_a_b_s': 'new_m', 'new_m_a_w_out': 'new_m', 'new_m_b_w_in': 'new_m', 'new_m_b_a_re': 'new_m', 'new_m_b_a_im': 'new_m', 'new_m_b_log_dt': 'new_m', 'new_m_b_b_re': 'new_m', 'new_m_b_b_im': 'new_m', 'new_m_b_c_re': 'new_m', 'new_m_b_c_im': 'new_m', 'new_m_b_d': 'new_m', 'new_m_b_w_glu': 'new_m', 'new_m_f_w_up': 'new_m', 'new_m_f_conv_w': 'new_m', 'new_m_f_conv_b': 'new_m', 'new_m_f_w_down': 'new_m', 'new_m_final_g': 'new_m', 'new_v_norm_mix_g': 'new_v', 'new_v_norm_ffn_g': 'new_v', 'new_v_a_w_in': 'new_v', 'new_v_a_g_v': 'new_v', 'new_v_a_w_s': 'new_v', 'new_v_a_b_s': 'new_v', 'new_v_a_w_out': 'new_v', 'new_v_b_w_in': 'new_v', 'new_v_b_a_re': 'new_v', 'new_v_b_a_im': 'new_v', 'new_v_b_log_dt': 'new_v', 'new_v_b_b_re': 'new_v', 'new_v_b_b_im': 'new_v', 'new_v_b_c_re': 'new_v', 'new_v_b_c_im': 'new_v', 'new_v_b_d': 'new_v', 'new_v_b_w_glu': 'new_v', 'new_v_f_w_up': 'new_v', 'new_v_f_conv_w': 'new_v', 'new_v_f_conv_b': 'new_v', 'new_v_f_w_down': 'new_v', 'new_v_final_g': 'new_v'}


def _forward(args):
    return _fwd_reference(*[args[k] for k in FWD_PARAMS])


def _output_shape():
    out = _jax.eval_shape(lambda: _forward(_fwd_setup_inputs(0)))
    return out.shape, out.dtype

N_MICROBATCH = 1
ADAM_LR = 0.001
ADAM_B1 = 0.9
ADAM_B2 = 0.999
ADAM_EPS = 1e-08
ADAM_WD = 0.01
ADAM_STEP = 10
PER_EXAMPLE_BATCH_AXIS = {'x': 0, 'loss_target': 0}
SHARED_INPUTS = []
_WEIGHT_DTYPES = {'norm_mix_g': _jnp.float32, 'norm_ffn_g': _jnp.float32, 'a_w_in': _jnp.float32, 'a_g_v': _jnp.float32, 'a_w_s': _jnp.float32, 'a_b_s': _jnp.float32, 'a_w_out': _jnp.float32, 'b_w_in': _jnp.float32, 'b_a_re': _jnp.float32, 'b_a_im': _jnp.float32, 'b_log_dt': _jnp.float32, 'b_b_re': _jnp.float32, 'b_b_im': _jnp.float32, 'b_c_re': _jnp.float32, 'b_c_im': _jnp.float32, 'b_d': _jnp.float32, 'b_w_glu': _jnp.float32, 'f_w_up': _jnp.float32, 'f_conv_w': _jnp.float32, 'f_conv_b': _jnp.float32, 'f_w_down': _jnp.float32, 'final_g': _jnp.float32}
MOMENT_SCALE = {'norm_mix_g': 4.698420e-02, 'norm_ffn_g': 5.549996e-02, 'a_w_in': 4.342958e-02, 'a_g_v': 1.890157e-02, 'a_w_s': 3.774187e-02, 'a_b_s': 5.389821e-02, 'a_w_out': 5.695080e-02, 'b_w_in': 2.470564e-02, 'b_a_re': 1.819754e-03, 'b_a_im': 1.963377e-03, 'b_log_dt': 1.026322e+00, 'b_b_re': 1.005123e-03, 'b_b_im': 9.875556e-04, 'b_c_re': 1.959415e-03, 'b_c_im': 1.927702e-03, 'b_d': 2.623672e-02, 'b_w_glu': 1.737483e-02, 'f_w_up': 2.357328e-02, 'f_conv_w': 2.367802e-02, 'f_conv_b': 2.302437e-02, 'f_w_down': 3.848277e-02, 'final_g': 1.606061e+01}


def _to_microbatches(a, axis):
    t = _jnp.moveaxis(a, axis, 0)
    t = t.reshape((N_MICROBATCH, t.shape[0] // N_MICROBATCH) + t.shape[1:])
    return _jnp.moveaxis(t, 1, axis + 1)


def setup_inputs(seed: int = 0) -> dict:
    inp = _fwd_setup_inputs(seed)
    key = _jax.random.fold_in(_jax.random.key(seed), 7919)
    shape, _ = _output_shape()
    out = dict(inp)
    out["loss_target"] = _jax.random.normal(_jax.random.fold_in(key, 0), shape, _jnp.float32)
    for i, name in enumerate(TWIN_WEIGHTS):
        w = inp[name].astype(_jnp.float32)
        if MOMENT_SCALE is None:
            s = _jnp.sqrt(_jnp.mean(_jnp.square(w)) + 1e-30)
        else:
            s = MOMENT_SCALE[name]
        km, kv = _jax.random.split(_jax.random.fold_in(key, i + 1))
        out[name] = w
        out["m_" + name] = s * _jax.random.normal(km, w.shape, _jnp.float32)
        out["v_" + name] = (s * s) * _jax.random.uniform(kv, w.shape, _jnp.float32, 0.5, 1.5)
    if N_MICROBATCH > 1:
        for name, axis in PER_EXAMPLE_BATCH_AXIS.items():
            out[name] = _to_microbatches(out[name], axis)
    return {'x': out['x'], 'norm_mix_g': out['norm_mix_g'], 'norm_ffn_g': out['norm_ffn_g'], 'a_w_in': out['a_w_in'], 'a_g_v': out['a_g_v'], 'a_w_s': out['a_w_s'], 'a_b_s': out['a_b_s'], 'a_w_out': out['a_w_out'], 'b_w_in': out['b_w_in'], 'b_a_re': out['b_a_re'], 'b_a_im': out['b_a_im'], 'b_log_dt': out['b_log_dt'], 'b_b_re': out['b_b_re'], 'b_b_im': out['b_b_im'], 'b_c_re': out['b_c_re'], 'b_c_im': out['b_c_im'], 'b_d': out['b_d'], 'b_w_glu': out['b_w_glu'], 'f_w_up': out['f_w_up'], 'f_conv_w': out['f_conv_w'], 'f_conv_b': out['f_conv_b'], 'f_w_down': out['f_w_down'], 'final_g': out['final_g'], 'loss_target': out['loss_target'], 'm_norm_mix_g': out['m_norm_mix_g'], 'm_norm_ffn_g': out['m_norm_ffn_g'], 'm_a_w_in': out['m_a_w_in'], 'm_a_g_v': out['m_a_g_v'], 'm_a_w_s': out['m_a_w_s'], 'm_a_b_s': out['m_a_b_s'], 'm_a_w_out': out['m_a_w_out'], 'm_b_w_in': out['m_b_w_in'], 'm_b_a_re': out['m_b_a_re'], 'm_b_a_im': out['m_b_a_im'], 'm_b_log_dt': out['m_b_log_dt'], 'm_b_b_re': out['m_b_b_re'], 'm_b_b_im': out['m_b_b_im'], 'm_b_c_re': out['m_b_c_re'], 'm_b_c_im': out['m_b_c_im'], 'm_b_d': out['m_b_d'], 'm_b_w_glu': out['m_b_w_glu'], 'm_f_w_up': out['m_f_w_up'], 'm_f_conv_w': out['m_f_conv_w'], 'm_f_conv_b': out['m_f_conv_b'], 'm_f_w_down': out['m_f_w_down'], 'm_final_g': out['m_final_g'], 'v_norm_mix_g': out['v_norm_mix_g'], 'v_norm_ffn_g': out['v_norm_ffn_g'], 'v_a_w_in': out['v_a_w_in'], 'v_a_g_v': out['v_a_g_v'], 'v_a_w_s': out['v_a_w_s'], 'v_a_b_s': out['v_a_b_s'], 'v_a_w_out': out['v_a_w_out'], 'v_b_w_in': out['v_b_w_in'], 'v_b_a_re': out['v_b_a_re'], 'v_b_a_im': out['v_b_a_im'], 'v_b_log_dt': out['v_b_log_dt'], 'v_b_b_re': out['v_b_b_re'], 'v_b_b_im': out['v_b_b_im'], 'v_b_c_re': out['v_b_c_re'], 'v_b_c_im': out['v_b_c_im'], 'v_b_d': out['v_b_d'], 'v_b_w_glu': out['v_b_w_glu'], 'v_f_w_up': out['v_f_w_up'], 'v_f_conv_w': out['v_f_conv_w'], 'v_f_conv_b': out['v_f_conv_b'], 'v_f_w_down': out['v_f_w_down'], 'v_final_g': out['v_final_g']}


def _loss(weights, diff, rest, loss_target):
    with _jax.named_scope("forward"):
        args = {**rest, TWIN_DIFF_INPUT: diff, **{k: w.astype(_WEIGHT_DTYPES[k]) for k, w in weights.items()}}
        y = _forward(args)
    with _jax.named_scope("loss_head"):
        err = _jnp.square(y.astype(_jnp.float32) - loss_target)
        return 0.5 * _jnp.sum(_jnp.mean(err, axis=-1)) if err.ndim else 0.5 * err


def _adamw(w, g, m, v):
    m = ADAM_B1 * m + (1.0 - ADAM_B1) * g
    v = ADAM_B2 * v + (1.0 - ADAM_B2) * _jnp.square(g)
    m_hat = m / (1.0 - ADAM_B1 ** ADAM_STEP)
    v_hat = v / (1.0 - ADAM_B2 ** ADAM_STEP)
    delta = -ADAM_LR * (m_hat / (_jnp.sqrt(v_hat) + ADAM_EPS) + ADAM_WD * w)
    return delta, m, v


def reference(x, norm_mix_g, norm_ffn_g, a_w_in, a_g_v, a_w_s, a_b_s, a_w_out, b_w_in, b_a_re, b_a_im, b_log_dt, b_b_re, b_b_im, b_c_re, b_c_im, b_d, b_w_glu, f_w_up, f_conv_w, f_conv_b, f_w_down, final_g, loss_target, m_norm_mix_g, m_norm_ffn_g, m_a_w_in, m_a_g_v, m_a_w_s, m_a_b_s, m_a_w_out, m_b_w_in, m_b_a_re, m_b_a_im, m_b_log_dt, m_b_b_re, m_b_b_im, m_b_c_re, m_b_c_im, m_b_d, m_b_w_glu, m_f_w_up, m_f_conv_w, m_f_conv_b, m_f_w_down, m_final_g, v_norm_mix_g, v_norm_ffn_g, v_a_w_in, v_a_g_v, v_a_w_s, v_a_b_s, v_a_w_out, v_b_w_in, v_b_a_re, v_b_a_im, v_b_log_dt, v_b_b_re, v_b_b_im, v_b_c_re, v_b_c_im, v_b_d, v_b_w_glu, v_f_w_up, v_f_conv_w, v_f_conv_b, v_f_w_down, v_final_g):
    given = dict(x=x, norm_mix_g=norm_mix_g, norm_ffn_g=norm_ffn_g, a_w_in=a_w_in, a_g_v=a_g_v, a_w_s=a_w_s, a_b_s=a_b_s, a_w_out=a_w_out, b_w_in=b_w_in, b_a_re=b_a_re, b_a_im=b_a_im, b_log_dt=b_log_dt, b_b_re=b_b_re, b_b_im=b_b_im, b_c_re=b_c_re, b_c_im=b_c_im, b_d=b_d, b_w_glu=b_w_glu, f_w_up=f_w_up, f_conv_w=f_conv_w, f_conv_b=f_conv_b, f_w_down=f_w_down, final_g=final_g, loss_target=loss_target, m_norm_mix_g=m_norm_mix_g, m_norm_ffn_g=m_norm_ffn_g, m_a_w_in=m_a_w_in, m_a_g_v=m_a_g_v, m_a_w_s=m_a_w_s, m_a_b_s=m_a_b_s, m_a_w_out=m_a_w_out, m_b_w_in=m_b_w_in, m_b_a_re=m_b_a_re, m_b_a_im=m_b_a_im, m_b_log_dt=m_b_log_dt, m_b_b_re=m_b_b_re, m_b_b_im=m_b_b_im, m_b_c_re=m_b_c_re, m_b_c_im=m_b_c_im, m_b_d=m_b_d, m_b_w_glu=m_b_w_glu, m_f_w_up=m_f_w_up, m_f_conv_w=m_f_conv_w, m_f_conv_b=m_f_conv_b, m_f_w_down=m_f_w_down, m_final_g=m_final_g, v_norm_mix_g=v_norm_mix_g, v_norm_ffn_g=v_norm_ffn_g, v_a_w_in=v_a_w_in, v_a_g_v=v_a_g_v, v_a_w_s=v_a_w_s, v_a_b_s=v_a_b_s, v_a_w_out=v_a_w_out, v_b_w_in=v_b_w_in, v_b_a_re=v_b_a_re, v_b_a_im=v_b_a_im, v_b_log_dt=v_b_log_dt, v_b_b_re=v_b_b_re, v_b_b_im=v_b_b_im, v_b_c_re=v_b_c_re, v_b_c_im=v_b_c_im, v_b_d=v_b_d, v_b_w_glu=v_b_w_glu, v_f_w_up=v_f_w_up, v_f_conv_w=v_f_conv_w, v_f_conv_b=v_f_conv_b, v_f_w_down=v_f_w_down, v_final_g=v_final_g)
    weights = {n: given[n] for n in TWIN_WEIGHTS}
    shared = {n: given[n] for n in SHARED_INPUTS}
    per_example = {n: given[n] for n in ['x']}
    grad_fn = _jax.value_and_grad(_loss, argnums=(0, 1))

    def one_microbatch(ex, loss_target):
        ex = dict(ex)
        diff = ex.pop(TWIN_DIFF_INPUT)
        return grad_fn(weights, diff, {**shared, **ex}, loss_target)

    if N_MICROBATCH == 1:
        loss, (grad_w, grad_x) = one_microbatch(per_example, given["loss_target"])
    else:
        def body(carry, xs):
            loss_sum, grad_sum = carry
            l_k, (gw_k, gx_k) = one_microbatch(xs[0], xs[1])
            with _jax.named_scope("update"):
                return (loss_sum + l_k, _jax.tree.map(_jnp.add, grad_sum, gw_k)), gx_k

        init = (_jnp.zeros((), _jnp.float32), _jax.tree.map(_jnp.zeros_like, weights))
        (loss, grad_w), grad_x = _jax.lax.scan(body, init, (per_example, given["loss_target"]))
    with _jax.named_scope("update"):
        delta_w, new_m, new_v = {}, {}, {}
        for n in TWIN_WEIGHTS:
            delta_w[n], new_m[n], new_v[n] = _adamw(weights[n], grad_w[n], given["m_" + n], given["v_" + n])
    return (loss, grad_x, *[grad_w[n] for n in TWIN_WEIGHTS], *[delta_w[n] for n in TWIN_WEIGHTS],
            *[new_m[n] for n in TWIN_WEIGHTS], *[new_v[n] for n in TWIN_WEIGHTS])
```

```python
import math

import jax
import jax.numpy as jnp
from jax import lax
from jax.experimental import pallas as pl
from jax.experimental.pallas import tpu as pltpu

F32 = jnp.float32
BF16 = jnp.bfloat16
N_DEV = 8
AXES = ("x", "y", "c")
EPS = 1e-6
LANES = 128
SUBLANES = 8
VMEM_BIG = 56 * 1024 * 1024
VMEM_MID = 40 * 1024 * 1024
ADAM_LR, ADAM_B1, ADAM_B2, ADAM_EPS, ADAM_WD, ADAM_STEP = 0.001, 0.9, 0.999, 1e-08, 0.01, 10
MESH = pl.DeviceIdType.MESH
GELU_C = math.sqrt(2.0 / math.pi)
GELU_K = 0.044715


def _pick(n, prefs):
    for p in prefs:
        if n % p == 0:
            return p
    return n


def _params(sem, vmem=None):
    return pltpu.CompilerParams(dimension_semantics=sem, vmem_limit_bytes=vmem)


def _gelu(x):
    return 0.5 * x * (1.0 + jnp.tanh(GELU_C * (x + GELU_K * x * x * x)))


def _gelu_grad(x):
    x2 = x * x
    th = jnp.tanh(GELU_C * x * (1.0 + GELU_K * x2))
    return 0.5 * (1.0 + th) + 0.5 * x * (1.0 - th * th) * GELU_C * (1.0 + 3.0 * GELU_K * x2)


def _sigmoid(x):
    return 1.0 / (1.0 + jnp.exp(-x))


def _dot_nn(a, b):
    return jnp.dot(a, b, preferred_element_type=F32)


def _dot_nt(a, b):
    return lax.dot_general(a, b, (((1,), (1,)), ((), ())), preferred_element_type=F32)


def _dot_tn(a, b):
    return lax.dot_general(a, b, (((0,), (0,)), ((), ())), preferred_element_type=F32)


M_TILES = (1024, 512, 256, 128)
N_TILES = (1408, 1024, 512, 384, 256, 128)
K_TILES = (1408, 1024, 512, 384, 256, 128)


def _as3(b):
    return b if b.ndim == 3 else b[None]


def mm_nn(a, b, *, res=None, out_dtype=F32, name):
    b3 = _as3(b)
    M, K = a.shape
    J, _, nb = b3.shape
    tm, tn, tk = _pick(M, M_TILES), _pick(nb, N_TILES), _pick(K, K_TILES)
    per, nk = nb // tn, K // tk

    def body(*refs):
        if res is None:
            a_ref, b_ref, o_ref, acc = refs
        else:
            a_ref, b_ref, r_ref, o_ref, acc = refs
        k = pl.program_id(2)

        @pl.when(k == 0)
        def _():
            acc[...] = jnp.zeros_like(acc)

        acc[...] += _dot_nn(a_ref[...], b_ref[...])

        @pl.when(k == nk - 1)
        def _():
            r = acc[...]
            if res is not None:
                r = r + r_ref[...]
            o_ref[...] = r.astype(out_dtype)

    in_specs = [pl.BlockSpec((tm, tk), lambda i, n, k: (i, k)),
                pl.BlockSpec((None, tk, tn), lambda i, n, k: (n // per, k, n % per))]
    args = [a, b3]
    if res is not None:
        in_specs.append(pl.BlockSpec((tm, tn), lambda i, n, k: (i, n)))
        args.append(res)
    return pl.pallas_call(
        body, grid=(M // tm, J * per, nk), in_specs=in_specs,
        out_specs=pl.BlockSpec((tm, tn), lambda i, n, k: (i, n)),
        out_shape=jax.ShapeDtypeStruct((M, J * nb), out_dtype),
        scratch_shapes=[pltpu.VMEM((tm, tn), F32)], name=name,
        compiler_params=_params(("parallel", "parallel", "arbitrary"), VMEM_BIG))(*args)


def mm_nt(dy, b, *, out_dtype=F32, name):
    b3 = _as3(b)
    M, N = dy.shape
    J, K, nb = b3.shape
    tm, tn, tk = _pick(M, M_TILES), _pick(nb, N_TILES), _pick(K, K_TILES)
    per, nn = nb // tn, (J * nb) // tn

    def body(d_ref, b_ref, o_ref, acc):
        n = pl.program_id(2)

        @pl.when(n == 0)
        def _():
            acc[...] = jnp.zeros_like(acc)

        acc[...] += _dot_nt(d_ref[...], b_ref[...])

        @pl.when(n == nn - 1)
        def _():
            o_ref[...] = acc[...].astype(out_dtype)

    return pl.pallas_call(
        body, grid=(M // tm, K // tk, nn),
        in_specs=[pl.BlockSpec((tm, tn), lambda i, k, n: (i, n)),
                  pl.BlockSpec((None, tk, tn), lambda i, k, n: (n // per, k, n % per))],
        out_specs=pl.BlockSpec((tm, tk), lambda i, k, n: (i, k)),
        out_shape=jax.ShapeDtypeStruct((M, K), out_dtype),
        scratch_shapes=[pltpu.VMEM((tm, tk), F32)], name=name,
        compiler_params=_params(("parallel", "parallel", "arbitrary"), VMEM_BIG))(dy, b3)


def mm_tn(x, dy, *, blocks, out_dtype=BF16, name):
    M, K = x.shape
    _, N = dy.shape
    nb = N // blocks
    tm, tn, tk = _pick(M, M_TILES), _pick(nb, N_TILES), _pick(K, K_TILES)
    per, nm = nb // tn, M // tm

    def body(x_ref, d_ref, o_ref, acc):
        m = pl.program_id(2)

        @pl.when(m == 0)
        def _():
            acc[...] = jnp.zeros_like(acc)

        acc[...] += _dot_tn(x_ref[...], d_ref[...])

        @pl.when(m == nm - 1)
        def _():
            o_ref[...] = acc[...].astype(out_dtype)

    return pl.pallas_call(
        body, grid=(K // tk, N // tn, nm),
        in_specs=[pl.BlockSpec((tm, tk), lambda k, n, m: (m, k)),
                  pl.BlockSpec((tm, tn), lambda k, n, m: (m, n))],
        out_specs=pl.BlockSpec((None, tk, tn), lambda k, n, m: (n // per, k, n % per)),
        out_shape=jax.ShapeDtypeStruct((blocks, K, nb), out_dtype),
        scratch_shapes=[pltpu.VMEM((tk, tn), F32)], name=name,
        compiler_params=_params(("parallel", "parallel", "arbitrary"), VMEM_BIG))(x, dy)


ROW_TILES = (256, 128)


def rms_fwd(h, g, *, name):
    T, D = h.shape
    tm = _pick(T, ROW_TILES)

    def body(h_ref, g_ref, o_ref):
        x = h_ref[...]
        r = lax.rsqrt(jnp.mean(x * x, axis=-1, keepdims=True) + EPS)
        o_ref[...] = (x * r * g_ref[...]).astype(BF16)

    return pl.pallas_call(
        body, grid=(T // tm,),
        in_specs=[pl.BlockSpec((tm, D), lambda i: (i, 0)), pl.BlockSpec((1, D), lambda i: (0, 0))],
        out_specs=pl.BlockSpec((tm, D), lambda i: (i, 0)),
        out_shape=jax.ShapeDtypeStruct((T, D), BF16), name=name,
        compiler_params=_params(("parallel",), VMEM_MID))(h, g)


def rms_bwd(h, g, dhn, dres, *, name):
    T, D = h.shape
    tm = _pick(T, ROW_TILES)

    def body(h_ref, g_ref, d_ref, r_ref, dh_ref, dg_ref):
        @pl.when(pl.program_id(0) == 0)
        def _():
            dg_ref[...] = jnp.zeros_like(dg_ref)

        x = h_ref[...]
        r = lax.rsqrt(jnp.mean(x * x, axis=-1, keepdims=True) + EPS)
        xh = x * r
        dy = d_ref[...]
        dxh = dy * g_ref[...]
        dh_ref[...] = r_ref[...] + r * (dxh - xh * jnp.mean(dxh * xh, axis=-1, keepdims=True))
        dg_ref[...] += jnp.sum(dy * xh, axis=0, keepdims=True)

    row = pl.BlockSpec((tm, D), lambda i: (i, 0))
    vec = pl.BlockSpec((1, D), lambda i: (0, 0))
    return pl.pallas_call(
        body, grid=(T // tm,), in_specs=[row, vec, row, row], out_specs=[row, vec],
        out_shape=[jax.ShapeDtypeStruct((T, D), F32), jax.ShapeDtypeStruct((1, D), F32)], name=name,
        compiler_params=_params(("arbitrary",), VMEM_MID))(h, g, dhn, dres)


def loss_head(h, g, tgt, *, name):
    T, D = h.shape
    tm = _pick(T, ROW_TILES)

    def body(h_ref, g_ref, t_ref, l_ref, dh_ref, dg_ref):
        @pl.when(pl.program_id(0) == 0)
        def _():
            dg_ref[...] = jnp.zeros_like(dg_ref)
            l_ref[...] = jnp.zeros_like(l_ref)

        x = h_ref[...]
        gg = g_ref[...]
        r = lax.rsqrt(jnp.mean(x * x, axis=-1, keepdims=True) + EPS)
        xh = x * r
        e = xh * gg - t_ref[...]
        l_ref[...] += 0.5 * jnp.sum(jnp.mean(e * e, axis=-1, keepdims=True), axis=0, keepdims=True)
        dy = e * (1.0 / D)
        dxh = dy * gg
        dh_ref[...] = r * (dxh - xh * jnp.mean(dxh * xh, axis=-1, keepdims=True))
        dg_ref[...] += jnp.sum(dy * xh, axis=0, keepdims=True)

    row = pl.BlockSpec((tm, D), lambda i: (i, 0))
    vec = pl.BlockSpec((1, D), lambda i: (0, 0))
    return pl.pallas_call(
        body, grid=(T // tm,), in_specs=[row, vec, row],
        out_specs=[pl.BlockSpec((SUBLANES, LANES), lambda i: (0, 0)), row, vec],
        out_shape=[jax.ShapeDtypeStruct((SUBLANES, LANES), F32), jax.ShapeDtypeStruct((T, D), F32),
                   jax.ShapeDtypeStruct((1, D), F32)], name=name,
        compiler_params=_params(("arbitrary",), VMEM_MID))(h, g, tgt)


def _sgu_common(p, gv, w_ref, bexp, E, H, CH):
    Dg = E // H
    z = _gelu(p)
    u, v = z[:, :E], z[:, E:]
    r = lax.rsqrt(jnp.mean(v * v, axis=-1, keepdims=True) + EPS)
    vhat = v * r
    vn = (vhat * gv).astype(BF16)
    row = lax.broadcasted_iota(jnp.int32, (CH, CH), 0)
    col = lax.broadcasted_iota(jnp.int32, (CH, CH), 1)
    causal = row >= col
    ws = [jnp.where(causal, w_ref[hh], 0.0).astype(BF16) for hh in range(H)]
    s = jnp.concatenate([_dot_nn(ws[hh], vn[:, hh * Dg:(hh + 1) * Dg]) for hh in range(H)], axis=1) + bexp
    return u, r, vhat, vn, causal, ws, s


def sgu_fwd(p, g_v, w_s, bexp, *, name):
    T, E2 = p.shape
    E = E2 // 2
    H, CH, _ = w_s.shape

    def body(p_ref, gv_ref, w_ref, b_ref, o_ref):
        u, _, _, _, _, _, s = _sgu_common(p_ref[...], gv_ref[...], w_ref, b_ref[...], E, H, CH)
        o_ref[...] = (u * s).astype(BF16)

    return pl.pallas_call(
        body, grid=(T // CH,),
        in_specs=[pl.BlockSpec((CH, E2), lambda i: (i, 0)), pl.BlockSpec((1, E), lambda i: (0, 0)),
                  pl.BlockSpec((H, CH, CH), lambda i: (0, 0, 0)), pl.BlockSpec((CH, E), lambda i: (0, 0))],
        out_specs=pl.BlockSpec((CH, E), lambda i: (i, 0)),
        out_shape=jax.ShapeDtypeStruct((T, E), BF16), name=name,
        compiler_params=_params(("parallel",), VMEM_BIG))(p, g_v, w_s, bexp)


def sgu_bwd(p, d_us, g_v, w_s, bexp, *, name):
    T, E2 = p.shape
    E = E2 // 2
    H, CH, _ = w_s.shape
    Dg = E // H

    def body(p_ref, d_ref, gv_ref, w_ref, b_ref, dp_ref, dw_ref, db_ref, dg_ref):
        @pl.when(pl.program_id(0) == 0)
        def _():
            dw_ref[...] = jnp.zeros_like(dw_ref)
            db_ref[...] = jnp.zeros_like(db_ref)
            dg_ref[...] = jnp.zeros_like(dg_ref)

        p = p_ref[...]
        gv = gv_ref[...]
        u, r, vhat, vn, causal, ws, s = _sgu_common(p, gv, w_ref, b_ref[...], E, H, CH)
        d = d_ref[...]
        du = d * s
        ds = d * u
        lane = lax.broadcasted_iota(jnp.int32, (CH, LANES), 1)
        dvn_parts = []
        db = jnp.zeros((CH, LANES), F32)
        for hh in range(H):
            ds_h = ds[:, hh * Dg:(hh + 1) * Dg]
            ds_hb = ds_h.astype(BF16)
            dw_ref[hh] += jnp.where(causal, _dot_nt(ds_hb, vn[:, hh * Dg:(hh + 1) * Dg]), 0.0)
            dvn_parts.append(_dot_tn(ws[hh], ds_hb))
            db = db + jnp.where(lane == hh, jnp.sum(ds_h, axis=1, keepdims=True), 0.0)
        db_ref[...] += db
        dvn = jnp.concatenate(dvn_parts, axis=1)
        dg_ref[...] += jnp.sum(dvn * vhat, axis=0, keepdims=True)
        dvh = dvn * gv
        dv = r * (dvh - vhat * jnp.mean(dvh * vhat, axis=-1, keepdims=True))
        dp_ref[...] = (jnp.concatenate([du, dv], axis=1) * _gelu_grad(p)).astype(BF16)

    return pl.pallas_call(
        body, grid=(T // CH,),
        in_specs=[pl.BlockSpec((CH, E2), lambda i: (i, 0)), pl.BlockSpec((CH, E), lambda i: (i, 0)),
                  pl.BlockSpec((1, E), lambda i: (0, 0)), pl.BlockSpec((H, CH, CH), lambda i: (0, 0, 0)),
                  pl.BlockSpec((CH, E), lambda i: (0, 0))],
        out_specs=[pl.BlockSpec((CH, E2), lambda i: (i, 0)), pl.BlockSpec((H, CH, CH), lambda i: (0, 0, 0)),
                   pl.BlockSpec((CH, LANES), lambda i: (0, 0)), pl.BlockSpec((1, E), lambda i: (0, 0))],
        out_shape=[jax.ShapeDtypeStruct((T, E2), BF16), jax.ShapeDtypeStruct((H, CH, CH), F32),
                   jax.ShapeDtypeStruct((CH, LANES), F32), jax.ShapeDtypeStruct((1, E), F32)], name=name,
        compiler_params=_params(("arbitrary",), VMEM_BIG))(p, d_us, g_v, w_s, bexp)


def _s5_disc(a_re, a_im, log_dt, b_re, b_im):
    dt = jnp.exp(log_dt)
    mag = jnp.exp(dt * a_re)
    ar, ai = mag * jnp.cos(dt * a_im), mag * jnp.sin(dt * a_im)
    den = a_re * a_re + a_im * a_im
    qr = ((ar - 1.0) * a_re + ai * a_im) / den
    qi = (ai * a_re - (ar - 1.0) * a_im) / den
    return ar, ai, qr[None] * b_re - qi[None] * b_im, qr[None] * b_im + qi[None] * b_re


def s5_disc_fwd(a_re, a_im, log_dt, b_re, b_im, *, name):
    G, P = a_re.shape
    C = b_re.shape[0]

    def body(ar_ref, ai_ref, dt_ref, br_ref, bi_ref, o_ar, o_ai, o_br, o_bi):
        ar, ai, br, bi = _s5_disc(ar_ref[...], ai_ref[...], dt_ref[...], br_ref[...], bi_ref[...])
        o_ar[...] = ar
        o_ai[...] = ai
        o_br[...] = br
        o_bi[...] = bi

    gp = jax.ShapeDtypeStruct((G, P), F32)
    cgp = jax.ShapeDtypeStruct((C, G, P), F32)
    return pl.pallas_call(body, out_shape=[gp, gp, cgp, cgp], name=name)(a_re, a_im, log_dt, b_re, b_im)


def s5_disc_bwd(a_re, a_im, log_dt, b_re, b_im, d_ar, d_ai, d_br, d_bi, *, name):
    G, P = a_re.shape
    C = b_re.shape[0]

    def body(ar_ref, ai_ref, dt_ref, br_ref, bi_ref, g0, g1, g2, g3, o0, o1, o2, o3, o4):
        prim = (ar_ref[...], ai_ref[...], dt_ref[...], br_ref[...], bi_ref[...])
        _, vjp = jax.vjp(_s5_disc, *prim)
        outs = vjp((g0[...], g1[...], g2[...], g3[...]))
        for o, v in zip((o0, o1, o2, o3, o4), outs):
            o[...] = v

    gp = jax.ShapeDtypeStruct((G, P), F32)
    cgp = jax.ShapeDtypeStruct((C, G, P), F32)
    return pl.pallas_call(
        body, out_shape=[gp, gp, jax.ShapeDtypeStruct((G, 1), F32), cgp, cgp], name=name,
    )(a_re, a_im, log_dt, b_re, b_im, d_ar, d_ai, d_br, d_bi)


S5_KG = 8


def _scan_rows(xr_ref, xi_ref, row0, tb, a_r, a_i, h0r, h0i, *, reverse):
    if reverse:
        a_i = -a_i

    def step(j, carry):
        hr, hi = carry
        t = row0 + ((tb - 1 - j) if reverse else j)
        nr = a_r * hr - a_i * hi + xr_ref[pl.ds(t, 1), :]
        ni = a_r * hi + a_i * hr + xi_ref[pl.ds(t, 1), :]
        xr_ref[pl.ds(t, 1), :] = nr
        xi_ref[pl.ds(t, 1), :] = ni
        return nr, ni

    return lax.fori_loop(0, tb, step, (h0r, h0i), unroll=8)


def _s5_specs(tb, UC, SC, rev_nb=None):
    tmap = (lambda b: b) if rev_nb is None else (lambda b: rev_nb - 1 - b)
    row = pl.BlockSpec((tb, UC), lambda b, k: (tmap(b), k))
    wsp = pl.BlockSpec((None, UC, SC), lambda b, k: (k, 0, 0))
    vsc = pl.BlockSpec((None, 1, SC), lambda b, k: (k, 0, 0))
    vuc = pl.BlockSpec((None, 1, UC), lambda b, k: (k, 0, 0))
    st = pl.BlockSpec((None, None, 1, SC), lambda b, k: (tmap(b), k, 0, 0))
    return row, wsp, vsc, vuc, st


def s5_fwd(uu, bd_r, bd_i, ct_r, ct_i, ab_r, ab_i, dd, *, tb, name):
    T, E = uu.shape
    KB, UC, SC = bd_r.shape
    NB = T // tb

    def body(u_ref, bdr, bdi, ctr, cti, ar_ref, ai_ref, dd_ref, y_ref, q_ref, sr_ref, si_ref, xr, xi, cr, ci):
        b, k = pl.program_id(0), pl.program_id(1)

        @pl.when(b == 0)
        def _():
            cr[k] = jnp.zeros((1, SC), F32)
            ci[k] = jnp.zeros((1, SC), F32)

        h0r, h0i = cr[k], ci[k]
        sr_ref[...] = h0r
        si_ref[...] = h0i
        u = u_ref[...]
        ub = u.astype(BF16)
        xr[...] = _dot_nn(ub, bdr[...])
        xi[...] = _dot_nn(ub, bdi[...])
        hr, hi = _scan_rows(xr, xi, 0, tb, ar_ref[...], ai_ref[...], h0r, h0i, reverse=False)
        cr[k] = hr
        ci[k] = hi
        y = _dot_nt(xr[...].astype(BF16), ctr[...]) - _dot_nt(xi[...].astype(BF16), cti[...]) + dd_ref[...] * u
        y_ref[...] = y
        q_ref[...] = _gelu(y).astype(BF16)

    row, wsp, vsc, vuc, st = _s5_specs(tb, UC, SC)
    stsh = jax.ShapeDtypeStruct((NB, KB, 1, SC), F32)
    return pl.pallas_call(
        body, grid=(NB, KB), in_specs=[row, wsp, wsp, wsp, wsp, vsc, vsc, vuc], out_specs=[row, row, st, st],
        out_shape=[jax.ShapeDtypeStruct((T, E), F32), jax.ShapeDtypeStruct((T, E), BF16), stsh, stsh],
        scratch_shapes=[pltpu.VMEM((tb, SC), F32), pltpu.VMEM((tb, SC), F32),
                        pltpu.VMEM((KB, 1, SC), F32), pltpu.VMEM((KB, 1, SC), F32)],
        name=name, compiler_params=_params(("arbitrary", "arbitrary"), VMEM_MID),
    )(uu, bd_r, bd_i, ct_r, ct_i, ab_r, ab_i, dd)


def s5_bwd(uu, y, dq, st_r, st_i, bd_r, bd_i, ct_r, ct_i, ab_r, ab_i, dd, *, tb, name):
    T, E = uu.shape
    KB, UC, SC = bd_r.shape
    NB = T // tb
    HDR = SUBLANES

    def body(u_ref, y_ref, dq_ref, sr_ref, si_ref, bdr, bdi, ctr, cti, ar_ref, ai_ref, dd_ref,
             du_ref, obr, obi, ocr, oci, odar, odai, oddd,
             hr, hi, gr, gi, kr, ki, abr, abi, acr, aci):
        b, k = pl.program_id(0), pl.program_id(1)

        @pl.when(b == 0)
        def _():
            z1 = jnp.zeros((1, SC), F32)
            kr[k] = z1
            ki[k] = z1
            odar[k] = z1
            odai[k] = z1
            oddd[k] = jnp.zeros((1, UC), F32)
            zw = jnp.zeros((UC, SC), F32)
            abr[k] = zw
            abi[k] = zw
            acr[k] = zw
            aci[k] = zw

        @pl.when(jnp.logical_and(b == 0, k == 0))
        def _():
            hr[pl.ds(0, HDR), :] = jnp.zeros((HDR, SC), F32)
            hi[pl.ds(0, HDR), :] = jnp.zeros((HDR, SC), F32)

        u = u_ref[...]
        ub = u.astype(BF16)
        dy = dq_ref[...] * _gelu_grad(y_ref[...])
        dyb = dy.astype(BF16)
        oddd[k] += jnp.sum(dy * u, axis=0, keepdims=True)
        a_r, a_i = ar_ref[...], ai_ref[...]
        s0r, s0i = sr_ref[...], si_ref[...]
        hr[pl.ds(HDR - 1, 1), :] = s0r
        hi[pl.ds(HDR - 1, 1), :] = s0i
        hr[pl.ds(HDR, tb), :] = _dot_nn(ub, bdr[...])
        hi[pl.ds(HDR, tb), :] = _dot_nn(ub, bdi[...])
        gr[...] = _dot_nn(dyb, ctr[...])
        gi[...] = -_dot_nn(dyb, cti[...])
        _scan_rows(hr, hi, HDR, tb, a_r, a_i, s0r, s0i, reverse=False)
        g0r, g0i = _scan_rows(gr, gi, 0, tb, a_r, a_i, kr[k], ki[k], reverse=True)
        kr[k] = g0r
        ki[k] = g0i
        first = lax.broadcasted_iota(jnp.int32, (SUBLANES, SC), 0) == 0

        def slab(j, acc):
            s_r, s_i = acc
            o = pl.multiple_of(j * SUBLANES, SUBLANES)
            pr = jnp.where(first, pltpu.roll(hr[pl.ds(o, SUBLANES), :], 1, axis=0),
                           pltpu.roll(hr[pl.ds(o + HDR, SUBLANES), :], 1, axis=0))
            pi = jnp.where(first, pltpu.roll(hi[pl.ds(o, SUBLANES), :], 1, axis=0),
                           pltpu.roll(hi[pl.ds(o + HDR, SUBLANES), :], 1, axis=0))
            g_r = gr[pl.ds(o, SUBLANES), :]
            g_i = gi[pl.ds(o, SUBLANES), :]
            return s_r + g_r * pr + g_i * pi, s_i + g_i * pr - g_r * pi

        z8 = jnp.zeros((SUBLANES, SC), F32)
        s_r, s_i = lax.fori_loop(0, tb // SUBLANES, slab, (z8, z8))
        odar[k] += jnp.sum(s_r, axis=0, keepdims=True)
        odai[k] += jnp.sum(s_i, axis=0, keepdims=True)
        g_rb = gr[...].astype(BF16)
        g_ib = gi[...].astype(BF16)
        h_rb = hr[pl.ds(HDR, tb), :].astype(BF16)
        h_ib = hi[pl.ds(HDR, tb), :].astype(BF16)
        du_ref[...] = (_dot_nt(g_rb, bdr[...]) + _dot_nt(g_ib, bdi[...]) + dd_ref[...] * dy).astype(BF16)
        abr[k] += _dot_tn(ub, g_rb)
        abi[k] += _dot_tn(ub, g_ib)
        acr[k] += _dot_tn(dyb, h_rb)
        aci[k] -= _dot_tn(dyb, h_ib)

        @pl.when(jnp.logical_and(b == NB - 1, k == KB - 1))
        def _():
            pltpu.sync_copy(abr, obr)
            pltpu.sync_copy(abi, obi)
            pltpu.sync_copy(acr, ocr)
            pltpu.sync_copy(aci, oci)

    row, wsp, vsc, vuc, st = _s5_specs(tb, UC, SC, rev_nb=NB)
    hbm = pl.BlockSpec(memory_space=pltpu.HBM)
    full_sc = pl.BlockSpec((KB, 1, SC), lambda b, k: (0, 0, 0))
    full_uc = pl.BlockSpec((KB, 1, UC), lambda b, k: (0, 0, 0))
    wsh = jax.ShapeDtypeStruct((KB, UC, SC), F32)
    acc = pltpu.VMEM((KB, UC, SC), F32)
    return pl.pallas_call(
        body, grid=(NB, KB),
        in_specs=[row, row, row, st, st, wsp, wsp, wsp, wsp, vsc, vsc, vuc],
        out_specs=[row, hbm, hbm, hbm, hbm, full_sc, full_sc, full_uc],
        out_shape=[jax.ShapeDtypeStruct((T, E), BF16), wsh, wsh, wsh, wsh,
                   jax.ShapeDtypeStruct((KB, 1, SC), F32), jax.ShapeDtypeStruct((KB, 1, SC), F32),
                   jax.ShapeDtypeStruct((KB, 1, UC), F32)],
        scratch_shapes=[pltpu.VMEM((tb + HDR, SC), F32), pltpu.VMEM((tb + HDR, SC), F32),
                        pltpu.VMEM((tb, SC), F32), pltpu.VMEM((tb, SC), F32),
                        pltpu.VMEM((KB, 1, SC), F32), pltpu.VMEM((KB, 1, SC), F32), acc, acc, acc, acc],
        name=name, compiler_params=_params(("arbitrary", "arbitrary"), VMEM_BIG),
    )(uu, y, dq, st_r, st_i, bd_r, bd_i, ct_r, ct_i, ab_r, ab_i, dd)


def glu_fwd(h, pg, *, name):
    T, D = h.shape
    tm = _pick(T, ROW_TILES)

    def body(h_ref, a_ref, b_ref, o_ref):
        o_ref[...] = h_ref[...] + a_ref[...] * _sigmoid(b_ref[...])

    row = pl.BlockSpec((tm, D), lambda i: (i, 0))
    return pl.pallas_call(
        body, grid=(T // tm,), in_specs=[row, row, pl.BlockSpec((tm, D), lambda i: (i, 1))], out_specs=row,
        out_shape=jax.ShapeDtypeStruct((T, D), F32), name=name,
        compiler_params=_params(("parallel",), VMEM_MID))(h, pg, pg)


def glu_bwd(d, pg, *, name):
    T, D = d.shape
    tm = _pick(T, ROW_TILES)

    def body(d_ref, a_ref, b_ref, o_ref):
        dv = d_ref[...]
        sg = _sigmoid(b_ref[...])
        da = dv * sg
        db = dv * a_ref[...] * sg * (1.0 - sg)
        o_ref[...] = jnp.where(pl.program_id(1) == 0, da, db).astype(BF16)

    row = pl.BlockSpec((tm, D), lambda i, hf: (i, 0))
    return pl.pallas_call(
        body, grid=(T // tm, 2), in_specs=[row, row, pl.BlockSpec((tm, D), lambda i, hf: (i, 1))],
        out_specs=pl.BlockSpec((tm, D), lambda i, hf: (i, hf)),
        out_shape=jax.ShapeDtypeStruct((T, 2 * D), BF16), name=name,
        compiler_params=_params(("parallel", "arbitrary"), VMEM_MID))(d, pg, pg)


def _shift_down(x, halo, s):
    r = pltpu.roll(x, s, axis=0)
    hr = pltpu.roll(halo, s, axis=0)
    row = lax.broadcasted_iota(jnp.int32, halo.shape, 0)
    head = jnp.where(row < s, hr, r[:SUBLANES])
    return jnp.concatenate([head, r[SUBLANES:]], axis=0)


def _shift_up(x, halo, s):
    n = x.shape[0]
    r = pltpu.roll(x, n - s, axis=0)
    hr = pltpu.roll(halo, SUBLANES - s, axis=0)
    row = lax.broadcasted_iota(jnp.int32, halo.shape, 0)
    tail = jnp.where(row >= SUBLANES - s, hr, r[n - SUBLANES:])
    return jnp.concatenate([r[:n - SUBLANES], tail], axis=0)


def _conv_acc(z, zh, w, b, first):
    kw = w.shape[0]
    zh = jnp.where(first, 0.0, zh)
    acc = b + w[kw - 1:kw] * z
    shifted = []
    for k in range(kw - 1):
        zs = _shift_down(z, zh, kw - 1 - k)
        shifted.append(zs)
        acc = acc + w[k:k + 1] * zs
    return acc, shifted


def _conv_specs(T, F, tm, tc, KW):
    nfb = F // tc
    rb = tm // SUBLANES

    def main(off):
        return pl.BlockSpec((tm, tc), lambda i, c: (i, c + off))

    def halo(off):
        return pl.BlockSpec((SUBLANES, tc), lambda i, c: (jnp.maximum(i * rb - 1, 0), c + off))

    def wspec(off):
        return pl.BlockSpec((None, KW, tc), lambda i, c: (c + off, 0, 0))

    def bspec(off):
        return pl.BlockSpec((1, tc), lambda i, c: (0, c + off))

    return nfb, main, halo, wspec, bspec


def convglu_fwd(z, cw, cb, *, name):
    T, F2 = z.shape
    F = F2 // 2
    _, KW, tc = cw.shape
    tm = _pick(T, (256, 128))
    nfb, main, halo, wspec, bspec = _conv_specs(T, F, tm, tc, KW)

    def body(zg, zgh, zv, zvh, wg, wv, bg, bv, o_ref):
        first = pl.program_id(0) == 0
        g, _ = _conv_acc(zg[...], zgh[...], wg[...], bg[...], first)
        v, _ = _conv_acc(zv[...], zvh[...], wv[...], bv[...], first)
        o_ref[...] = (g * _sigmoid(g) * v).astype(BF16)

    return pl.pallas_call(
        body, grid=(T // tm, nfb),
        in_specs=[main(0), halo(0), main(nfb), halo(nfb), wspec(0), wspec(nfb), bspec(0), bspec(nfb)],
        out_specs=pl.BlockSpec((tm, tc), lambda i, c: (i, c)),
        out_shape=jax.ShapeDtypeStruct((T, F), BF16), name=name,
        compiler_params=_params(("parallel", "parallel"), VMEM_BIG))(z, z, z, z, cw, cw, cb, cb)


def convglu_bwd_acc(z, da, cw, cb, *, name):
    T, F2 = z.shape
    F = F2 // 2
    _, KW, tc = cw.shape
    tm = _pick(T, (256, 128))
    nfb, main, halo, wspec, bspec = _conv_specs(T, F, tm, tc, KW)

    def body(zg, zgh, zv, zvh, wg, wv, bg, bv, da_ref, o_ref, dwg, dwv, dbg, dbv):
        i = pl.program_id(1)
        first = i == 0

        @pl.when(first)
        def _():
            for o in (dwg, dwv, dbg, dbv):
                o[...] = jnp.zeros_like(o)

        zg_v, zv_v = zg[...], zv[...]
        g, sg_ = _conv_acc(zg_v, zgh[...], wg[...], bg[...], first)
        v, sv_ = _conv_acc(zv_v, zvh[...], wv[...], bv[...], first)
        d = da_ref[...]
        sig = _sigmoid(g)
        dg = d * v * sig * (1.0 + g * (1.0 - sig))
        dv = d * g * sig
        o_ref[0] = dg.astype(BF16)
        o_ref[1] = dv.astype(BF16)
        dbg[...] += jnp.sum(dg, axis=0, keepdims=True)
        dbv[...] += jnp.sum(dv, axis=0, keepdims=True)
        for k in range(KW):
            xg = zg_v if k == KW - 1 else sg_[k]
            xv = zv_v if k == KW - 1 else sv_[k]
            dwg[pl.ds(k, 1), :] += jnp.sum(dg * xg, axis=0, keepdims=True)
            dwv[pl.ds(k, 1), :] += jnp.sum(dv * xv, axis=0, keepdims=True)

    def sw(spec_fn, off):
        s = spec_fn(off)
        return pl.BlockSpec(s.block_shape, lambda c, i, f=s.index_map: f(i, c))

    both = jax.ShapeDtypeStruct((2, T, F), BF16)
    dwsh = jax.ShapeDtypeStruct((nfb, KW, tc), F32)
    dbsh = jax.ShapeDtypeStruct((1, F), F32)
    outs = pl.pallas_call(
        body, grid=(nfb, T // tm),
        in_specs=[sw(main, 0), sw(halo, 0), sw(main, nfb), sw(halo, nfb), sw(wspec, 0), sw(wspec, nfb),
                  sw(bspec, 0), sw(bspec, nfb), pl.BlockSpec((tm, tc), lambda c, i: (i, c))],
        out_specs=[pl.BlockSpec((2, tm, tc), lambda c, i: (0, i, c)),
                   pl.BlockSpec((None, KW, tc), lambda c, i: (c, 0, 0)), pl.BlockSpec((None, KW, tc), lambda c, i: (c, 0, 0)),
                   pl.BlockSpec((1, tc), lambda c, i: (0, c)), pl.BlockSpec((1, tc), lambda c, i: (0, c))],
        out_shape=[both, dwsh, dwsh, dbsh, dbsh], name=name,
        compiler_params=_params(("parallel", "arbitrary"), VMEM_BIG))(z, z, z, z, cw, cw, cb, cb, da)
    return outs


def conv_bwd_in(dacc, cw, *, name):
    _, T, F = dacc.shape
    _, KW, tc = cw.shape
    nfb = F // tc
    tm = _pick(T, (256, 128))
    rb = tm // (2 * SUBLANES)
    last_blk = T // (2 * SUBLANES) - 1

    def body(d_ref, dn_ref, w_ref, o_ref):
        last = pl.program_id(0) == pl.num_programs(0) - 1
        d = d_ref[...].astype(F32)
        dn = jnp.where(last, 0.0, dn_ref[...].astype(F32)[:SUBLANES])
        w = w_ref[...]
        out = w[KW - 1:KW] * d
        for k in range(KW - 1):
            out = out + w[k:k + 1] * _shift_up(d, dn, KW - 1 - k)
        o_ref[...] = out.astype(BF16)

    return pl.pallas_call(
        body, grid=(T // tm, 2, nfb),
        in_specs=[pl.BlockSpec((None, tm, tc), lambda i, hf, c: (hf, i, c)),
                  pl.BlockSpec((None, 2 * SUBLANES, tc), lambda i, hf, c: (hf, jnp.minimum((i + 1) * rb, last_blk), c)),
                  pl.BlockSpec((None, KW, tc), lambda i, hf, c: (hf * nfb + c, 0, 0))],
        out_specs=pl.BlockSpec((tm, tc), lambda i, hf, c: (i, hf * nfb + c)),
        out_shape=jax.ShapeDtypeStruct((T, 2 * F), BF16), name=name,
        compiler_params=_params(("parallel", "parallel", "parallel"), VMEM_BIG))(dacc, dacc, cw)


def _my_place():
    x, y, c = (lax.axis_index(a) for a in AXES)
    return x, y, c, 4 * x + 2 * y + c


def _peer(m, x, y, c):
    px = 1 - x if (m >> 2) & 1 else x
    py = 1 - y if (m >> 1) & 1 else y
    pc = 1 - c if m & 1 else c
    return (px, py, pc), 4 * px + 2 * py + pc


def _exchange(ins, out_shapes, plan, *, bcast, name):
    n_in, n_out, n = len(ins), len(out_shapes), len(plan)

    def body(*refs):
        in_refs, out_refs = refs[:n_in], refs[n_in:n_in + n_out]
        send_sems, recv_sems, loc_sems = refs[n_in + n_out:]
        x, y, c, me = _my_place()

        def src(f, who):
            r = in_refs[plan[f][0]]
            return r if bcast else r.at[who]

        def dst(f, who):
            r = out_refs[plan[f][1]]
            lay = plan[f][2]
            return r.at[who] if lay is None else r.at[lay, who]

        def remote(f, m, landing):
            dev, plin = _peer(m, x, y, c)
            return pltpu.make_async_remote_copy(
                src_ref=src(f, plin), dst_ref=dst(f, plin if landing else me), send_sem=send_sems.at[f, m - 1],
                recv_sem=recv_sems.at[f, m - 1], device_id=dev, device_id_type=MESH)

        locs = [pltpu.make_async_copy(src(f, me), dst(f, me), loc_sems.at[f]) for f in range(n)]
        for cp in locs:
            cp.start()
        for m in range(1, N_DEV):
            for f in range(n):
                remote(f, m, False).start()
        for m in range(1, N_DEV):
            for f in range(n):
                remote(f, m, True).wait()
        for cp in locs:
            cp.wait()

    hbm = pl.BlockSpec(memory_space=pltpu.HBM)
    return pl.pallas_call(
        body, in_specs=[hbm] * n_in, out_specs=[hbm] * n_out, out_shape=out_shapes,
        scratch_shapes=[pltpu.SemaphoreType.DMA((n, N_DEV - 1)), pltpu.SemaphoreType.DMA((n, N_DEV - 1)),
                        pltpu.SemaphoreType.DMA((n,))],
        name=name)(*ins)


def all_gather(shards, *, name):
    outs = [jax.ShapeDtypeStruct((N_DEV,) + s.shape, s.dtype) for s in shards]
    return _exchange(shards, outs, [(a, a, None) for a in range(len(shards))], bcast=True, name=name)


def exchange_layers(pieces, *, name):
    p0 = pieces[0]
    out = jax.ShapeDtypeStruct((len(pieces),) + p0.shape, p0.dtype)
    return _exchange(pieces, [out], [(l, 0, l) for l in range(len(pieces))], bcast=False, name=name)[0]


def _adamw(w, g, m, v):
    m = ADAM_B1 * m + (1.0 - ADAM_B1) * g
    v = ADAM_B2 * v + (1.0 - ADAM_B2) * (g * g)
    m_hat = m / (1.0 - ADAM_B1 ** ADAM_STEP)
    v_hat = v / (1.0 - ADAM_B2 ** ADAM_STEP)
    delta = -ADAM_LR * (m_hat / (jnp.sqrt(v_hat) + ADAM_EPS) + ADAM_WD * w)
    return delta, m, v


def adam_reduce(recv, w, m, v, *, name):
    L, _, R, C = recv.shape
    budget = 4 * 1024 * 1024
    tr = R
    for cand in (1024, 512, 352, 256, 176, 128, 64, 32, 16):
        if R % cand == 0 and N_DEV * cand * C * recv.dtype.itemsize <= budget:
            tr = cand
            break

    def body(r_ref, w_ref, m_ref, v_ref, g_ref, d_ref, nm_ref, nv_ref):
        g = r_ref[0].astype(F32)
        for s in range(1, N_DEV):
            g = g + r_ref[s].astype(F32)
        d, nm, nv = _adamw(w_ref[...], g, m_ref[...], v_ref[...])
        g_ref[...] = g
        d_ref[...] = d
        nm_ref[...] = nm
        nv_ref[...] = nv

    blk = pl.BlockSpec((None, tr, C), lambda l, r: (l, r, 0))
    sh = jax.ShapeDtypeStruct((L, R, C), F32)
    return pl.pallas_call(
        body, grid=(L, R // tr),
        in_specs=[pl.BlockSpec((None, N_DEV, tr, C), lambda l, r: (l, 0, r, 0)), blk, blk, blk],
        out_specs=[blk] * 4, out_shape=[sh] * 4, name=name,
        compiler_params=_params(("parallel", "parallel"), VMEM_MID))(recv, w, m, v)


def sum_slots(recv, *, name):
    _, R, C = recv.shape
    tr = _pick(R, (512, 256, 128, 64, 32, 16, 8))

    def body(r_ref, o_ref):
        g = r_ref[0]
        for s in range(1, N_DEV):
            g = g + r_ref[s]
        o_ref[...] = g

    return pl.pallas_call(
        body, grid=(R // tr,), in_specs=[pl.BlockSpec((N_DEV, tr, C), lambda r: (0, r, 0))],
        out_specs=pl.BlockSpec((tr, C), lambda r: (r, 0)), out_shape=jax.ShapeDtypeStruct((R, C), F32),
        name=name, compiler_params=_params(("parallel",)))(recv)


def adam_flat(g, w, m, v, *, name):
    R, C = g.shape
    tr = _pick(R, (512, 256, 128, 64, 32, 16, 8))

    def body(g_ref, w_ref, m_ref, v_ref, d_ref, nm_ref, nv_ref):
        d, nm, nv = _adamw(w_ref[...], g_ref[...], m_ref[...], v_ref[...])
        d_ref[...] = d
        nm_ref[...] = nm
        nv_ref[...] = nv

    blk = pl.BlockSpec((tr, C), lambda r: (r, 0))
    sh = jax.ShapeDtypeStruct((R, C), F32)
    return pl.pallas_call(body, grid=(R // tr,), in_specs=[blk] * 4, out_specs=[blk] * 3, out_shape=[sh] * 3,
                          name=name, compiler_params=_params(("parallel",)))(g, w, m, v)


WEIGHTS = ("norm_mix_g", "norm_ffn_g", "a_w_in", "a_g_v", "a_w_s", "a_b_s", "a_w_out", "b_w_in", "b_a_re", "b_a_im",
           "b_log_dt", "b_b_re", "b_b_im", "b_c_re", "b_c_im", "b_d", "b_w_glu", "f_w_up", "f_conv_w", "f_conv_b",
           "f_w_down", "final_g")
BIG = ("a_w_in", "a_w_out", "b_w_in", "b_w_glu", "f_w_up", "f_w_down")
FLAT_CHUNK = N_DEV * SUBLANES * LANES


def _expand(blocks, eye):
    KB, KG, C, P = blocks.shape
    return (blocks[:, :, :, None, :] * eye[None, :, None, :, None]).reshape(KB, KG * C, KG * P)


def _diag_blocks(dense, KG, C, P, eye):
    KB = dense.shape[0]
    return jnp.einsum("kgchp,gh->kgcp", dense.reshape(KB, KG, C, KG, P), eye)


def _flatten_pack(parts):
    flat = jnp.concatenate([p.reshape(-1) for p in parts])
    pad = (-flat.shape[0]) % FLAT_CHUNK
    return jnp.pad(flat, (0, pad))


def _unpack(flat, like):
    out, o = [], 0
    for p in like:
        n = math.prod(p.shape)
        out.append(flat[o:o + n].reshape(p.shape))
        o += n
    return out


def kernel(x, norm_mix_g, norm_ffn_g, a_w_in, a_g_v, a_w_s, a_b_s, a_w_out, b_w_in, b_a_re, b_a_im, b_log_dt, b_b_re, b_b_im, b_c_re, b_c_im, b_d, b_w_glu, f_w_up, f_conv_w, f_conv_b, f_w_down, final_g, loss_target, m_norm_mix_g, m_norm_ffn_g, m_a_w_in, m_a_g_v, m_a_w_s, m_a_b_s, m_a_w_out, m_b_w_in, m_b_a_re, m_b_a_im, m_b_log_dt, m_b_b_re, m_b_b_im, m_b_c_re, m_b_c_im, m_b_d, m_b_w_glu, m_f_w_up, m_f_conv_w, m_f_conv_b, m_f_w_down, m_final_g, v_norm_mix_g, v_norm_ffn_g, v_a_w_in, v_a_g_v, v_a_w_s, v_a_b_s, v_a_w_out, v_b_w_in, v_b_a_re, v_b_a_im, v_b_log_dt, v_b_b_re, v_b_b_im, v_b_c_re, v_b_c_im, v_b_d, v_b_w_glu, v_f_w_up, v_f_conv_w, v_f_conv_b, v_f_w_down, v_final_g):
    env = dict(locals())
    W = {n: env[n] for n in WEIGHTS}
    Mo = {n: env["m_" + n] for n in WEIGHTS}
    Vo = {n: env["v_" + n] for n in WEIGHTS}

    _, T, D = x.shape
    depth = norm_mix_g.shape[0]
    E_A = a_g_v.shape[1]
    H = a_w_s.shape[1]
    G, P, C = b_b_re.shape[1], b_b_re.shape[2], b_b_re.shape[3]
    E_B = G * C
    KG = S5_KG
    KB = G // KG
    F2 = f_conv_b.shape[1]
    tb = _pick(T, (512, 256, 128))
    eye = jnp.eye(KG, dtype=F32)
    _, _, _, me = _my_place()

    wb = {n: W[n].astype(BF16) for n in BIG}

    h = x[0]
    saved = []
    for i in range(depth):
        j = i // 2
        s = {}
        if i % 2 == 0:
            g_in, g_out, g_up, g_dn, g_cw = all_gather(
                [wb["a_w_in"][j], wb["a_w_out"][j], wb["f_w_up"][i], wb["f_w_down"][i], f_conv_w[i]],
                name=f"gather_a{i}")
            s["w_in"], s["w_out"] = g_in, g_out.reshape(E_A, D)
        else:
            g_in, g_glu, g_dd, g_up, g_dn, g_cw = all_gather(
                [wb["b_w_in"][j], wb["b_w_glu"][j], b_d[j][None], wb["f_w_up"][i], wb["f_w_down"][i], f_conv_w[i]],
                name=f"gather_b{i}")
            s["w_in"], s["w_glu"] = g_in.reshape(D, E_B), g_glu
            s["dd"] = g_dd.reshape(KB, 1, KG * C)
        s["w_up"], s["w_dn"], s["cw"] = g_up, g_dn.reshape(F2 // 2, D), g_cw
        s["h"] = h
        s["hn"] = rms_fwd(h, norm_mix_g[i][None], name=f"rms_mix{i}")
        if i % 2 == 0:
            s["p"] = mm_nn(s["hn"], s["w_in"], name=f"a_in{i}")
            s["bexp"] = jnp.repeat(a_b_s[j].T, E_A // H, axis=1)
            s["us"] = sgu_fwd(s["p"], a_g_v[j][None], a_w_s[j], s["bexp"], name=f"sgu_fwd{i}")
            h_mid = mm_nn(s["us"], s["w_out"], res=h, name=f"a_out{i}")
        else:
            s["uu"] = mm_nn(s["hn"], s["w_in"], name=f"b_in{i}")
            s["prm"] = (b_a_re[j], b_a_im[j], b_log_dt[j][:, None],
                        b_b_re[j].transpose(2, 0, 1), b_b_im[j].transpose(2, 0, 1))
            ar, ai, bbr, bbi = s5_disc_fwd(*s["prm"], name=f"s5_disc{i}")
            to_blocks = lambda t: t.reshape(C, KB, KG, P).transpose(1, 2, 0, 3)
            s["bd_r"] = _expand(to_blocks(bbr), eye).astype(BF16)
            s["bd_i"] = _expand(to_blocks(bbi), eye).astype(BF16)
            s["ct_r"] = _expand(b_c_re[j].reshape(KB, KG, C, P), eye).astype(BF16)
            s["ct_i"] = _expand(b_c_im[j].reshape(KB, KG, C, P), eye).astype(BF16)
            s["ab_r"], s["ab_i"] = ar.reshape(KB, 1, KG * P), ai.reshape(KB, 1, KG * P)
            s["y"], s["q"], s["st_r"], s["st_i"] = s5_fwd(
                s["uu"], s["bd_r"], s["bd_i"], s["ct_r"], s["ct_i"], s["ab_r"], s["ab_i"], s["dd"],
                tb=tb, name=f"s5_fwd{i}")
            s["pg"] = mm_nn(s["q"], s["w_glu"], name=f"b_glu{i}")
            h_mid = glu_fwd(h, s["pg"], name=f"glu_fwd{i}")
        s["h_mid"] = h_mid
        s["hn2"] = rms_fwd(h_mid, norm_ffn_g[i][None], name=f"rms_ffn{i}")
        s["z"] = mm_nn(s["hn2"], s["w_up"], name=f"f_up{i}")
        s["a"] = convglu_fwd(s["z"], s["cw"], f_conv_b[i][None], name=f"convglu_fwd{i}")
        h = mm_nn(s["a"], s["w_dn"], res=h_mid, name=f"f_down{i}")
        saved.append(s)

    loss_tile, dh, dg_final = loss_head(h, final_g[None], loss_target[0], name="loss_head")
    loss = lax.psum(loss_tile[0, 0], AXES)

    gbig = {n: [None] * W[n].shape[0] for n in BIG}
    gs = {n: [None] * W[n].shape[0] for n in WEIGHTS if n not in BIG and n != "final_g"}
    for i in reversed(range(depth)):
        j = i // 2
        s = saved[i]
        dhb = dh.astype(BF16)
        gbig["f_w_down"][i] = mm_tn(s["a"], dhb, blocks=1, name=f"g_down{i}").reshape(N_DEV, F2 // 2 // N_DEV, D)
        da = mm_nt(dhb, s["w_dn"], name=f"d_a{i}")
        dacc, dwg, dwv, dbg, dbv = convglu_bwd_acc(s["z"], da, s["cw"], f_conv_b[i][None], name=f"convglu_bwd{i}")
        gs["f_conv_w"][i] = jnp.concatenate([dwg, dwv], axis=0)
        gs["f_conv_b"][i] = jnp.concatenate([dbg, dbv], axis=1)[0]
        dz = conv_bwd_in(dacc, s["cw"], name=f"conv_bwd_in{i}")
        gbig["f_w_up"][i] = mm_tn(s["hn2"], dz, blocks=N_DEV, name=f"g_up{i}")
        dhn2 = mm_nt(dz, s["w_up"], name=f"d_hn2{i}")
        dh_mid, dg = rms_bwd(s["h_mid"], norm_ffn_g[i][None], dhn2, dh, name=f"rms_ffn_bwd{i}")
        gs["norm_ffn_g"][i] = dg[0]
        dmb = dh_mid.astype(BF16)
        if i % 2 == 0:
            gbig["a_w_out"][j] = mm_tn(s["us"], dmb, blocks=1, name=f"g_aout{i}").reshape(N_DEV, E_A // N_DEV, D)
            d_us = mm_nt(dmb, s["w_out"], name=f"d_us{i}")
            dp, dws, dbt, dgv = sgu_bwd(s["p"], d_us, a_g_v[j][None], a_w_s[j], s["bexp"], name=f"sgu_bwd{i}")
            gs["a_w_s"][j], gs["a_b_s"][j], gs["a_g_v"][j] = dws, dbt[:, :H].T, dgv[0]
            gbig["a_w_in"][j] = mm_tn(s["hn"], dp, blocks=N_DEV, name=f"g_ain{i}")
            dhn = mm_nt(dp, s["w_in"], name=f"d_hn_a{i}")
        else:
            dpg = glu_bwd(dh_mid, s["pg"], name=f"glu_bwd{i}")
            gbig["b_w_glu"][j] = mm_tn(s["q"], dpg, blocks=N_DEV, name=f"g_glu{i}")
            dq = mm_nt(dpg, s["w_glu"], name=f"d_q{i}")
            duu, dbr, dbi, dcr, dci, dar, dai, ddd = s5_bwd(
                s["uu"], s["y"], dq, s["st_r"], s["st_i"], s["bd_r"], s["bd_i"], s["ct_r"], s["ct_i"],
                s["ab_r"], s["ab_i"], s["dd"], tb=tb, name=f"s5_bwd{i}")
            from_blocks = lambda t: _diag_blocks(t, KG, C, P, eye).transpose(2, 0, 1, 3).reshape(C, G, P)
            d_are, d_aim, d_ldt, d_bre, d_bim = s5_disc_bwd(
                *s["prm"], dar.reshape(G, P), dai.reshape(G, P), from_blocks(dbr), from_blocks(dbi),
                name=f"s5_disc_bwd{i}")
            gs["b_a_re"][j], gs["b_a_im"][j], gs["b_log_dt"][j] = d_are, d_aim, d_ldt[:, 0]
            gs["b_b_re"][j], gs["b_b_im"][j] = d_bre.transpose(1, 2, 0), d_bim.transpose(1, 2, 0)
            gs["b_c_re"][j] = _diag_blocks(dcr, KG, C, P, eye).reshape(G, C, P)
            gs["b_c_im"][j] = _diag_blocks(dci, KG, C, P, eye).reshape(G, C, P)
            gs["b_d"][j] = ddd.reshape(E_B)
            gbig["b_w_in"][j] = mm_tn(s["hn"], duu, blocks=1, name=f"g_bin{i}").reshape(N_DEV, D // N_DEV, E_B)
            dhn = mm_nt(duu, s["w_in"], name=f"d_hn_b{i}")
        dh, dg = rms_bwd(s["h"], norm_mix_g[i][None], dhn, dh_mid, name=f"rms_mix_bwd{i}")
        gs["norm_mix_g"][i] = dg[0]
    grad_x = dh[None]

    grads, deltas, new_m, new_v = {}, {}, {}, {}
    for n in BIG:
        recv = exchange_layers(gbig[n], name=f"xchg_{n}")
        grads[n], deltas[n], new_m[n], new_v[n] = adam_reduce(recv, W[n], Mo[n], Vo[n], name=f"adam_{n}")

    small = [n for n in WEIGHTS if n not in BIG]
    full = {n: (dg_final[0] if n == "final_g" else jnp.stack(gs[n])) for n in small}
    flat = _flatten_pack([full[n] for n in small])
    rows = flat.shape[0] // N_DEV // LANES
    recv = exchange_layers([flat.reshape(N_DEV, rows, LANES)], name="xchg_small")[0]
    part = sum_slots(recv, name="sum_small")
    tot = all_gather([part], name="gather_small")[0].reshape(-1)
    red = dict(zip(small, _unpack(tot, [full[n] for n in small])))
    red["f_conv_w"] = lax.dynamic_index_in_dim(red["f_conv_w"], me, axis=1, keepdims=False)
    red["b_d"] = lax.dynamic_slice_in_dim(red["b_d"], me * (E_B // N_DEV), E_B // N_DEV, axis=1)
    gflat = _flatten_pack([red[n] for n in small]).reshape(-1, LANES)
    d_f, m_f, v_f = adam_flat(
        gflat, _flatten_pack([W[n] for n in small]).reshape(-1, LANES),
        _flatten_pack([Mo[n] for n in small]).reshape(-1, LANES),
        _flatten_pack([Vo[n] for n in small]).reshape(-1, LANES), name="adam_small")
    like = [W[n] for n in small]
    for n, d_, m_, v_ in zip(small, _unpack(d_f.reshape(-1), like), _unpack(m_f.reshape(-1), like),
                             _unpack(v_f.reshape(-1), like)):
        grads[n], deltas[n], new_m[n], new_v[n] = red[n], d_, m_, v_

    return (loss, grad_x, *[grads[n] for n in WEIGHTS], *[deltas[n] for n in WEIGHTS],
            *[new_m[n] for n in WEIGHTS], *[new_v[n] for n in WEIGHTS])
```

```python
import math

import jax
import jax.numpy as jnp
from jax import lax
from jax.experimental import pallas as pl
from jax.experimental.pallas import tpu as pltpu

F32 = jnp.float32
BF16 = jnp.bfloat16
N_DEV = 8
AXES = ("x", "y", "c")
EPS = 1e-6
LANES = 128
SUBLANES = 8
VMEM_BIG = 56 * 1024 * 1024
VMEM_MID = 40 * 1024 * 1024
ADAM_LR, ADAM_B1, ADAM_B2, ADAM_EPS, ADAM_WD, ADAM_STEP = 0.001, 0.9, 0.999, 1e-08, 0.01, 10
MESH = pl.DeviceIdType.MESH
GELU_C = math.sqrt(2.0 / math.pi)
GELU_K = 0.044715


def _pick(n, prefs):
    for p in prefs:
        if n % p == 0:
            return p
    return n


def _params(sem, vmem=None):
    return pltpu.CompilerParams(dimension_semantics=sem, vmem_limit_bytes=vmem)


def _gelu(x):
    return 0.5 * x * (1.0 + jnp.tanh(GELU_C * (x + GELU_K * x * x * x)))


def _gelu_grad(x):
    x2 = x * x
    th = jnp.tanh(GELU_C * x * (1.0 + GELU_K * x2))
    return 0.5 * (1.0 + th) + 0.5 * x * (1.0 - th * th) * GELU_C * (1.0 + 3.0 * GELU_K * x2)


def _sigmoid(x):
    return 1.0 / (1.0 + jnp.exp(-x))


def _dot_nn(a, b):
    return jnp.dot(a, b, preferred_element_type=F32)


def _dot_nt(a, b):
    return lax.dot_general(a, b, (((1,), (1,)), ((), ())), preferred_element_type=F32)


def _dot_tn(a, b):
    return lax.dot_general(a, b, (((0,), (0,)), ((), ())), preferred_element_type=F32)


M_TILES = (1024, 512, 256, 128)
N_TILES = (1408, 1024, 512, 384, 256, 128)
K_TILES = (1408, 1024, 512, 384, 256, 128)


def _as3(b):
    return b if b.ndim == 3 else b[None]


def mm_nn(a, b, *, res=None, out_dtype=F32, name):
    b3 = _as3(b)
    M, K = a.shape
    J, _, nb = b3.shape
    tm, tn, tk = _pick(M, M_TILES), _pick(nb, N_TILES), _pick(K, K_TILES)
    per, nk = nb // tn, K // tk

    def body(*refs):
        if res is None:
            a_ref, b_ref, o_ref, acc = refs
        else:
            a_ref, b_ref, r_ref, o_ref, acc = refs
        k = pl.program_id(2)

        @pl.when(k == 0)
        def _():
            acc[...] = jnp.zeros_like(acc)

        acc[...] += _dot_nn(a_ref[...], b_ref[...])

        @pl.when(k == nk - 1)
        def _():
            r = acc[...]
            if res is not None:
                r = r + r_ref[...]
            o_ref[...] = r.astype(out_dtype)

    in_specs = [pl.BlockSpec((tm, tk), lambda i, n, k: (i, k)),
                pl.BlockSpec((None, tk, tn), lambda i, n, k: (n // per, k, n % per))]
    args = [a, b3]
    if res is not None:
        in_specs.append(pl.BlockSpec((tm, tn), lambda i, n, k: (i, n)))
        args.append(res)
    return pl.pallas_call(
        body, grid=(M // tm, J * per, nk), in_specs=in_specs,
        out_specs=pl.BlockSpec((tm, tn), lambda i, n, k: (i, n)),
        out_shape=jax.ShapeDtypeStruct((M, J * nb), out_dtype),
        scratch_shapes=[pltpu.VMEM((tm, tn), F32)], name=name,
        compiler_params=_params(("parallel", "parallel", "arbitrary"), VMEM_BIG))(*args)


def mm_nt(dy, b, *, out_dtype=F32, name):
    b3 = _as3(b)
    M, N = dy.shape
    J, K, nb = b3.shape
    tm, tn, tk = _pick(M, M_TILES), _pick(nb, N_TILES), _pick(K, K_TILES)
    per, nn = nb // tn, (J * nb) // tn

    def body(d_ref, b_ref, o_ref, acc):
        n = pl.program_id(2)

        @pl.when(n == 0)
        def _():
            acc[...] = jnp.zeros_like(acc)

        acc[...] += _dot_nt(d_ref[...], b_ref[...])

        @pl.when(n == nn - 1)
        def _():
            o_ref[...] = acc[...].astype(out_dtype)

    return pl.pallas_call(
        body, grid=(M // tm, K // tk, nn),
        in_specs=[pl.BlockSpec((tm, tn), lambda i, k, n: (i, n)),
                  pl.BlockSpec((None, tk, tn), lambda i, k, n: (n // per, k, n % per))],
        out_specs=pl.BlockSpec((tm, tk), lambda i, k, n: (i, k)),
        out_shape=jax.ShapeDtypeStruct((M, K), out_dtype),
        scratch_shapes=[pltpu.VMEM((tm, tk), F32)], name=name,
        compiler_params=_params(("parallel", "parallel", "arbitrary"), VMEM_BIG))(dy, b3)


def mm_tn(x, dy, *, blocks, out_dtype=BF16, name):
    M, K = x.shape
    _, N = dy.shape
    nb = N // blocks
    tm, tn, tk = _pick(M, M_TILES), _pick(nb, N_TILES), _pick(K, K_TILES)
    per, nm = nb // tn, M // tm

    def body(x_ref, d_ref, o_ref, acc):
        m = pl.program_id(2)

        @pl.when(m == 0)
        def _():
            acc[...] = jnp.zeros_like(acc)

        acc[...] += _dot_tn(x_ref[...], d_ref[...])

        @pl.when(m == nm - 1)
        def _():
            o_ref[...] = acc[...].astype(out_dtype)

    return pl.pallas_call(
        body, grid=(K // tk, N // tn, nm),
        in_specs=[pl.BlockSpec((tm, tk), lambda k, n, m: (m, k)),
                  pl.BlockSpec((tm, tn), lambda k, n, m: (m, n))],
        out_specs=pl.BlockSpec((None, tk, tn), lambda k, n, m: (n // per, k, n % per)),
        out_shape=jax.ShapeDtypeStruct((blocks, K, nb), out_dtype),
        scratch_shapes=[pltpu.VMEM((tk, tn), F32)], name=name,
        compiler_params=_params(("parallel", "parallel", "arbitrary"), VMEM_BIG))(x, dy)


ROW_TILES = (256, 128)


def rms_fwd(h, g, *, name):
    T, D = h.shape
    tm = _pick(T, ROW_TILES)

    def body(h_ref, g_ref, o_ref):
        x = h_ref[...]
        r = lax.rsqrt(jnp.mean(x * x, axis=-1, keepdims=True) + EPS)
        o_ref[...] = (x * r * g_ref[...]).astype(BF16)

    return pl.pallas_call(
        body, grid=(T // tm,),
        in_specs=[pl.BlockSpec((tm, D), lambda i: (i, 0)), pl.BlockSpec((1, D), lambda i: (0, 0))],
        out_specs=pl.BlockSpec((tm, D), lambda i: (i, 0)),
        out_shape=jax.ShapeDtypeStruct((T, D), BF16), name=name,
        compiler_params=_params(("parallel",), VMEM_MID))(h, g)


def rms_bwd(h, g, dhn, dres, *, name):
    T, D = h.shape
    tm = _pick(T, ROW_TILES)

    def body(h_ref, g_ref, d_ref, r_ref, dh_ref, dg_ref):
        @pl.when(pl.program_id(0) == 0)
        def _():
            dg_ref[...] = jnp.zeros_like(dg_ref)

        x = h_ref[...]
        r = lax.rsqrt(jnp.mean(x * x, axis=-1, keepdims=True) + EPS)
        xh = x * r
        dy = d_ref[...]
        dxh = dy * g_ref[...]
        dh_ref[...] = r_ref[...] + r * (dxh - xh * jnp.mean(dxh * xh, axis=-1, keepdims=True))
        dg_ref[...] += jnp.sum(dy * xh, axis=0, keepdims=True)

    row = pl.BlockSpec((tm, D), lambda i: (i, 0))
    vec = pl.BlockSpec((1, D), lambda i: (0, 0))
    return pl.pallas_call(
        body, grid=(T // tm,), in_specs=[row, vec, row, row], out_specs=[row, vec],
        out_shape=[jax.ShapeDtypeStruct((T, D), F32), jax.ShapeDtypeStruct((1, D), F32)], name=name,
        compiler_params=_params(("arbitrary",), VMEM_MID))(h, g, dhn, dres)


def loss_head(h, g, tgt, *, name):
    T, D = h.shape
    tm = _pick(T, ROW_TILES)

    def body(h_ref, g_ref, t_ref, l_ref, dh_ref, dg_ref):
        @pl.when(pl.program_id(0) == 0)
        def _():
            dg_ref[...] = jnp.zeros_like(dg_ref)
            l_ref[...] = jnp.zeros_like(l_ref)

        x = h_ref[...]
        gg = g_ref[...]
        r = lax.rsqrt(jnp.mean(x * x, axis=-1, keepdims=True) + EPS)
        xh = x * r
        e = xh * gg - t_ref[...]
        l_ref[...] += 0.5 * jnp.sum(jnp.mean(e * e, axis=-1, keepdims=True), axis=0, keepdims=True)
        dy = e * (1.0 / D)
        dxh = dy * gg
        dh_ref[...] = r * (dxh - xh * jnp.mean(dxh * xh, axis=-1, keepdims=True))
        dg_ref[...] += jnp.sum(dy * xh, axis=0, keepdims=True)

    row = pl.BlockSpec((tm, D), lambda i: (i, 0))
    vec = pl.BlockSpec((1, D), lambda i: (0, 0))
    return pl.pallas_call(
        body, grid=(T // tm,), in_specs=[row, vec, row],
        out_specs=[pl.BlockSpec((SUBLANES, LANES), lambda i: (0, 0)), row, vec],
        out_shape=[jax.ShapeDtypeStruct((SUBLANES, LANES), F32), jax.ShapeDtypeStruct((T, D), F32),
                   jax.ShapeDtypeStruct((1, D), F32)], name=name,
        compiler_params=_params(("arbitrary",), VMEM_MID))(h, g, tgt)


def _sgu_common(p, gv, w_ref, bexp, E, H, CH):
    Dg = E // H
    z = _gelu(p)
    u, v = z[:, :E], z[:, E:]
    r = lax.rsqrt(jnp.mean(v * v, axis=-1, keepdims=True) + EPS)
    vhat = v * r
    vn = (vhat * gv).astype(BF16)
    row = lax.broadcasted_iota(jnp.int32, (CH, CH), 0)
    col = lax.broadcasted_iota(jnp.int32, (CH, CH), 1)
    causal = row >= col
    ws = [jnp.where(causal, w_ref[hh], 0.0).astype(BF16) for hh in range(H)]
    s = jnp.concatenate([_dot_nn(ws[hh], vn[:, hh * Dg:(hh + 1) * Dg]) for hh in range(H)], axis=1) + bexp
    return u, r, vhat, vn, causal, ws, s


def sgu_fwd(p, g_v, w_s, bexp, *, name):
    T, E2 = p.shape
    E = E2 // 2
    H, CH, _ = w_s.shape

    def body(p_ref, gv_ref, w_ref, b_ref, o_ref):
        u, _, _, _, _, _, s = _sgu_common(p_ref[...], gv_ref[...], w_ref, b_ref[...], E, H, CH)
        o_ref[...] = (u * s).astype(BF16)

    return pl.pallas_call(
        body, grid=(T // CH,),
        in_specs=[pl.BlockSpec((CH, E2), lambda i: (i, 0)), pl.BlockSpec((1, E), lambda i: (0, 0)),
                  pl.BlockSpec((H, CH, CH), lambda i: (0, 0, 0)), pl.BlockSpec((CH, E), lambda i: (0, 0))],
        out_specs=pl.BlockSpec((CH, E), lambda i: (i, 0)),
        out_shape=jax.ShapeDtypeStruct((T, E), BF16), name=name,
        compiler_params=_params(("parallel",), VMEM_BIG))(p, g_v, w_s, bexp)


def sgu_bwd(p, d_us, g_v, w_s, bexp, *, name):
    T, E2 = p.shape
    E = E2 // 2
    H, CH, _ = w_s.shape
    Dg = E // H

    def body(p_ref, d_ref, gv_ref, w_ref, b_ref, dp_ref, dw_ref, db_ref, dg_ref):
        @pl.when(pl.program_id(0) == 0)
        def _():
            dw_ref[...] = jnp.zeros_like(dw_ref)
            db_ref[...] = jnp.zeros_like(db_ref)
            dg_ref[...] = jnp.zeros_like(dg_ref)

        p = p_ref[...]
        gv = gv_ref[...]
        u, r, vhat, vn, causal, ws, s = _sgu_common(p, gv, w_ref, b_ref[...], E, H, CH)
        d = d_ref[...]
        du = d * s
        ds = d * u
        lane = lax.broadcasted_iota(jnp.int32, (CH, LANES), 1)
        dvn_parts = []
        db = jnp.zeros((CH, LANES), F32)
        for hh in range(H):
            ds_h = ds[:, hh * Dg:(hh + 1) * Dg]
            ds_hb = ds_h.astype(BF16)
            dw_ref[hh] += jnp.where(causal, _dot_nt(ds_hb, vn[:, hh * Dg:(hh + 1) * Dg]), 0.0)
            dvn_parts.append(_dot_tn(ws[hh], ds_hb))
            db = db + jnp.where(lane == hh, jnp.sum(ds_h, axis=1, keepdims=True), 0.0)
        db_ref[...] += db
        dvn = jnp.concatenate(dvn_parts, axis=1)
        dg_ref[...] += jnp.sum(dvn * vhat, axis=0, keepdims=True)
        dvh = dvn * gv
        dv = r * (dvh - vhat * jnp.mean(dvh * vhat, axis=-1, keepdims=True))
        dp_ref[...] = (jnp.concatenate([du, dv], axis=1) * _gelu_grad(p)).astype(BF16)

    return pl.pallas_call(
        body, grid=(T // CH,),
        in_specs=[pl.BlockSpec((CH, E2), lambda i: (i, 0)), pl.BlockSpec((CH, E), lambda i: (i, 0)),
                  pl.BlockSpec((1, E), lambda i: (0, 0)), pl.BlockSpec((H, CH, CH), lambda i: (0, 0, 0)),
                  pl.BlockSpec((CH, E), lambda i: (0, 0))],
        out_specs=[pl.BlockSpec((CH, E2), lambda i: (i, 0)), pl.BlockSpec((H, CH, CH), lambda i: (0, 0, 0)),
                   pl.BlockSpec((CH, LANES), lambda i: (0, 0)), pl.BlockSpec((1, E), lambda i: (0, 0))],
        out_shape=[jax.ShapeDtypeStruct((T, E2), BF16), jax.ShapeDtypeStruct((H, CH, CH), F32),
                   jax.ShapeDtypeStruct((CH, LANES), F32), jax.ShapeDtypeStruct((1, E), F32)], name=name,
        compiler_params=_params(("arbitrary",), VMEM_BIG))(p, d_us, g_v, w_s, bexp)


def _s5_disc(a_re, a_im, log_dt, b_re, b_im):
    dt = jnp.exp(log_dt)
    mag = jnp.exp(dt * a_re)
    ar, ai = mag * jnp.cos(dt * a_im), mag * jnp.sin(dt * a_im)
    den = a_re * a_re + a_im * a_im
    qr = ((ar - 1.0) * a_re + ai * a_im) / den
    qi = (ai * a_re - (ar - 1.0) * a_im) / den
    return ar, ai, qr[None] * b_re - qi[None] * b_im, qr[None] * b_im + qi[None] * b_re


def s5_disc_fwd(a_re, a_im, log_dt, b_re, b_im, *, name):
    G, P = a_re.shape
    C = b_re.shape[0]

    def body(ar_ref, ai_ref, dt_ref, br_ref, bi_ref, o_ar, o_ai, o_br, o_bi):
        ar, ai, br, bi = _s5_disc(ar_ref[...], ai_ref[...], dt_ref[...], br_ref[...], bi_ref[...])
        o_ar[...] = ar
        o_ai[...] = ai
        o_br[...] = br
        o_bi[...] = bi

    gp = jax.ShapeDtypeStruct((G, P), F32)
    cgp = jax.ShapeDtypeStruct((C, G, P), F32)
    return pl.pallas_call(body, out_shape=[gp, gp, cgp, cgp], name=name)(a_re, a_im, log_dt, b_re, b_im)


def s5_disc_bwd(a_re, a_im, log_dt, b_re, b_im, d_ar, d_ai, d_br, d_bi, *, name):
    G, P = a_re.shape
    C = b_re.shape[0]

    def body(ar_ref, ai_ref, dt_ref, br_ref, bi_ref, g0, g1, g2, g3, o0, o1, o2, o3, o4):
        prim = (ar_ref[...], ai_ref[...], dt_ref[...], br_ref[...], bi_ref[...])
        _, vjp = jax.vjp(_s5_disc, *prim)
        outs = vjp((g0[...], g1[...], g2[...], g3[...]))
        for o, v in zip((o0, o1, o2, o3, o4), outs):
            o[...] = v

    gp = jax.ShapeDtypeStruct((G, P), F32)
    cgp = jax.ShapeDtypeStruct((C, G, P), F32)
    return pl.pallas_call(
        body, out_shape=[gp, gp, jax.ShapeDtypeStruct((G, 1), F32), cgp, cgp], name=name,
    )(a_re, a_im, log_dt, b_re, b_im, d_ar, d_ai, d_br, d_bi)


S5_KG = 8


def _scan_rows(xr_ref, xi_ref, row0, tb, a_r, a_i, h0r, h0i, *, reverse):
    if reverse:
        a_i = -a_i

    def step(j, carry):
        hr, hi = carry
        t = row0 + ((tb - 1 - j) if reverse else j)
        nr = a_r * hr - a_i * hi + xr_ref[pl.ds(t, 1), :]
        ni = a_r * hi + a_i * hr + xi_ref[pl.ds(t, 1), :]
        xr_ref[pl.ds(t, 1), :] = nr
        xi_ref[pl.ds(t, 1), :] = ni
        return nr, ni

    return lax.fori_loop(0, tb, step, (h0r, h0i), unroll=8)


def _s5_specs(tb, UC, SC, rev_nb=None):
    tmap = (lambda b: b) if rev_nb is None else (lambda b: rev_nb - 1 - b)
    row = pl.BlockSpec((tb, UC), lambda b, k: (tmap(b), k))
    wsp = pl.BlockSpec((None, UC, SC), lambda b, k: (k, 0, 0))
    vsc = pl.BlockSpec((None, 1, SC), lambda b, k: (k, 0, 0))
    vuc = pl.BlockSpec((None, 1, UC), lambda b, k: (k, 0, 0))
    st = pl.BlockSpec((None, None, 1, SC), lambda b, k: (tmap(b), k, 0, 0))
    return row, wsp, vsc, vuc, st


def s5_fwd(uu, bd_r, bd_i, ct_r, ct_i, ab_r, ab_i, dd, *, tb, name):
    T, E = uu.shape
    KB, UC, SC = bd_r.shape
    NB = T // tb

    def body(u_ref, bdr, bdi, ctr, cti, ar_ref, ai_ref, dd_ref, y_ref, q_ref, sr_ref, si_ref, xr, xi, cr, ci):
        b, k = pl.program_id(0), pl.program_id(1)

        @pl.when(b == 0)
        def _():
            cr[k] = jnp.zeros((1, SC), F32)
            ci[k] = jnp.zeros((1, SC), F32)

        h0r, h0i = cr[k], ci[k]
        sr_ref[...] = h0r
        si_ref[...] = h0i
        u = u_ref[...]
        ub = u.astype(BF16)
        xr[...] = _dot_nn(ub, bdr[...])
        xi[...] = _dot_nn(ub, bdi[...])
        hr, hi = _scan_rows(xr, xi, 0, tb, ar_ref[...], ai_ref[...], h0r, h0i, reverse=False)
        cr[k] = hr
        ci[k] = hi
        y = _dot_nt(xr[...].astype(BF16), ctr[...]) - _dot_nt(xi[...].astype(BF16), cti[...]) + dd_ref[...] * u
        y_ref[...] = y
        q_ref[...] = _gelu(y).astype(BF16)

    row, wsp, vsc, vuc, st = _s5_specs(tb, UC, SC)
    stsh = jax.ShapeDtypeStruct((NB, KB, 1, SC), F32)
    return pl.pallas_call(
        body, grid=(NB, KB), in_specs=[row, wsp, wsp, wsp, wsp, vsc, vsc, vuc], out_specs=[row, row, st, st],
        out_shape=[jax.ShapeDtypeStruct((T, E), F32), jax.ShapeDtypeStruct((T, E), BF16), stsh, stsh],
        scratch_shapes=[pltpu.VMEM((tb, SC), F32), pltpu.VMEM((tb, SC), F32),
                        pltpu.VMEM((KB, 1, SC), F32), pltpu.VMEM((KB, 1, SC), F32)],
        name=name, compiler_params=_params(("arbitrary", "arbitrary"), VMEM_MID),
    )(uu, bd_r, bd_i, ct_r, ct_i, ab_r, ab_i, dd)


def s5_bwd(uu, y, dq, st_r, st_i, bd_r, bd_i, ct_r, ct_i, ab_r, ab_i, dd, *, tb, name):
    T, E = uu.shape
    KB, UC, SC = bd_r.shape
    NB = T // tb
    HDR = SUBLANES

    def body(u_ref, y_ref, dq_ref, sr_ref, si_ref, bdr, bdi, ctr, cti, ar_ref, ai_ref, dd_ref,
             du_ref, obr, obi, ocr, oci, odar, odai, oddd,
             hr, hi, gr, gi, kr, ki, abr, abi, acr, aci):
        b, k = pl.program_id(0), pl.program_id(1)

        @pl.when(b == 0)
        def _():
            z1 = jnp.zeros((1, SC), F32)
            kr[k] = z1
            ki[k] = z1
            odar[k] = z1
            odai[k] = z1
            oddd[k] = jnp.zeros((1, UC), F32)
            zw = jnp.zeros((UC, SC), F32)
            abr[k] = zw
            abi[k] = zw
            acr[k] = zw
            aci[k] = zw

        @pl.when(jnp.logical_and(b == 0, k == 0))
        def _():
            hr[pl.ds(0, HDR), :] = jnp.zeros((HDR, SC), F32)
            hi[pl.ds(0, HDR), :] = jnp.zeros((HDR, SC), F32)

        u = u_ref[...]
        ub = u.astype(BF16)
        dy = dq_ref[...] * _gelu_grad(y_ref[...])
        dyb = dy.astype(BF16)
        oddd[k] += jnp.sum(dy * u, axis=0, keepdims=True)
        a_r, a_i = ar_ref[...], ai_ref[...]
        s0r, s0i = sr_ref[...], si_ref[...]
        hr[pl.ds(HDR - 1, 1), :] = s0r
        hi[pl.ds(HDR - 1, 1), :] = s0i
        hr[pl.ds(HDR, tb), :] = _dot_nn(ub, bdr[...])
        hi[pl.ds(HDR, tb), :] = _dot_nn(ub, bdi[...])
        gr[...] = _dot_nn(dyb, ctr[...])
        gi[...] = -_dot_nn(dyb, cti[...])
        _scan_rows(hr, hi, HDR, tb, a_r, a_i, s0r, s0i, reverse=False)
        g0r, g0i = _scan_rows(gr, gi, 0, tb, a_r, a_i, kr[k], ki[k], reverse=True)
        kr[k] = g0r
        ki[k] = g0i
        first = lax.broadcasted_iota(jnp.int32, (SUBLANES, SC), 0) == 0

        def slab(j, acc):
            s_r, s_i = acc
            o = pl.multiple_of(j * SUBLANES, SUBLANES)
            pr = jnp.where(first, pltpu.roll(hr[pl.ds(o, SUBLANES), :], 1, axis=0),
                           pltpu.roll(hr[pl.ds(o + HDR, SUBLANES), :], 1, axis=0))
            pi = jnp.where(first, pltpu.roll(hi[pl.ds(o, SUBLANES), :], 1, axis=0),
                           pltpu.roll(hi[pl.ds(o + HDR, SUBLANES), :], 1, axis=0))
            g_r = gr[pl.ds(o, SUBLANES), :]
            g_i = gi[pl.ds(o, SUBLANES), :]
            return s_r + g_r * pr + g_i * pi, s_i + g_i * pr - g_r * pi

        z8 = jnp.zeros((SUBLANES, SC), F32)
        s_r, s_i = lax.fori_loop(0, tb // SUBLANES, slab, (z8, z8))
        odar[k] += jnp.sum(s_r, axis=0, keepdims=True)
        odai[k] += jnp.sum(s_i, axis=0, keepdims=True)
        g_rb = gr[...].astype(BF16)
        g_ib = gi[...].astype(BF16)
        h_rb = hr[pl.ds(HDR, tb), :].astype(BF16)
        h_ib = hi[pl.ds(HDR, tb), :].astype(BF16)
        du_ref[...] = (_dot_nt(g_rb, bdr[...]) + _dot_nt(g_ib, bdi[...]) + dd_ref[...] * dy).astype(BF16)
        abr[k] += _dot_tn(ub, g_rb)
        abi[k] += _dot_tn(ub, g_ib)
        acr[k] += _dot_tn(dyb, h_rb)
        aci[k] -= _dot_tn(dyb, h_ib)

        @pl.when(jnp.logical_and(b == NB - 1, k == KB - 1))
        def _():
            pltpu.sync_copy(abr, obr)
            pltpu.sync_copy(abi, obi)
            pltpu.sync_copy(acr, ocr)
            pltpu.sync_copy(aci, oci)

    row, wsp, vsc, vuc, st = _s5_specs(tb, UC, SC, rev_nb=NB)
    hbm = pl.BlockSpec(memory_space=pltpu.HBM)
    full_sc = pl.BlockSpec((KB, 1, SC), lambda b, k: (0, 0, 0))
    full_uc = pl.BlockSpec((KB, 1, UC), lambda b, k: (0, 0, 0))
    wsh = jax.ShapeDtypeStruct((KB, UC, SC), F32)
    acc = pltpu.VMEM((KB, UC, SC), F32)
    return pl.pallas_call(
        body, grid=(NB, KB),
        in_specs=[row, row, row, st, st, wsp, wsp, wsp, wsp, vsc, vsc, vuc],
        out_specs=[row, hbm, hbm, hbm, hbm, full_sc, full_sc, full_uc],
        out_shape=[jax.ShapeDtypeStruct((T, E), BF16), wsh, wsh, wsh, wsh,
                   jax.ShapeDtypeStruct((KB, 1, SC), F32), jax.ShapeDtypeStruct((KB, 1, SC), F32),
                   jax.ShapeDtypeStruct((KB, 1, UC), F32)],
        scratch_shapes=[pltpu.VMEM((tb + HDR, SC), F32), pltpu.VMEM((tb + HDR, SC), F32),
                        pltpu.VMEM((tb, SC), F32), pltpu.VMEM((tb, SC), F32),
                        pltpu.VMEM((KB, 1, SC), F32), pltpu.VMEM((KB, 1, SC), F32), acc, acc, acc, acc],
        name=name, compiler_params=_params(("arbitrary", "arbitrary"), VMEM_BIG),
    )(uu, y, dq, st_r, st_i, bd_r, bd_i, ct_r, ct_i, ab_r, ab_i, dd)


def glu_fwd(h, pg, *, name):
    T, D = h.shape
    tm = _pick(T, ROW_TILES)

    def body(h_ref, a_ref, b_ref, o_ref):
        o_ref[...] = h_ref[...] + a_ref[...] * _sigmoid(b_ref[...])

    row = pl.BlockSpec((tm, D), lambda i: (i, 0))
    return pl.pallas_call(
        body, grid=(T // tm,), in_specs=[row, row, pl.BlockSpec((tm, D), lambda i: (i, 1))], out_specs=row,
        out_shape=jax.ShapeDtypeStruct((T, D), F32), name=name,
        compiler_params=_params(("parallel",), VMEM_MID))(h, pg, pg)


def glu_bwd(d, pg, *, name):
    T, D = d.shape
    tm = _pick(T, ROW_TILES)

    def body(d_ref, a_ref, b_ref, o_ref):
        dv = d_ref[...]
        sg = _sigmoid(b_ref[...])
        da = dv * sg
        db = dv * a_ref[...] * sg * (1.0 - sg)
        o_ref[...] = jnp.where(pl.program_id(1) == 0, da, db).astype(BF16)

    row = pl.BlockSpec((tm, D), lambda i, hf: (i, 0))
    return pl.pallas_call(
        body, grid=(T // tm, 2), in_specs=[row, row, pl.BlockSpec((tm, D), lambda i, hf: (i, 1))],
        out_specs=pl.BlockSpec((tm, D), lambda i, hf: (i, hf)),
        out_shape=jax.ShapeDtypeStruct((T, 2 * D), BF16), name=name,
        compiler_params=_params(("parallel", "arbitrary"), VMEM_MID))(d, pg, pg)


def _shift_down(x, halo, s):
    r = pltpu.roll(x, s, axis=0)
    hr = pltpu.roll(halo, s, axis=0)
    row = lax.broadcasted_iota(jnp.int32, halo.shape, 0)
    head = jnp.where(row < s, hr, r[:SUBLANES])
    return jnp.concatenate([head, r[SUBLANES:]], axis=0)


def _shift_up(x, halo, s):
    n = x.shape[0]
    r = pltpu.roll(x, n - s, axis=0)
    hr = pltpu.roll(halo, SUBLANES - s, axis=0)
    row = lax.broadcasted_iota(jnp.int32, halo.shape, 0)
    tail = jnp.where(row >= SUBLANES - s, hr, r[n - SUBLANES:])
    return jnp.concatenate([r[:n - SUBLANES], tail], axis=0)


def _conv_acc(z, zh, w, b, first):
    kw = w.shape[0]
    zh = jnp.where(first, 0.0, zh)
    acc = b + w[kw - 1:kw] * z
    shifted = []
    for k in range(kw - 1):
        zs = _shift_down(z, zh, kw - 1 - k)
        shifted.append(zs)
        acc = acc + w[k:k + 1] * zs
    return acc, shifted


def _conv_specs(T, F, tm, tc, KW):
    nfb = F // tc
    rb = tm // SUBLANES

    def main(off):
        return pl.BlockSpec((tm, tc), lambda i, c: (i, c + off))

    def halo(off):
        return pl.BlockSpec((SUBLANES, tc), lambda i, c: (jnp.maximum(i * rb - 1, 0), c + off))

    def wspec(off):
        return pl.BlockSpec((None, KW, tc), lambda i, c: (c + off, 0, 0))

    def bspec(off):
        return pl.BlockSpec((1, tc), lambda i, c: (0, c + off))

    return nfb, main, halo, wspec, bspec


def convglu_fwd(z, cw, cb, *, name):
    T, F2 = z.shape
    F = F2 // 2
    _, KW, tc = cw.shape
    tm = _pick(T, (256, 128))
    nfb, main, halo, wspec, bspec = _conv_specs(T, F, tm, tc, KW)

    def body(zg, zgh, zv, zvh, wg, wv, bg, bv, o_ref):
        first = pl.program_id(0) == 0
        g, _ = _conv_acc(zg[...], zgh[...], wg[...], bg[...], first)
        v, _ = _conv_acc(zv[...], zvh[...], wv[...], bv[...], first)
        o_ref[...] = (g * _sigmoid(g) * v).astype(BF16)

    return pl.pallas_call(
        body, grid=(T // tm, nfb),
        in_specs=[main(0), halo(0), main(nfb), halo(nfb), wspec(0), wspec(nfb), bspec(0), bspec(nfb)],
        out_specs=pl.BlockSpec((tm, tc), lambda i, c: (i, c)),
        out_shape=jax.ShapeDtypeStruct((T, F), BF16), name=name,
        compiler_params=_params(("parallel", "parallel"), VMEM_BIG))(z, z, z, z, cw, cw, cb, cb)


def convglu_bwd_acc(z, da, cw, cb, *, name):
    T, F2 = z.shape
    F = F2 // 2
    _, KW, tc = cw.shape
    tm = _pick(T, (256, 128))
    nfb, main, halo, wspec, bspec = _conv_specs(T, F, tm, tc, KW)

    def body(zg, zgh, zv, zvh, wg, wv, bg, bv, da_ref, o_ref, dwg, dwv, dbg, dbv):
        i = pl.program_id(1)
        first = i == 0

        @pl.when(first)
        def _():
            for o in (dwg, dwv, dbg, dbv):
                o[...] = jnp.zeros_like(o)

        zg_v, zv_v = zg[...], zv[...]
        g, sg_ = _conv_acc(zg_v, zgh[...], wg[...], bg[...], first)
        v, sv_ = _conv_acc(zv_v, zvh[...], wv[...], bv[...], first)
        d = da_ref[...]
        sig = _sigmoid(g)
        dg = d * v * sig * (1.0 + g * (1.0 - sig))
        dv = d * g * sig
        o_ref[0] = dg.astype(BF16)
        o_ref[1] = dv.astype(BF16)
        dbg[...] += jnp.sum(dg, axis=0, keepdims=True)
        dbv[...] += jnp.sum(dv, axis=0, keepdims=True)
        for k in range(KW):
            xg = zg_v if k == KW - 1 else sg_[k]
            xv = zv_v if k == KW - 1 else sv_[k]
            dwg[pl.ds(k, 1), :] += jnp.sum(dg * xg, axis=0, keepdims=True)
            dwv[pl.ds(k, 1), :] += jnp.sum(dv * xv, axis=0, keepdims=True)

    def sw(spec_fn, off):
        s = spec_fn(off)
        return pl.BlockSpec(s.block_shape, lambda c, i, f=s.index_map: f(i, c))

    both = jax.ShapeDtypeStruct((2, T, F), BF16)
    dwsh = jax.ShapeDtypeStruct((nfb, KW, tc), F32)
    dbsh = jax.ShapeDtypeStruct((1, F), F32)
    outs = pl.pallas_call(
        body, grid=(nfb, T // tm),
        in_specs=[sw(main, 0), sw(halo, 0), sw(main, nfb), sw(halo, nfb), sw(wspec, 0), sw(wspec, nfb),
                  sw(bspec, 0), sw(bspec, nfb), pl.BlockSpec((tm, tc), lambda c, i: (i, c))],
        out_specs=[pl.BlockSpec((2, tm, tc), lambda c, i: (0, i, c)),
                   pl.BlockSpec((None, KW, tc), lambda c, i: (c, 0, 0)), pl.BlockSpec((None, KW, tc), lambda c, i: (c, 0, 0)),
                   pl.BlockSpec((1, tc), lambda c, i: (0, c)), pl.BlockSpec((1, tc), lambda c, i: (0, c))],
        out_shape=[both, dwsh, dwsh, dbsh, dbsh], name=name,
        compiler_params=_params(("parallel", "arbitrary"), VMEM_BIG))(z, z, z, z, cw, cw, cb, cb, da)
    return outs


def conv_bwd_in(dacc, cw, *, name):
    _, T, F = dacc.shape
    _, KW, tc = cw.shape
    nfb = F // tc
    tm = _pick(T, (256, 128))
    rb = tm // (2 * SUBLANES)
    last_blk = T // (2 * SUBLANES) - 1

    def body(d_ref, dn_ref, w_ref, o_ref):
        last = pl.program_id(0) == pl.num_programs(0) - 1
        d = d_ref[...].astype(F32)
        dn = jnp.where(last, 0.0, dn_ref[...].astype(F32)[:SUBLANES])
        w = w_ref[...]
        out = w[KW - 1:KW] * d
        for k in range(KW - 1):
            out = out + w[k:k + 1] * _shift_up(d, dn, KW - 1 - k)
        o_ref[...] = out.astype(BF16)

    return pl.pallas_call(
        body, grid=(T // tm, 2, nfb),
        in_specs=[pl.BlockSpec((None, tm, tc), lambda i, hf, c: (hf, i, c)),
                  pl.BlockSpec((None, 2 * SUBLANES, tc), lambda i, hf, c: (hf, jnp.minimum((i + 1) * rb, last_blk), c)),
                  pl.BlockSpec((None, KW, tc), lambda i, hf, c: (hf * nfb + c, 0, 0))],
        out_specs=pl.BlockSpec((tm, tc), lambda i, hf, c: (i, hf * nfb + c)),
        out_shape=jax.ShapeDtypeStruct((T, 2 * F), BF16), name=name,
        compiler_params=_params(("parallel", "parallel", "parallel"), VMEM_BIG))(dacc, dacc, cw)


def _my_place():
    x, y, c = (lax.axis_index(a) for a in AXES)
    return x, y, c, 4 * x + 2 * y + c


def _peer(m, x, y, c):
    px = 1 - x if (m >> 2) & 1 else x
    py = 1 - y if (m >> 1) & 1 else y
    pc = 1 - c if m & 1 else c
    return (px, py, pc), 4 * px + 2 * py + pc


def _exchange(ins, out_shapes, plan, *, bcast, name):
    n_in, n_out, n = len(ins), len(out_shapes), len(plan)

    def body(*refs):
        in_refs, out_refs = refs[:n_in], refs[n_in:n_in + n_out]
        send_sems, recv_sems, loc_sems = refs[n_in + n_out:]
        x, y, c, me = _my_place()

        def src(f, who):
            r = in_refs[plan[f][0]]
            return r if bcast else r.at[who]

        def dst(f, who):
            r = out_refs[plan[f][1]]
            lay = plan[f][2]
            return r.at[who] if lay is None else r.at[lay, who]

        def remote(f, m, landing):
            dev, plin = _peer(m, x, y, c)
            return pltpu.make_async_remote_copy(
                src_ref=src(f, plin), dst_ref=dst(f, plin if landing else me), send_sem=send_sems.at[f, m - 1],
                recv_sem=recv_sems.at[f, m - 1], device_id=dev, device_id_type=MESH)

        locs = [pltpu.make_async_copy(src(f, me), dst(f, me), loc_sems.at[f]) for f in range(n)]
        for cp in locs:
            cp.start()
        for m in range(1, N_DEV):
            for f in range(n):
                remote(f, m, False).start()
        for m in range(1, N_DEV):
            for f in range(n):
                remote(f, m, True).wait()
        for cp in locs:
            cp.wait()

    hbm = pl.BlockSpec(memory_space=pltpu.HBM)
    return pl.pallas_call(
        body, in_specs=[hbm] * n_in, out_specs=[hbm] * n_out, out_shape=out_shapes,
        scratch_shapes=[pltpu.SemaphoreType.DMA((n, N_DEV - 1)), pltpu.SemaphoreType.DMA((n, N_DEV - 1)),
                        pltpu.SemaphoreType.DMA((n,))],
        name=name)(*ins)


def all_gather(shards, *, name):
    outs = [jax.ShapeDtypeStruct((N_DEV,) + s.shape, s.dtype) for s in shards]
    return _exchange(shards, outs, [(a, a, None) for a in range(len(shards))], bcast=True, name=name)


def exchange_layers(pieces, *, name):
    p0 = pieces[0]
    out = jax.ShapeDtypeStruct((len(pieces),) + p0.shape, p0.dtype)
    return _exchange(pieces, [out], [(l, 0, l) for l in range(len(pieces))], bcast=False, name=name)[0]


_HBM = pl.BlockSpec(memory_space=pltpu.HBM)
_SEM = pl.BlockSpec(memory_space=pltpu.SEMAPHORE)
_EFFECT = pltpu.SideEffectType.DATAFLOW_SIDE_EFFECTING


def _split_copy(in_refs, land_refs, send_sems, recv_sems, bcast, f, m, place, landing):
    x, y, c, me = place
    dev, plin = _peer(m, x, y, c)
    src = in_refs[f] if bcast else in_refs[f].at[plin]
    return pltpu.make_async_remote_copy(
        src_ref=src, dst_ref=land_refs[f].at[plin if landing else me],
        send_sem=send_sems.at[f * (N_DEV - 1) + m - 1], recv_sem=recv_sems.at[f * (N_DEV - 1) + m - 1],
        device_id=dev, device_id_type=MESH)


def exchange_start(ins, *, bcast, name):
    n = len(ins)
    lands = [lax.empty(((N_DEV,) + a.shape) if bcast else a.shape, a.dtype) for a in ins]

    def body(*refs):
        in_refs, land_refs = refs[:n], refs[n:2 * n]
        send_sems, recv_sems, token = refs[2 * n], refs[2 * n + 1], refs[-1]
        place = _my_place()
        me = place[3]
        for f in range(n):
            pltpu.sync_copy(in_refs[f] if bcast else in_refs[f].at[me], land_refs[f].at[me])
        for m in range(1, N_DEV):
            for f in range(n):
                _split_copy(in_refs, land_refs, send_sems, recv_sems, bcast, f, m, place, False).start()
        token[...] = jnp.zeros_like(token)

    arrs = [pltpu.with_memory_space_constraint(a, pltpu.HBM) for a in (*ins, *lands)]
    sems = pltpu.SemaphoreType.DMA((n * (N_DEV - 1),))
    outs = pl.pallas_call(
        body, name=name,
        out_shape=(sems, sems, *[pltpu.HBM(a.shape, a.dtype) for a in arrs],
                   jax.ShapeDtypeStruct((SUBLANES, LANES), F32)),
        in_specs=[_HBM] * (2 * n),
        out_specs=(_SEM, _SEM, *[_HBM] * (2 * n), pl.BlockSpec(memory_space=pltpu.VMEM)),
        input_output_aliases={i: 2 + i for i in range(2 * n)},
        compiler_params=pltpu.CompilerParams(has_side_effects=_EFFECT))(*arrs)
    return outs[0], outs[1], list(outs[2:2 + 2 * n]), outs[-1]


def exchange_wait(started, after, *, bcast, name):
    send_sems, recv_sems, thrus, _ = started
    n = len(thrus) // 2

    def body(*refs):
        in_refs, land_refs = refs[:n], refs[n:2 * n]
        send, recv = refs[2 * n], refs[2 * n + 1]
        place = _my_place()
        for m in range(1, N_DEV):
            for f in range(n):
                cp = _split_copy(in_refs, land_refs, send, recv, bcast, f, m, place, True)
                cp.wait_send()
                cp.wait_recv()

    outs = pl.pallas_call(
        body, name=name, out_shape=[pltpu.HBM(a.shape, a.dtype) for a in thrus],
        in_specs=[_HBM] * (2 * n) + [_SEM, _SEM, pl.BlockSpec(memory_space=pl.ANY)], out_specs=[_HBM] * (2 * n),
        input_output_aliases={i: i for i in range(2 * n)},
        compiler_params=pltpu.CompilerParams(has_side_effects=_EFFECT))(*thrus, send_sems, recv_sems, after)
    return list(outs[n:])


def _after(x, token):
    return lax.optimization_barrier((x, token))[0]


def _adamw(w, g, m, v):
    m = ADAM_B1 * m + (1.0 - ADAM_B1) * g
    v = ADAM_B2 * v + (1.0 - ADAM_B2) * (g * g)
    m_hat = m / (1.0 - ADAM_B1 ** ADAM_STEP)
    v_hat = v / (1.0 - ADAM_B2 ** ADAM_STEP)
    delta = -ADAM_LR * (m_hat / (jnp.sqrt(v_hat) + ADAM_EPS) + ADAM_WD * w)
    return delta, m, v


def adam_reduce(recv, w, m, v, l, prev, *, name):
    _, R, C = recv.shape
    L = w.shape[0]
    budget = 4 * 1024 * 1024
    tr = R
    for cand in (1024, 512, 352, 256, 176, 128, 64, 32, 16):
        if R % cand == 0 and N_DEV * cand * C * recv.dtype.itemsize <= budget:
            tr = cand
            break

    def body(r_ref, w_ref, m_ref, v_ref, *rest):
        g_ref, d_ref, nm_ref, nv_ref = rest[-4:]
        g = r_ref[0].astype(F32)
        for s in range(1, N_DEV):
            g = g + r_ref[s].astype(F32)
        d, nm, nv = _adamw(w_ref[...], g, m_ref[...], v_ref[...])
        g_ref[...] = g
        d_ref[...] = d
        nm_ref[...] = nm
        nv_ref[...] = nv

    blk = pl.BlockSpec((None, tr, C), lambda r: (l, r, 0))
    sh = jax.ShapeDtypeStruct((L, R, C), F32)
    extra = [] if prev is None else list(prev)
    return pl.pallas_call(
        body, grid=(R // tr,),
        in_specs=[pl.BlockSpec((N_DEV, tr, C), lambda r: (0, r, 0)), blk, blk, blk]
        + [pl.BlockSpec(memory_space=pl.ANY)] * len(extra),
        out_specs=[blk] * 4, out_shape=[sh] * 4, name=name,
        input_output_aliases={4 + i: i for i in range(len(extra))},
        compiler_params=_params(("parallel",), VMEM_MID))(recv, w, m, v, *extra)


def sum_slots(recv, *, name):
    _, R, C = recv.shape
    tr = _pick(R, (512, 256, 128, 64, 32, 16, 8))

    def body(r_ref, o_ref):
        g = r_ref[0]
        for s in range(1, N_DEV):
            g = g + r_ref[s]
        o_ref[...] = g

    return pl.pallas_call(
        body, grid=(R // tr,), in_specs=[pl.BlockSpec((N_DEV, tr, C), lambda r: (0, r, 0))],
        out_specs=pl.BlockSpec((tr, C), lambda r: (r, 0)), out_shape=jax.ShapeDtypeStruct((R, C), F32),
        name=name, compiler_params=_params(("parallel",)))(recv)


def adam_flat(g, w, m, v, *, name):
    R, C = g.shape
    tr = _pick(R, (512, 256, 128, 64, 32, 16, 8))

    def body(g_ref, w_ref, m_ref, v_ref, d_ref, nm_ref, nv_ref):
        d, nm, nv = _adamw(w_ref[...], g_ref[...], m_ref[...], v_ref[...])
        d_ref[...] = d
        nm_ref[...] = nm
        nv_ref[...] = nv

    blk = pl.BlockSpec((tr, C), lambda r: (r, 0))
    sh = jax.ShapeDtypeStruct((R, C), F32)
    return pl.pallas_call(body, grid=(R // tr,), in_specs=[blk] * 4, out_specs=[blk] * 3, out_shape=[sh] * 3,
                          name=name, compiler_params=_params(("parallel",)))(g, w, m, v)


WEIGHTS = ("norm_mix_g", "norm_ffn_g", "a_w_in", "a_g_v", "a_w_s", "a_b_s", "a_w_out", "b_w_in", "b_a_re", "b_a_im",
           "b_log_dt", "b_b_re", "b_b_im", "b_c_re", "b_c_im", "b_d", "b_w_glu", "f_w_up", "f_conv_w", "f_conv_b",
           "f_w_down", "final_g")
BIG = ("a_w_in", "a_w_out", "b_w_in", "b_w_glu", "f_w_up", "f_w_down")
FLAT_CHUNK = N_DEV * SUBLANES * LANES


def _expand(blocks, eye):
    KB, KG, C, P = blocks.shape
    return (blocks[:, :, :, None, :] * eye[None, :, None, :, None]).reshape(KB, KG * C, KG * P)


def _diag_blocks(dense, KG, C, P, eye):
    KB = dense.shape[0]
    return jnp.einsum("kgchp,gh->kgcp", dense.reshape(KB, KG, C, KG, P), eye)


def _flatten_pack(parts):
    flat = jnp.concatenate([p.reshape(-1) for p in parts])
    pad = (-flat.shape[0]) % FLAT_CHUNK
    return jnp.pad(flat, (0, pad))


def _unpack(flat, like):
    out, o = [], 0
    for p in like:
        n = math.prod(p.shape)
        out.append(flat[o:o + n].reshape(p.shape))
        o += n
    return out


def kernel(x, norm_mix_g, norm_ffn_g, a_w_in, a_g_v, a_w_s, a_b_s, a_w_out, b_w_in, b_a_re, b_a_im, b_log_dt, b_b_re, b_b_im, b_c_re, b_c_im, b_d, b_w_glu, f_w_up, f_conv_w, f_conv_b, f_w_down, final_g, loss_target, m_norm_mix_g, m_norm_ffn_g, m_a_w_in, m_a_g_v, m_a_w_s, m_a_b_s, m_a_w_out, m_b_w_in, m_b_a_re, m_b_a_im, m_b_log_dt, m_b_b_re, m_b_b_im, m_b_c_re, m_b_c_im, m_b_d, m_b_w_glu, m_f_w_up, m_f_conv_w, m_f_conv_b, m_f_w_down, m_final_g, v_norm_mix_g, v_norm_ffn_g, v_a_w_in, v_a_g_v, v_a_w_s, v_a_b_s, v_a_w_out, v_b_w_in, v_b_a_re, v_b_a_im, v_b_log_dt, v_b_b_re, v_b_b_im, v_b_c_re, v_b_c_im, v_b_d, v_b_w_glu, v_f_w_up, v_f_conv_w, v_f_conv_b, v_f_w_down, v_final_g):
    env = dict(locals())
    W = {n: env[n] for n in WEIGHTS}
    Mo = {n: env["m_" + n] for n in WEIGHTS}
    Vo = {n: env["v_" + n] for n in WEIGHTS}

    _, T, D = x.shape
    depth = norm_mix_g.shape[0]
    E_A = a_g_v.shape[1]
    H = a_w_s.shape[1]
    G, P, C = b_b_re.shape[1], b_b_re.shape[2], b_b_re.shape[3]
    E_B = G * C
    KG = S5_KG
    KB = G // KG
    F2 = f_conv_b.shape[1]
    tb = _pick(T, (512, 256, 128))
    eye = jnp.eye(KG, dtype=F32)
    _, _, _, me = _my_place()

    wb = {n: W[n].astype(BF16) for n in BIG}

    def shards(i):
        j = i // 2
        if i % 2 == 0:
            return [wb["a_w_in"][j], wb["a_w_out"][j], wb["f_w_up"][i], wb["f_w_down"][i], f_conv_w[i]]
        return [wb["b_w_in"][j], wb["b_w_glu"][j], b_d[j][None], wb["f_w_up"][i], wb["f_w_down"][i], f_conv_w[i]]

    h = x[0]
    started = exchange_start(shards(0), bcast=True, name="gather_start0")
    saved = []
    for i in range(depth):
        j = i // 2
        s = {}
        gathered = exchange_wait(started, h, bcast=True, name=f"gather_wait{i}")
        if i + 1 < depth:
            gathered, nxt = lax.optimization_barrier((gathered, shards(i + 1)))
            started = exchange_start(nxt, bcast=True, name=f"gather_start{i + 1}")
            h = _after(h, started[3])
        if i % 2 == 0:
            g_in, g_out, g_up, g_dn, g_cw = gathered
            s["w_in"], s["w_out"] = g_in, g_out.reshape(E_A, D)
        else:
            g_in, g_glu, g_dd, g_up, g_dn, g_cw = gathered
            s["w_in"], s["w_glu"] = g_in.reshape(D, E_B), g_glu
            s["dd"] = g_dd.reshape(KB, 1, KG * C)
        s["w_up"], s["w_dn"], s["cw"] = g_up, g_dn.reshape(F2 // 2, D), g_cw
        s["h"] = h
        s["hn"] = rms_fwd(h, norm_mix_g[i][None], name=f"rms_mix{i}")
        if i % 2 == 0:
            s["p"] = mm_nn(s["hn"], s["w_in"], name=f"a_in{i}")
            s["bexp"] = jnp.repeat(a_b_s[j].T, E_A // H, axis=1)
            s["us"] = sgu_fwd(s["p"], a_g_v[j][None], a_w_s[j], s["bexp"], name=f"sgu_fwd{i}")
            h_mid = mm_nn(s["us"], s["w_out"], res=h, name=f"a_out{i}")
        else:
            s["uu"] = mm_nn(s["hn"], s["w_in"], name=f"b_in{i}")
            s["prm"] = (b_a_re[j], b_a_im[j], b_log_dt[j][:, None],
                        b_b_re[j].transpose(2, 0, 1), b_b_im[j].transpose(2, 0, 1))
            ar, ai, bbr, bbi = s5_disc_fwd(*s["prm"], name=f"s5_disc{i}")
            to_blocks = lambda t: t.reshape(C, KB, KG, P).transpose(1, 2, 0, 3)
            s["bd_r"] = _expand(to_blocks(bbr), eye).astype(BF16)
            s["bd_i"] = _expand(to_blocks(bbi), eye).astype(BF16)
            s["ct_r"] = _expand(b_c_re[j].reshape(KB, KG, C, P), eye).astype(BF16)
            s["ct_i"] = _expand(b_c_im[j].reshape(KB, KG, C, P), eye).astype(BF16)
            s["ab_r"], s["ab_i"] = ar.reshape(KB, 1, KG * P), ai.reshape(KB, 1, KG * P)
            s["y"], s["q"], s["st_r"], s["st_i"] = s5_fwd(
                s["uu"], s["bd_r"], s["bd_i"], s["ct_r"], s["ct_i"], s["ab_r"], s["ab_i"], s["dd"],
                tb=tb, name=f"s5_fwd{i}")
            s["pg"] = mm_nn(s["q"], s["w_glu"], name=f"b_glu{i}")
            h_mid = glu_fwd(h, s["pg"], name=f"glu_fwd{i}")
        s["h_mid"] = h_mid
        s["hn2"] = rms_fwd(h_mid, norm_ffn_g[i][None], name=f"rms_ffn{i}")
        s["z"] = mm_nn(s["hn2"], s["w_up"], name=f"f_up{i}")
        s["a"] = convglu_fwd(s["z"], s["cw"], f_conv_b[i][None], name=f"convglu_fwd{i}")
        h = mm_nn(s["a"], s["w_dn"], res=h_mid, name=f"f_down{i}")
        saved.append(s)

    loss_tile, dh, dg_final = loss_head(h, final_g[None], loss_target[0], name="loss_head")
    loss = lax.psum(loss_tile[0, 0], AXES)

    gbig = {n: [None] * W[n].shape[0] for n in BIG}
    gs = {n: [None] * W[n].shape[0] for n in WEIGHTS if n not in BIG and n != "final_g"}
    pending = []
    for i in reversed(range(depth)):
        j = i // 2
        s = saved[i]
        dhb = dh.astype(BF16)
        gbig["f_w_down"][i] = mm_tn(s["a"], dhb, blocks=1, name=f"g_down{i}").reshape(N_DEV, F2 // 2 // N_DEV, D)
        da = mm_nt(dhb, s["w_dn"], name=f"d_a{i}")
        dacc, dwg, dwv, dbg, dbv = convglu_bwd_acc(s["z"], da, s["cw"], f_conv_b[i][None], name=f"convglu_bwd{i}")
        gs["f_conv_w"][i] = jnp.concatenate([dwg, dwv], axis=0)
        gs["f_conv_b"][i] = jnp.concatenate([dbg, dbv], axis=1)[0]
        dz = conv_bwd_in(dacc, s["cw"], name=f"conv_bwd_in{i}")
        gbig["f_w_up"][i] = mm_tn(s["hn2"], dz, blocks=N_DEV, name=f"g_up{i}")
        dhn2 = mm_nt(dz, s["w_up"], name=f"d_hn2{i}")
        dh_mid, dg = rms_bwd(s["h_mid"], norm_ffn_g[i][None], dhn2, dh, name=f"rms_ffn_bwd{i}")
        gs["norm_ffn_g"][i] = dg[0]
        st = exchange_start([gbig["f_w_down"][i], gbig["f_w_up"][i]], bcast=False, name=f"xchg_ffn_start{i}")
        pending.append((st, [("f_w_down", i), ("f_w_up", i)], f"ffn{i}"))
        dh_mid = _after(dh_mid, st[3])
        dmb = dh_mid.astype(BF16)
        if i % 2 == 0:
            gbig["a_w_out"][j] = mm_tn(s["us"], dmb, blocks=1, name=f"g_aout{i}").reshape(N_DEV, E_A // N_DEV, D)
            d_us = mm_nt(dmb, s["w_out"], name=f"d_us{i}")
            dp, dws, dbt, dgv = sgu_bwd(s["p"], d_us, a_g_v[j][None], a_w_s[j], s["bexp"], name=f"sgu_bwd{i}")
            gs["a_w_s"][j], gs["a_b_s"][j], gs["a_g_v"][j] = dws, dbt[:, :H].T, dgv[0]
            gbig["a_w_in"][j] = mm_tn(s["hn"], dp, blocks=N_DEV, name=f"g_ain{i}")
            dhn = mm_nt(dp, s["w_in"], name=f"d_hn_a{i}")
        else:
            dpg = glu_bwd(dh_mid, s["pg"], name=f"glu_bwd{i}")
            gbig["b_w_glu"][j] = mm_tn(s["q"], dpg, blocks=N_DEV, name=f"g_glu{i}")
            dq = mm_nt(dpg, s["w_glu"], name=f"d_q{i}")
            duu, dbr, dbi, dcr, dci, dar, dai, ddd = s5_bwd(
                s["uu"], s["y"], dq, s["st_r"], s["st_i"], s["bd_r"], s["bd_i"], s["ct_r"], s["ct_i"],
                s["ab_r"], s["ab_i"], s["dd"], tb=tb, name=f"s5_bwd{i}")
            from_blocks = lambda t: _diag_blocks(t, KG, C, P, eye).transpose(2, 0, 1, 3).reshape(C, G, P)
            d_are, d_aim, d_ldt, d_bre, d_bim = s5_disc_bwd(
                *s["prm"], dar.reshape(G, P), dai.reshape(G, P), from_blocks(dbr), from_blocks(dbi),
                name=f"s5_disc_bwd{i}")
            gs["b_a_re"][j], gs["b_a_im"][j], gs["b_log_dt"][j] = d_are, d_aim, d_ldt[:, 0]
            gs["b_b_re"][j], gs["b_b_im"][j] = d_bre.transpose(1, 2, 0), d_bim.transpose(1, 2, 0)
            gs["b_c_re"][j] = _diag_blocks(dcr, KG, C, P, eye).reshape(G, C, P)
            gs["b_c_im"][j] = _diag_blocks(dci, KG, C, P, eye).reshape(G, C, P)
            gs["b_d"][j] = ddd.reshape(E_B)
            gbig["b_w_in"][j] = mm_tn(s["hn"], duu, blocks=1, name=f"g_bin{i}").reshape(N_DEV, D // N_DEV, E_B)
            dhn = mm_nt(duu, s["w_in"], name=f"d_hn_b{i}")
        mix = ("a_w_out", "a_w_in") if i % 2 == 0 else ("b_w_glu", "b_w_in")
        st = exchange_start([gbig[n][j] for n in mix], bcast=False, name=f"xchg_mix_start{i}")
        pending.append((st, [(n, j) for n in mix], f"mix{i}"))
        dhn = _after(dhn, st[3])
        dh, dg = rms_bwd(s["h"], norm_mix_g[i][None], dhn, dh_mid, name=f"rms_mix_bwd{i}")
        gs["norm_mix_g"][i] = dg[0]
    grad_x = dh[None]

    grads, deltas, new_m, new_v = {}, {}, {}, {}
    small = [n for n in WEIGHTS if n not in BIG]
    full = {n: (dg_final[0] if n == "final_g" else jnp.stack(gs[n])) for n in small}
    flat = _flatten_pack([full[n] for n in small])
    rows = flat.shape[0] // N_DEV // LANES
    recv = exchange_layers([flat.reshape(N_DEV, rows, LANES)], name="xchg_small")[0]
    part = sum_slots(recv, name="sum_small")
    tot = all_gather([part], name="gather_small")[0].reshape(-1)
    red = dict(zip(small, _unpack(tot, [full[n] for n in small])))
    red["f_conv_w"] = lax.dynamic_index_in_dim(red["f_conv_w"], me, axis=1, keepdims=False)
    red["b_d"] = lax.dynamic_slice_in_dim(red["b_d"], me * (E_B // N_DEV), E_B // N_DEV, axis=1)
    gflat = _flatten_pack([red[n] for n in small]).reshape(-1, LANES)
    d_f, m_f, v_f = adam_flat(
        gflat, _flatten_pack([W[n] for n in small]).reshape(-1, LANES),
        _flatten_pack([Mo[n] for n in small]).reshape(-1, LANES),
        _flatten_pack([Vo[n] for n in small]).reshape(-1, LANES), name="adam_small")
    like = [W[n] for n in small]
    for n, d_, m_, v_ in zip(small, _unpack(d_f.reshape(-1), like), _unpack(m_f.reshape(-1), like),
                             _unpack(v_f.reshape(-1), like)):
        grads[n], deltas[n], new_m[n], new_v[n] = red[n], d_, m_, v_

    done = {n: None for n in BIG}
    for st, items, tag in pending:
        recvs = exchange_wait(st, d_f, bcast=False, name=f"xchg_wait_{tag}")
        for (n, l), recv in zip(items, recvs):
            done[n] = adam_reduce(recv, W[n], Mo[n], Vo[n], l, done[n], name=f"adam_{n}{l}")
    for n in BIG:
        grads[n], deltas[n], new_m[n], new_v[n] = done[n]

    return (loss, grad_x, *[grads[n] for n in WEIGHTS], *[deltas[n] for n in WEIGHTS],
            *[new_m[n] for n in WEIGHTS], *[new_v[n] for n in WEIGHTS])
```

```python
import math

import jax
import jax.numpy as jnp
from jax import lax
from jax.experimental import pallas as pl
from jax.experimental.pallas import tpu as pltpu

F32 = jnp.float32
BF16 = jnp.bfloat16
N_DEV = 8
AXES = ("x", "y", "c")
EPS = 1e-6
LANES = 128
SUBLANES = 8
VMEM_BIG = 56 * 1024 * 1024
VMEM_MID = 40 * 1024 * 1024
ADAM_LR, ADAM_B1, ADAM_B2, ADAM_EPS, ADAM_WD, ADAM_STEP = 0.001, 0.9, 0.999, 1e-08, 0.01, 10
MESH = pl.DeviceIdType.MESH
GELU_C = math.sqrt(2.0 / math.pi)
GELU_K = 0.044715


def _pick(n, prefs):
    for p in prefs:
        if n % p == 0:
            return p
    return n


def _params(sem, vmem=None):
    return pltpu.CompilerParams(dimension_semantics=sem, vmem_limit_bytes=vmem)


def _gelu(x):
    return 0.5 * x * (1.0 + jnp.tanh(GELU_C * (x + GELU_K * x * x * x)))


def _gelu_grad(x):
    x2 = x * x
    th = jnp.tanh(GELU_C * x * (1.0 + GELU_K * x2))
    return 0.5 * (1.0 + th) + 0.5 * x * (1.0 - th * th) * GELU_C * (1.0 + 3.0 * GELU_K * x2)


def _sigmoid(x):
    return 1.0 / (1.0 + jnp.exp(-x))


def _dot_nn(a, b):
    return jnp.dot(a, b, preferred_element_type=F32)


def _dot_nt(a, b):
    return lax.dot_general(a, b, (((1,), (1,)), ((), ())), preferred_element_type=F32)


def _dot_tn(a, b):
    return lax.dot_general(a, b, (((0,), (0,)), ((), ())), preferred_element_type=F32)


M_TILES = (1024, 512, 256, 128)
N_TILES = (1408, 1024, 512, 384, 256, 128)
K_TILES = (1408, 1024, 512, 384, 256, 128)


def _as3(b):
    return b if b.ndim == 3 else b[None]


def mm_nn(a, b, *, res=None, out_dtype=F32, name):
    b3 = _as3(b)
    M, K = a.shape
    J, _, nb = b3.shape
    tm, tn, tk = _pick(M, M_TILES), _pick(nb, N_TILES), _pick(K, K_TILES)
    per, nk = nb // tn, K // tk

    def body(*refs):
        if res is None:
            a_ref, b_ref, o_ref, acc = refs
        else:
            a_ref, b_ref, r_ref, o_ref, acc = refs
        k = pl.program_id(2)

        @pl.when(k == 0)
        def _():
            acc[...] = jnp.zeros_like(acc)

        acc[...] += _dot_nn(a_ref[...], b_ref[...])

        @pl.when(k == nk - 1)
        def _():
            r = acc[...]
            if res is not None:
                r = r + r_ref[...]
            o_ref[...] = r.astype(out_dtype)

    in_specs = [pl.BlockSpec((tm, tk), lambda i, n, k: (i, k)),
                pl.BlockSpec((None, tk, tn), lambda i, n, k: (n // per, k, n % per))]
    args = [a, b3]
    if res is not None:
        in_specs.append(pl.BlockSpec((tm, tn), lambda i, n, k: (i, n)))
        args.append(res)
    return pl.pallas_call(
        body, grid=(M // tm, J * per, nk), in_specs=in_specs,
        out_specs=pl.BlockSpec((tm, tn), lambda i, n, k: (i, n)),
        out_shape=jax.ShapeDtypeStruct((M, J * nb), out_dtype),
        scratch_shapes=[pltpu.VMEM((tm, tn), F32)], name=name,
        compiler_params=_params(("parallel", "parallel", "arbitrary"), VMEM_BIG))(*args)


def mm_nt(dy, b, *, out_dtype=F32, name):
    b3 = _as3(b)
    M, N = dy.shape
    J, K, nb = b3.shape
    tm, tn, tk = _pick(M, M_TILES), _pick(nb, N_TILES), _pick(K, K_TILES)
    per, nn = nb // tn, (J * nb) // tn

    def body(d_ref, b_ref, o_ref, acc):
        n = pl.program_id(2)

        @pl.when(n == 0)
        def _():
            acc[...] = jnp.zeros_like(acc)

        acc[...] += _dot_nt(d_ref[...], b_ref[...])

        @pl.when(n == nn - 1)
        def _():
            o_ref[...] = acc[...].astype(out_dtype)

    return pl.pallas_call(
        body, grid=(M // tm, K // tk, nn),
        in_specs=[pl.BlockSpec((tm, tn), lambda i, k, n: (i, n)),
                  pl.BlockSpec((None, tk, tn), lambda i, k, n: (n // per, k, n % per))],
        out_specs=pl.BlockSpec((tm, tk), lambda i, k, n: (i, k)),
        out_shape=jax.ShapeDtypeStruct((M, K), out_dtype),
        scratch_shapes=[pltpu.VMEM((tm, tk), F32)], name=name,
        compiler_params=_params(("parallel", "parallel", "arbitrary"), VMEM_BIG))(dy, b3)


def mm_tn(x, dy, *, blocks, out_dtype=BF16, name):
    M, K = x.shape
    _, N = dy.shape
    nb = N // blocks
    tm, tn, tk = _pick(M, M_TILES), _pick(nb, N_TILES), _pick(K, K_TILES)
    per, nm = nb // tn, M // tm

    def body(x_ref, d_ref, o_ref, acc):
        m = pl.program_id(2)

        @pl.when(m == 0)
        def _():
            acc[...] = jnp.zeros_like(acc)

        acc[...] += _dot_tn(x_ref[...], d_ref[...])

        @pl.when(m == nm - 1)
        def _():
            o_ref[...] = acc[...].astype(out_dtype)

    return pl.pallas_call(
        body, grid=(K // tk, N // tn, nm),
        in_specs=[pl.BlockSpec((tm, tk), lambda k, n, m: (m, k)),
                  pl.BlockSpec((tm, tn), lambda k, n, m: (m, n))],
        out_specs=pl.BlockSpec((None, tk, tn), lambda k, n, m: (n // per, k, n % per)),
        out_shape=jax.ShapeDtypeStruct((blocks, K, nb), out_dtype),
        scratch_shapes=[pltpu.VMEM((tk, tn), F32)], name=name,
        compiler_params=_params(("parallel", "parallel", "arbitrary"), VMEM_BIG))(x, dy)


ROW_TILES = (256, 128)


_DEP = pl.BlockSpec(memory_space=pl.ANY)


def rms_fwd(h, g, *, dep=None, name):
    T, D = h.shape
    tm = _pick(T, ROW_TILES)
    deps = [] if dep is None else [dep]

    def body(h_ref, g_ref, *rest):
        o_ref = rest[-1]
        x = h_ref[...]
        r = lax.rsqrt(jnp.mean(x * x, axis=-1, keepdims=True) + EPS)
        o_ref[...] = (x * r * g_ref[...]).astype(BF16)

    return pl.pallas_call(
        body, grid=(T // tm,),
        in_specs=[pl.BlockSpec((tm, D), lambda i: (i, 0)), pl.BlockSpec((1, D), lambda i: (0, 0))] + [_DEP] * len(deps),
        out_specs=pl.BlockSpec((tm, D), lambda i: (i, 0)),
        out_shape=jax.ShapeDtypeStruct((T, D), BF16), name=name,
        compiler_params=_params(("parallel",), VMEM_MID))(h, g, *deps)


def rms_bwd(h, g, dhn, dres, *, dep=None, name):
    T, D = h.shape
    tm = _pick(T, ROW_TILES)
    deps = [] if dep is None else [dep]

    def body(h_ref, g_ref, d_ref, r_ref, *rest):
        dh_ref, dg_ref = rest[-2:]
        @pl.when(pl.program_id(0) == 0)
        def _():
            dg_ref[...] = jnp.zeros_like(dg_ref)

        x = h_ref[...]
        r = lax.rsqrt(jnp.mean(x * x, axis=-1, keepdims=True) + EPS)
        xh = x * r
        dy = d_ref[...]
        dxh = dy * g_ref[...]
        dh_ref[...] = r_ref[...] + r * (dxh - xh * jnp.mean(dxh * xh, axis=-1, keepdims=True))
        dg_ref[...] += jnp.sum(dy * xh, axis=0, keepdims=True)

    row = pl.BlockSpec((tm, D), lambda i: (i, 0))
    vec = pl.BlockSpec((1, D), lambda i: (0, 0))
    return pl.pallas_call(
        body, grid=(T // tm,), in_specs=[row, vec, row, row] + [_DEP] * len(deps), out_specs=[row, vec],
        out_shape=[jax.ShapeDtypeStruct((T, D), F32), jax.ShapeDtypeStruct((1, D), F32)], name=name,
        compiler_params=_params(("arbitrary",), VMEM_MID))(h, g, dhn, dres, *deps)


def loss_head(h, g, tgt, *, name):
    T, D = h.shape
    tm = _pick(T, ROW_TILES)

    def body(h_ref, g_ref, t_ref, l_ref, dh_ref, dg_ref):
        @pl.when(pl.program_id(0) == 0)
        def _():
            dg_ref[...] = jnp.zeros_like(dg_ref)
            l_ref[...] = jnp.zeros_like(l_ref)

        x = h_ref[...]
        gg = g_ref[...]
        r = lax.rsqrt(jnp.mean(x * x, axis=-1, keepdims=True) + EPS)
        xh = x * r
        e = xh * gg - t_ref[...]
        l_ref[...] += 0.5 * jnp.sum(jnp.mean(e * e, axis=-1, keepdims=True), axis=0, keepdims=True)
        dy = e * (1.0 / D)
        dxh = dy * gg
        dh_ref[...] = r * (dxh - xh * jnp.mean(dxh * xh, axis=-1, keepdims=True))
        dg_ref[...] += jnp.sum(dy * xh, axis=0, keepdims=True)

    row = pl.BlockSpec((tm, D), lambda i: (i, 0))
    vec = pl.BlockSpec((1, D), lambda i: (0, 0))
    return pl.pallas_call(
        body, grid=(T // tm,), in_specs=[row, vec, row],
        out_specs=[pl.BlockSpec((SUBLANES, LANES), lambda i: (0, 0)), row, vec],
        out_shape=[jax.ShapeDtypeStruct((SUBLANES, LANES), F32), jax.ShapeDtypeStruct((T, D), F32),
                   jax.ShapeDtypeStruct((1, D), F32)], name=name,
        compiler_params=_params(("arbitrary",), VMEM_MID))(h, g, tgt)


def _sgu_common(p, gv, w_ref, bexp, E, H, CH):
    Dg = E // H
    z = _gelu(p)
    u, v = z[:, :E], z[:, E:]
    r = lax.rsqrt(jnp.mean(v * v, axis=-1, keepdims=True) + EPS)
    vhat = v * r
    vn = (vhat * gv).astype(BF16)
    row = lax.broadcasted_iota(jnp.int32, (CH, CH), 0)
    col = lax.broadcasted_iota(jnp.int32, (CH, CH), 1)
    causal = row >= col
    ws = [jnp.where(causal, w_ref[hh], 0.0).astype(BF16) for hh in range(H)]
    s = jnp.concatenate([_dot_nn(ws[hh], vn[:, hh * Dg:(hh + 1) * Dg]) for hh in range(H)], axis=1) + bexp
    return u, r, vhat, vn, causal, ws, s


def sgu_fwd(p, g_v, w_s, bexp, *, name):
    T, E2 = p.shape
    E = E2 // 2
    H, CH, _ = w_s.shape

    def body(p_ref, gv_ref, w_ref, b_ref, o_ref):
        u, _, _, _, _, _, s = _sgu_common(p_ref[...], gv_ref[...], w_ref, b_ref[...], E, H, CH)
        o_ref[...] = (u * s).astype(BF16)

    return pl.pallas_call(
        body, grid=(T // CH,),
        in_specs=[pl.BlockSpec((CH, E2), lambda i: (i, 0)), pl.BlockSpec((1, E), lambda i: (0, 0)),
                  pl.BlockSpec((H, CH, CH), lambda i: (0, 0, 0)), pl.BlockSpec((CH, E), lambda i: (0, 0))],
        out_specs=pl.BlockSpec((CH, E), lambda i: (i, 0)),
        out_shape=jax.ShapeDtypeStruct((T, E), BF16), name=name,
        compiler_params=_params(("parallel",), VMEM_BIG))(p, g_v, w_s, bexp)


def sgu_bwd(p, d_us, g_v, w_s, bexp, *, name):
    T, E2 = p.shape
    E = E2 // 2
    H, CH, _ = w_s.shape
    Dg = E // H

    def body(p_ref, d_ref, gv_ref, w_ref, b_ref, dp_ref, dw_ref, db_ref, dg_ref):
        @pl.when(pl.program_id(0) == 0)
        def _():
            dw_ref[...] = jnp.zeros_like(dw_ref)
            db_ref[...] = jnp.zeros_like(db_ref)
            dg_ref[...] = jnp.zeros_like(dg_ref)

        p = p_ref[...]
        gv = gv_ref[...]
        u, r, vhat, vn, causal, ws, s = _sgu_common(p, gv, w_ref, b_ref[...], E, H, CH)
        d = d_ref[...]
        du = d * s
        ds = d * u
        lane = lax.broadcasted_iota(jnp.int32, (CH, LANES), 1)
        dvn_parts = []
        db = jnp.zeros((CH, LANES), F32)
        for hh in range(H):
            ds_h = ds[:, hh * Dg:(hh + 1) * Dg]
            ds_hb = ds_h.astype(BF16)
            dw_ref[hh] += jnp.where(causal, _dot_nt(ds_hb, vn[:, hh * Dg:(hh + 1) * Dg]), 0.0)
            dvn_parts.append(_dot_tn(ws[hh], ds_hb))
            db = db + jnp.where(lane == hh, jnp.sum(ds_h, axis=1, keepdims=True), 0.0)
        db_ref[...] += db
        dvn = jnp.concatenate(dvn_parts, axis=1)
        dg_ref[...] += jnp.sum(dvn * vhat, axis=0, keepdims=True)
        dvh = dvn * gv
        dv = r * (dvh - vhat * jnp.mean(dvh * vhat, axis=-1, keepdims=True))
        dp_ref[...] = (jnp.concatenate([du, dv], axis=1) * _gelu_grad(p)).astype(BF16)

    return pl.pallas_call(
        body, grid=(T // CH,),
        in_specs=[pl.BlockSpec((CH, E2), lambda i: (i, 0)), pl.BlockSpec((CH, E), lambda i: (i, 0)),
                  pl.BlockSpec((1, E), lambda i: (0, 0)), pl.BlockSpec((H, CH, CH), lambda i: (0, 0, 0)),
                  pl.BlockSpec((CH, E), lambda i: (0, 0))],
        out_specs=[pl.BlockSpec((CH, E2), lambda i: (i, 0)), pl.BlockSpec((H, CH, CH), lambda i: (0, 0, 0)),
                   pl.BlockSpec((CH, LANES), lambda i: (0, 0)), pl.BlockSpec((1, E), lambda i: (0, 0))],
        out_shape=[jax.ShapeDtypeStruct((T, E2), BF16), jax.ShapeDtypeStruct((H, CH, CH), F32),
                   jax.ShapeDtypeStruct((CH, LANES), F32), jax.ShapeDtypeStruct((1, E), F32)], name=name,
        compiler_params=_params(("arbitrary",), VMEM_BIG))(p, d_us, g_v, w_s, bexp)


def _s5_disc(a_re, a_im, log_dt, b_re, b_im):
    dt = jnp.exp(log_dt)
    mag = jnp.exp(dt * a_re)
    ar, ai = mag * jnp.cos(dt * a_im), mag * jnp.sin(dt * a_im)
    den = a_re * a_re + a_im * a_im
    qr = ((ar - 1.0) * a_re + ai * a_im) / den
    qi = (ai * a_re - (ar - 1.0) * a_im) / den
    return ar, ai, qr[None] * b_re - qi[None] * b_im, qr[None] * b_im + qi[None] * b_re


def s5_disc_fwd(a_re, a_im, log_dt, b_re, b_im, *, name):
    G, P = a_re.shape
    C = b_re.shape[0]

    def body(ar_ref, ai_ref, dt_ref, br_ref, bi_ref, o_ar, o_ai, o_br, o_bi):
        ar, ai, br, bi = _s5_disc(ar_ref[...], ai_ref[...], dt_ref[...], br_ref[...], bi_ref[...])
        o_ar[...] = ar
        o_ai[...] = ai
        o_br[...] = br
        o_bi[...] = bi

    gp = jax.ShapeDtypeStruct((G, P), F32)
    cgp = jax.ShapeDtypeStruct((C, G, P), F32)
    return pl.pallas_call(body, out_shape=[gp, gp, cgp, cgp], name=name)(a_re, a_im, log_dt, b_re, b_im)


def s5_disc_bwd(a_re, a_im, log_dt, b_re, b_im, d_ar, d_ai, d_br, d_bi, *, name):
    G, P = a_re.shape
    C = b_re.shape[0]

    def body(ar_ref, ai_ref, dt_ref, br_ref, bi_ref, g0, g1, g2, g3, o0, o1, o2, o3, o4):
        prim = (ar_ref[...], ai_ref[...], dt_ref[...], br_ref[...], bi_ref[...])
        _, vjp = jax.vjp(_s5_disc, *prim)
        outs = vjp((g0[...], g1[...], g2[...], g3[...]))
        for o, v in zip((o0, o1, o2, o3, o4), outs):
            o[...] = v

    gp = jax.ShapeDtypeStruct((G, P), F32)
    cgp = jax.ShapeDtypeStruct((C, G, P), F32)
    return pl.pallas_call(
        body, out_shape=[gp, gp, jax.ShapeDtypeStruct((G, 1), F32), cgp, cgp], name=name,
    )(a_re, a_im, log_dt, b_re, b_im, d_ar, d_ai, d_br, d_bi)


S5_KG = 8


def _scan_rows(xr_ref, xi_ref, row0, tb, a_r, a_i, h0r, h0i, *, reverse):
    if reverse:
        a_i = -a_i

    def step(j, carry):
        hr, hi = carry
        t = row0 + ((tb - 1 - j) if reverse else j)
        nr = a_r * hr - a_i * hi + xr_ref[pl.ds(t, 1), :]
        ni = a_r * hi + a_i * hr + xi_ref[pl.ds(t, 1), :]
        xr_ref[pl.ds(t, 1), :] = nr
        xi_ref[pl.ds(t, 1), :] = ni
        return nr, ni

    return lax.fori_loop(0, tb, step, (h0r, h0i), unroll=8)


def _s5_specs(tb, UC, SC, rev_nb=None):
    tmap = (lambda b: b) if rev_nb is None else (lambda b: rev_nb - 1 - b)
    row = pl.BlockSpec((tb, UC), lambda b, k: (tmap(b), k))
    wsp = pl.BlockSpec((None, UC, SC), lambda b, k: (k, 0, 0))
    vsc = pl.BlockSpec((None, 1, SC), lambda b, k: (k, 0, 0))
    vuc = pl.BlockSpec((None, 1, UC), lambda b, k: (k, 0, 0))
    st = pl.BlockSpec((None, None, 1, SC), lambda b, k: (tmap(b), k, 0, 0))
    return row, wsp, vsc, vuc, st


def s5_fwd(uu, bd_r, bd_i, ct_r, ct_i, ab_r, ab_i, dd, *, tb, name):
    T, E = uu.shape
    KB, UC, SC = bd_r.shape
    NB = T // tb

    def body(u_ref, bdr, bdi, ctr, cti, ar_ref, ai_ref, dd_ref, y_ref, q_ref, sr_ref, si_ref, xr, xi, cr, ci):
        b, k = pl.program_id(0), pl.program_id(1)

        @pl.when(b == 0)
        def _():
            cr[k] = jnp.zeros((1, SC), F32)
            ci[k] = jnp.zeros((1, SC), F32)

        h0r, h0i = cr[k], ci[k]
        sr_ref[...] = h0r
        si_ref[...] = h0i
        u = u_ref[...]
        ub = u.astype(BF16)
        xr[...] = _dot_nn(ub, bdr[...])
        xi[...] = _dot_nn(ub, bdi[...])
        hr, hi = _scan_rows(xr, xi, 0, tb, ar_ref[...], ai_ref[...], h0r, h0i, reverse=False)
        cr[k] = hr
        ci[k] = hi
        y = _dot_nt(xr[...].astype(BF16), ctr[...]) - _dot_nt(xi[...].astype(BF16), cti[...]) + dd_ref[...] * u
        y_ref[...] = y
        q_ref[...] = _gelu(y).astype(BF16)

    row, wsp, vsc, vuc, st = _s5_specs(tb, UC, SC)
    stsh = jax.ShapeDtypeStruct((NB, KB, 1, SC), F32)
    return pl.pallas_call(
        body, grid=(NB, KB), in_specs=[row, wsp, wsp, wsp, wsp, vsc, vsc, vuc], out_specs=[row, row, st, st],
        out_shape=[jax.ShapeDtypeStruct((T, E), F32), jax.ShapeDtypeStruct((T, E), BF16), stsh, stsh],
        scratch_shapes=[pltpu.VMEM((tb, SC), F32), pltpu.VMEM((tb, SC), F32),
                        pltpu.VMEM((KB, 1, SC), F32), pltpu.VMEM((KB, 1, SC), F32)],
        name=name, compiler_params=_params(("arbitrary", "arbitrary"), VMEM_MID),
    )(uu, bd_r, bd_i, ct_r, ct_i, ab_r, ab_i, dd)


def s5_bwd(uu, y, dq, st_r, st_i, bd_r, bd_i, ct_r, ct_i, ab_r, ab_i, dd, *, tb, name):
    T, E = uu.shape
    KB, UC, SC = bd_r.shape
    NB = T // tb
    HDR = SUBLANES

    def body(u_ref, y_ref, dq_ref, sr_ref, si_ref, bdr, bdi, ctr, cti, ar_ref, ai_ref, dd_ref,
             du_ref, obr, obi, ocr, oci, odar, odai, oddd,
             hr, hi, gr, gi, kr, ki, abr, abi, acr, aci):
        b, k = pl.program_id(0), pl.program_id(1)

        @pl.when(b == 0)
        def _():
            z1 = jnp.zeros((1, SC), F32)
            kr[k] = z1
            ki[k] = z1
            odar[k] = z1
            odai[k] = z1
            oddd[k] = jnp.zeros((1, UC), F32)
            zw = jnp.zeros((UC, SC), F32)
            abr[k] = zw
            abi[k] = zw
            acr[k] = zw
            aci[k] = zw

        @pl.when(jnp.logical_and(b == 0, k == 0))
        def _():
            hr[pl.ds(0, HDR), :] = jnp.zeros((HDR, SC), F32)
            hi[pl.ds(0, HDR), :] = jnp.zeros((HDR, SC), F32)

        u = u_ref[...]
        ub = u.astype(BF16)
        dy = dq_ref[...] * _gelu_grad(y_ref[...])
        dyb = dy.astype(BF16)
        oddd[k] += jnp.sum(dy * u, axis=0, keepdims=True)
        a_r, a_i = ar_ref[...], ai_ref[...]
        s0r, s0i = sr_ref[...], si_ref[...]
        hr[pl.ds(HDR - 1, 1), :] = s0r
        hi[pl.ds(HDR - 1, 1), :] = s0i
        hr[pl.ds(HDR, tb), :] = _dot_nn(ub, bdr[...])
        hi[pl.ds(HDR, tb), :] = _dot_nn(ub, bdi[...])
        gr[...] = _dot_nn(dyb, ctr[...])
        gi[...] = -_dot_nn(dyb, cti[...])
        _scan_rows(hr, hi, HDR, tb, a_r, a_i, s0r, s0i, reverse=False)
        g0r, g0i = _scan_rows(gr, gi, 0, tb, a_r, a_i, kr[k], ki[k], reverse=True)
        kr[k] = g0r
        ki[k] = g0i
        first = lax.broadcasted_iota(jnp.int32, (SUBLANES, SC), 0) == 0

        def slab(j, acc):
            s_r, s_i = acc
            o = pl.multiple_of(j * SUBLANES, SUBLANES)
            pr = jnp.where(first, pltpu.roll(hr[pl.ds(o, SUBLANES), :], 1, axis=0),
                           pltpu.roll(hr[pl.ds(o + HDR, SUBLANES), :], 1, axis=0))
            pi = jnp.where(first, pltpu.roll(hi[pl.ds(o, SUBLANES), :], 1, axis=0),
                           pltpu.roll(hi[pl.ds(o + HDR, SUBLANES), :], 1, axis=0))
            g_r = gr[pl.ds(o, SUBLANES), :]
            g_i = gi[pl.ds(o, SUBLANES), :]
            return s_r + g_r * pr + g_i * pi, s_i + g_i * pr - g_r * pi

        z8 = jnp.zeros((SUBLANES, SC), F32)
        s_r, s_i = lax.fori_loop(0, tb // SUBLANES, slab, (z8, z8))
        odar[k] += jnp.sum(s_r, axis=0, keepdims=True)
        odai[k] += jnp.sum(s_i, axis=0, keepdims=True)
        g_rb = gr[...].astype(BF16)
        g_ib = gi[...].astype(BF16)
        h_rb = hr[pl.ds(HDR, tb), :].astype(BF16)
        h_ib = hi[pl.ds(HDR, tb), :].astype(BF16)
        du_ref[...] = (_dot_nt(g_rb, bdr[...]) + _dot_nt(g_ib, bdi[...]) + dd_ref[...] * dy).astype(BF16)
        abr[k] += _dot_tn(ub, g_rb)
        abi[k] += _dot_tn(ub, g_ib)
        acr[k] += _dot_tn(dyb, h_rb)
        aci[k] -= _dot_tn(dyb, h_ib)

        @pl.when(jnp.logical_and(b == NB - 1, k == KB - 1))
        def _():
            pltpu.sync_copy(abr, obr)
            pltpu.sync_copy(abi, obi)
            pltpu.sync_copy(acr, ocr)
            pltpu.sync_copy(aci, oci)

    row, wsp, vsc, vuc, st = _s5_specs(tb, UC, SC, rev_nb=NB)
    hbm = pl.BlockSpec(memory_space=pltpu.HBM)
    full_sc = pl.BlockSpec((KB, 1, SC), lambda b, k: (0, 0, 0))
    full_uc = pl.BlockSpec((KB, 1, UC), lambda b, k: (0, 0, 0))
    wsh = jax.ShapeDtypeStruct((KB, UC, SC), F32)
    acc = pltpu.VMEM((KB, UC, SC), F32)
    return pl.pallas_call(
        body, grid=(NB, KB),
        in_specs=[row, row, row, st, st, wsp, wsp, wsp, wsp, vsc, vsc, vuc],
        out_specs=[row, hbm, hbm, hbm, hbm, full_sc, full_sc, full_uc],
        out_shape=[jax.ShapeDtypeStruct((T, E), BF16), wsh, wsh, wsh, wsh,
                   jax.ShapeDtypeStruct((KB, 1, SC), F32), jax.ShapeDtypeStruct((KB, 1, SC), F32),
                   jax.ShapeDtypeStruct((KB, 1, UC), F32)],
        scratch_shapes=[pltpu.VMEM((tb + HDR, SC), F32), pltpu.VMEM((tb + HDR, SC), F32),
                        pltpu.VMEM((tb, SC), F32), pltpu.VMEM((tb, SC), F32),
                        pltpu.VMEM((KB, 1, SC), F32), pltpu.VMEM((KB, 1, SC), F32), acc, acc, acc, acc],
        name=name, compiler_params=_params(("arbitrary", "arbitrary"), VMEM_BIG),
    )(uu, y, dq, st_r, st_i, bd_r, bd_i, ct_r, ct_i, ab_r, ab_i, dd)


def glu_fwd(h, pg, *, name):
    T, D = h.shape
    tm = _pick(T, ROW_TILES)

    def body(h_ref, a_ref, b_ref, o_ref):
        o_ref[...] = h_ref[...] + a_ref[...] * _sigmoid(b_ref[...])

    row = pl.BlockSpec((tm, D), lambda i: (i, 0))
    return pl.pallas_call(
        body, grid=(T // tm,), in_specs=[row, row, pl.BlockSpec((tm, D), lambda i: (i, 1))], out_specs=row,
        out_shape=jax.ShapeDtypeStruct((T, D), F32), name=name,
        compiler_params=_params(("parallel",), VMEM_MID))(h, pg, pg)


def glu_bwd(d, pg, *, name):
    T, D = d.shape
    tm = _pick(T, ROW_TILES)

    def body(d_ref, a_ref, b_ref, o_ref):
        dv = d_ref[...]
        sg = _sigmoid(b_ref[...])
        da = dv * sg
        db = dv * a_ref[...] * sg * (1.0 - sg)
        o_ref[...] = jnp.where(pl.program_id(1) == 0, da, db).astype(BF16)

    row = pl.BlockSpec((tm, D), lambda i, hf: (i, 0))
    return pl.pallas_call(
        body, grid=(T // tm, 2), in_specs=[row, row, pl.BlockSpec((tm, D), lambda i, hf: (i, 1))],
        out_specs=pl.BlockSpec((tm, D), lambda i, hf: (i, hf)),
        out_shape=jax.ShapeDtypeStruct((T, 2 * D), BF16), name=name,
        compiler_params=_params(("parallel", "arbitrary"), VMEM_MID))(d, pg, pg)


def _shift_down(x, halo, s):
    r = pltpu.roll(x, s, axis=0)
    hr = pltpu.roll(halo, s, axis=0)
    row = lax.broadcasted_iota(jnp.int32, halo.shape, 0)
    head = jnp.where(row < s, hr, r[:SUBLANES])
    return jnp.concatenate([head, r[SUBLANES:]], axis=0)


def _shift_up(x, halo, s):
    n = x.shape[0]
    r = pltpu.roll(x, n - s, axis=0)
    hr = pltpu.roll(halo, SUBLANES - s, axis=0)
    row = lax.broadcasted_iota(jnp.int32, halo.shape, 0)
    tail = jnp.where(row >= SUBLANES - s, hr, r[n - SUBLANES:])
    return jnp.concatenate([r[:n - SUBLANES], tail], axis=0)


def _conv_acc(z, zh, w, b, first):
    kw = w.shape[0]
    zh = jnp.where(first, 0.0, zh)
    acc = b + w[kw - 1:kw] * z
    shifted = []
    for k in range(kw - 1):
        zs = _shift_down(z, zh, kw - 1 - k)
        shifted.append(zs)
        acc = acc + w[k:k + 1] * zs
    return acc, shifted


def _conv_specs(T, F, tm, tc, KW):
    nfb = F // tc
    rb = tm // SUBLANES

    def main(off):
        return pl.BlockSpec((tm, tc), lambda i, c: (i, c + off))

    def halo(off):
        return pl.BlockSpec((SUBLANES, tc), lambda i, c: (jnp.maximum(i * rb - 1, 0), c + off))

    def wspec(off):
        return pl.BlockSpec((None, KW, tc), lambda i, c: (c + off, 0, 0))

    def bspec(off):
        return pl.BlockSpec((1, tc), lambda i, c: (0, c + off))

    return nfb, main, halo, wspec, bspec


def convglu_fwd(z, cw, cb, *, name):
    T, F2 = z.shape
    F = F2 // 2
    _, KW, tc = cw.shape
    tm = _pick(T, (256, 128))
    nfb, main, halo, wspec, bspec = _conv_specs(T, F, tm, tc, KW)

    def body(zg, zgh, zv, zvh, wg, wv, bg, bv, o_ref):
        first = pl.program_id(0) == 0
        g, _ = _conv_acc(zg[...], zgh[...], wg[...], bg[...], first)
        v, _ = _conv_acc(zv[...], zvh[...], wv[...], bv[...], first)
        o_ref[...] = (g * _sigmoid(g) * v).astype(BF16)

    return pl.pallas_call(
        body, grid=(T // tm, nfb),
        in_specs=[main(0), halo(0), main(nfb), halo(nfb), wspec(0), wspec(nfb), bspec(0), bspec(nfb)],
        out_specs=pl.BlockSpec((tm, tc), lambda i, c: (i, c)),
        out_shape=jax.ShapeDtypeStruct((T, F), BF16), name=name,
        compiler_params=_params(("parallel", "parallel"), VMEM_BIG))(z, z, z, z, cw, cw, cb, cb)


def convglu_bwd_acc(z, da, cw, cb, *, name):
    T, F2 = z.shape
    F = F2 // 2
    _, KW, tc = cw.shape
    tm = _pick(T, (256, 128))
    nfb, main, halo, wspec, bspec = _conv_specs(T, F, tm, tc, KW)

    def body(zg, zgh, zv, zvh, wg, wv, bg, bv, da_ref, o_ref, dwg, dwv, dbg, dbv):
        i = pl.program_id(1)
        first = i == 0

        @pl.when(first)
        def _():
            for o in (dwg, dwv, dbg, dbv):
                o[...] = jnp.zeros_like(o)

        zg_v, zv_v = zg[...], zv[...]
        g, sg_ = _conv_acc(zg_v, zgh[...], wg[...], bg[...], first)
        v, sv_ = _conv_acc(zv_v, zvh[...], wv[...], bv[...], first)
        d = da_ref[...]
        sig = _sigmoid(g)
        dg = d * v * sig * (1.0 + g * (1.0 - sig))
        dv = d * g * sig
        o_ref[0] = dg.astype(BF16)
        o_ref[1] = dv.astype(BF16)
        dbg[...] += jnp.sum(dg, axis=0, keepdims=True)
        dbv[...] += jnp.sum(dv, axis=0, keepdims=True)
        for k in range(KW):
            xg = zg_v if k == KW - 1 else sg_[k]
            xv = zv_v if k == KW - 1 else sv_[k]
            dwg[pl.ds(k, 1), :] += jnp.sum(dg * xg, axis=0, keepdims=True)
            dwv[pl.ds(k, 1), :] += jnp.sum(dv * xv, axis=0, keepdims=True)

    def sw(spec_fn, off):
        s = spec_fn(off)
        return pl.BlockSpec(s.block_shape, lambda c, i, f=s.index_map: f(i, c))

    both = jax.ShapeDtypeStruct((2, T, F), BF16)
    dwsh = jax.ShapeDtypeStruct((nfb, KW, tc), F32)
    dbsh = jax.ShapeDtypeStruct((1, F), F32)
    outs = pl.pallas_call(
        body, grid=(nfb, T // tm),
        in_specs=[sw(main, 0), sw(halo, 0), sw(main, nfb), sw(halo, nfb), sw(wspec, 0), sw(wspec, nfb),
                  sw(bspec, 0), sw(bspec, nfb), pl.BlockSpec((tm, tc), lambda c, i: (i, c))],
        out_specs=[pl.BlockSpec((2, tm, tc), lambda c, i: (0, i, c)),
                   pl.BlockSpec((None, KW, tc), lambda c, i: (c, 0, 0)), pl.BlockSpec((None, KW, tc), lambda c, i: (c, 0, 0)),
                   pl.BlockSpec((1, tc), lambda c, i: (0, c)), pl.BlockSpec((1, tc), lambda c, i: (0, c))],
        out_shape=[both, dwsh, dwsh, dbsh, dbsh], name=name,
        compiler_params=_params(("parallel", "arbitrary"), VMEM_BIG))(z, z, z, z, cw, cw, cb, cb, da)
    return outs


def conv_bwd_in(dacc, cw, *, name):
    _, T, F = dacc.shape
    _, KW, tc = cw.shape
    nfb = F // tc
    tm = _pick(T, (256, 128))
    rb = tm // (2 * SUBLANES)
    last_blk = T // (2 * SUBLANES) - 1

    def body(d_ref, dn_ref, w_ref, o_ref):
        last = pl.program_id(0) == pl.num_programs(0) - 1
        d = d_ref[...].astype(F32)
        dn = jnp.where(last, 0.0, dn_ref[...].astype(F32)[:SUBLANES])
        w = w_ref[...]
        out = w[KW - 1:KW] * d
        for k in range(KW - 1):
            out = out + w[k:k + 1] * _shift_up(d, dn, KW - 1 - k)
        o_ref[...] = out.astype(BF16)

    return pl.pallas_call(
        body, grid=(T // tm, 2, nfb),
        in_specs=[pl.BlockSpec((None, tm, tc), lambda i, hf, c: (hf, i, c)),
                  pl.BlockSpec((None, 2 * SUBLANES, tc), lambda i, hf, c: (hf, jnp.minimum((i + 1) * rb, last_blk), c)),
                  pl.BlockSpec((None, KW, tc), lambda i, hf, c: (hf * nfb + c, 0, 0))],
        out_specs=pl.BlockSpec((tm, tc), lambda i, hf, c: (i, hf * nfb + c)),
        out_shape=jax.ShapeDtypeStruct((T, 2 * F), BF16), name=name,
        compiler_params=_params(("parallel", "parallel", "parallel"), VMEM_BIG))(dacc, dacc, cw)


def _my_place():
    x, y, c = (lax.axis_index(a) for a in AXES)
    return x, y, c, 4 * x + 2 * y + c


def _peer(m, x, y, c):
    px = 1 - x if (m >> 2) & 1 else x
    py = 1 - y if (m >> 1) & 1 else y
    pc = 1 - c if m & 1 else c
    return (px, py, pc), 4 * px + 2 * py + pc


def _exchange(ins, out_shapes, plan, *, bcast, name):
    n_in, n_out, n = len(ins), len(out_shapes), len(plan)

    def body(*refs):
        in_refs, out_refs = refs[:n_in], refs[n_in:n_in + n_out]
        send_sems, recv_sems, loc_sems = refs[n_in + n_out:]
        x, y, c, me = _my_place()

        def src(f, who):
            r = in_refs[plan[f][0]]
            return r if bcast else r.at[who]

        def dst(f, who):
            r = out_refs[plan[f][1]]
            lay = plan[f][2]
            return r.at[who] if lay is None else r.at[lay, who]

        def remote(f, m, landing):
            dev, plin = _peer(m, x, y, c)
            return pltpu.make_async_remote_copy(
                src_ref=src(f, plin), dst_ref=dst(f, plin if landing else me), send_sem=send_sems.at[f, m - 1],
                recv_sem=recv_sems.at[f, m - 1], device_id=dev, device_id_type=MESH)

        locs = [pltpu.make_async_copy(src(f, me), dst(f, me), loc_sems.at[f]) for f in range(n)]
        for cp in locs:
            cp.start()
        for m in range(1, N_DEV):
            for f in range(n):
                remote(f, m, False).start()
        for m in range(1, N_DEV):
            for f in range(n):
                remote(f, m, True).wait()
        for cp in locs:
            cp.wait()

    hbm = pl.BlockSpec(memory_space=pltpu.HBM)
    return pl.pallas_call(
        body, in_specs=[hbm] * n_in, out_specs=[hbm] * n_out, out_shape=out_shapes,
        scratch_shapes=[pltpu.SemaphoreType.DMA((n, N_DEV - 1)), pltpu.SemaphoreType.DMA((n, N_DEV - 1)),
                        pltpu.SemaphoreType.DMA((n,))],
        name=name)(*ins)


def all_gather(shards, *, name):
    outs = [jax.ShapeDtypeStruct((N_DEV,) + s.shape, s.dtype) for s in shards]
    return _exchange(shards, outs, [(a, a, None) for a in range(len(shards))], bcast=True, name=name)


def exchange_layers(pieces, *, name):
    p0 = pieces[0]
    out = jax.ShapeDtypeStruct((len(pieces),) + p0.shape, p0.dtype)
    return _exchange(pieces, [out], [(l, 0, l) for l in range(len(pieces))], bcast=False, name=name)[0]


_HBM = pl.BlockSpec(memory_space=pltpu.HBM)
_SEM = pl.BlockSpec(memory_space=pltpu.SEMAPHORE)
_EFFECT = pltpu.SideEffectType.DATAFLOW_SIDE_EFFECTING


def _split_copy(in_refs, land_refs, send_sems, recv_sems, bcast, f, m, place, landing):
    x, y, c, me = place
    dev, plin = _peer(m, x, y, c)
    src = in_refs[f] if bcast else in_refs[f].at[plin]
    return pltpu.make_async_remote_copy(
        src_ref=src, dst_ref=land_refs[f].at[plin if landing else me],
        send_sem=send_sems.at[f * (N_DEV - 1) + m - 1], recv_sem=recv_sems.at[f * (N_DEV - 1) + m - 1],
        device_id=dev, device_id_type=MESH)


def exchange_start(ins, *, bcast, dep=None, name):
    n = len(ins)
    lands = [lax.empty(((N_DEV,) + a.shape) if bcast else a.shape, a.dtype) for a in ins]
    deps = [] if dep is None else [dep]
    nd = len(deps)

    def body(*refs):
        in_refs, land_refs = refs[:n], refs[n:2 * n]
        send_sems, recv_sems, token = refs[2 * n + nd], refs[2 * n + nd + 1], refs[-1]
        place = _my_place()
        me = place[3]
        for f in range(n):
            pltpu.sync_copy(in_refs[f] if bcast else in_refs[f].at[me], land_refs[f].at[me])
        for m in range(1, N_DEV):
            for f in range(n):
                _split_copy(in_refs, land_refs, send_sems, recv_sems, bcast, f, m, place, False).start()
        token[...] = jnp.zeros_like(token)

    arrs = [pltpu.with_memory_space_constraint(a, pltpu.HBM) for a in (*ins, *lands)]
    sems = pltpu.SemaphoreType.DMA((n * (N_DEV - 1),))
    outs = pl.pallas_call(
        body, name=name,
        out_shape=(sems, sems, *[pltpu.HBM(a.shape, a.dtype) for a in arrs],
                   jax.ShapeDtypeStruct((SUBLANES, LANES), F32)),
        in_specs=[_HBM] * (2 * n) + [_DEP] * nd,
        out_specs=(_SEM, _SEM, *[_HBM] * (2 * n), pl.BlockSpec(memory_space=pltpu.VMEM)),
        input_output_aliases={i: 2 + i for i in range(2 * n)},
        compiler_params=pltpu.CompilerParams(has_side_effects=_EFFECT))(*arrs, *deps)
    return outs[0], outs[1], list(outs[2:2 + 2 * n]), outs[-1]


def exchange_wait(started, after, *, bcast, name):
    send_sems, recv_sems, thrus, _ = started
    n = len(thrus) // 2

    def body(*refs):
        in_refs, land_refs = refs[:n], refs[n:2 * n]
        send, recv = refs[2 * n], refs[2 * n + 1]
        place = _my_place()
        for m in range(1, N_DEV):
            for f in range(n):
                cp = _split_copy(in_refs, land_refs, send, recv, bcast, f, m, place, True)
                cp.wait_send()
                cp.wait_recv()

    outs = pl.pallas_call(
        body, name=name, out_shape=[pltpu.HBM(a.shape, a.dtype) for a in thrus],
        in_specs=[_HBM] * (2 * n) + [_SEM, _SEM, pl.BlockSpec(memory_space=pl.ANY)], out_specs=[_HBM] * (2 * n),
        input_output_aliases={i: i for i in range(2 * n)},
        compiler_params=pltpu.CompilerParams(has_side_effects=_EFFECT))(*thrus, send_sems, recv_sems, after)
    return list(outs[n:])


def _adamw(w, g, m, v):
    m = ADAM_B1 * m + (1.0 - ADAM_B1) * g
    v = ADAM_B2 * v + (1.0 - ADAM_B2) * (g * g)
    m_hat = m / (1.0 - ADAM_B1 ** ADAM_STEP)
    v_hat = v / (1.0 - ADAM_B2 ** ADAM_STEP)
    delta = -ADAM_LR * (m_hat / (jnp.sqrt(v_hat) + ADAM_EPS) + ADAM_WD * w)
    return delta, m, v


def adam_reduce(recv, w, m, v, l, prev, *, name):
    _, R, C = recv.shape
    L = w.shape[0]
    budget = 4 * 1024 * 1024
    tr = R
    for cand in (1024, 512, 352, 256, 176, 128, 64, 32, 16):
        if R % cand == 0 and N_DEV * cand * C * recv.dtype.itemsize <= budget:
            tr = cand
            break

    def body(r_ref, w_ref, m_ref, v_ref, *rest):
        g_ref, d_ref, nm_ref, nv_ref = rest[-4:]
        g = r_ref[0].astype(F32)
        for s in range(1, N_DEV):
            g = g + r_ref[s].astype(F32)
        d, nm, nv = _adamw(w_ref[...], g, m_ref[...], v_ref[...])
        g_ref[...] = g
        d_ref[...] = d
        nm_ref[...] = nm
        nv_ref[...] = nv

    blk = pl.BlockSpec((None, tr, C), lambda r: (l, r, 0))
    sh = jax.ShapeDtypeStruct((L, R, C), F32)
    extra = [] if prev is None else list(prev)
    return pl.pallas_call(
        body, grid=(R // tr,),
        in_specs=[pl.BlockSpec((N_DEV, tr, C), lambda r: (0, r, 0)), blk, blk, blk]
        + [pl.BlockSpec(memory_space=pl.ANY)] * len(extra),
        out_specs=[blk] * 4, out_shape=[sh] * 4, name=name,
        input_output_aliases={4 + i: i for i in range(len(extra))},
        compiler_params=_params(("parallel",), VMEM_MID))(recv, w, m, v, *extra)


def sum_slots(recv, *, name):
    _, R, C = recv.shape
    tr = _pick(R, (512, 256, 128, 64, 32, 16, 8))

    def body(r_ref, o_ref):
        g = r_ref[0]
        for s in range(1, N_DEV):
            g = g + r_ref[s]
        o_ref[...] = g

    return pl.pallas_call(
        body, grid=(R // tr,), in_specs=[pl.BlockSpec((N_DEV, tr, C), lambda r: (0, r, 0))],
        out_specs=pl.BlockSpec((tr, C), lambda r: (r, 0)), out_shape=jax.ShapeDtypeStruct((R, C), F32),
        name=name, compiler_params=_params(("parallel",)))(recv)


def adam_flat(g, w, m, v, *, name):
    R, C = g.shape
    tr = _pick(R, (512, 256, 128, 64, 32, 16, 8))

    def body(g_ref, w_ref, m_ref, v_ref, d_ref, nm_ref, nv_ref):
        d, nm, nv = _adamw(w_ref[...], g_ref[...], m_ref[...], v_ref[...])
        d_ref[...] = d
        nm_ref[...] = nm
        nv_ref[...] = nv

    blk = pl.BlockSpec((tr, C), lambda r: (r, 0))
    sh = jax.ShapeDtypeStruct((R, C), F32)
    return pl.pallas_call(body, grid=(R // tr,), in_specs=[blk] * 4, out_specs=[blk] * 3, out_shape=[sh] * 3,
                          name=name, compiler_params=_params(("parallel",)))(g, w, m, v)


WEIGHTS = ("norm_mix_g", "norm_ffn_g", "a_w_in", "a_g_v", "a_w_s", "a_b_s", "a_w_out", "b_w_in", "b_a_re", "b_a_im",
           "b_log_dt", "b_b_re", "b_b_im", "b_c_re", "b_c_im", "b_d", "b_w_glu", "f_w_up", "f_conv_w", "f_conv_b",
           "f_w_down", "final_g")
BIG = ("a_w_in", "a_w_out", "b_w_in", "b_w_glu", "f_w_up", "f_w_down")
FLAT_CHUNK = N_DEV * SUBLANES * LANES


def _expand(blocks, eye):
    KB, KG, C, P = blocks.shape
    return (blocks[:, :, :, None, :] * eye[None, :, None, :, None]).reshape(KB, KG * C, KG * P)


def _diag_blocks(dense, KG, C, P, eye):
    KB = dense.shape[0]
    return jnp.einsum("kgchp,gh->kgcp", dense.reshape(KB, KG, C, KG, P), eye)


def _flatten_pack(parts):
    flat = jnp.concatenate([p.reshape(-1) for p in parts])
    pad = (-flat.shape[0]) % FLAT_CHUNK
    return jnp.pad(flat, (0, pad))


def _unpack(flat, like):
    out, o = [], 0
    for p in like:
        n = math.prod(p.shape)
        out.append(flat[o:o + n].reshape(p.shape))
        o += n
    return out


def kernel(x, norm_mix_g, norm_ffn_g, a_w_in, a_g_v, a_w_s, a_b_s, a_w_out, b_w_in, b_a_re, b_a_im, b_log_dt, b_b_re, b_b_im, b_c_re, b_c_im, b_d, b_w_glu, f_w_up, f_conv_w, f_conv_b, f_w_down, final_g, loss_target, m_norm_mix_g, m_norm_ffn_g, m_a_w_in, m_a_g_v, m_a_w_s, m_a_b_s, m_a_w_out, m_b_w_in, m_b_a_re, m_b_a_im, m_b_log_dt, m_b_b_re, m_b_b_im, m_b_c_re, m_b_c_im, m_b_d, m_b_w_glu, m_f_w_up, m_f_conv_w, m_f_conv_b, m_f_w_down, m_final_g, v_norm_mix_g, v_norm_ffn_g, v_a_w_in, v_a_g_v, v_a_w_s, v_a_b_s, v_a_w_out, v_b_w_in, v_b_a_re, v_b_a_im, v_b_log_dt, v_b_b_re, v_b_b_im, v_b_c_re, v_b_c_im, v_b_d, v_b_w_glu, v_f_w_up, v_f_conv_w, v_f_conv_b, v_f_w_down, v_final_g):
    env = dict(locals())
    W = {n: env[n] for n in WEIGHTS}
    Mo = {n: env["m_" + n] for n in WEIGHTS}
    Vo = {n: env["v_" + n] for n in WEIGHTS}

    _, T, D = x.shape
    depth = norm_mix_g.shape[0]
    E_A = a_g_v.shape[1]
    H = a_w_s.shape[1]
    G, P, C = b_b_re.shape[1], b_b_re.shape[2], b_b_re.shape[3]
    E_B = G * C
    KG = S5_KG
    KB = G // KG
    F2 = f_conv_b.shape[1]
    tb = _pick(T, (512, 256, 128))
    eye = jnp.eye(KG, dtype=F32)
    _, _, _, me = _my_place()

    wb = {n: W[n].astype(BF16) for n in BIG}

    def shards(i):
        j = i // 2
        if i % 2 == 0:
            return [wb["a_w_in"][j], wb["a_w_out"][j], wb["f_w_up"][i], wb["f_w_down"][i], f_conv_w[i]]
        return [wb["b_w_in"][j], wb["b_w_glu"][j], b_d[j][None], wb["f_w_up"][i], wb["f_w_down"][i], f_conv_w[i]]

    h = x[0]
    started = exchange_start(shards(0), bcast=True, name="gather_start0")
    saved = []
    for i in range(depth):
        j = i // 2
        s = {}
        gathered = exchange_wait(started, h, bcast=True, name=f"gather_wait{i}")
        dep = None
        if i + 1 < depth:
            started = exchange_start(shards(i + 1), bcast=True, dep=gathered[0], name=f"gather_start{i + 1}")
            dep = started[3]
        if i % 2 == 0:
            g_in, g_out, g_up, g_dn, g_cw = gathered
            s["w_in"], s["w_out"] = g_in, g_out.reshape(E_A, D)
        else:
            g_in, g_glu, g_dd, g_up, g_dn, g_cw = gathered
            s["w_in"], s["w_glu"] = g_in.reshape(D, E_B), g_glu
            s["dd"] = g_dd.reshape(KB, 1, KG * C)
        s["w_up"], s["w_dn"], s["cw"] = g_up, g_dn.reshape(F2 // 2, D), g_cw
        s["h"] = h
        s["hn"] = rms_fwd(h, norm_mix_g[i][None], dep=dep, name=f"rms_mix{i}")
        if i % 2 == 0:
            s["p"] = mm_nn(s["hn"], s["w_in"], name=f"a_in{i}")
            s["bexp"] = jnp.repeat(a_b_s[j].T, E_A // H, axis=1)
            s["us"] = sgu_fwd(s["p"], a_g_v[j][None], a_w_s[j], s["bexp"], name=f"sgu_fwd{i}")
            h_mid = mm_nn(s["us"], s["w_out"], res=h, name=f"a_out{i}")
        else:
            s["uu"] = mm_nn(s["hn"], s["w_in"], name=f"b_in{i}")
            s["prm"] = (b_a_re[j], b_a_im[j], b_log_dt[j][:, None],
                        b_b_re[j].transpose(2, 0, 1), b_b_im[j].transpose(2, 0, 1))
            ar, ai, bbr, bbi = s5_disc_fwd(*s["prm"], name=f"s5_disc{i}")
            to_blocks = lambda t: t.reshape(C, KB, KG, P).transpose(1, 2, 0, 3)
            s["bd_r"] = _expand(to_blocks(bbr), eye).astype(BF16)
            s["bd_i"] = _expand(to_blocks(bbi), eye).astype(BF16)
            s["ct_r"] = _expand(b_c_re[j].reshape(KB, KG, C, P), eye).astype(BF16)
            s["ct_i"] = _expand(b_c_im[j].reshape(KB, KG, C, P), eye).astype(BF16)
            s["ab_r"], s["ab_i"] = ar.reshape(KB, 1, KG * P), ai.reshape(KB, 1, KG * P)
            s["y"], s["q"], s["st_r"], s["st_i"] = s5_fwd(
                s["uu"], s["bd_r"], s["bd_i"], s["ct_r"], s["ct_i"], s["ab_r"], s["ab_i"], s["dd"],
                tb=tb, name=f"s5_fwd{i}")
            s["pg"] = mm_nn(s["q"], s["w_glu"], name=f"b_glu{i}")
            h_mid = glu_fwd(h, s["pg"], name=f"glu_fwd{i}")
        s["h_mid"] = h_mid
        s["hn2"] = rms_fwd(h_mid, norm_ffn_g[i][None], name=f"rms_ffn{i}")
        s["z"] = mm_nn(s["hn2"], s["w_up"], name=f"f_up{i}")
        s["a"] = convglu_fwd(s["z"], s["cw"], f_conv_b[i][None], name=f"convglu_fwd{i}")
        h = mm_nn(s["a"], s["w_dn"], res=h_mid, name=f"f_down{i}")
        saved.append(s)

    loss_tile, dh, dg_final = loss_head(h, final_g[None], loss_target[0], name="loss_head")
    loss = lax.psum(loss_tile[0, 0], AXES)

    gbig = {n: [None] * W[n].shape[0] for n in BIG}
    gs = {n: [None] * W[n].shape[0] for n in WEIGHTS if n not in BIG and n != "final_g"}
    pending = []
    for i in reversed(range(depth)):
        j = i // 2
        s = saved[i]
        dhb = dh.astype(BF16)
        gbig["f_w_down"][i] = mm_tn(s["a"], dhb, blocks=1, name=f"g_down{i}").reshape(N_DEV, F2 // 2 // N_DEV, D)
        da = mm_nt(dhb, s["w_dn"], name=f"d_a{i}")
        dacc, dwg, dwv, dbg, dbv = convglu_bwd_acc(s["z"], da, s["cw"], f_conv_b[i][None], name=f"convglu_bwd{i}")
        gs["f_conv_w"][i] = jnp.concatenate([dwg, dwv], axis=0)
        gs["f_conv_b"][i] = jnp.concatenate([dbg, dbv], axis=1)[0]
        dz = conv_bwd_in(dacc, s["cw"], name=f"conv_bwd_in{i}")
        gbig["f_w_up"][i] = mm_tn(s["hn2"], dz, blocks=N_DEV, name=f"g_up{i}")
        dhn2 = mm_nt(dz, s["w_up"], name=f"d_hn2{i}")
        st = exchange_start([gbig["f_w_down"][i], gbig["f_w_up"][i]], bcast=False, name=f"xchg_ffn_start{i}")
        pending.append((st, [("f_w_down", i), ("f_w_up", i)], f"ffn{i}"))
        dh_mid, dg = rms_bwd(s["h_mid"], norm_ffn_g[i][None], dhn2, dh, dep=st[3], name=f"rms_ffn_bwd{i}")
        gs["norm_ffn_g"][i] = dg[0]
        dmb = dh_mid.astype(BF16)
        if i % 2 == 0:
            gbig["a_w_out"][j] = mm_tn(s["us"], dmb, blocks=1, name=f"g_aout{i}").reshape(N_DEV, E_A // N_DEV, D)
            d_us = mm_nt(dmb, s["w_out"], name=f"d_us{i}")
            dp, dws, dbt, dgv = sgu_bwd(s["p"], d_us, a_g_v[j][None], a_w_s[j], s["bexp"], name=f"sgu_bwd{i}")
            gs["a_w_s"][j], gs["a_b_s"][j], gs["a_g_v"][j] = dws, dbt[:, :H].T, dgv[0]
            gbig["a_w_in"][j] = mm_tn(s["hn"], dp, blocks=N_DEV, name=f"g_ain{i}")
            dhn = mm_nt(dp, s["w_in"], name=f"d_hn_a{i}")
        else:
            dpg = glu_bwd(dh_mid, s["pg"], name=f"glu_bwd{i}")
            gbig["b_w_glu"][j] = mm_tn(s["q"], dpg, blocks=N_DEV, name=f"g_glu{i}")
            dq = mm_nt(dpg, s["w_glu"], name=f"d_q{i}")
            duu, dbr, dbi, dcr, dci, dar, dai, ddd = s5_bwd(
                s["uu"], s["y"], dq, s["st_r"], s["st_i"], s["bd_r"], s["bd_i"], s["ct_r"], s["ct_i"],
                s["ab_r"], s["ab_i"], s["dd"], tb=tb, name=f"s5_bwd{i}")
            from_blocks = lambda t: _diag_blocks(t, KG, C, P, eye).transpose(2, 0, 1, 3).reshape(C, G, P)
            d_are, d_aim, d_ldt, d_bre, d_bim = s5_disc_bwd(
                *s["prm"], dar.reshape(G, P), dai.reshape(G, P), from_blocks(dbr), from_blocks(dbi),
                name=f"s5_disc_bwd{i}")
            gs["b_a_re"][j], gs["b_a_im"][j], gs["b_log_dt"][j] = d_are, d_aim, d_ldt[:, 0]
            gs["b_b_re"][j], gs["b_b_im"][j] = d_bre.transpose(1, 2, 0), d_bim.transpose(1, 2, 0)
            gs["b_c_re"][j] = _diag_blocks(dcr, KG, C, P, eye).reshape(G, C, P)
            gs["b_c_im"][j] = _diag_blocks(dci, KG, C, P, eye).reshape(G, C, P)
            gs["b_d"][j] = ddd.reshape(E_B)
            gbig["b_w_in"][j] = mm_tn(s["hn"], duu, blocks=1, name=f"g_bin{i}").reshape(N_DEV, D // N_DEV, E_B)
            dhn = mm_nt(duu, s["w_in"], name=f"d_hn_b{i}")
        mix = ("a_w_out", "a_w_in") if i % 2 == 0 else ("b_w_glu", "b_w_in")
        st = exchange_start([gbig[n][j] for n in mix], bcast=False, name=f"xchg_mix_start{i}")
        pending.append((st, [(n, j) for n in mix], f"mix{i}"))
        dh, dg = rms_bwd(s["h"], norm_mix_g[i][None], dhn, dh_mid, dep=st[3], name=f"rms_mix_bwd{i}")
        gs["norm_mix_g"][i] = dg[0]
    grad_x = dh[None]

    grads, deltas, new_m, new_v = {}, {}, {}, {}
    small = [n for n in WEIGHTS if n not in BIG]
    full = {n: (dg_final[0] if n == "final_g" else jnp.stack(gs[n])) for n in small}
    flat = _flatten_pack([full[n] for n in small])
    rows = flat.shape[0] // N_DEV // LANES
    recv = exchange_layers([flat.reshape(N_DEV, rows, LANES)], name="xchg_small")[0]
    part = sum_slots(recv, name="sum_small")
    tot = all_gather([part], name="gather_small")[0].reshape(-1)
    red = dict(zip(small, _unpack(tot, [full[n] for n in small])))
    red["f_conv_w"] = lax.dynamic_index_in_dim(red["f_conv_w"], me, axis=1, keepdims=False)
    red["b_d"] = lax.dynamic_slice_in_dim(red["b_d"], me * (E_B // N_DEV), E_B // N_DEV, axis=1)
    gflat = _flatten_pack([red[n] for n in small]).reshape(-1, LANES)
    d_f, m_f, v_f = adam_flat(
        gflat, _flatten_pack([W[n] for n in small]).reshape(-1, LANES),
        _flatten_pack([Mo[n] for n in small]).reshape(-1, LANES),
        _flatten_pack([Vo[n] for n in small]).reshape(-1, LANES), name="adam_small")
    like = [W[n] for n in small]
    for n, d_, m_, v_ in zip(small, _unpack(d_f.reshape(-1), like), _unpack(m_f.reshape(-1), like),
                             _unpack(v_f.reshape(-1), like)):
        grads[n], deltas[n], new_m[n], new_v[n] = red[n], d_, m_, v_

    done = {n: None for n in BIG}
    after = d_f
    for st, items, tag in pending:
        recvs = exchange_wait(st, after, bcast=False, name=f"xchg_wait_{tag}")
        for (n, l), recv in zip(items, recvs):
            done[n] = adam_reduce(recv, W[n], Mo[n], Vo[n], l, done[n], name=f"adam_{n}{l}")
            after = done[n][0]
    for n in BIG:
        grads[n], deltas[n], new_m[n], new_v[n] = done[n]

    return (loss, grad_x, *[grads[n] for n in WEIGHTS], *[deltas[n] for n in WEIGHTS],
            *[new_m[n] for n in WEIGHTS], *[new_v[n] for n in WEIGHTS])
```

```python
import math

import jax
import jax.numpy as jnp
from jax import lax
from jax.experimental import pallas as pl
from jax.experimental.pallas import tpu as pltpu

F32 = jnp.float32
BF16 = jnp.bfloat16
N_DEV = 8
AXES = ("x", "y", "c")
EPS = 1e-6
LANES = 128
SUBLANES = 8
VMEM_BIG = 56 * 1024 * 1024
VMEM_MID = 40 * 1024 * 1024
ADAM_LR, ADAM_B1, ADAM_B2, ADAM_EPS, ADAM_WD, ADAM_STEP = 0.001, 0.9, 0.999, 1e-08, 0.01, 10
MESH = pl.DeviceIdType.MESH
GELU_C = math.sqrt(2.0 / math.pi)
GELU_K = 0.044715


def _pick(n, prefs):
    for p in prefs:
        if n % p == 0:
            return p
    return n


def _params(sem, vmem=None):
    return pltpu.CompilerParams(dimension_semantics=sem, vmem_limit_bytes=vmem)


def _gelu(x):
    return 0.5 * x * (1.0 + jnp.tanh(GELU_C * (x + GELU_K * x * x * x)))


def _gelu_grad(x):
    x2 = x * x
    th = jnp.tanh(GELU_C * x * (1.0 + GELU_K * x2))
    return 0.5 * (1.0 + th) + 0.5 * x * (1.0 - th * th) * GELU_C * (1.0 + 3.0 * GELU_K * x2)


def _sigmoid(x):
    return 1.0 / (1.0 + jnp.exp(-x))


def _dot_nn(a, b):
    return jnp.dot(a, b, preferred_element_type=F32)


def _dot_nt(a, b):
    return lax.dot_general(a, b, (((1,), (1,)), ((), ())), preferred_element_type=F32)


def _dot_tn(a, b):
    return lax.dot_general(a, b, (((0,), (0,)), ((), ())), preferred_element_type=F32)


M_TILES = (1024, 512, 256, 128)
N_TILES = (1408, 1024, 512, 384, 256, 128)
K_TILES = (1408, 1024, 512, 384, 256, 128)


def _as3(b):
    return b if b.ndim == 3 else b[None]


def mm_nn(a, b, *, res=None, out_dtype=F32, name):
    b3 = _as3(b)
    M, K = a.shape
    J, _, nb = b3.shape
    tm, tn, tk = _pick(M, M_TILES), _pick(nb, N_TILES), _pick(K, K_TILES)
    per, nk = nb // tn, K // tk

    def body(*refs):
        if res is None:
            a_ref, b_ref, o_ref, acc = refs
        else:
            a_ref, b_ref, r_ref, o_ref, acc = refs
        k = pl.program_id(2)

        @pl.when(k == 0)
        def _():
            acc[...] = jnp.zeros_like(acc)

        acc[...] += _dot_nn(a_ref[...], b_ref[...])

        @pl.when(k == nk - 1)
        def _():
            r = acc[...]
            if res is not None:
                r = r + r_ref[...]
            o_ref[...] = r.astype(out_dtype)

    in_specs = [pl.BlockSpec((tm, tk), lambda i, n, k: (i, k)),
                pl.BlockSpec((None, tk, tn), lambda i, n, k: (n // per, k, n % per))]
    args = [a, b3]
    if res is not None:
        in_specs.append(pl.BlockSpec((tm, tn), lambda i, n, k: (i, n)))
        args.append(res)
    return pl.pallas_call(
        body, grid=(M // tm, J * per, nk), in_specs=in_specs,
        out_specs=pl.BlockSpec((tm, tn), lambda i, n, k: (i, n)),
        out_shape=jax.ShapeDtypeStruct((M, J * nb), out_dtype),
        scratch_shapes=[pltpu.VMEM((tm, tn), F32)], name=name,
        compiler_params=_params(("parallel", "parallel", "arbitrary"), VMEM_BIG))(*args)


def mm_nt(dy, b, *, out_dtype=F32, name):
    b3 = _as3(b)
    M, N = dy.shape
    J, K, nb = b3.shape
    tm, tn, tk = _pick(M, M_TILES), _pick(nb, N_TILES), _pick(K, K_TILES)
    per, nn = nb // tn, (J * nb) // tn

    def body(d_ref, b_ref, o_ref, acc):
        n = pl.program_id(2)

        @pl.when(n == 0)
        def _():
            acc[...] = jnp.zeros_like(acc)

        acc[...] += _dot_nt(d_ref[...], b_ref[...])

        @pl.when(n == nn - 1)
        def _():
            o_ref[...] = acc[...].astype(out_dtype)

    return pl.pallas_call(
        body, grid=(M // tm, K // tk, nn),
        in_specs=[pl.BlockSpec((tm, tn), lambda i, k, n: (i, n)),
                  pl.BlockSpec((None, tk, tn), lambda i, k, n: (n // per, k, n % per))],
        out_specs=pl.BlockSpec((tm, tk), lambda i, k, n: (i, k)),
        out_shape=jax.ShapeDtypeStruct((M, K), out_dtype),
        scratch_shapes=[pltpu.VMEM((tm, tk), F32)], name=name,
        compiler_params=_params(("parallel", "parallel", "arbitrary"), VMEM_BIG))(dy, b3)


def mm_tn(x, dy, *, blocks, out_dtype=BF16, name):
    M, K = x.shape
    _, N = dy.shape
    nb = N // blocks
    tm, tn, tk = _pick(M, M_TILES), _pick(nb, N_TILES), _pick(K, K_TILES)
    per, nm = nb // tn, M // tm

    def body(x_ref, d_ref, o_ref, acc):
        m = pl.program_id(2)

        @pl.when(m == 0)
        def _():
            acc[...] = jnp.zeros_like(acc)

        acc[...] += _dot_tn(x_ref[...], d_ref[...])

        @pl.when(m == nm - 1)
        def _():
            o_ref[...] = acc[...].astype(out_dtype)

    return pl.pallas_call(
        body, grid=(K // tk, N // tn, nm),
        in_specs=[pl.BlockSpec((tm, tk), lambda k, n, m: (m, k)),
                  pl.BlockSpec((tm, tn), lambda k, n, m: (m, n))],
        out_specs=pl.BlockSpec((None, tk, tn), lambda k, n, m: (n // per, k, n % per)),
        out_shape=jax.ShapeDtypeStruct((blocks, K, nb), out_dtype),
        scratch_shapes=[pltpu.VMEM((tk, tn), F32)], name=name,
        compiler_params=_params(("parallel", "parallel", "arbitrary"), VMEM_BIG))(x, dy)


ROW_TILES = (256, 128)


_DEP = pl.BlockSpec(memory_space=pl.ANY)


def rms_fwd(h, g, *, dep=None, name):
    T, D = h.shape
    tm = _pick(T, ROW_TILES)
    deps = [] if dep is None else [dep]

    def body(h_ref, g_ref, *rest):
        o_ref = rest[-1]
        x = h_ref[...]
        r = lax.rsqrt(jnp.mean(x * x, axis=-1, keepdims=True) + EPS)
        o_ref[...] = (x * r * g_ref[...]).astype(BF16)

    return pl.pallas_call(
        body, grid=(T // tm,),
        in_specs=[pl.BlockSpec((tm, D), lambda i: (i, 0)), pl.BlockSpec((1, D), lambda i: (0, 0))] + [_DEP] * len(deps),
        out_specs=pl.BlockSpec((tm, D), lambda i: (i, 0)),
        out_shape=jax.ShapeDtypeStruct((T, D), BF16), name=name,
        compiler_params=_params(("parallel",), VMEM_MID))(h, g, *deps)


def rms_bwd(h, g, dhn, dres, *, dep=None, name):
    T, D = h.shape
    tm = _pick(T, ROW_TILES)
    deps = [] if dep is None else [dep]

    def body(h_ref, g_ref, d_ref, r_ref, *rest):
        dh_ref, dg_ref = rest[-2:]
        @pl.when(pl.program_id(0) == 0)
        def _():
            dg_ref[...] = jnp.zeros_like(dg_ref)

        x = h_ref[...]
        r = lax.rsqrt(jnp.mean(x * x, axis=-1, keepdims=True) + EPS)
        xh = x * r
        dy = d_ref[...]
        dxh = dy * g_ref[...]
        dh_ref[...] = r_ref[...] + r * (dxh - xh * jnp.mean(dxh * xh, axis=-1, keepdims=True))
        dg_ref[...] += jnp.sum(dy * xh, axis=0, keepdims=True)

    row = pl.BlockSpec((tm, D), lambda i: (i, 0))
    vec = pl.BlockSpec((1, D), lambda i: (0, 0))
    return pl.pallas_call(
        body, grid=(T // tm,), in_specs=[row, vec, row, row] + [_DEP] * len(deps), out_specs=[row, vec],
        out_shape=[jax.ShapeDtypeStruct((T, D), F32), jax.ShapeDtypeStruct((1, D), F32)], name=name,
        compiler_params=_params(("arbitrary",), VMEM_MID))(h, g, dhn, dres, *deps)


def loss_head(h, g, tgt, *, name):
    T, D = h.shape
    tm = _pick(T, ROW_TILES)

    def body(h_ref, g_ref, t_ref, l_ref, dh_ref, dg_ref):
        @pl.when(pl.program_id(0) == 0)
        def _():
            dg_ref[...] = jnp.zeros_like(dg_ref)
            l_ref[...] = jnp.zeros_like(l_ref)

        x = h_ref[...]
        gg = g_ref[...]
        r = lax.rsqrt(jnp.mean(x * x, axis=-1, keepdims=True) + EPS)
        xh = x * r
        e = xh * gg - t_ref[...]
        l_ref[...] += 0.5 * jnp.sum(jnp.mean(e * e, axis=-1, keepdims=True), axis=0, keepdims=True)
        dy = e * (1.0 / D)
        dxh = dy * gg
        dh_ref[...] = r * (dxh - xh * jnp.mean(dxh * xh, axis=-1, keepdims=True))
        dg_ref[...] += jnp.sum(dy * xh, axis=0, keepdims=True)

    row = pl.BlockSpec((tm, D), lambda i: (i, 0))
    vec = pl.BlockSpec((1, D), lambda i: (0, 0))
    return pl.pallas_call(
        body, grid=(T // tm,), in_specs=[row, vec, row],
        out_specs=[pl.BlockSpec((SUBLANES, LANES), lambda i: (0, 0)), row, vec],
        out_shape=[jax.ShapeDtypeStruct((SUBLANES, LANES), F32), jax.ShapeDtypeStruct((T, D), F32),
                   jax.ShapeDtypeStruct((1, D), F32)], name=name,
        compiler_params=_params(("arbitrary",), VMEM_MID))(h, g, tgt)


def _sgu_common(p, gv, w_ref, bexp, E, H, CH):
    Dg = E // H
    z = _gelu(p)
    u, v = z[:, :E], z[:, E:]
    r = lax.rsqrt(jnp.mean(v * v, axis=-1, keepdims=True) + EPS)
    vhat = v * r
    vn = (vhat * gv).astype(BF16)
    row = lax.broadcasted_iota(jnp.int32, (CH, CH), 0)
    col = lax.broadcasted_iota(jnp.int32, (CH, CH), 1)
    causal = row >= col
    ws = [jnp.where(causal, w_ref[hh], 0.0).astype(BF16) for hh in range(H)]
    s = jnp.concatenate([_dot_nn(ws[hh], vn[:, hh * Dg:(hh + 1) * Dg]) for hh in range(H)], axis=1) + bexp
    return u, r, vhat, vn, causal, ws, s


def sgu_fwd(p, g_v, w_s, bexp, *, name):
    T, E2 = p.shape
    E = E2 // 2
    H, CH, _ = w_s.shape

    def body(p_ref, gv_ref, w_ref, b_ref, o_ref):
        u, _, _, _, _, _, s = _sgu_common(p_ref[...], gv_ref[...], w_ref, b_ref[...], E, H, CH)
        o_ref[...] = (u * s).astype(BF16)

    return pl.pallas_call(
        body, grid=(T // CH,),
        in_specs=[pl.BlockSpec((CH, E2), lambda i: (i, 0)), pl.BlockSpec((1, E), lambda i: (0, 0)),
                  pl.BlockSpec((H, CH, CH), lambda i: (0, 0, 0)), pl.BlockSpec((CH, E), lambda i: (0, 0))],
        out_specs=pl.BlockSpec((CH, E), lambda i: (i, 0)),
        out_shape=jax.ShapeDtypeStruct((T, E), BF16), name=name,
        compiler_params=_params(("parallel",), VMEM_BIG))(p, g_v, w_s, bexp)


def sgu_bwd(p, d_us, g_v, w_s, bexp, *, name):
    T, E2 = p.shape
    E = E2 // 2
    H, CH, _ = w_s.shape
    Dg = E // H

    def body(p_ref, d_ref, gv_ref, w_ref, b_ref, dp_ref, dw_ref, db_ref, dg_ref):
        @pl.when(pl.program_id(0) == 0)
        def _():
            dw_ref[...] = jnp.zeros_like(dw_ref)
            db_ref[...] = jnp.zeros_like(db_ref)
            dg_ref[...] = jnp.zeros_like(dg_ref)

        p = p_ref[...]
        gv = gv_ref[...]
        u, r, vhat, vn, causal, ws, s = _sgu_common(p, gv, w_ref, b_ref[...], E, H, CH)
        d = d_ref[...]
        du = d * s
        ds = d * u
        lane = lax.broadcasted_iota(jnp.int32, (CH, LANES), 1)
        dvn_parts = []
        db = jnp.zeros((CH, LANES), F32)
        for hh in range(H):
            ds_h = ds[:, hh * Dg:(hh + 1) * Dg]
            ds_hb = ds_h.astype(BF16)
            dw_ref[hh] += jnp.where(causal, _dot_nt(ds_hb, vn[:, hh * Dg:(hh + 1) * Dg]), 0.0)
            dvn_parts.append(_dot_tn(ws[hh], ds_hb))
            db = db + jnp.where(lane == hh, jnp.sum(ds_h, axis=1, keepdims=True), 0.0)
        db_ref[...] += db
        dvn = jnp.concatenate(dvn_parts, axis=1)
        dg_ref[...] += jnp.sum(dvn * vhat, axis=0, keepdims=True)
        dvh = dvn * gv
        dv = r * (dvh - vhat * jnp.mean(dvh * vhat, axis=-1, keepdims=True))
        dp_ref[...] = (jnp.concatenate([du, dv], axis=1) * _gelu_grad(p)).astype(BF16)

    return pl.pallas_call(
        body, grid=(T // CH,),
        in_specs=[pl.BlockSpec((CH, E2), lambda i: (i, 0)), pl.BlockSpec((CH, E), lambda i: (i, 0)),
                  pl.BlockSpec((1, E), lambda i: (0, 0)), pl.BlockSpec((H, CH, CH), lambda i: (0, 0, 0)),
                  pl.BlockSpec((CH, E), lambda i: (0, 0))],
        out_specs=[pl.BlockSpec((CH, E2), lambda i: (i, 0)), pl.BlockSpec((H, CH, CH), lambda i: (0, 0, 0)),
                   pl.BlockSpec((CH, LANES), lambda i: (0, 0)), pl.BlockSpec((1, E), lambda i: (0, 0))],
        out_shape=[jax.ShapeDtypeStruct((T, E2), BF16), jax.ShapeDtypeStruct((H, CH, CH), F32),
                   jax.ShapeDtypeStruct((CH, LANES), F32), jax.ShapeDtypeStruct((1, E), F32)], name=name,
        compiler_params=_params(("arbitrary",), VMEM_BIG))(p, d_us, g_v, w_s, bexp)


def _s5_disc(a_re, a_im, log_dt, b_re, b_im):
    dt = jnp.exp(log_dt)
    mag = jnp.exp(dt * a_re)
    ar, ai = mag * jnp.cos(dt * a_im), mag * jnp.sin(dt * a_im)
    den = a_re * a_re + a_im * a_im
    qr = ((ar - 1.0) * a_re + ai * a_im) / den
    qi = (ai * a_re - (ar - 1.0) * a_im) / den
    return ar, ai, qr[None] * b_re - qi[None] * b_im, qr[None] * b_im + qi[None] * b_re


def s5_disc_fwd(a_re, a_im, log_dt, b_re, b_im, *, name):
    G, P = a_re.shape
    C = b_re.shape[0]

    def body(ar_ref, ai_ref, dt_ref, br_ref, bi_ref, o_ar, o_ai, o_br, o_bi):
        ar, ai, br, bi = _s5_disc(ar_ref[...], ai_ref[...], dt_ref[...], br_ref[...], bi_ref[...])
        o_ar[...] = ar
        o_ai[...] = ai
        o_br[...] = br
        o_bi[...] = bi

    gp = jax.ShapeDtypeStruct((G, P), F32)
    cgp = jax.ShapeDtypeStruct((C, G, P), F32)
    return pl.pallas_call(body, out_shape=[gp, gp, cgp, cgp], name=name)(a_re, a_im, log_dt, b_re, b_im)


def s5_disc_bwd(a_re, a_im, log_dt, b_re, b_im, d_ar, d_ai, d_br, d_bi, *, name):
    G, P = a_re.shape
    C = b_re.shape[0]

    def body(ar_ref, ai_ref, dt_ref, br_ref, bi_ref, g0, g1, g2, g3, o0, o1, o2, o3, o4):
        prim = (ar_ref[...], ai_ref[...], dt_ref[...], br_ref[...], bi_ref[...])
        _, vjp = jax.vjp(_s5_disc, *prim)
        outs = vjp((g0[...], g1[...], g2[...], g3[...]))
        for o, v in zip((o0, o1, o2, o3, o4), outs):
            o[...] = v

    gp = jax.ShapeDtypeStruct((G, P), F32)
    cgp = jax.ShapeDtypeStruct((C, G, P), F32)
    return pl.pallas_call(
        body, out_shape=[gp, gp, jax.ShapeDtypeStruct((G, 1), F32), cgp, cgp], name=name,
    )(a_re, a_im, log_dt, b_re, b_im, d_ar, d_ai, d_br, d_bi)


S5_KG = 8


def _planes(x):
    return [x[:, c * LANES:(c + 1) * LANES] for c in range(x.shape[1] // LANES)]


def _store_planes(ref, row0, val):
    for c, p in enumerate(_planes(val)):
        ref[c, pl.ds(row0, val.shape[0]), :] = p


def _load_planes(ref, row0, rows):
    return jnp.concatenate([ref[c, pl.ds(row0, rows), :] for c in range(ref.shape[0])], axis=1)


def _build_powers(a_r, a_i, S, pf_r, pf_i, pr_r=None, pr_i=None):
    ar, ai = _planes(a_r), _planes(a_i)
    NP = len(ar)

    def step(i, carry):
        out = []
        for c in range(NP):
            p_r, p_i = carry[2 * c], carry[2 * c + 1]
            pf_r[c, pl.ds(i, 1), :] = p_r
            pf_i[c, pl.ds(i, 1), :] = p_i
            if pr_r is not None:
                pr_r[c, pl.ds(S - 1 - i, 1), :] = p_r
                pr_i[c, pl.ds(S - 1 - i, 1), :] = -p_i
            out += [ar[c] * p_r - ai[c] * p_i, ar[c] * p_i + ai[c] * p_r]
        return tuple(out)

    init = []
    for c in range(NP):
        init += [ar[c], ai[c]]
    lax.fori_loop(0, S, step, tuple(init))


def _scan_seg(hr, hi, hrow0, tb, a_r, a_i, pw_r, pw_i, h0r, h0i, *, reverse):
    NP = hr.shape[0]
    S = tb // SUBLANES
    if reverse:
        a_i = -a_i
    ar, ai = _planes(a_r), _planes(a_i)

    def step(i, carry):
        j = (S - 1 - i) if reverse else i
        slab = pl.ds(pl.multiple_of(hrow0 + j * SUBLANES, SUBLANES), SUBLANES)
        out = []
        for c in range(NP):
            nr = ar[c] * carry[2 * c] - ai[c] * carry[2 * c + 1] + hr[c, slab, :]
            ni = ar[c] * carry[2 * c + 1] + ai[c] * carry[2 * c] + hi[c, slab, :]
            hr[c, slab, :] = nr
            hi[c, slab, :] = ni
            out += [nr, ni]
        return tuple(out)

    z = jnp.zeros((SUBLANES, LANES), F32)
    loc = lax.fori_loop(0, S, step, (z,) * (2 * NP), unroll=2)
    h0r_p, h0i_p = _planes(h0r), _planes(h0i)
    top = 0 if reverse else S - 1
    order = range(SUBLANES - 1, -1, -1) if reverse else range(SUBLANES)
    out_r, out_i, ent_r, ent_i = [], [], [], []
    for c in range(NP):
        s_r, s_i = pw_r[c, pl.ds(top, 1), :], pw_i[c, pl.ds(top, 1), :]
        c_r, c_i = h0r_p[c], h0i_p[c]
        in_r, in_i = [None] * SUBLANES, [None] * SUBLANES
        for seg in order:
            in_r[seg], in_i[seg] = c_r, c_i
            l_r, l_i = loc[2 * c][seg:seg + 1], loc[2 * c + 1][seg:seg + 1]
            c_r, c_i = s_r * c_r - s_i * c_i + l_r, s_r * c_i + s_i * c_r + l_i
        out_r.append(c_r)
        out_i.append(c_i)
        ent_r.append(jnp.concatenate(in_r, axis=0))
        ent_i.append(jnp.concatenate(in_i, axis=0))

    def fix(j, _):
        slab = pl.ds(pl.multiple_of(hrow0 + j * SUBLANES, SUBLANES), SUBLANES)
        for c in range(NP):
            p_r, p_i = pw_r[c, pl.ds(j, 1), :], pw_i[c, pl.ds(j, 1), :]
            hr[c, slab, :] += p_r * ent_r[c] - p_i * ent_i[c]
            hi[c, slab, :] += p_r * ent_i[c] + p_i * ent_r[c]
        return 0

    lax.fori_loop(0, S, fix, 0, unroll=2)
    return ent_r, ent_i, jnp.concatenate(out_r, axis=1), jnp.concatenate(out_i, axis=1)


def _perm_matrices(tb):
    r = jnp.arange(tb)
    pm = (r[None, :] == ((r % SUBLANES) * (tb // SUBLANES) + r // SUBLANES)[:, None]).astype(BF16)
    return pm, pm.T


def _split2(x):
    hi = x.astype(BF16)
    return hi, (x - hi.astype(F32)).astype(BF16)


def _s5_specs(tb, UC, SC, rev_nb=None):
    tmap = (lambda b: b) if rev_nb is None else (lambda b: rev_nb - 1 - b)
    row = pl.BlockSpec((tb, UC), lambda b, k: (tmap(b), k))
    wsp = pl.BlockSpec((None, UC, SC), lambda b, k: (k, 0, 0))
    vsc = pl.BlockSpec((None, 1, SC), lambda b, k: (k, 0, 0))
    vuc = pl.BlockSpec((None, 1, UC), lambda b, k: (k, 0, 0))
    st = pl.BlockSpec((None, None, 1, SC), lambda b, k: (tmap(b), k, 0, 0))
    return row, wsp, vsc, vuc, st


def s5_fwd(uu, pm, pmt, bd_r, bd_i, ct_r, ct_i, ab_r, ab_i, dd, *, tb, name):
    T, E = uu.shape
    KB, UC, SC = bd_r.shape
    NB = T // tb
    NP, S = SC // LANES, tb // SUBLANES

    def body(u_ref, pm_ref, pmt_ref, bdr, bdi, ctr, cti, ar_ref, ai_ref, dd_ref, y_ref, q_ref, sr_ref, si_ref,
             hr, hi, cr, ci, pf_r, pf_i):
        b, k = pl.program_id(0), pl.program_id(1)
        a_r, a_i = ar_ref[...], ai_ref[...]

        @pl.when(b == 0)
        def _():
            cr[k] = jnp.zeros((1, SC), F32)
            ci[k] = jnp.zeros((1, SC), F32)
            _build_powers(a_r, a_i, S, pf_r.at[k], pf_i.at[k])

        h0r, h0i = cr[k], ci[k]
        sr_ref[...] = h0r
        si_ref[...] = h0i
        u = u_ref[...]
        up = _dot_nn(pm_ref[...], u.astype(BF16)).astype(BF16)
        _store_planes(hr, 0, _dot_nn(up, bdr[...]))
        _store_planes(hi, 0, _dot_nn(up, bdi[...]))
        _, _, o_r, o_i = _scan_seg(hr, hi, 0, tb, a_r, a_i, pf_r.at[k], pf_i.at[k], h0r, h0i, reverse=False)
        cr[k] = o_r
        ci[k] = o_i
        ys = (_dot_nt(_load_planes(hr, 0, tb).astype(BF16), ctr[...])
              - _dot_nt(_load_planes(hi, 0, tb).astype(BF16), cti[...]))
        pmt_v = pmt_ref[...]
        y = sum(_dot_nn(pmt_v, part) for part in _split2(ys)) + dd_ref[...] * u
        y_ref[...] = y
        q_ref[...] = _gelu(y).astype(BF16)

    row, wsp, vsc, vuc, st = _s5_specs(tb, UC, SC)
    psp = pl.BlockSpec((tb, tb), lambda b, k: (0, 0))
    stsh = jax.ShapeDtypeStruct((NB, KB, 1, SC), F32)
    pw = pltpu.VMEM((KB, NP, S, LANES), F32)
    pln = pltpu.VMEM((NP, tb, LANES), F32)
    return pl.pallas_call(
        body, grid=(NB, KB), in_specs=[row, psp, psp, wsp, wsp, wsp, wsp, vsc, vsc, vuc],
        out_specs=[row, row, st, st],
        out_shape=[jax.ShapeDtypeStruct((T, E), F32), jax.ShapeDtypeStruct((T, E), BF16), stsh, stsh],
        scratch_shapes=[pln, pln, pltpu.VMEM((KB, 1, SC), F32), pltpu.VMEM((KB, 1, SC), F32), pw, pw],
        name=name, compiler_params=_params(("arbitrary", "arbitrary"), VMEM_MID),
    )(uu, pm, pmt, bd_r, bd_i, ct_r, ct_i, ab_r, ab_i, dd)


def s5_bwd(uu, y, dq, pm, pmt, st_r, st_i, bd_r, bd_i, ct_r, ct_i, ab_r, ab_i, dd, *, tb, name):
    T, E = uu.shape
    KB, UC, SC = bd_r.shape
    NB = T // tb
    HDR = SUBLANES
    NP, S = SC // LANES, tb // SUBLANES
    pw = pltpu.VMEM((KB, NP, S, LANES), F32)
    pln = pltpu.VMEM((NP, tb, LANES), F32)

    def body(u_ref, y_ref, dq_ref, pm_ref, pmt_ref, sr_ref, si_ref, bdr, bdi, ctr, cti, ar_ref, ai_ref, dd_ref,
             du_ref, obr, obi, ocr, oci, odar, odai, oddd,
             hr, hi, gr, gi, kr, ki, abr, abi, acr, aci, pf_r, pf_i, pr_r, pr_i):
        b, k = pl.program_id(0), pl.program_id(1)
        a_r, a_i = ar_ref[...], ai_ref[...]

        @pl.when(b == 0)
        def _():
            _build_powers(a_r, a_i, S, pf_r.at[k], pf_i.at[k], pr_r.at[k], pr_i.at[k])
            z1 = jnp.zeros((1, SC), F32)
            kr[k] = z1
            ki[k] = z1
            odar[k] = z1
            odai[k] = z1
            oddd[k] = jnp.zeros((1, UC), F32)
            zw = jnp.zeros((UC, SC), F32)
            abr[k] = zw
            abi[k] = zw
            acr[k] = zw
            aci[k] = zw

        u = u_ref[...]
        dy = dq_ref[...] * _gelu_grad(y_ref[...])
        oddd[k] += jnp.sum(dy * u, axis=0, keepdims=True)
        pm_v = pm_ref[...]
        ub = _dot_nn(pm_v, u.astype(BF16)).astype(BF16)
        dyb = _dot_nn(pm_v, dy.astype(BF16)).astype(BF16)
        s0r, s0i = sr_ref[...], si_ref[...]
        _store_planes(hr, HDR, _dot_nn(ub, bdr[...]))
        _store_planes(hi, HDR, _dot_nn(ub, bdi[...]))
        e_r, e_i, _, _ = _scan_seg(hr, hi, HDR, tb, a_r, a_i, pf_r.at[k], pf_i.at[k], s0r, s0i, reverse=False)
        for c in range(NP):
            hr[c, pl.ds(0, HDR), :] = e_r[c]
            hi[c, pl.ds(0, HDR), :] = e_i[c]
        _store_planes(gr, 0, _dot_nn(dyb, ctr[...]))
        _store_planes(gi, 0, -_dot_nn(dyb, cti[...]))
        _, _, g0r, g0i = _scan_seg(gr, gi, 0, tb, a_r, a_i, pr_r.at[k], pr_i.at[k], kr[k], ki[k], reverse=True)
        kr[k] = g0r
        ki[k] = g0i

        def slab(j, acc):
            o = pl.multiple_of(j * SUBLANES, SUBLANES)
            out = []
            for c in range(NP):
                p_r, p_i = hr[c, pl.ds(o, SUBLANES), :], hi[c, pl.ds(o, SUBLANES), :]
                g_r, g_i = gr[c, pl.ds(o, SUBLANES), :], gi[c, pl.ds(o, SUBLANES), :]
                out += [acc[2 * c] + g_r * p_r + g_i * p_i, acc[2 * c + 1] + g_i * p_r - g_r * p_i]
            return tuple(out)

        z8 = jnp.zeros((SUBLANES, LANES), F32)
        acc = lax.fori_loop(0, S, slab, (z8,) * (2 * NP), unroll=2)
        odar[k] += jnp.concatenate([jnp.sum(acc[2 * c], axis=0, keepdims=True) for c in range(NP)], axis=1)
        odai[k] += jnp.concatenate([jnp.sum(acc[2 * c + 1], axis=0, keepdims=True) for c in range(NP)], axis=1)
        g_rb = _load_planes(gr, 0, tb).astype(BF16)
        g_ib = _load_planes(gi, 0, tb).astype(BF16)
        h_rb = _load_planes(hr, HDR, tb).astype(BF16)
        h_ib = _load_planes(hi, HDR, tb).astype(BF16)
        dus = _dot_nt(g_rb, bdr[...]) + _dot_nt(g_ib, bdi[...])
        pmt_v = pmt_ref[...]
        du = sum(_dot_nn(pmt_v, part) for part in _split2(dus)) + dd_ref[...] * dy
        du_ref[...] = du.astype(BF16)
        abr[k] += _dot_tn(ub, g_rb)
        abi[k] += _dot_tn(ub, g_ib)
        acr[k] += _dot_tn(dyb, h_rb)
        aci[k] -= _dot_tn(dyb, h_ib)

        @pl.when(jnp.logical_and(b == NB - 1, k == KB - 1))
        def _():
            pltpu.sync_copy(abr, obr)
            pltpu.sync_copy(abi, obi)
            pltpu.sync_copy(acr, ocr)
            pltpu.sync_copy(aci, oci)

    row, wsp, vsc, vuc, st = _s5_specs(tb, UC, SC, rev_nb=NB)
    psp = pl.BlockSpec((tb, tb), lambda b, k: (0, 0))
    hbm = pl.BlockSpec(memory_space=pltpu.HBM)
    full_sc = pl.BlockSpec((KB, 1, SC), lambda b, k: (0, 0, 0))
    full_uc = pl.BlockSpec((KB, 1, UC), lambda b, k: (0, 0, 0))
    wsh = jax.ShapeDtypeStruct((KB, UC, SC), F32)
    acc = pltpu.VMEM((KB, UC, SC), F32)
    return pl.pallas_call(
        body, grid=(NB, KB),
        in_specs=[row, row, row, psp, psp, st, st, wsp, wsp, wsp, wsp, vsc, vsc, vuc],
        out_specs=[row, hbm, hbm, hbm, hbm, full_sc, full_sc, full_uc],
        out_shape=[jax.ShapeDtypeStruct((T, E), BF16), wsh, wsh, wsh, wsh,
                   jax.ShapeDtypeStruct((KB, 1, SC), F32), jax.ShapeDtypeStruct((KB, 1, SC), F32),
                   jax.ShapeDtypeStruct((KB, 1, UC), F32)],
        scratch_shapes=[pltpu.VMEM((NP, tb + HDR, LANES), F32), pltpu.VMEM((NP, tb + HDR, LANES), F32), pln, pln,
                        pltpu.VMEM((KB, 1, SC), F32), pltpu.VMEM((KB, 1, SC), F32), acc, acc, acc, acc,
                        pw, pw, pw, pw],
        name=name, compiler_params=_params(("arbitrary", "arbitrary"), VMEM_BIG),
    )(uu, y, dq, pm, pmt, st_r, st_i, bd_r, bd_i, ct_r, ct_i, ab_r, ab_i, dd)


def glu_fwd(h, pg, *, name):
    T, D = h.shape
    tm = _pick(T, ROW_TILES)

    def body(h_ref, a_ref, b_ref, o_ref):
        o_ref[...] = h_ref[...] + a_ref[...] * _sigmoid(b_ref[...])

    row = pl.BlockSpec((tm, D), lambda i: (i, 0))
    return pl.pallas_call(
        body, grid=(T // tm,), in_specs=[row, row, pl.BlockSpec((tm, D), lambda i: (i, 1))], out_specs=row,
        out_shape=jax.ShapeDtypeStruct((T, D), F32), name=name,
        compiler_params=_params(("parallel",), VMEM_MID))(h, pg, pg)


def glu_bwd(d, pg, *, name):
    T, D = d.shape
    tm = _pick(T, ROW_TILES)

    def body(d_ref, a_ref, b_ref, o_ref):
        dv = d_ref[...]
        sg = _sigmoid(b_ref[...])
        da = dv * sg
        db = dv * a_ref[...] * sg * (1.0 - sg)
        o_ref[...] = jnp.where(pl.program_id(1) == 0, da, db).astype(BF16)

    row = pl.BlockSpec((tm, D), lambda i, hf: (i, 0))
    return pl.pallas_call(
        body, grid=(T // tm, 2), in_specs=[row, row, pl.BlockSpec((tm, D), lambda i, hf: (i, 1))],
        out_specs=pl.BlockSpec((tm, D), lambda i, hf: (i, hf)),
        out_shape=jax.ShapeDtypeStruct((T, 2 * D), BF16), name=name,
        compiler_params=_params(("parallel", "arbitrary"), VMEM_MID))(d, pg, pg)


def _shift_down(x, halo, s):
    r = pltpu.roll(x, s, axis=0)
    hr = pltpu.roll(halo, s, axis=0)
    row = lax.broadcasted_iota(jnp.int32, halo.shape, 0)
    head = jnp.where(row < s, hr, r[:SUBLANES])
    return jnp.concatenate([head, r[SUBLANES:]], axis=0)


def _shift_up(x, halo, s):
    n = x.shape[0]
    r = pltpu.roll(x, n - s, axis=0)
    hr = pltpu.roll(halo, SUBLANES - s, axis=0)
    row = lax.broadcasted_iota(jnp.int32, halo.shape, 0)
    tail = jnp.where(row >= SUBLANES - s, hr, r[n - SUBLANES:])
    return jnp.concatenate([r[:n - SUBLANES], tail], axis=0)


def _conv_acc(z, zh, w, b, first):
    kw = w.shape[0]
    zh = jnp.where(first, 0.0, zh)
    acc = b + w[kw - 1:kw] * z
    shifted = []
    for k in range(kw - 1):
        zs = _shift_down(z, zh, kw - 1 - k)
        shifted.append(zs)
        acc = acc + w[k:k + 1] * zs
    return acc, shifted


def _conv_specs(T, F, tm, tc, KW):
    nfb = F // tc
    rb = tm // SUBLANES

    def main(off):
        return pl.BlockSpec((tm, tc), lambda i, c: (i, c + off))

    def halo(off):
        return pl.BlockSpec((SUBLANES, tc), lambda i, c: (jnp.maximum(i * rb - 1, 0), c + off))

    def wspec(off):
        return pl.BlockSpec((None, KW, tc), lambda i, c: (c + off, 0, 0))

    def bspec(off):
        return pl.BlockSpec((1, tc), lambda i, c: (0, c + off))

    return nfb, main, halo, wspec, bspec


def convglu_fwd(z, cw, cb, *, name):
    T, F2 = z.shape
    F = F2 // 2
    _, KW, tc = cw.shape
    tm = _pick(T, (256, 128))
    nfb, main, halo, wspec, bspec = _conv_specs(T, F, tm, tc, KW)

    def body(zg, zgh, zv, zvh, wg, wv, bg, bv, o_ref):
        first = pl.program_id(0) == 0
        g, _ = _conv_acc(zg[...], zgh[...], wg[...], bg[...], first)
        v, _ = _conv_acc(zv[...], zvh[...], wv[...], bv[...], first)
        o_ref[...] = (g * _sigmoid(g) * v).astype(BF16)

    return pl.pallas_call(
        body, grid=(T // tm, nfb),
        in_specs=[main(0), halo(0), main(nfb), halo(nfb), wspec(0), wspec(nfb), bspec(0), bspec(nfb)],
        out_specs=pl.BlockSpec((tm, tc), lambda i, c: (i, c)),
        out_shape=jax.ShapeDtypeStruct((T, F), BF16), name=name,
        compiler_params=_params(("parallel", "parallel"), VMEM_BIG))(z, z, z, z, cw, cw, cb, cb)


def convglu_bwd_acc(z, da, cw, cb, *, name):
    T, F2 = z.shape
    F = F2 // 2
    _, KW, tc = cw.shape
    tm = _pick(T, (256, 128))
    nfb, main, halo, wspec, bspec = _conv_specs(T, F, tm, tc, KW)

    def body(zg, zgh, zv, zvh, wg, wv, bg, bv, da_ref, o_ref, dwg, dwv, dbg, dbv):
        i = pl.program_id(1)
        first = i == 0

        @pl.when(first)
        def _():
            for o in (dwg, dwv, dbg, dbv):
                o[...] = jnp.zeros_like(o)

        zg_v, zv_v = zg[...], zv[...]
        g, sg_ = _conv_acc(zg_v, zgh[...], wg[...], bg[...], first)
        v, sv_ = _conv_acc(zv_v, zvh[...], wv[...], bv[...], first)
        d = da_ref[...]
        sig = _sigmoid(g)
        dg = d * v * sig * (1.0 + g * (1.0 - sig))
        dv = d * g * sig
        o_ref[0] = dg.astype(BF16)
        o_ref[1] = dv.astype(BF16)
        dbg[...] += jnp.sum(dg, axis=0, keepdims=True)
        dbv[...] += jnp.sum(dv, axis=0, keepdims=True)
        for k in range(KW):
            xg = zg_v if k == KW - 1 else sg_[k]
            xv = zv_v if k == KW - 1 else sv_[k]
            dwg[pl.ds(k, 1), :] += jnp.sum(dg * xg, axis=0, keepdims=True)
            dwv[pl.ds(k, 1), :] += jnp.sum(dv * xv, axis=0, keepdims=True)

    def sw(spec_fn, off):
        s = spec_fn(off)
        return pl.BlockSpec(s.block_shape, lambda c, i, f=s.index_map: f(i, c))

    both = jax.ShapeDtypeStruct((2, T, F), BF16)
    dwsh = jax.ShapeDtypeStruct((nfb, KW, tc), F32)
    dbsh = jax.ShapeDtypeStruct((1, F), F32)
    outs = pl.pallas_call(
        body, grid=(nfb, T // tm),
        in_specs=[sw(main, 0), sw(halo, 0), sw(main, nfb), sw(halo, nfb), sw(wspec, 0), sw(wspec, nfb),
                  sw(bspec, 0), sw(bspec, nfb), pl.BlockSpec((tm, tc), lambda c, i: (i, c))],
        out_specs=[pl.BlockSpec((2, tm, tc), lambda c, i: (0, i, c)),
                   pl.BlockSpec((None, KW, tc), lambda c, i: (c, 0, 0)), pl.BlockSpec((None, KW, tc), lambda c, i: (c, 0, 0)),
                   pl.BlockSpec((1, tc), lambda c, i: (0, c)), pl.BlockSpec((1, tc), lambda c, i: (0, c))],
        out_shape=[both, dwsh, dwsh, dbsh, dbsh], name=name,
        compiler_params=_params(("parallel", "arbitrary"), VMEM_BIG))(z, z, z, z, cw, cw, cb, cb, da)
    return outs


def conv_bwd_in(dacc, cw, *, name):
    _, T, F = dacc.shape
    _, KW, tc = cw.shape
    nfb = F // tc
    tm = _pick(T, (256, 128))
    rb = tm // (2 * SUBLANES)
    last_blk = T // (2 * SUBLANES) - 1

    def body(d_ref, dn_ref, w_ref, o_ref):
        last = pl.program_id(0) == pl.num_programs(0) - 1
        d = d_ref[...].astype(F32)
        dn = jnp.where(last, 0.0, dn_ref[...].astype(F32)[:SUBLANES])
        w = w_ref[...]
        out = w[KW - 1:KW] * d
        for k in range(KW - 1):
            out = out + w[k:k + 1] * _shift_up(d, dn, KW - 1 - k)
        o_ref[...] = out.astype(BF16)

    return pl.pallas_call(
        body, grid=(T // tm, 2, nfb),
        in_specs=[pl.BlockSpec((None, tm, tc), lambda i, hf, c: (hf, i, c)),
                  pl.BlockSpec((None, 2 * SUBLANES, tc), lambda i, hf, c: (hf, jnp.minimum((i + 1) * rb, last_blk), c)),
                  pl.BlockSpec((None, KW, tc), lambda i, hf, c: (hf * nfb + c, 0, 0))],
        out_specs=pl.BlockSpec((tm, tc), lambda i, hf, c: (i, hf * nfb + c)),
        out_shape=jax.ShapeDtypeStruct((T, 2 * F), BF16), name=name,
        compiler_params=_params(("parallel", "parallel", "parallel"), VMEM_BIG))(dacc, dacc, cw)


def _my_place():
    x, y, c = (lax.axis_index(a) for a in AXES)
    return x, y, c, 4 * x + 2 * y + c


def _peer(m, x, y, c):
    px = 1 - x if (m >> 2) & 1 else x
    py = 1 - y if (m >> 1) & 1 else y
    pc = 1 - c if m & 1 else c
    return (px, py, pc), 4 * px + 2 * py + pc


def _exchange(ins, out_shapes, plan, *, bcast, name):
    n_in, n_out, n = len(ins), len(out_shapes), len(plan)

    def body(*refs):
        in_refs, out_refs = refs[:n_in], refs[n_in:n_in + n_out]
        send_sems, recv_sems, loc_sems = refs[n_in + n_out:]
        x, y, c, me = _my_place()

        def src(f, who):
            r = in_refs[plan[f][0]]
            return r if bcast else r.at[who]

        def dst(f, who):
            r = out_refs[plan[f][1]]
            lay = plan[f][2]
            return r.at[who] if lay is None else r.at[lay, who]

        def remote(f, m, landing):
            dev, plin = _peer(m, x, y, c)
            return pltpu.make_async_remote_copy(
                src_ref=src(f, plin), dst_ref=dst(f, plin if landing else me), send_sem=send_sems.at[f, m - 1],
                recv_sem=recv_sems.at[f, m - 1], device_id=dev, device_id_type=MESH)

        locs = [pltpu.make_async_copy(src(f, me), dst(f, me), loc_sems.at[f]) for f in range(n)]
        for cp in locs:
            cp.start()
        for m in range(1, N_DEV):
            for f in range(n):
                remote(f, m, False).start()
        for m in range(1, N_DEV):
            for f in range(n):
                remote(f, m, True).wait()
        for cp in locs:
            cp.wait()

    hbm = pl.BlockSpec(memory_space=pltpu.HBM)
    return pl.pallas_call(
        body, in_specs=[hbm] * n_in, out_specs=[hbm] * n_out, out_shape=out_shapes,
        scratch_shapes=[pltpu.SemaphoreType.DMA((n, N_DEV - 1)), pltpu.SemaphoreType.DMA((n, N_DEV - 1)),
                        pltpu.SemaphoreType.DMA((n,))],
        name=name)(*ins)


def all_gather(shards, *, name):
    outs = [jax.ShapeDtypeStruct((N_DEV,) + s.shape, s.dtype) for s in shards]
    return _exchange(shards, outs, [(a, a, None) for a in range(len(shards))], bcast=True, name=name)


_HBM = pl.BlockSpec(memory_space=pltpu.HBM)
_SEM = pl.BlockSpec(memory_space=pltpu.SEMAPHORE)
_EFFECT = pltpu.SideEffectType.DATAFLOW_SIDE_EFFECTING


def _split_copy(in_refs, land_refs, send_sems, recv_sems, bcast, f, m, place, landing):
    x, y, c, me = place
    dev, plin = _peer(m, x, y, c)
    src = in_refs[f] if bcast else in_refs[f].at[plin]
    return pltpu.make_async_remote_copy(
        src_ref=src, dst_ref=land_refs[f].at[plin if landing else me],
        send_sem=send_sems.at[f * (N_DEV - 1) + m - 1], recv_sem=recv_sems.at[f * (N_DEV - 1) + m - 1],
        device_id=dev, device_id_type=MESH)


def _own_copy(in_refs, land_refs, own_sems, bcast, f, place):
    me = place[3]
    return pltpu.make_async_copy(in_refs[f] if bcast else in_refs[f].at[me], land_refs[f].at[me], own_sems.at[f])


def exchange_start(ins, *, bcast, dep=None, name):
    n = len(ins)
    lands = [lax.empty(((N_DEV,) + a.shape) if bcast else a.shape, a.dtype) for a in ins]
    deps = [] if dep is None else [dep]
    nd = len(deps)

    def body(*refs):
        in_refs, land_refs = refs[:n], refs[n:2 * n]
        send_sems, recv_sems, own_sems = refs[2 * n + nd:2 * n + nd + 3]
        token = refs[-1]
        place = _my_place()
        for m in range(1, N_DEV):
            for f in range(n):
                _split_copy(in_refs, land_refs, send_sems, recv_sems, bcast, f, m, place, False).start()
        for f in range(n):
            _own_copy(in_refs, land_refs, own_sems, bcast, f, place).start()
        token[...] = jnp.zeros_like(token)

    arrs = [pltpu.with_memory_space_constraint(a, pltpu.HBM) for a in (*ins, *lands)]
    sems = pltpu.SemaphoreType.DMA((n * (N_DEV - 1),))
    outs = pl.pallas_call(
        body, name=name,
        out_shape=(sems, sems, pltpu.SemaphoreType.DMA((n,)), *[pltpu.HBM(a.shape, a.dtype) for a in arrs],
                   jax.ShapeDtypeStruct((SUBLANES, LANES), F32)),
        in_specs=[_HBM] * (2 * n) + [_DEP] * nd,
        out_specs=(_SEM, _SEM, _SEM, *[_HBM] * (2 * n), pl.BlockSpec(memory_space=pltpu.VMEM)),
        input_output_aliases={i: 3 + i for i in range(2 * n)},
        compiler_params=pltpu.CompilerParams(has_side_effects=_EFFECT))(*arrs, *deps)
    return outs[0], outs[1], outs[2], list(outs[3:3 + 2 * n]), outs[-1]


def exchange_wait(started, after, *, bcast, name):
    send_sems, recv_sems, own_sems, thrus, _ = started
    n = len(thrus) // 2

    def body(*refs):
        in_refs, land_refs = refs[:n], refs[n:2 * n]
        send, recv, own = refs[2 * n:2 * n + 3]
        place = _my_place()
        for f in range(n):
            _own_copy(in_refs, land_refs, own, bcast, f, place).wait()
        for m in range(1, N_DEV):
            for f in range(n):
                cp = _split_copy(in_refs, land_refs, send, recv, bcast, f, m, place, True)
                cp.wait_send()
                cp.wait_recv()

    outs = pl.pallas_call(
        body, name=name, out_shape=[pltpu.HBM(a.shape, a.dtype) for a in thrus],
        in_specs=[_HBM] * (2 * n) + [_SEM, _SEM, _SEM, pl.BlockSpec(memory_space=pl.ANY)], out_specs=[_HBM] * (2 * n),
        input_output_aliases={i: i for i in range(2 * n)},
        compiler_params=pltpu.CompilerParams(has_side_effects=_EFFECT))(
            *thrus, send_sems, recv_sems, own_sems, after)
    return list(outs[n:])


def _adamw(w, g, m, v):
    m = ADAM_B1 * m + (1.0 - ADAM_B1) * g
    v = ADAM_B2 * v + (1.0 - ADAM_B2) * (g * g)
    m_hat = m / (1.0 - ADAM_B1 ** ADAM_STEP)
    v_hat = v / (1.0 - ADAM_B2 ** ADAM_STEP)
    delta = -ADAM_LR * (m_hat / (jnp.sqrt(v_hat) + ADAM_EPS) + ADAM_WD * w)
    return delta, m, v


def adam_reduce(recv, w, m, v, l, prev, *, name):
    _, R, C = recv.shape
    L = w.shape[0]
    budget = 4 * 1024 * 1024
    tr = R
    for cand in (1024, 512, 352, 256, 176, 128, 64, 32, 16):
        if R % cand == 0 and N_DEV * cand * C * recv.dtype.itemsize <= budget:
            tr = cand
            break

    def body(r_ref, w_ref, m_ref, v_ref, *rest):
        g_ref, d_ref, nm_ref, nv_ref = rest[-4:]
        g = r_ref[0].astype(F32)
        for s in range(1, N_DEV):
            g = g + r_ref[s].astype(F32)
        d, nm, nv = _adamw(w_ref[...], g, m_ref[...], v_ref[...])
        g_ref[...] = g
        d_ref[...] = d
        nm_ref[...] = nm
        nv_ref[...] = nv

    blk = pl.BlockSpec((None, tr, C), lambda r: (l, r, 0))
    sh = jax.ShapeDtypeStruct((L, R, C), F32)
    extra = [] if prev is None else list(prev)
    return pl.pallas_call(
        body, grid=(R // tr,),
        in_specs=[pl.BlockSpec((N_DEV, tr, C), lambda r: (0, r, 0)), blk, blk, blk]
        + [pl.BlockSpec(memory_space=pl.ANY)] * len(extra),
        out_specs=[blk] * 4, out_shape=[sh] * 4, name=name,
        input_output_aliases={4 + i: i for i in range(len(extra))},
        compiler_params=_params(("parallel",), VMEM_MID))(recv, w, m, v, *extra)


def sum_slots(recv, *, name):
    _, R, C = recv.shape
    tr = _pick(R, (512, 256, 128, 64, 32, 16, 8))

    def body(r_ref, o_ref):
        g = r_ref[0]
        for s in range(1, N_DEV):
            g = g + r_ref[s]
        o_ref[...] = g

    return pl.pallas_call(
        body, grid=(R // tr,), in_specs=[pl.BlockSpec((N_DEV, tr, C), lambda r: (0, r, 0))],
        out_specs=pl.BlockSpec((tr, C), lambda r: (r, 0)), out_shape=jax.ShapeDtypeStruct((R, C), F32),
        name=name, compiler_params=_params(("parallel",)))(recv)


def adam_flat(g, w, m, v, *, name):
    R, C = g.shape
    tr = _pick(R, (512, 256, 128, 64, 32, 16, 8))

    def body(g_ref, w_ref, m_ref, v_ref, d_ref, nm_ref, nv_ref):
        d, nm, nv = _adamw(w_ref[...], g_ref[...], m_ref[...], v_ref[...])
        d_ref[...] = d
        nm_ref[...] = nm
        nv_ref[...] = nv

    blk = pl.BlockSpec((tr, C), lambda r: (r, 0))
    sh = jax.ShapeDtypeStruct((R, C), F32)
    return pl.pallas_call(body, grid=(R // tr,), in_specs=[blk] * 4, out_specs=[blk] * 3, out_shape=[sh] * 3,
                          name=name, compiler_params=_params(("parallel",)))(g, w, m, v)


WEIGHTS = ("norm_mix_g", "norm_ffn_g", "a_w_in", "a_g_v", "a_w_s", "a_b_s", "a_w_out", "b_w_in", "b_a_re", "b_a_im",
           "b_log_dt", "b_b_re", "b_b_im", "b_c_re", "b_c_im", "b_d", "b_w_glu", "f_w_up", "f_conv_w", "f_conv_b",
           "f_w_down", "final_g")
BIG = ("a_w_in", "a_w_out", "b_w_in", "b_w_glu", "f_w_up", "f_w_down")
FLAT_CHUNK = N_DEV * SUBLANES * LANES


def _expand(blocks, eye):
    KB, KG, C, P = blocks.shape
    return (blocks[:, :, :, None, :] * eye[None, :, None, :, None]).reshape(KB, KG * C, KG * P)


def _diag_blocks(dense, KG, C, P, eye):
    KB = dense.shape[0]
    return jnp.einsum("kgchp,gh->kgcp", dense.reshape(KB, KG, C, KG, P), eye)


def _flatten_pack(parts):
    flat = jnp.concatenate([p.reshape(-1) for p in parts])
    pad = (-flat.shape[0]) % FLAT_CHUNK
    return jnp.pad(flat, (0, pad))


def _unpack(flat, like):
    out, o = [], 0
    for p in like:
        n = math.prod(p.shape)
        out.append(flat[o:o + n].reshape(p.shape))
        o += n
    return out


def kernel(x, norm_mix_g, norm_ffn_g, a_w_in, a_g_v, a_w_s, a_b_s, a_w_out, b_w_in, b_a_re, b_a_im, b_log_dt, b_b_re, b_b_im, b_c_re, b_c_im, b_d, b_w_glu, f_w_up, f_conv_w, f_conv_b, f_w_down, final_g, loss_target, m_norm_mix_g, m_norm_ffn_g, m_a_w_in, m_a_g_v, m_a_w_s, m_a_b_s, m_a_w_out, m_b_w_in, m_b_a_re, m_b_a_im, m_b_log_dt, m_b_b_re, m_b_b_im, m_b_c_re, m_b_c_im, m_b_d, m_b_w_glu, m_f_w_up, m_f_conv_w, m_f_conv_b, m_f_w_down, m_final_g, v_norm_mix_g, v_norm_ffn_g, v_a_w_in, v_a_g_v, v_a_w_s, v_a_b_s, v_a_w_out, v_b_w_in, v_b_a_re, v_b_a_im, v_b_log_dt, v_b_b_re, v_b_b_im, v_b_c_re, v_b_c_im, v_b_d, v_b_w_glu, v_f_w_up, v_f_conv_w, v_f_conv_b, v_f_w_down, v_final_g):
    env = dict(locals())
    W = {n: env[n] for n in WEIGHTS}
    Mo = {n: env["m_" + n] for n in WEIGHTS}
    Vo = {n: env["v_" + n] for n in WEIGHTS}

    _, T, D = x.shape
    depth = norm_mix_g.shape[0]
    E_A = a_g_v.shape[1]
    H = a_w_s.shape[1]
    G, P, C = b_b_re.shape[1], b_b_re.shape[2], b_b_re.shape[3]
    E_B = G * C
    KG = S5_KG
    KB = G // KG
    F2 = f_conv_b.shape[1]
    tb = _pick(T, (512, 256, 128))
    pm, pmt = _perm_matrices(tb)
    eye = jnp.eye(KG, dtype=F32)
    _, _, _, me = _my_place()

    wb = {n: W[n].astype(BF16) for n in BIG}

    def shards(i):
        j = i // 2
        if i % 2 == 0:
            return [wb["a_w_in"][j], wb["a_w_out"][j], wb["f_w_up"][i], wb["f_w_down"][i], f_conv_w[i]]
        return [wb["b_w_in"][j], wb["b_w_glu"][j], b_d[j][None], wb["f_w_up"][i], wb["f_w_down"][i], f_conv_w[i]]

    h = x[0]
    started = exchange_start(shards(0), bcast=True, name="gather_start0")
    saved = []
    for i in range(depth):
        j = i // 2
        s = {}
        gathered = exchange_wait(started, h, bcast=True, name=f"gather_wait{i}")
        dep = None
        if i + 1 < depth:
            started = exchange_start(shards(i + 1), bcast=True, dep=gathered[0], name=f"gather_start{i + 1}")
            dep = started[-1]
        if i % 2 == 0:
            g_in, g_out, g_up, g_dn, g_cw = gathered
            s["w_in"], s["w_out"] = g_in, g_out.reshape(E_A, D)
        else:
            g_in, g_glu, g_dd, g_up, g_dn, g_cw = gathered
            s["w_in"], s["w_glu"] = g_in.reshape(D, E_B), g_glu
            s["dd"] = g_dd.reshape(KB, 1, KG * C)
        s["w_up"], s["w_dn"], s["cw"] = g_up, g_dn.reshape(F2 // 2, D), g_cw
        s["h"] = h
        s["hn"] = rms_fwd(h, norm_mix_g[i][None], dep=dep, name=f"rms_mix{i}")
        if i % 2 == 0:
            s["p"] = mm_nn(s["hn"], s["w_in"], name=f"a_in{i}")
            s["bexp"] = jnp.repeat(a_b_s[j].T, E_A // H, axis=1)
            s["us"] = sgu_fwd(s["p"], a_g_v[j][None], a_w_s[j], s["bexp"], name=f"sgu_fwd{i}")
            h_mid = mm_nn(s["us"], s["w_out"], res=h, name=f"a_out{i}")
        else:
            s["uu"] = mm_nn(s["hn"], s["w_in"], name=f"b_in{i}")
            s["prm"] = (b_a_re[j], b_a_im[j], b_log_dt[j][:, None],
                        b_b_re[j].transpose(2, 0, 1), b_b_im[j].transpose(2, 0, 1))
            ar, ai, bbr, bbi = s5_disc_fwd(*s["prm"], name=f"s5_disc{i}")
            to_blocks = lambda t: t.reshape(C, KB, KG, P).transpose(1, 2, 0, 3)
            s["bd_r"] = _expand(to_blocks(bbr), eye).astype(BF16)
            s["bd_i"] = _expand(to_blocks(bbi), eye).astype(BF16)
            s["ct_r"] = _expand(b_c_re[j].reshape(KB, KG, C, P), eye).astype(BF16)
            s["ct_i"] = _expand(b_c_im[j].reshape(KB, KG, C, P), eye).astype(BF16)
            s["ab_r"], s["ab_i"] = ar.reshape(KB, 1, KG * P), ai.reshape(KB, 1, KG * P)
            s["y"], s["q"], s["st_r"], s["st_i"] = s5_fwd(
                s["uu"], pm, pmt, s["bd_r"], s["bd_i"], s["ct_r"], s["ct_i"], s["ab_r"], s["ab_i"], s["dd"],
                tb=tb, name=f"s5_fwd{i}")
            s["pg"] = mm_nn(s["q"], s["w_glu"], name=f"b_glu{i}")
            h_mid = glu_fwd(h, s["pg"], name=f"glu_fwd{i}")
        s["h_mid"] = h_mid
        s["hn2"] = rms_fwd(h_mid, norm_ffn_g[i][None], name=f"rms_ffn{i}")
        s["z"] = mm_nn(s["hn2"], s["w_up"], name=f"f_up{i}")
        s["a"] = convglu_fwd(s["z"], s["cw"], f_conv_b[i][None], name=f"convglu_fwd{i}")
        h = mm_nn(s["a"], s["w_dn"], res=h_mid, name=f"f_down{i}")
        saved.append(s)

    loss_tile, dh, dg_final = loss_head(h, final_g[None], loss_target[0], name="loss_head")
    loss = lax.psum(loss_tile[0, 0], AXES)

    gbig = {n: [None] * W[n].shape[0] for n in BIG}
    gs = {n: [None] * W[n].shape[0] for n in WEIGHTS if n not in BIG and n != "final_g"}
    pending = []
    for i in reversed(range(depth)):
        j = i // 2
        s = saved[i]
        dhb = dh.astype(BF16)
        gbig["f_w_down"][i] = mm_tn(s["a"], dhb, blocks=1, name=f"g_down{i}").reshape(N_DEV, F2 // 2 // N_DEV, D)
        da = mm_nt(dhb, s["w_dn"], name=f"d_a{i}")
        dacc, dwg, dwv, dbg, dbv = convglu_bwd_acc(s["z"], da, s["cw"], f_conv_b[i][None], name=f"convglu_bwd{i}")
        gs["f_conv_w"][i] = jnp.concatenate([dwg, dwv], axis=0)
        gs["f_conv_b"][i] = jnp.concatenate([dbg, dbv], axis=1)[0]
        dz = conv_bwd_in(dacc, s["cw"], name=f"conv_bwd_in{i}")
        gbig["f_w_up"][i] = mm_tn(s["hn2"], dz, blocks=N_DEV, name=f"g_up{i}")
        dhn2 = mm_nt(dz, s["w_up"], name=f"d_hn2{i}")
        st = exchange_start([gbig["f_w_down"][i], gbig["f_w_up"][i]], bcast=False, name=f"xchg_ffn_start{i}")
        pending.append((st, [("f_w_down", i), ("f_w_up", i)], f"ffn{i}"))
        dh_mid, dg = rms_bwd(s["h_mid"], norm_ffn_g[i][None], dhn2, dh, dep=st[-1], name=f"rms_ffn_bwd{i}")
        gs["norm_ffn_g"][i] = dg[0]
        dmb = dh_mid.astype(BF16)
        if i % 2 == 0:
            gbig["a_w_out"][j] = mm_tn(s["us"], dmb, blocks=1, name=f"g_aout{i}").reshape(N_DEV, E_A // N_DEV, D)
            d_us = mm_nt(dmb, s["w_out"], name=f"d_us{i}")
            dp, dws, dbt, dgv = sgu_bwd(s["p"], d_us, a_g_v[j][None], a_w_s[j], s["bexp"], name=f"sgu_bwd{i}")
            gs["a_w_s"][j], gs["a_b_s"][j], gs["a_g_v"][j] = dws, dbt[:, :H].T, dgv[0]
            gbig["a_w_in"][j] = mm_tn(s["hn"], dp, blocks=N_DEV, name=f"g_ain{i}")
            dhn = mm_nt(dp, s["w_in"], name=f"d_hn_a{i}")
        else:
            dpg = glu_bwd(dh_mid, s["pg"], name=f"glu_bwd{i}")
            gbig["b_w_glu"][j] = mm_tn(s["q"], dpg, blocks=N_DEV, name=f"g_glu{i}")
            dq = mm_nt(dpg, s["w_glu"], name=f"d_q{i}")
            duu, dbr, dbi, dcr, dci, dar, dai, ddd = s5_bwd(
                s["uu"], s["y"], dq, pm, pmt, s["st_r"], s["st_i"], s["bd_r"], s["bd_i"], s["ct_r"], s["ct_i"],
                s["ab_r"], s["ab_i"], s["dd"], tb=tb, name=f"s5_bwd{i}")
            from_blocks = lambda t: _diag_blocks(t, KG, C, P, eye).transpose(2, 0, 1, 3).reshape(C, G, P)
            d_are, d_aim, d_ldt, d_bre, d_bim = s5_disc_bwd(
                *s["prm"], dar.reshape(G, P), dai.reshape(G, P), from_blocks(dbr), from_blocks(dbi),
                name=f"s5_disc_bwd{i}")
            gs["b_a_re"][j], gs["b_a_im"][j], gs["b_log_dt"][j] = d_are, d_aim, d_ldt[:, 0]
            gs["b_b_re"][j], gs["b_b_im"][j] = d_bre.transpose(1, 2, 0), d_bim.transpose(1, 2, 0)
            gs["b_c_re"][j] = _diag_blocks(dcr, KG, C, P, eye).reshape(G, C, P)
            gs["b_c_im"][j] = _diag_blocks(dci, KG, C, P, eye).reshape(G, C, P)
            gs["b_d"][j] = ddd.reshape(E_B)
            gbig["b_w_in"][j] = mm_tn(s["hn"], duu, blocks=1, name=f"g_bin{i}").reshape(N_DEV, D // N_DEV, E_B)
            dhn = mm_nt(duu, s["w_in"], name=f"d_hn_b{i}")
        mix = ("a_w_out", "a_w_in") if i % 2 == 0 else ("b_w_glu", "b_w_in")
        st = exchange_start([gbig[n][j] for n in mix], bcast=False, name=f"xchg_mix_start{i}")
        pending.append((st, [(n, j) for n in mix], f"mix{i}"))
        dh, dg = rms_bwd(s["h"], norm_mix_g[i][None], dhn, dh_mid, dep=st[-1], name=f"rms_mix_bwd{i}")
        gs["norm_mix_g"][i] = dg[0]
    grad_x = dh[None]

    grads, deltas, new_m, new_v = {}, {}, {}, {}
    small = [n for n in WEIGHTS if n not in BIG]
    full = {n: (dg_final[0] if n == "final_g" else jnp.stack(gs[n])) for n in small}
    flat = _flatten_pack([full[n] for n in small])
    rows = flat.shape[0] // N_DEV // LANES
    st_small = exchange_start([flat.reshape(N_DEV, rows, LANES)], bcast=False, name="xchg_small_start")

    done = {n: None for n in BIG}
    after = st_small[-1]
    for st, items, tag in pending:
        recvs = exchange_wait(st, after, bcast=False, name=f"xchg_wait_{tag}")
        for (n, l), recv in zip(items, recvs):
            done[n] = adam_reduce(recv, W[n], Mo[n], Vo[n], l, done[n], name=f"adam_{n}{l}")
            after = done[n][0]
    for n in BIG:
        grads[n], deltas[n], new_m[n], new_v[n] = done[n]

    recv = exchange_wait(st_small, after, bcast=False, name="xchg_small_wait")[0]
    part = sum_slots(recv, name="sum_small")
    tot = all_gather([part], name="gather_small")[0].reshape(-1)
    red = dict(zip(small, _unpack(tot, [full[n] for n in small])))
    red["f_conv_w"] = lax.dynamic_index_in_dim(red["f_conv_w"], me, axis=1, keepdims=False)
    red["b_d"] = lax.dynamic_slice_in_dim(red["b_d"], me * (E_B // N_DEV), E_B // N_DEV, axis=1)
    gflat = _flatten_pack([red[n] for n in small]).reshape(-1, LANES)
    d_f, m_f, v_f = adam_flat(
        gflat, _flatten_pack([W[n] for n in small]).reshape(-1, LANES),
        _flatten_pack([Mo[n] for n in small]).reshape(-1, LANES),
        _flatten_pack([Vo[n] for n in small]).reshape(-1, LANES), name="adam_small")
    like = [W[n] for n in small]
    for n, d_, m_, v_ in zip(small, _unpack(d_f.reshape(-1), like), _unpack(m_f.reshape(-1), like),
                             _unpack(v_f.reshape(-1), like)):
        grads[n], deltas[n], new_m[n], new_v[n] = red[n], d_, m_, v_

    return (loss, grad_x, *[grads[n] for n in WEIGHTS], *[deltas[n] for n in WEIGHTS],
            *[new_m[n] for n in WEIGHTS], *[new_v[n] for n in WEIGHTS])
```

```python
import math

import jax
import jax.numpy as jnp
from jax import lax
from jax.experimental import pallas as pl
from jax.experimental.pallas import tpu as pltpu

F32 = jnp.float32
BF16 = jnp.bfloat16
N_DEV = 8
AXES = ("x", "y", "c")
EPS = 1e-6
LANES = 128
SUBLANES = 8
VMEM_BIG = 56 * 1024 * 1024
VMEM_MID = 40 * 1024 * 1024
ADAM_LR, ADAM_B1, ADAM_B2, ADAM_EPS, ADAM_WD, ADAM_STEP = 0.001, 0.9, 0.999, 1e-08, 0.01, 10
MESH = pl.DeviceIdType.MESH
GELU_C = math.sqrt(2.0 / math.pi)
GELU_K = 0.044715


def _pick(n, prefs):
    for p in prefs:
        if n % p == 0:
            return p
    return n


def _params(sem, vmem=None):
    return pltpu.CompilerParams(dimension_semantics=sem, vmem_limit_bytes=vmem)


def _gelu(x):
    return 0.5 * x * (1.0 + jnp.tanh(GELU_C * (x + GELU_K * x * x * x)))


def _gelu_grad(x):
    x2 = x * x
    th = jnp.tanh(GELU_C * x * (1.0 + GELU_K * x2))
    return 0.5 * (1.0 + th) + 0.5 * x * (1.0 - th * th) * GELU_C * (1.0 + 3.0 * GELU_K * x2)


def _sigmoid(x):
    return 1.0 / (1.0 + jnp.exp(-x))


def _dot_nn(a, b):
    return jnp.dot(a, b, preferred_element_type=F32)


def _dot_nt(a, b):
    return lax.dot_general(a, b, (((1,), (1,)), ((), ())), preferred_element_type=F32)


def _dot_tn(a, b):
    return lax.dot_general(a, b, (((0,), (0,)), ((), ())), preferred_element_type=F32)


M_TILES = (1024, 512, 256, 128)
TN_M_TILES = (2048, 1024, 512, 256, 128)
FULL_K = 2048
NT_SPAN = 2816
N_TILES = (1408, 1024, 512, 384, 256, 128)
K_TILES = (1408, 1024, 512, 384, 256, 128)


def _as3(b):
    return b if b.ndim == 3 else b[None]


def mm_nn(a, b, *, res=None, out_dtype=F32, name):
    b3 = _as3(b)
    M, K = a.shape
    J, _, nb = b3.shape
    tm, tn = _pick(M, M_TILES), _pick(nb, N_TILES)
    tk = K if K <= FULL_K else _pick(K, (2816,) + K_TILES)
    per, nk = nb // tn, K // tk

    def body(*refs):
        if res is None:
            a_ref, b_ref, o_ref, acc = refs
        else:
            a_ref, b_ref, r_ref, o_ref, acc = refs
        k = pl.program_id(2)
        if nk == 1:
            r = _dot_nn(a_ref[...], b_ref[...])
            if res is not None:
                r = r + r_ref[...]
            o_ref[...] = r.astype(out_dtype)
            return

        @pl.when(k == 0)
        def _():
            acc[...] = jnp.zeros_like(acc)

        acc[...] += _dot_nn(a_ref[...], b_ref[...])

        @pl.when(k == nk - 1)
        def _():
            r = acc[...]
            if res is not None:
                r = r + r_ref[...]
            o_ref[...] = r.astype(out_dtype)

    in_specs = [pl.BlockSpec((tm, tk), lambda i, n, k: (i, k)),
                pl.BlockSpec((None, tk, tn), lambda i, n, k: (n // per, k, n % per))]
    args = [a, b3]
    if res is not None:
        in_specs.append(pl.BlockSpec((tm, tn), lambda i, n, k: (i, n)))
        args.append(res)
    return pl.pallas_call(
        body, grid=(M // tm, J * per, nk), in_specs=in_specs,
        out_specs=pl.BlockSpec((tm, tn), lambda i, n, k: (i, n)),
        out_shape=jax.ShapeDtypeStruct((M, J * nb), out_dtype),
        scratch_shapes=[pltpu.VMEM((tm, tn) if nk > 1 else (SUBLANES, LANES), F32)], name=name,
        compiler_params=_params(("parallel", "parallel", "arbitrary"), VMEM_BIG))(*args)


def mm_nt(dy, b, *, out_dtype=F32, name):
    b3 = _as3(b)
    M, N = dy.shape
    J, K, nb = b3.shape
    tm = _pick(M, M_TILES)
    tn = nb if (J == 1 and nb <= FULL_K) else _pick(nb, N_TILES)
    grp = max([g for g in (4, 2, 1) if J % g == 0 and g * tn <= NT_SPAN]) if tn == nb else 1
    tk = _pick(K, K_TILES) if (grp > 1 or K > FULL_K) else K
    per, nn = nb // tn, (J * nb) // (tn * grp)

    def body(d_ref, b_ref, o_ref, acc):
        if grp > 1:
            r = sum(_dot_nt(d_ref[:, g * tn:(g + 1) * tn], b_ref[g]) for g in range(grp))
        else:
            r = _dot_nt(d_ref[...], b_ref[...])
        if nn == 1:
            o_ref[...] = r.astype(out_dtype)
            return
        n = pl.program_id(2)

        @pl.when(n == 0)
        def _():
            acc[...] = jnp.zeros_like(acc)

        acc[...] += r

        @pl.when(n == nn - 1)
        def _():
            o_ref[...] = acc[...].astype(out_dtype)

    if grp > 1:
        b_spec = pl.BlockSpec((grp, tk, tn), lambda i, k, n: (n, k, 0))
    else:
        b_spec = pl.BlockSpec((None, tk, tn), lambda i, k, n: (n // per, k, n % per))
    return pl.pallas_call(
        body, grid=(M // tm, K // tk, nn),
        in_specs=[pl.BlockSpec((tm, tn * grp), lambda i, k, n: (i, n)), b_spec],
        out_specs=pl.BlockSpec((tm, tk), lambda i, k, n: (i, k)),
        out_shape=jax.ShapeDtypeStruct((M, K), out_dtype),
        scratch_shapes=[pltpu.VMEM((tm, tk) if nn > 1 else (SUBLANES, LANES), F32)], name=name,
        compiler_params=_params(("parallel", "parallel", "arbitrary"), VMEM_BIG))(dy, b3)


def mm_tn(x, dy, *, blocks, out_dtype=BF16, name):
    M, K = x.shape
    _, N = dy.shape
    nb = N // blocks
    tm, tn, tk = _pick(M, TN_M_TILES), _pick(nb, N_TILES), _pick(K, K_TILES)
    per, nm = nb // tn, M // tm

    def body(x_ref, d_ref, o_ref, acc):
        m = pl.program_id(2)

        @pl.when(m == 0)
        def _():
            acc[...] = jnp.zeros_like(acc)

        acc[...] += _dot_tn(x_ref[...], d_ref[...])

        @pl.when(m == nm - 1)
        def _():
            o_ref[...] = acc[...].astype(out_dtype)

    return pl.pallas_call(
        body, grid=(K // tk, N // tn, nm),
        in_specs=[pl.BlockSpec((tm, tk), lambda k, n, m: (m, k)),
                  pl.BlockSpec((tm, tn), lambda k, n, m: (m, n))],
        out_specs=pl.BlockSpec((None, tk, tn), lambda k, n, m: (n // per, k, n % per)),
        out_shape=jax.ShapeDtypeStruct((blocks, K, nb), out_dtype),
        scratch_shapes=[pltpu.VMEM((tk, tn), F32)], name=name,
        compiler_params=_params(("parallel", "parallel", "arbitrary"), VMEM_BIG))(x, dy)


ROW_TILES = (256, 128)


_DEP = pl.BlockSpec(memory_space=pl.ANY)


def rms_fwd(h, g, *, dep=None, name):
    T, D = h.shape
    tm = _pick(T, ROW_TILES)
    deps = [] if dep is None else [dep]

    def body(h_ref, g_ref, *rest):
        o_ref = rest[-1]
        x = h_ref[...]
        r = lax.rsqrt(jnp.mean(x * x, axis=-1, keepdims=True) + EPS)
        o_ref[...] = (x * r * g_ref[...]).astype(BF16)

    return pl.pallas_call(
        body, grid=(T // tm,),
        in_specs=[pl.BlockSpec((tm, D), lambda i: (i, 0)), pl.BlockSpec((1, D), lambda i: (0, 0))] + [_DEP] * len(deps),
        out_specs=pl.BlockSpec((tm, D), lambda i: (i, 0)),
        out_shape=jax.ShapeDtypeStruct((T, D), BF16), name=name,
        compiler_params=_params(("parallel",), VMEM_MID))(h, g, *deps)


def rms_bwd(h, g, dhn, dres, *, dep=None, name):
    T, D = h.shape
    tm = _pick(T, ROW_TILES)
    deps = [] if dep is None else [dep]

    def body(h_ref, g_ref, d_ref, r_ref, *rest):
        dh_ref, dhb_ref, dg_ref = rest[-3:]

        @pl.when(pl.program_id(0) == 0)
        def _():
            dg_ref[...] = jnp.zeros_like(dg_ref)

        x = h_ref[...]
        r = lax.rsqrt(jnp.mean(x * x, axis=-1, keepdims=True) + EPS)
        xh = x * r
        dy = d_ref[...]
        dxh = dy * g_ref[...]
        dh = r_ref[...] + r * (dxh - xh * jnp.mean(dxh * xh, axis=-1, keepdims=True))
        dh_ref[...] = dh
        dhb_ref[...] = dh.astype(BF16)
        dg_ref[...] += jnp.sum(dy * xh, axis=0, keepdims=True)

    row = pl.BlockSpec((tm, D), lambda i: (i, 0))
    vec = pl.BlockSpec((1, D), lambda i: (0, 0))
    return pl.pallas_call(
        body, grid=(T // tm,), in_specs=[row, vec, row, row] + [_DEP] * len(deps), out_specs=[row, row, vec],
        out_shape=[jax.ShapeDtypeStruct((T, D), F32), jax.ShapeDtypeStruct((T, D), BF16),
                   jax.ShapeDtypeStruct((1, D), F32)], name=name,
        compiler_params=_params(("arbitrary",), VMEM_MID))(h, g, dhn, dres, *deps)


def loss_head(h, g, tgt, *, name):
    T, D = h.shape
    tm = _pick(T, ROW_TILES)

    def body(h_ref, g_ref, t_ref, l_ref, dh_ref, dhb_ref, dg_ref):
        @pl.when(pl.program_id(0) == 0)
        def _():
            dg_ref[...] = jnp.zeros_like(dg_ref)
            l_ref[...] = jnp.zeros_like(l_ref)

        x = h_ref[...]
        gg = g_ref[...]
        r = lax.rsqrt(jnp.mean(x * x, axis=-1, keepdims=True) + EPS)
        xh = x * r
        e = xh * gg - t_ref[...]
        l_ref[...] += 0.5 * jnp.sum(jnp.mean(e * e, axis=-1, keepdims=True), axis=0, keepdims=True)
        dy = e * (1.0 / D)
        dxh = dy * gg
        dh = r * (dxh - xh * jnp.mean(dxh * xh, axis=-1, keepdims=True))
        dh_ref[...] = dh
        dhb_ref[...] = dh.astype(BF16)
        dg_ref[...] += jnp.sum(dy * xh, axis=0, keepdims=True)

    row = pl.BlockSpec((tm, D), lambda i: (i, 0))
    vec = pl.BlockSpec((1, D), lambda i: (0, 0))
    return pl.pallas_call(
        body, grid=(T // tm,), in_specs=[row, vec, row],
        out_specs=[pl.BlockSpec((SUBLANES, LANES), lambda i: (0, 0)), row, row, vec],
        out_shape=[jax.ShapeDtypeStruct((SUBLANES, LANES), F32), jax.ShapeDtypeStruct((T, D), F32),
                   jax.ShapeDtypeStruct((T, D), BF16), jax.ShapeDtypeStruct((1, D), F32)], name=name,
        compiler_params=_params(("arbitrary",), VMEM_MID))(h, g, tgt)


def _sgu_common(p, gv, w_ref, bexp, E, H, CH):
    Dg = E // H
    z = _gelu(p)
    u, v = z[:, :E], z[:, E:]
    r = lax.rsqrt(jnp.mean(v * v, axis=-1, keepdims=True) + EPS)
    vhat = v * r
    vn = (vhat * gv).astype(BF16)
    row = lax.broadcasted_iota(jnp.int32, (CH, CH), 0)
    col = lax.broadcasted_iota(jnp.int32, (CH, CH), 1)
    causal = row >= col
    ws = [jnp.where(causal, w_ref[hh], 0.0).astype(BF16) for hh in range(H)]
    s = jnp.concatenate([_dot_nn(ws[hh], vn[:, hh * Dg:(hh + 1) * Dg]) for hh in range(H)], axis=1) + bexp
    return u, r, vhat, vn, causal, ws, s


def sgu_fwd(p, g_v, w_s, bexp, *, name):
    T, E2 = p.shape
    E = E2 // 2
    H, CH, _ = w_s.shape

    def body(p_ref, gv_ref, w_ref, b_ref, o_ref):
        u, _, _, _, _, _, s = _sgu_common(p_ref[...], gv_ref[...], w_ref, b_ref[...], E, H, CH)
        o_ref[...] = (u * s).astype(BF16)

    return pl.pallas_call(
        body, grid=(T // CH,),
        in_specs=[pl.BlockSpec((CH, E2), lambda i: (i, 0)), pl.BlockSpec((1, E), lambda i: (0, 0)),
                  pl.BlockSpec((H, CH, CH), lambda i: (0, 0, 0)), pl.BlockSpec((CH, E), lambda i: (0, 0))],
        out_specs=pl.BlockSpec((CH, E), lambda i: (i, 0)),
        out_shape=jax.ShapeDtypeStruct((T, E), BF16), name=name,
        compiler_params=_params(("parallel",), VMEM_BIG))(p, g_v, w_s, bexp)


def sgu_bwd(p, d_us, g_v, w_s, bexp, *, name):
    T, E2 = p.shape
    E = E2 // 2
    H, CH, _ = w_s.shape
    Dg = E // H

    def body(p_ref, d_ref, gv_ref, w_ref, b_ref, dp_ref, dw_ref, db_ref, dg_ref):
        @pl.when(pl.program_id(0) == 0)
        def _():
            dw_ref[...] = jnp.zeros_like(dw_ref)
            db_ref[...] = jnp.zeros_like(db_ref)
            dg_ref[...] = jnp.zeros_like(dg_ref)

        p = p_ref[...]
        gv = gv_ref[...]
        u, r, vhat, vn, causal, ws, s = _sgu_common(p, gv, w_ref, b_ref[...], E, H, CH)
        d = d_ref[...]
        du = d * s
        ds = d * u
        lane = lax.broadcasted_iota(jnp.int32, (CH, LANES), 1)
        dvn_parts = []
        db = jnp.zeros((CH, LANES), F32)
        for hh in range(H):
            ds_h = ds[:, hh * Dg:(hh + 1) * Dg]
            ds_hb = ds_h.astype(BF16)
            dw_ref[hh] += jnp.where(causal, _dot_nt(ds_hb, vn[:, hh * Dg:(hh + 1) * Dg]), 0.0)
            dvn_parts.append(_dot_tn(ws[hh], ds_hb))
            db = db + jnp.where(lane == hh, jnp.sum(ds_h, axis=1, keepdims=True), 0.0)
        db_ref[...] += db
        dvn = jnp.concatenate(dvn_parts, axis=1)
        dg_ref[...] += jnp.sum(dvn * vhat, axis=0, keepdims=True)
        dvh = dvn * gv
        dv = r * (dvh - vhat * jnp.mean(dvh * vhat, axis=-1, keepdims=True))
        dp_ref[...] = (jnp.concatenate([du, dv], axis=1) * _gelu_grad(p)).astype(BF16)

    return pl.pallas_call(
        body, grid=(T // CH,),
        in_specs=[pl.BlockSpec((CH, E2), lambda i: (i, 0)), pl.BlockSpec((CH, E), lambda i: (i, 0)),
                  pl.BlockSpec((1, E), lambda i: (0, 0)), pl.BlockSpec((H, CH, CH), lambda i: (0, 0, 0)),
                  pl.BlockSpec((CH, E), lambda i: (0, 0))],
        out_specs=[pl.BlockSpec((CH, E2), lambda i: (i, 0)), pl.BlockSpec((H, CH, CH), lambda i: (0, 0, 0)),
                   pl.BlockSpec((CH, LANES), lambda i: (0, 0)), pl.BlockSpec((1, E), lambda i: (0, 0))],
        out_shape=[jax.ShapeDtypeStruct((T, E2), BF16), jax.ShapeDtypeStruct((H, CH, CH), F32),
                   jax.ShapeDtypeStruct((CH, LANES), F32), jax.ShapeDtypeStruct((1, E), F32)], name=name,
        compiler_params=_params(("arbitrary",), VMEM_BIG))(p, d_us, g_v, w_s, bexp)


def _s5_disc(a_re, a_im, log_dt, b_re, b_im):
    dt = jnp.exp(log_dt)
    mag = jnp.exp(dt * a_re)
    ar, ai = mag * jnp.cos(dt * a_im), mag * jnp.sin(dt * a_im)
    den = a_re * a_re + a_im * a_im
    qr = ((ar - 1.0) * a_re + ai * a_im) / den
    qi = (ai * a_re - (ar - 1.0) * a_im) / den
    return ar, ai, qr[None] * b_re - qi[None] * b_im, qr[None] * b_im + qi[None] * b_re


def s5_disc_fwd(a_re, a_im, log_dt, b_re, b_im, *, name):
    G, P = a_re.shape
    C = b_re.shape[0]

    def body(ar_ref, ai_ref, dt_ref, br_ref, bi_ref, o_ar, o_ai, o_br, o_bi):
        ar, ai, br, bi = _s5_disc(ar_ref[...], ai_ref[...], dt_ref[...], br_ref[...], bi_ref[...])
        o_ar[...] = ar
        o_ai[...] = ai
        o_br[...] = br
        o_bi[...] = bi

    gp = jax.ShapeDtypeStruct((G, P), F32)
    cgp = jax.ShapeDtypeStruct((C, G, P), F32)
    return pl.pallas_call(body, out_shape=[gp, gp, cgp, cgp], name=name)(a_re, a_im, log_dt, b_re, b_im)


def s5_disc_bwd(a_re, a_im, log_dt, b_re, b_im, d_ar, d_ai, d_br, d_bi, *, name):
    G, P = a_re.shape
    C = b_re.shape[0]

    def body(ar_ref, ai_ref, dt_ref, br_ref, bi_ref, g0, g1, g2, g3, o0, o1, o2, o3, o4):
        prim = (ar_ref[...], ai_ref[...], dt_ref[...], br_ref[...], bi_ref[...])
        _, vjp = jax.vjp(_s5_disc, *prim)
        outs = vjp((g0[...], g1[...], g2[...], g3[...]))
        for o, v in zip((o0, o1, o2, o3, o4), outs):
            o[...] = v

    gp = jax.ShapeDtypeStruct((G, P), F32)
    cgp = jax.ShapeDtypeStruct((C, G, P), F32)
    return pl.pallas_call(
        body, out_shape=[gp, gp, jax.ShapeDtypeStruct((G, 1), F32), cgp, cgp], name=name,
    )(a_re, a_im, log_dt, b_re, b_im, d_ar, d_ai, d_br, d_bi)


S5_KG = 8


def _planes(x):
    return [x[:, c * LANES:(c + 1) * LANES] for c in range(x.shape[1] // LANES)]


def _store_planes(ref, row0, val):
    for c, p in enumerate(_planes(val)):
        ref[c, pl.ds(row0, val.shape[0]), :] = p


def _load_planes(ref, row0, rows):
    return jnp.concatenate([ref[c, pl.ds(row0, rows), :] for c in range(ref.shape[0])], axis=1)


def _build_powers(a_r, a_i, S, pf_r, pf_i, pr_r=None, pr_i=None):
    ar, ai = _planes(a_r), _planes(a_i)
    NP = len(ar)

    def step(i, carry):
        out = []
        for c in range(NP):
            p_r, p_i = carry[2 * c], carry[2 * c + 1]
            pf_r[c, pl.ds(i, 1), :] = p_r
            pf_i[c, pl.ds(i, 1), :] = p_i
            if pr_r is not None:
                pr_r[c, pl.ds(S - 1 - i, 1), :] = p_r
                pr_i[c, pl.ds(S - 1 - i, 1), :] = -p_i
            out += [ar[c] * p_r - ai[c] * p_i, ar[c] * p_i + ai[c] * p_r]
        return tuple(out)

    init = []
    for c in range(NP):
        init += [ar[c], ai[c]]
    lax.fori_loop(0, S, step, tuple(init))


def _scan_seg(hr, hi, hrow0, tb, a_r, a_i, pw_r, pw_i, h0r, h0i, *, reverse):
    NP = hr.shape[0]
    S = tb // SUBLANES
    if reverse:
        a_i = -a_i
    ar, ai = _planes(a_r), _planes(a_i)

    def step(i, carry):
        j = (S - 1 - i) if reverse else i
        slab = pl.ds(pl.multiple_of(hrow0 + j * SUBLANES, SUBLANES), SUBLANES)
        out = []
        for c in range(NP):
            nr = ar[c] * carry[2 * c] - ai[c] * carry[2 * c + 1] + hr[c, slab, :]
            ni = ar[c] * carry[2 * c + 1] + ai[c] * carry[2 * c] + hi[c, slab, :]
            hr[c, slab, :] = nr
            hi[c, slab, :] = ni
            out += [nr, ni]
        return tuple(out)

    z = jnp.zeros((SUBLANES, LANES), F32)
    loc = lax.fori_loop(0, S, step, (z,) * (2 * NP), unroll=2)
    h0r_p, h0i_p = _planes(h0r), _planes(h0i)
    top = 0 if reverse else S - 1
    order = range(SUBLANES - 1, -1, -1) if reverse else range(SUBLANES)
    out_r, out_i, ent_r, ent_i = [], [], [], []
    for c in range(NP):
        s_r, s_i = pw_r[c, pl.ds(top, 1), :], pw_i[c, pl.ds(top, 1), :]
        c_r, c_i = h0r_p[c], h0i_p[c]
        in_r, in_i = [None] * SUBLANES, [None] * SUBLANES
        for seg in order:
            in_r[seg], in_i[seg] = c_r, c_i
            l_r, l_i = loc[2 * c][seg:seg + 1], loc[2 * c + 1][seg:seg + 1]
            c_r, c_i = s_r * c_r - s_i * c_i + l_r, s_r * c_i + s_i * c_r + l_i
        out_r.append(c_r)
        out_i.append(c_i)
        ent_r.append(jnp.concatenate(in_r, axis=0))
        ent_i.append(jnp.concatenate(in_i, axis=0))

    def fix(j, _):
        slab = pl.ds(pl.multiple_of(hrow0 + j * SUBLANES, SUBLANES), SUBLANES)
        for c in range(NP):
            p_r, p_i = pw_r[c, pl.ds(j, 1), :], pw_i[c, pl.ds(j, 1), :]
            hr[c, slab, :] += p_r * ent_r[c] - p_i * ent_i[c]
            hi[c, slab, :] += p_r * ent_i[c] + p_i * ent_r[c]
        return 0

    lax.fori_loop(0, S, fix, 0, unroll=2)
    return ent_r, ent_i, jnp.concatenate(out_r, axis=1), jnp.concatenate(out_i, axis=1)


def _perm_matrices(tb):
    r = jnp.arange(tb)
    pm = (r[None, :] == ((r % SUBLANES) * (tb // SUBLANES) + r // SUBLANES)[:, None]).astype(BF16)
    return pm, pm.T


def _split2(x):
    hi = x.astype(BF16)
    return hi, (x - hi.astype(F32)).astype(BF16)


def _s5_specs(tb, UC, SC, rev_nb=None):
    tmap = (lambda b: b) if rev_nb is None else (lambda b: rev_nb - 1 - b)
    row = pl.BlockSpec((tb, UC), lambda b, k: (tmap(b), k))
    wsp = pl.BlockSpec((None, UC, SC), lambda b, k: (k, 0, 0))
    vsc = pl.BlockSpec((None, 1, SC), lambda b, k: (k, 0, 0))
    vuc = pl.BlockSpec((None, 1, UC), lambda b, k: (k, 0, 0))
    st = pl.BlockSpec((None, None, 1, SC), lambda b, k: (tmap(b), k, 0, 0))
    return row, wsp, vsc, vuc, st


def s5_fwd(uu, pm, pmt, bd_r, bd_i, ct_r, ct_i, ab_r, ab_i, dd, *, tb, name):
    T, E = uu.shape
    KB, UC, SC = bd_r.shape
    NB = T // tb
    NP, S = SC // LANES, tb // SUBLANES

    def body(u_ref, pm_ref, pmt_ref, bdr, bdi, ctr, cti, ar_ref, ai_ref, dd_ref, y_ref, q_ref, sr_ref, si_ref,
             hr, hi, cr, ci, pf_r, pf_i):
        b, k = pl.program_id(0), pl.program_id(1)
        a_r, a_i = ar_ref[...], ai_ref[...]

        @pl.when(b == 0)
        def _():
            cr[k] = jnp.zeros((1, SC), F32)
            ci[k] = jnp.zeros((1, SC), F32)
            _build_powers(a_r, a_i, S, pf_r.at[k], pf_i.at[k])

        h0r, h0i = cr[k], ci[k]
        sr_ref[...] = h0r
        si_ref[...] = h0i
        u = u_ref[...]
        up = _dot_nn(pm_ref[...], u.astype(BF16)).astype(BF16)
        _store_planes(hr, 0, _dot_nn(up, bdr[...]))
        _store_planes(hi, 0, _dot_nn(up, bdi[...]))
        _, _, o_r, o_i = _scan_seg(hr, hi, 0, tb, a_r, a_i, pf_r.at[k], pf_i.at[k], h0r, h0i, reverse=False)
        cr[k] = o_r
        ci[k] = o_i
        ys = (_dot_nt(_load_planes(hr, 0, tb).astype(BF16), ctr[...])
              - _dot_nt(_load_planes(hi, 0, tb).astype(BF16), cti[...]))
        pmt_v = pmt_ref[...]
        y = sum(_dot_nn(pmt_v, part) for part in _split2(ys)) + dd_ref[...] * u
        y_ref[...] = y
        q_ref[...] = _gelu(y).astype(BF16)

    row, wsp, vsc, vuc, st = _s5_specs(tb, UC, SC)
    psp = pl.BlockSpec((tb, tb), lambda b, k: (0, 0))
    stsh = jax.ShapeDtypeStruct((NB, KB, 1, SC), F32)
    pw = pltpu.VMEM((KB, NP, S, LANES), F32)
    pln = pltpu.VMEM((NP, tb, LANES), F32)
    return pl.pallas_call(
        body, grid=(NB, KB), in_specs=[row, psp, psp, wsp, wsp, wsp, wsp, vsc, vsc, vuc],
        out_specs=[row, row, st, st],
        out_shape=[jax.ShapeDtypeStruct((T, E), F32), jax.ShapeDtypeStruct((T, E), BF16), stsh, stsh],
        scratch_shapes=[pln, pln, pltpu.VMEM((KB, 1, SC), F32), pltpu.VMEM((KB, 1, SC), F32), pw, pw],
        name=name, compiler_params=_params(("arbitrary", "arbitrary"), VMEM_MID),
    )(uu, pm, pmt, bd_r, bd_i, ct_r, ct_i, ab_r, ab_i, dd)


def s5_bwd(uu, y, dq, pm, pmt, st_r, st_i, bd_r, bd_i, ct_r, ct_i, ab_r, ab_i, dd, *, tb, name):
    T, E = uu.shape
    KB, UC, SC = bd_r.shape
    NB = T // tb
    HDR = SUBLANES
    NP, S = SC // LANES, tb // SUBLANES
    pw = pltpu.VMEM((KB, NP, S, LANES), F32)
    pln = pltpu.VMEM((NP, tb, LANES), F32)

    def body(u_ref, y_ref, dq_ref, pm_ref, pmt_ref, sr_ref, si_ref, bdr, bdi, ctr, cti, ar_ref, ai_ref, dd_ref,
             du_ref, obr, obi, ocr, oci, odar, odai, oddd,
             hr, hi, gr, gi, kr, ki, abr, abi, acr, aci, pf_r, pf_i, pr_r, pr_i):
        b, k = pl.program_id(0), pl.program_id(1)
        a_r, a_i = ar_ref[...], ai_ref[...]

        @pl.when(b == 0)
        def _():
            _build_powers(a_r, a_i, S, pf_r.at[k], pf_i.at[k], pr_r.at[k], pr_i.at[k])
            z1 = jnp.zeros((1, SC), F32)
            kr[k] = z1
            ki[k] = z1
            odar[k] = z1
            odai[k] = z1
            oddd[k] = jnp.zeros((1, UC), F32)
            zw = jnp.zeros((UC, SC), F32)
            abr[k] = zw
            abi[k] = zw
            acr[k] = zw
            aci[k] = zw

        u = u_ref[...]
        dy = dq_ref[...] * _gelu_grad(y_ref[...])
        oddd[k] += jnp.sum(dy * u, axis=0, keepdims=True)
        pm_v = pm_ref[...]
        ub = _dot_nn(pm_v, u.astype(BF16)).astype(BF16)
        dyb = _dot_nn(pm_v, dy.astype(BF16)).astype(BF16)
        s0r, s0i = sr_ref[...], si_ref[...]
        _store_planes(hr, HDR, _dot_nn(ub, bdr[...]))
        _store_planes(hi, HDR, _dot_nn(ub, bdi[...]))
        e_r, e_i, _, _ = _scan_seg(hr, hi, HDR, tb, a_r, a_i, pf_r.at[k], pf_i.at[k], s0r, s0i, reverse=False)
        for c in range(NP):
            hr[c, pl.ds(0, HDR), :] = e_r[c]
            hi[c, pl.ds(0, HDR), :] = e_i[c]
        _store_planes(gr, 0, _dot_nn(dyb, ctr[...]))
        _store_planes(gi, 0, -_dot_nn(dyb, cti[...]))
        _, _, g0r, g0i = _scan_seg(gr, gi, 0, tb, a_r, a_i, pr_r.at[k], pr_i.at[k], kr[k], ki[k], reverse=True)
        kr[k] = g0r
        ki[k] = g0i

        def slab(j, acc):
            o = pl.multiple_of(j * SUBLANES, SUBLANES)
            out = []
            for c in range(NP):
                p_r, p_i = hr[c, pl.ds(o, SUBLANES), :], hi[c, pl.ds(o, SUBLANES), :]
                g_r, g_i = gr[c, pl.ds(o, SUBLANES), :], gi[c, pl.ds(o, SUBLANES), :]
                out += [acc[2 * c] + g_r * p_r + g_i * p_i, acc[2 * c + 1] + g_i * p_r - g_r * p_i]
            return tuple(out)

        z8 = jnp.zeros((SUBLANES, LANES), F32)
        acc = lax.fori_loop(0, S, slab, (z8,) * (2 * NP), unroll=2)
        odar[k] += jnp.concatenate([jnp.sum(acc[2 * c], axis=0, keepdims=True) for c in range(NP)], axis=1)
        odai[k] += jnp.concatenate([jnp.sum(acc[2 * c + 1], axis=0, keepdims=True) for c in range(NP)], axis=1)
        g_rb = _load_planes(gr, 0, tb).astype(BF16)
        g_ib = _load_planes(gi, 0, tb).astype(BF16)
        h_rb = _load_planes(hr, HDR, tb).astype(BF16)
        h_ib = _load_planes(hi, HDR, tb).astype(BF16)
        dus = _dot_nt(g_rb, bdr[...]) + _dot_nt(g_ib, bdi[...])
        pmt_v = pmt_ref[...]
        du = sum(_dot_nn(pmt_v, part) for part in _split2(dus)) + dd_ref[...] * dy
        du_ref[...] = du.astype(BF16)
        abr[k] += _dot_tn(ub, g_rb)
        abi[k] += _dot_tn(ub, g_ib)
        acr[k] += _dot_tn(dyb, h_rb)
        aci[k] -= _dot_tn(dyb, h_ib)

        @pl.when(jnp.logical_and(b == NB - 1, k == KB - 1))
        def _():
            pltpu.sync_copy(abr, obr)
            pltpu.sync_copy(abi, obi)
            pltpu.sync_copy(acr, ocr)
            pltpu.sync_copy(aci, oci)

    row, wsp, vsc, vuc, st = _s5_specs(tb, UC, SC, rev_nb=NB)
    psp = pl.BlockSpec((tb, tb), lambda b, k: (0, 0))
    hbm = pl.BlockSpec(memory_space=pltpu.HBM)
    full_sc = pl.BlockSpec((KB, 1, SC), lambda b, k: (0, 0, 0))
    full_uc = pl.BlockSpec((KB, 1, UC), lambda b, k: (0, 0, 0))
    wsh = jax.ShapeDtypeStruct((KB, UC, SC), F32)
    acc = pltpu.VMEM((KB, UC, SC), F32)
    return pl.pallas_call(
        body, grid=(NB, KB),
        in_specs=[row, row, row, psp, psp, st, st, wsp, wsp, wsp, wsp, vsc, vsc, vuc],
        out_specs=[row, hbm, hbm, hbm, hbm, full_sc, full_sc, full_uc],
        out_shape=[jax.ShapeDtypeStruct((T, E), BF16), wsh, wsh, wsh, wsh,
                   jax.ShapeDtypeStruct((KB, 1, SC), F32), jax.ShapeDtypeStruct((KB, 1, SC), F32),
                   jax.ShapeDtypeStruct((KB, 1, UC), F32)],
        scratch_shapes=[pltpu.VMEM((NP, tb + HDR, LANES), F32), pltpu.VMEM((NP, tb + HDR, LANES), F32), pln, pln,
                        pltpu.VMEM((KB, 1, SC), F32), pltpu.VMEM((KB, 1, SC), F32), acc, acc, acc, acc,
                        pw, pw, pw, pw],
        name=name, compiler_params=_params(("arbitrary", "arbitrary"), VMEM_BIG),
    )(uu, y, dq, pm, pmt, st_r, st_i, bd_r, bd_i, ct_r, ct_i, ab_r, ab_i, dd)


def glu_fwd(h, pg, *, name):
    T, D = h.shape
    tm = _pick(T, ROW_TILES)

    def body(h_ref, a_ref, b_ref, o_ref):
        o_ref[...] = h_ref[...] + a_ref[...] * _sigmoid(b_ref[...])

    row = pl.BlockSpec((tm, D), lambda i: (i, 0))
    return pl.pallas_call(
        body, grid=(T // tm,), in_specs=[row, row, pl.BlockSpec((tm, D), lambda i: (i, 1))], out_specs=row,
        out_shape=jax.ShapeDtypeStruct((T, D), F32), name=name,
        compiler_params=_params(("parallel",), VMEM_MID))(h, pg, pg)


def glu_bwd(d, pg, *, name):
    T, D = d.shape
    tm = _pick(T, ROW_TILES)

    def body(d_ref, a_ref, b_ref, o_ref):
        dv = d_ref[...]
        sg = _sigmoid(b_ref[...])
        da = dv * sg
        db = dv * a_ref[...] * sg * (1.0 - sg)
        o_ref[...] = jnp.where(pl.program_id(1) == 0, da, db).astype(BF16)

    row = pl.BlockSpec((tm, D), lambda i, hf: (i, 0))
    return pl.pallas_call(
        body, grid=(T // tm, 2), in_specs=[row, row, pl.BlockSpec((tm, D), lambda i, hf: (i, 1))],
        out_specs=pl.BlockSpec((tm, D), lambda i, hf: (i, hf)),
        out_shape=jax.ShapeDtypeStruct((T, 2 * D), BF16), name=name,
        compiler_params=_params(("parallel", "arbitrary"), VMEM_MID))(d, pg, pg)


def _shift_down(x, halo, s):
    r = pltpu.roll(x, s, axis=0)
    hr = pltpu.roll(halo, s, axis=0)
    row = lax.broadcasted_iota(jnp.int32, halo.shape, 0)
    head = jnp.where(row < s, hr, r[:SUBLANES])
    return jnp.concatenate([head, r[SUBLANES:]], axis=0)


def _shift_up(x, halo, s):
    n = x.shape[0]
    r = pltpu.roll(x, n - s, axis=0)
    hr = pltpu.roll(halo, SUBLANES - s, axis=0)
    row = lax.broadcasted_iota(jnp.int32, halo.shape, 0)
    tail = jnp.where(row >= SUBLANES - s, hr, r[n - SUBLANES:])
    return jnp.concatenate([r[:n - SUBLANES], tail], axis=0)


def _conv_acc(z, zh, w, b, first):
    kw = w.shape[0]
    zh = jnp.where(first, 0.0, zh)
    acc = b + w[kw - 1:kw] * z
    shifted = []
    for k in range(kw - 1):
        zs = _shift_down(z, zh, kw - 1 - k)
        shifted.append(zs)
        acc = acc + w[k:k + 1] * zs
    return acc, shifted


def _conv_specs(T, F, tm, tc, KW):
    nfb = F // tc
    rb = tm // SUBLANES

    def main(off):
        return pl.BlockSpec((tm, tc), lambda i, c: (i, c + off))

    def halo(off):
        return pl.BlockSpec((SUBLANES, tc), lambda i, c: (jnp.maximum(i * rb - 1, 0), c + off))

    def wspec(off):
        return pl.BlockSpec((None, KW, tc), lambda i, c: (c + off, 0, 0))

    def bspec(off):
        return pl.BlockSpec((1, tc), lambda i, c: (0, c + off))

    return nfb, main, halo, wspec, bspec


def convglu_fwd(z, cw, cb, *, name):
    T, F2 = z.shape
    F = F2 // 2
    _, KW, tc = cw.shape
    tm = _pick(T, (256, 128))
    nfb, main, halo, wspec, bspec = _conv_specs(T, F, tm, tc, KW)

    def body(zg, zgh, zv, zvh, wg, wv, bg, bv, o_ref):
        first = pl.program_id(0) == 0
        g, _ = _conv_acc(zg[...], zgh[...], wg[...], bg[...], first)
        v, _ = _conv_acc(zv[...], zvh[...], wv[...], bv[...], first)
        o_ref[...] = (g * _sigmoid(g) * v).astype(BF16)

    return pl.pallas_call(
        body, grid=(T // tm, nfb),
        in_specs=[main(0), halo(0), main(nfb), halo(nfb), wspec(0), wspec(nfb), bspec(0), bspec(nfb)],
        out_specs=pl.BlockSpec((tm, tc), lambda i, c: (i, c)),
        out_shape=jax.ShapeDtypeStruct((T, F), BF16), name=name,
        compiler_params=_params(("parallel", "parallel"), VMEM_BIG))(z, z, z, z, cw, cw, cb, cb)


def convglu_bwd_acc(z, da, cw, cb, *, name):
    T, F2 = z.shape
    F = F2 // 2
    _, KW, tc = cw.shape
    tm = _pick(T, (256, 128))
    nfb, main, halo, wspec, bspec = _conv_specs(T, F, tm, tc, KW)

    def body(zg, zgh, zv, zvh, wg, wv, bg, bv, da_ref, o_ref, dwg, dwv, dbg, dbv):
        i = pl.program_id(1)
        first = i == 0

        @pl.when(first)
        def _():
            for o in (dwg, dwv, dbg, dbv):
                o[...] = jnp.zeros_like(o)

        zg_v, zv_v = zg[...], zv[...]
        g, sg_ = _conv_acc(zg_v, zgh[...], wg[...], bg[...], first)
        v, sv_ = _conv_acc(zv_v, zvh[...], wv[...], bv[...], first)
        d = da_ref[...]
        sig = _sigmoid(g)
        dg = d * v * sig * (1.0 + g * (1.0 - sig))
        dv = d * g * sig
        o_ref[0] = dg.astype(BF16)
        o_ref[1] = dv.astype(BF16)
        dbg[...] += jnp.sum(dg, axis=0, keepdims=True)
        dbv[...] += jnp.sum(dv, axis=0, keepdims=True)
        for k in range(KW):
            xg = zg_v if k == KW - 1 else sg_[k]
            xv = zv_v if k == KW - 1 else sv_[k]
            dwg[pl.ds(k, 1), :] += jnp.sum(dg * xg, axis=0, keepdims=True)
            dwv[pl.ds(k, 1), :] += jnp.sum(dv * xv, axis=0, keepdims=True)

    def sw(spec_fn, off):
        s = spec_fn(off)
        return pl.BlockSpec(s.block_shape, lambda c, i, f=s.index_map: f(i, c))

    both = jax.ShapeDtypeStruct((2, T, F), BF16)
    dwsh = jax.ShapeDtypeStruct((nfb, KW, tc), F32)
    dbsh = jax.ShapeDtypeStruct((1, F), F32)
    outs = pl.pallas_call(
        body, grid=(nfb, T // tm),
        in_specs=[sw(main, 0), sw(halo, 0), sw(main, nfb), sw(halo, nfb), sw(wspec, 0), sw(wspec, nfb),
                  sw(bspec, 0), sw(bspec, nfb), pl.BlockSpec((tm, tc), lambda c, i: (i, c))],
        out_specs=[pl.BlockSpec((2, tm, tc), lambda c, i: (0, i, c)),
                   pl.BlockSpec((None, KW, tc), lambda c, i: (c, 0, 0)), pl.BlockSpec((None, KW, tc), lambda c, i: (c, 0, 0)),
                   pl.BlockSpec((1, tc), lambda c, i: (0, c)), pl.BlockSpec((1, tc), lambda c, i: (0, c))],
        out_shape=[both, dwsh, dwsh, dbsh, dbsh], name=name,
        compiler_params=_params(("parallel", "arbitrary"), VMEM_BIG))(z, z, z, z, cw, cw, cb, cb, da)
    return outs


def conv_bwd_in(dacc, cw, *, name):
    _, T, F = dacc.shape
    _, KW, tc = cw.shape
    nfb = F // tc
    tm = _pick(T, (256, 128))
    rb = tm // (2 * SUBLANES)
    last_blk = T // (2 * SUBLANES) - 1

    def body(d_ref, dn_ref, w_ref, o_ref):
        last = pl.program_id(0) == pl.num_programs(0) - 1
        d = d_ref[...].astype(F32)
        dn = jnp.where(last, 0.0, dn_ref[...].astype(F32)[:SUBLANES])
        w = w_ref[...]
        out = w[KW - 1:KW] * d
        for k in range(KW - 1):
            out = out + w[k:k + 1] * _shift_up(d, dn, KW - 1 - k)
        o_ref[...] = out.astype(BF16)

    return pl.pallas_call(
        body, grid=(T // tm, 2, nfb),
        in_specs=[pl.BlockSpec((None, tm, tc), lambda i, hf, c: (hf, i, c)),
                  pl.BlockSpec((None, 2 * SUBLANES, tc), lambda i, hf, c: (hf, jnp.minimum((i + 1) * rb, last_blk), c)),
                  pl.BlockSpec((None, KW, tc), lambda i, hf, c: (hf * nfb + c, 0, 0))],
        out_specs=pl.BlockSpec((tm, tc), lambda i, hf, c: (i, hf * nfb + c)),
        out_shape=jax.ShapeDtypeStruct((T, 2 * F), BF16), name=name,
        compiler_params=_params(("parallel", "parallel", "parallel"), VMEM_BIG))(dacc, dacc, cw)


def _my_place():
    x, y, c = (lax.axis_index(a) for a in AXES)
    return x, y, c, 4 * x + 2 * y + c


def _peer(m, x, y, c):
    px = 1 - x if (m >> 2) & 1 else x
    py = 1 - y if (m >> 1) & 1 else y
    pc = 1 - c if m & 1 else c
    return (px, py, pc), 4 * px + 2 * py + pc


def _exchange(ins, out_shapes, plan, *, bcast, name):
    n_in, n_out, n = len(ins), len(out_shapes), len(plan)

    def body(*refs):
        in_refs, out_refs = refs[:n_in], refs[n_in:n_in + n_out]
        send_sems, recv_sems, loc_sems = refs[n_in + n_out:]
        x, y, c, me = _my_place()

        def src(f, who):
            r = in_refs[plan[f][0]]
            return r if bcast else r.at[who]

        def dst(f, who):
            r = out_refs[plan[f][1]]
            lay = plan[f][2]
            return r.at[who] if lay is None else r.at[lay, who]

        def remote(f, m, landing):
            dev, plin = _peer(m, x, y, c)
            return pltpu.make_async_remote_copy(
                src_ref=src(f, plin), dst_ref=dst(f, plin if landing else me), send_sem=send_sems.at[f, m - 1],
                recv_sem=recv_sems.at[f, m - 1], device_id=dev, device_id_type=MESH)

        locs = [pltpu.make_async_copy(src(f, me), dst(f, me), loc_sems.at[f]) for f in range(n)]
        for cp in locs:
            cp.start()
        for m in range(1, N_DEV):
            for f in range(n):
                remote(f, m, False).start()
        for m in range(1, N_DEV):
            for f in range(n):
                remote(f, m, True).wait()
        for cp in locs:
            cp.wait()

    hbm = pl.BlockSpec(memory_space=pltpu.HBM)
    return pl.pallas_call(
        body, in_specs=[hbm] * n_in, out_specs=[hbm] * n_out, out_shape=out_shapes,
        scratch_shapes=[pltpu.SemaphoreType.DMA((n, N_DEV - 1)), pltpu.SemaphoreType.DMA((n, N_DEV - 1)),
                        pltpu.SemaphoreType.DMA((n,))],
        name=name)(*ins)


def all_gather(shards, *, name):
    outs = [jax.ShapeDtypeStruct((N_DEV,) + s.shape, s.dtype) for s in shards]
    return _exchange(shards, outs, [(a, a, None) for a in range(len(shards))], bcast=True, name=name)


_HBM = pl.BlockSpec(memory_space=pltpu.HBM)
_SEM = pl.BlockSpec(memory_space=pltpu.SEMAPHORE)
_EFFECT = pltpu.SideEffectType.DATAFLOW_SIDE_EFFECTING


def _split_copy(in_refs, land_refs, send_sems, recv_sems, bcast, f, m, place, landing):
    x, y, c, me = place
    dev, plin = _peer(m, x, y, c)
    src = in_refs[f] if bcast else in_refs[f].at[plin]
    return pltpu.make_async_remote_copy(
        src_ref=src, dst_ref=land_refs[f].at[plin if landing else me],
        send_sem=send_sems.at[f * (N_DEV - 1) + m - 1], recv_sem=recv_sems.at[f * (N_DEV - 1) + m - 1],
        device_id=dev, device_id_type=MESH)


def _own_copy(in_refs, land_refs, own_sems, bcast, f, place):
    me = place[3]
    return pltpu.make_async_copy(in_refs[f] if bcast else in_refs[f].at[me], land_refs[f].at[me], own_sems.at[f])


def exchange_start(ins, *, bcast, dep=None, name):
    n = len(ins)
    lands = [lax.empty(((N_DEV,) + a.shape) if bcast else a.shape, a.dtype) for a in ins]
    deps = [] if dep is None else [dep]
    nd = len(deps)

    def body(*refs):
        in_refs, land_refs = refs[:n], refs[n:2 * n]
        send_sems, recv_sems, own_sems = refs[2 * n + nd:2 * n + nd + 3]
        token = refs[-1]
        place = _my_place()
        for m in range(1, N_DEV):
            for f in range(n):
                _split_copy(in_refs, land_refs, send_sems, recv_sems, bcast, f, m, place, False).start()
        for f in range(n):
            _own_copy(in_refs, land_refs, own_sems, bcast, f, place).start()
        token[...] = jnp.zeros_like(token)

    arrs = [pltpu.with_memory_space_constraint(a, pltpu.HBM) for a in (*ins, *lands)]
    sems = pltpu.SemaphoreType.DMA((n * (N_DEV - 1),))
    outs = pl.pallas_call(
        body, name=name,
        out_shape=(sems, sems, pltpu.SemaphoreType.DMA((n,)), *[pltpu.HBM(a.shape, a.dtype) for a in arrs],
                   jax.ShapeDtypeStruct((SUBLANES, LANES), F32)),
        in_specs=[_HBM] * (2 * n) + [_DEP] * nd,
        out_specs=(_SEM, _SEM, _SEM, *[_HBM] * (2 * n), pl.BlockSpec(memory_space=pltpu.VMEM)),
        input_output_aliases={i: 3 + i for i in range(2 * n)},
        compiler_params=pltpu.CompilerParams(has_side_effects=_EFFECT))(*arrs, *deps)
    return outs[0], outs[1], outs[2], list(outs[3:3 + 2 * n]), outs[-1]


def exchange_wait(started, after, *, bcast, name):
    send_sems, recv_sems, own_sems, thrus, _ = started
    n = len(thrus) // 2

    def body(*refs):
        in_refs, land_refs = refs[:n], refs[n:2 * n]
        send, recv, own = refs[2 * n:2 * n + 3]
        place = _my_place()
        for f in range(n):
            _own_copy(in_refs, land_refs, own, bcast, f, place).wait()
        for m in range(1, N_DEV):
            for f in range(n):
                cp = _split_copy(in_refs, land_refs, send, recv, bcast, f, m, place, True)
                cp.wait_send()
                cp.wait_recv()

    outs = pl.pallas_call(
        body, name=name, out_shape=[pltpu.HBM(a.shape, a.dtype) for a in thrus],
        in_specs=[_HBM] * (2 * n) + [_SEM, _SEM, _SEM, pl.BlockSpec(memory_space=pl.ANY)], out_specs=[_HBM] * (2 * n),
        input_output_aliases={i: i for i in range(2 * n)},
        compiler_params=pltpu.CompilerParams(has_side_effects=_EFFECT))(
            *thrus, send_sems, recv_sems, own_sems, after)
    return list(outs[n:])


def _adamw(w, g, m, v):
    m = ADAM_B1 * m + (1.0 - ADAM_B1) * g
    v = ADAM_B2 * v + (1.0 - ADAM_B2) * (g * g)
    m_hat = m / (1.0 - ADAM_B1 ** ADAM_STEP)
    v_hat = v / (1.0 - ADAM_B2 ** ADAM_STEP)
    delta = -ADAM_LR * (m_hat / (jnp.sqrt(v_hat) + ADAM_EPS) + ADAM_WD * w)
    return delta, m, v


def adam_reduce(recv, w, m, v, l, prev, *, name):
    _, R, C = recv.shape
    L = w.shape[0]
    budget = 4 * 1024 * 1024
    tr = R
    for cand in (1024, 512, 352, 256, 176, 128, 64, 32, 16):
        if R % cand == 0 and N_DEV * cand * C * recv.dtype.itemsize <= budget:
            tr = cand
            break

    def body(r_ref, w_ref, m_ref, v_ref, *rest):
        g_ref, d_ref, nm_ref, nv_ref = rest[-4:]
        g = r_ref[0].astype(F32)
        for s in range(1, N_DEV):
            g = g + r_ref[s].astype(F32)
        d, nm, nv = _adamw(w_ref[...], g, m_ref[...], v_ref[...])
        g_ref[...] = g
        d_ref[...] = d
        nm_ref[...] = nm
        nv_ref[...] = nv

    blk = pl.BlockSpec((None, tr, C), lambda r: (l, r, 0))
    sh = jax.ShapeDtypeStruct((L, R, C), F32)
    extra = [] if prev is None else list(prev)
    return pl.pallas_call(
        body, grid=(R // tr,),
        in_specs=[pl.BlockSpec((N_DEV, tr, C), lambda r: (0, r, 0)), blk, blk, blk]
        + [pl.BlockSpec(memory_space=pl.ANY)] * len(extra),
        out_specs=[blk] * 4, out_shape=[sh] * 4, name=name,
        input_output_aliases={4 + i: i for i in range(len(extra))},
        compiler_params=_params(("parallel",), VMEM_MID))(recv, w, m, v, *extra)


def sum_slots(recv, *, name):
    _, R, C = recv.shape
    tr = _pick(R, (512, 256, 128, 64, 32, 16, 8))

    def body(r_ref, o_ref):
        g = r_ref[0]
        for s in range(1, N_DEV):
            g = g + r_ref[s]
        o_ref[...] = g

    return pl.pallas_call(
        body, grid=(R // tr,), in_specs=[pl.BlockSpec((N_DEV, tr, C), lambda r: (0, r, 0))],
        out_specs=pl.BlockSpec((tr, C), lambda r: (r, 0)), out_shape=jax.ShapeDtypeStruct((R, C), F32),
        name=name, compiler_params=_params(("parallel",)))(recv)


def adam_flat(g, w, m, v, *, name):
    R, C = g.shape
    tr = _pick(R, (512, 256, 128, 64, 32, 16, 8))

    def body(g_ref, w_ref, m_ref, v_ref, d_ref, nm_ref, nv_ref):
        d, nm, nv = _adamw(w_ref[...], g_ref[...], m_ref[...], v_ref[...])
        d_ref[...] = d
        nm_ref[...] = nm
        nv_ref[...] = nv

    blk = pl.BlockSpec((tr, C), lambda r: (r, 0))
    sh = jax.ShapeDtypeStruct((R, C), F32)
    return pl.pallas_call(body, grid=(R // tr,), in_specs=[blk] * 4, out_specs=[blk] * 3, out_shape=[sh] * 3,
                          name=name, compiler_params=_params(("parallel",)))(g, w, m, v)


WEIGHTS = ("norm_mix_g", "norm_ffn_g", "a_w_in", "a_g_v", "a_w_s", "a_b_s", "a_w_out", "b_w_in", "b_a_re", "b_a_im",
           "b_log_dt", "b_b_re", "b_b_im", "b_c_re", "b_c_im", "b_d", "b_w_glu", "f_w_up", "f_conv_w", "f_conv_b",
           "f_w_down", "final_g")
BIG = ("a_w_in", "a_w_out", "b_w_in", "b_w_glu", "f_w_up", "f_w_down")
FLAT_CHUNK = N_DEV * SUBLANES * LANES


def _expand(blocks, eye):
    KB, KG, C, P = blocks.shape
    return (blocks[:, :, :, None, :] * eye[None, :, None, :, None]).reshape(KB, KG * C, KG * P)


def _diag_blocks(dense, KG, C, P, eye):
    KB = dense.shape[0]
    return jnp.einsum("kgchp,gh->kgcp", dense.reshape(KB, KG, C, KG, P), eye)


def _flatten_pack(parts):
    flat = jnp.concatenate([p.reshape(-1) for p in parts])
    pad = (-flat.shape[0]) % FLAT_CHUNK
    return jnp.pad(flat, (0, pad))


def _unpack(flat, like):
    out, o = [], 0
    for p in like:
        n = math.prod(p.shape)
        out.append(flat[o:o + n].reshape(p.shape))
        o += n
    return out


def kernel(x, norm_mix_g, norm_ffn_g, a_w_in, a_g_v, a_w_s, a_b_s, a_w_out, b_w_in, b_a_re, b_a_im, b_log_dt, b_b_re, b_b_im, b_c_re, b_c_im, b_d, b_w_glu, f_w_up, f_conv_w, f_conv_b, f_w_down, final_g, loss_target, m_norm_mix_g, m_norm_ffn_g, m_a_w_in, m_a_g_v, m_a_w_s, m_a_b_s, m_a_w_out, m_b_w_in, m_b_a_re, m_b_a_im, m_b_log_dt, m_b_b_re, m_b_b_im, m_b_c_re, m_b_c_im, m_b_d, m_b_w_glu, m_f_w_up, m_f_conv_w, m_f_conv_b, m_f_w_down, m_final_g, v_norm_mix_g, v_norm_ffn_g, v_a_w_in, v_a_g_v, v_a_w_s, v_a_b_s, v_a_w_out, v_b_w_in, v_b_a_re, v_b_a_im, v_b_log_dt, v_b_b_re, v_b_b_im, v_b_c_re, v_b_c_im, v_b_d, v_b_w_glu, v_f_w_up, v_f_conv_w, v_f_conv_b, v_f_w_down, v_final_g):
    env = dict(locals())
    W = {n: env[n] for n in WEIGHTS}
    Mo = {n: env["m_" + n] for n in WEIGHTS}
    Vo = {n: env["v_" + n] for n in WEIGHTS}

    _, T, D = x.shape
    depth = norm_mix_g.shape[0]
    E_A = a_g_v.shape[1]
    H = a_w_s.shape[1]
    G, P, C = b_b_re.shape[1], b_b_re.shape[2], b_b_re.shape[3]
    E_B = G * C
    KG = S5_KG
    KB = G // KG
    F2 = f_conv_b.shape[1]
    tb = _pick(T, (512, 256, 128))
    pm, pmt = _perm_matrices(tb)
    eye = jnp.eye(KG, dtype=F32)
    _, _, _, me = _my_place()

    wb = {n: W[n].astype(BF16) for n in BIG}

    def shards(i):
        j = i // 2
        if i % 2 == 0:
            return [wb["a_w_in"][j], wb["a_w_out"][j], wb["f_w_up"][i], wb["f_w_down"][i], f_conv_w[i]]
        return [wb["b_w_in"][j], wb["b_w_glu"][j], b_d[j][None], wb["f_w_up"][i], wb["f_w_down"][i], f_conv_w[i]]

    h = x[0]
    started = exchange_start(shards(0), bcast=True, name="gather_start0")
    saved = []
    for i in range(depth):
        j = i // 2
        s = {}
        gathered = exchange_wait(started, h, bcast=True, name=f"gather_wait{i}")
        dep = None
        if i + 1 < depth:
            started = exchange_start(shards(i + 1), bcast=True, dep=gathered[0], name=f"gather_start{i + 1}")
            dep = started[-1]
        if i % 2 == 0:
            g_in, g_out, g_up, g_dn, g_cw = gathered
            s["w_in"], s["w_out"] = g_in, g_out.reshape(E_A, D)
        else:
            g_in, g_glu, g_dd, g_up, g_dn, g_cw = gathered
            s["w_in"], s["w_glu"] = g_in.reshape(D, E_B), g_glu
            s["dd"] = g_dd.reshape(KB, 1, KG * C)
        s["w_up"], s["w_dn"], s["cw"] = g_up, g_dn.reshape(F2 // 2, D), g_cw
        s["h"] = h
        s["hn"] = rms_fwd(h, norm_mix_g[i][None], dep=dep, name=f"rms_mix{i}")
        if i % 2 == 0:
            s["p"] = mm_nn(s["hn"], s["w_in"], name=f"a_in{i}")
            s["bexp"] = jnp.repeat(a_b_s[j].T, E_A // H, axis=1)
            s["us"] = sgu_fwd(s["p"], a_g_v[j][None], a_w_s[j], s["bexp"], name=f"sgu_fwd{i}")
            h_mid = mm_nn(s["us"], s["w_out"], res=h, name=f"a_out{i}")
        else:
            s["uu"] = mm_nn(s["hn"], s["w_in"], name=f"b_in{i}")
            s["prm"] = (b_a_re[j], b_a_im[j], b_log_dt[j][:, None],
                        b_b_re[j].transpose(2, 0, 1), b_b_im[j].transpose(2, 0, 1))
            ar, ai, bbr, bbi = s5_disc_fwd(*s["prm"], name=f"s5_disc{i}")
            to_blocks = lambda t: t.reshape(C, KB, KG, P).transpose(1, 2, 0, 3)
            s["bd_r"] = _expand(to_blocks(bbr), eye).astype(BF16)
            s["bd_i"] = _expand(to_blocks(bbi), eye).astype(BF16)
            s["ct_r"] = _expand(b_c_re[j].reshape(KB, KG, C, P), eye).astype(BF16)
            s["ct_i"] = _expand(b_c_im[j].reshape(KB, KG, C, P), eye).astype(BF16)
            s["ab_r"], s["ab_i"] = ar.reshape(KB, 1, KG * P), ai.reshape(KB, 1, KG * P)
            s["y"], s["q"], s["st_r"], s["st_i"] = s5_fwd(
                s["uu"], pm, pmt, s["bd_r"], s["bd_i"], s["ct_r"], s["ct_i"], s["ab_r"], s["ab_i"], s["dd"],
                tb=tb, name=f"s5_fwd{i}")
            s["pg"] = mm_nn(s["q"], s["w_glu"], name=f"b_glu{i}")
            h_mid = glu_fwd(h, s["pg"], name=f"glu_fwd{i}")
        s["h_mid"] = h_mid
        s["hn2"] = rms_fwd(h_mid, norm_ffn_g[i][None], name=f"rms_ffn{i}")
        s["z"] = mm_nn(s["hn2"], s["w_up"], name=f"f_up{i}")
        s["a"] = convglu_fwd(s["z"], s["cw"], f_conv_b[i][None], name=f"convglu_fwd{i}")
        h = mm_nn(s["a"], s["w_dn"], res=h_mid, name=f"f_down{i}")
        saved.append(s)

    loss_tile, dh, dhb, dg_final = loss_head(h, final_g[None], loss_target[0], name="loss_head")
    loss = lax.psum(loss_tile[0, 0], AXES)

    gbig = {n: [None] * W[n].shape[0] for n in BIG}
    gs = {n: [None] * W[n].shape[0] for n in WEIGHTS if n not in BIG and n != "final_g"}
    pending = []
    for i in reversed(range(depth)):
        j = i // 2
        s = saved[i]
        gbig["f_w_down"][i] = mm_tn(s["a"], dhb, blocks=1, name=f"g_down{i}").reshape(N_DEV, F2 // 2 // N_DEV, D)
        da = mm_nt(dhb, s["w_dn"], name=f"d_a{i}")
        dacc, dwg, dwv, dbg, dbv = convglu_bwd_acc(s["z"], da, s["cw"], f_conv_b[i][None], name=f"convglu_bwd{i}")
        gs["f_conv_w"][i] = jnp.concatenate([dwg, dwv], axis=0)
        gs["f_conv_b"][i] = jnp.concatenate([dbg, dbv], axis=1)[0]
        dz = conv_bwd_in(dacc, s["cw"], name=f"conv_bwd_in{i}")
        gbig["f_w_up"][i] = mm_tn(s["hn2"], dz, blocks=N_DEV, name=f"g_up{i}")
        dhn2 = mm_nt(dz, s["w_up"], name=f"d_hn2{i}")
        st = exchange_start([gbig["f_w_down"][i], gbig["f_w_up"][i]], bcast=False, name=f"xchg_ffn_start{i}")
        pending.append((st, [("f_w_down", i), ("f_w_up", i)], f"ffn{i}"))
        dh_mid, dmb, dg = rms_bwd(s["h_mid"], norm_ffn_g[i][None], dhn2, dh, dep=st[-1], name=f"rms_ffn_bwd{i}")
        gs["norm_ffn_g"][i] = dg[0]
        if i % 2 == 0:
            gbig["a_w_out"][j] = mm_tn(s["us"], dmb, blocks=1, name=f"g_aout{i}").reshape(N_DEV, E_A // N_DEV, D)
            d_us = mm_nt(dmb, s["w_out"], name=f"d_us{i}")
            dp, dws, dbt, dgv = sgu_bwd(s["p"], d_us, a_g_v[j][None], a_w_s[j], s["bexp"], name=f"sgu_bwd{i}")
            gs["a_w_s"][j], gs["a_b_s"][j], gs["a_g_v"][j] = dws, dbt[:, :H].T, dgv[0]
            gbig["a_w_in"][j] = mm_tn(s["hn"], dp, blocks=N_DEV, name=f"g_ain{i}")
            dhn = mm_nt(dp, s["w_in"], name=f"d_hn_a{i}")
        else:
            dpg = glu_bwd(dh_mid, s["pg"], name=f"glu_bwd{i}")
            gbig["b_w_glu"][j] = mm_tn(s["q"], dpg, blocks=N_DEV, name=f"g_glu{i}")
            dq = mm_nt(dpg, s["w_glu"], name=f"d_q{i}")
            duu, dbr, dbi, dcr, dci, dar, dai, ddd = s5_bwd(
                s["uu"], s["y"], dq, pm, pmt, s["st_r"], s["st_i"], s["bd_r"], s["bd_i"], s["ct_r"], s["ct_i"],
                s["ab_r"], s["ab_i"], s["dd"], tb=tb, name=f"s5_bwd{i}")
            from_blocks = lambda t: _diag_blocks(t, KG, C, P, eye).transpose(2, 0, 1, 3).reshape(C, G, P)
            d_are, d_aim, d_ldt, d_bre, d_bim = s5_disc_bwd(
                *s["prm"], dar.reshape(G, P), dai.reshape(G, P), from_blocks(dbr), from_blocks(dbi),
                name=f"s5_disc_bwd{i}")
            gs["b_a_re"][j], gs["b_a_im"][j], gs["b_log_dt"][j] = d_are, d_aim, d_ldt[:, 0]
            gs["b_b_re"][j], gs["b_b_im"][j] = d_bre.transpose(1, 2, 0), d_bim.transpose(1, 2, 0)
            gs["b_c_re"][j] = _diag_blocks(dcr, KG, C, P, eye).reshape(G, C, P)
            gs["b_c_im"][j] = _diag_blocks(dci, KG, C, P, eye).reshape(G, C, P)
            gs["b_d"][j] = ddd.reshape(E_B)
            gbig["b_w_in"][j] = mm_tn(s["hn"], duu, blocks=1, name=f"g_bin{i}").reshape(N_DEV, D // N_DEV, E_B)
            dhn = mm_nt(duu, s["w_in"], name=f"d_hn_b{i}")
        mix = ("a_w_out", "a_w_in") if i % 2 == 0 else ("b_w_glu", "b_w_in")
        st = exchange_start([gbig[n][j] for n in mix], bcast=False, name=f"xchg_mix_start{i}")
        pending.append((st, [(n, j) for n in mix], f"mix{i}"))
        dh, dhb, dg = rms_bwd(s["h"], norm_mix_g[i][None], dhn, dh_mid, dep=st[-1], name=f"rms_mix_bwd{i}")
        gs["norm_mix_g"][i] = dg[0]
    grad_x = dh[None]

    grads, deltas, new_m, new_v = {}, {}, {}, {}
    small = [n for n in WEIGHTS if n not in BIG]
    full = {n: (dg_final[0] if n == "final_g" else jnp.stack(gs[n])) for n in small}
    flat = _flatten_pack([full[n] for n in small])
    rows = flat.shape[0] // N_DEV // LANES
    st_small = exchange_start([flat.reshape(N_DEV, rows, LANES)], bcast=False, name="xchg_small_start")

    done = {n: None for n in BIG}
    after = st_small[-1]
    for st, items, tag in pending:
        recvs = exchange_wait(st, after, bcast=False, name=f"xchg_wait_{tag}")
        for (n, l), recv in zip(items, recvs):
            done[n] = adam_reduce(recv, W[n], Mo[n], Vo[n], l, done[n], name=f"adam_{n}{l}")
            after = done[n][0]
    for n in BIG:
        grads[n], deltas[n], new_m[n], new_v[n] = done[n]

    recv = exchange_wait(st_small, after, bcast=False, name="xchg_small_wait")[0]
    part = sum_slots(recv, name="sum_small")
    tot = all_gather([part], name="gather_small")[0].reshape(-1)
    red = dict(zip(small, _unpack(tot, [full[n] for n in small])))
    red["f_conv_w"] = lax.dynamic_index_in_dim(red["f_conv_w"], me, axis=1, keepdims=False)
    red["b_d"] = lax.dynamic_slice_in_dim(red["b_d"], me * (E_B // N_DEV), E_B // N_DEV, axis=1)
    gflat = _flatten_pack([red[n] for n in small]).reshape(-1, LANES)
    d_f, m_f, v_f = adam_flat(
        gflat, _flatten_pack([W[n] for n in small]).reshape(-1, LANES),
        _flatten_pack([Mo[n] for n in small]).reshape(-1, LANES),
        _flatten_pack([Vo[n] for n in small]).reshape(-1, LANES), name="adam_small")
    like = [W[n] for n in small]
    for n, d_, m_, v_ in zip(small, _unpack(d_f.reshape(-1), like), _unpack(m_f.reshape(-1), like),
                             _unpack(v_f.reshape(-1), like)):
        grads[n], deltas[n], new_m[n], new_v[n] = red[n], d_, m_, v_

    return (loss, grad_x, *[grads[n] for n in WEIGHTS], *[deltas[n] for n in WEIGHTS],
            *[new_m[n] for n in WEIGHTS], *[new_v[n] for n in WEIGHTS])
```

```python
import math

import jax
import jax.numpy as jnp
from jax import lax
from jax.experimental import pallas as pl
from jax.experimental.pallas import tpu as pltpu

F32 = jnp.float32
BF16 = jnp.bfloat16
N_DEV = 8
AXES = ("x", "y", "c")
EPS = 1e-6
LANES = 128
SUBLANES = 8
VMEM_BIG = 56 * 1024 * 1024
VMEM_MID = 40 * 1024 * 1024
ADAM_LR, ADAM_B1, ADAM_B2, ADAM_EPS, ADAM_WD, ADAM_STEP = 0.001, 0.9, 0.999, 1e-08, 0.01, 10
MESH = pl.DeviceIdType.MESH
GELU_C = math.sqrt(2.0 / math.pi)
GELU_K = 0.044715


def _pick(n, prefs):
    for p in prefs:
        if n % p == 0:
            return p
    return n


def _params(sem, vmem=None):
    return pltpu.CompilerParams(dimension_semantics=sem, vmem_limit_bytes=vmem)


def _gelu(x):
    return 0.5 * x * (1.0 + jnp.tanh(GELU_C * (x + GELU_K * x * x * x)))


def _gelu_grad(x):
    x2 = x * x
    th = jnp.tanh(GELU_C * x * (1.0 + GELU_K * x2))
    return 0.5 * (1.0 + th) + 0.5 * x * (1.0 - th * th) * GELU_C * (1.0 + 3.0 * GELU_K * x2)


def _sigmoid(x):
    return 1.0 / (1.0 + jnp.exp(-x))


def _dot_nn(a, b):
    return jnp.dot(a, b, preferred_element_type=F32)


def _dot_nt(a, b):
    return lax.dot_general(a, b, (((1,), (1,)), ((), ())), preferred_element_type=F32)


def _dot_tn(a, b):
    return lax.dot_general(a, b, (((0,), (0,)), ((), ())), preferred_element_type=F32)


M_TILES = (1024, 512, 256, 128)
TN_M_TILES = (2048, 1024, 512, 256, 128)
FULL_K = 2048
NT_SPAN = 2816
N_TILES = (1408, 1024, 512, 384, 256, 128)
K_TILES = (1408, 1024, 512, 384, 256, 128)


def _as3(b):
    return b if b.ndim == 3 else b[None]


def mm_nn(a, b, *, res=None, out_dtype=F32, name):
    b3 = _as3(b)
    M, K = a.shape
    J, _, nb = b3.shape
    tm, tn = _pick(M, M_TILES), _pick(nb, N_TILES)
    tk = K if K <= FULL_K else _pick(K, (2816,) + K_TILES)
    per, nk = nb // tn, K // tk

    def body(*refs):
        if res is None:
            a_ref, b_ref, o_ref, acc = refs
        else:
            a_ref, b_ref, r_ref, o_ref, acc = refs
        k = pl.program_id(2)
        if nk == 1:
            r = _dot_nn(a_ref[...], b_ref[...])
            if res is not None:
                r = r + r_ref[...]
            o_ref[...] = r.astype(out_dtype)
            return

        @pl.when(k == 0)
        def _():
            acc[...] = jnp.zeros_like(acc)

        acc[...] += _dot_nn(a_ref[...], b_ref[...])

        @pl.when(k == nk - 1)
        def _():
            r = acc[...]
            if res is not None:
                r = r + r_ref[...]
            o_ref[...] = r.astype(out_dtype)

    in_specs = [pl.BlockSpec((tm, tk), lambda i, n, k: (i, k)),
                pl.BlockSpec((None, tk, tn), lambda i, n, k: (n // per, k, n % per))]
    args = [a, b3]
    if res is not None:
        in_specs.append(pl.BlockSpec((tm, tn), lambda i, n, k: (i, n)))
        args.append(res)
    return pl.pallas_call(
        body, grid=(M // tm, J * per, nk), in_specs=in_specs,
        out_specs=pl.BlockSpec((tm, tn), lambda i, n, k: (i, n)),
        out_shape=jax.ShapeDtypeStruct((M, J * nb), out_dtype),
        scratch_shapes=[pltpu.VMEM((tm, tn) if nk > 1 else (SUBLANES, LANES), F32)], name=name,
        compiler_params=_params(("parallel", "parallel", "arbitrary"), VMEM_BIG))(*args)


def mm_nt(dy, b, *, out_dtype=F32, name):
    b3 = _as3(b)
    M, N = dy.shape
    J, K, nb = b3.shape
    tm = _pick(M, M_TILES)
    tn = nb if (J == 1 and nb <= FULL_K) else _pick(nb, N_TILES)
    grp = max([g for g in (4, 2, 1) if J % g == 0 and g * tn <= NT_SPAN]) if tn == nb else 1
    tk = _pick(K, K_TILES) if (grp > 1 or K > FULL_K) else K
    per, nn = nb // tn, (J * nb) // (tn * grp)

    def body(d_ref, b_ref, o_ref, acc):
        if grp > 1:
            r = sum(_dot_nt(d_ref[:, g * tn:(g + 1) * tn], b_ref[g]) for g in range(grp))
        else:
            r = _dot_nt(d_ref[...], b_ref[...])
        if nn == 1:
            o_ref[...] = r.astype(out_dtype)
            return
        n = pl.program_id(2)

        @pl.when(n == 0)
        def _():
            acc[...] = jnp.zeros_like(acc)

        acc[...] += r

        @pl.when(n == nn - 1)
        def _():
            o_ref[...] = acc[...].astype(out_dtype)

    if grp > 1:
        b_spec = pl.BlockSpec((grp, tk, tn), lambda i, k, n: (n, k, 0))
    else:
        b_spec = pl.BlockSpec((None, tk, tn), lambda i, k, n: (n // per, k, n % per))
    return pl.pallas_call(
        body, grid=(M // tm, K // tk, nn),
        in_specs=[pl.BlockSpec((tm, tn * grp), lambda i, k, n: (i, n)), b_spec],
        out_specs=pl.BlockSpec((tm, tk), lambda i, k, n: (i, k)),
        out_shape=jax.ShapeDtypeStruct((M, K), out_dtype),
        scratch_shapes=[pltpu.VMEM((tm, tk) if nn > 1 else (SUBLANES, LANES), F32)], name=name,
        compiler_params=_params(("parallel", "parallel", "arbitrary"), VMEM_BIG))(dy, b3)


def mm_tn(x, dy, *, blocks, out_dtype=BF16, name):
    M, K = x.shape
    _, N = dy.shape
    nb = N // blocks
    tm, tn, tk = _pick(M, TN_M_TILES), _pick(nb, N_TILES), _pick(K, K_TILES)
    per, nm = nb // tn, M // tm

    def body(x_ref, d_ref, o_ref, acc):
        m = pl.program_id(2)

        @pl.when(m == 0)
        def _():
            acc[...] = jnp.zeros_like(acc)

        acc[...] += _dot_tn(x_ref[...], d_ref[...])

        @pl.when(m == nm - 1)
        def _():
            o_ref[...] = acc[...].astype(out_dtype)

    return pl.pallas_call(
        body, grid=(K // tk, N // tn, nm),
        in_specs=[pl.BlockSpec((tm, tk), lambda k, n, m: (m, k)),
                  pl.BlockSpec((tm, tn), lambda k, n, m: (m, n))],
        out_specs=pl.BlockSpec((None, tk, tn), lambda k, n, m: (n // per, k, n % per)),
        out_shape=jax.ShapeDtypeStruct((blocks, K, nb), out_dtype),
        scratch_shapes=[pltpu.VMEM((tk, tn), F32)], name=name,
        compiler_params=_params(("parallel", "parallel", "arbitrary"), VMEM_BIG))(x, dy)


ROW_TILES = (256, 128)


_DEP = pl.BlockSpec(memory_space=pl.ANY)


def rms_fwd(h, g, *, dep=None, name):
    T, D = h.shape
    tm = _pick(T, ROW_TILES)
    deps = [] if dep is None else [dep]

    def body(h_ref, g_ref, *rest):
        o_ref = rest[-1]
        x = h_ref[...]
        r = lax.rsqrt(jnp.mean(x * x, axis=-1, keepdims=True) + EPS)
        o_ref[...] = (x * r * g_ref[...]).astype(BF16)

    return pl.pallas_call(
        body, grid=(T // tm,),
        in_specs=[pl.BlockSpec((tm, D), lambda i: (i, 0)), pl.BlockSpec((1, D), lambda i: (0, 0))] + [_DEP] * len(deps),
        out_specs=pl.BlockSpec((tm, D), lambda i: (i, 0)),
        out_shape=jax.ShapeDtypeStruct((T, D), BF16), name=name,
        compiler_params=_params(("parallel",), VMEM_MID))(h, g, *deps)


def rms_bwd(h, g, dhn, dres, *, dep=None, name):
    T, D = h.shape
    tm = _pick(T, ROW_TILES)
    deps = [] if dep is None else [dep]

    def body(h_ref, g_ref, d_ref, r_ref, *rest):
        dh_ref, dhb_ref, dg_ref = rest[-3:]

        @pl.when(pl.program_id(0) == 0)
        def _():
            dg_ref[...] = jnp.zeros_like(dg_ref)

        x = h_ref[...]
        r = lax.rsqrt(jnp.mean(x * x, axis=-1, keepdims=True) + EPS)
        xh = x * r
        dy = d_ref[...]
        dxh = dy * g_ref[...]
        dh = r_ref[...] + r * (dxh - xh * jnp.mean(dxh * xh, axis=-1, keepdims=True))
        dh_ref[...] = dh
        dhb_ref[...] = dh.astype(BF16)
        dg_ref[...] += jnp.sum(dy * xh, axis=0, keepdims=True)

    row = pl.BlockSpec((tm, D), lambda i: (i, 0))
    vec = pl.BlockSpec((1, D), lambda i: (0, 0))
    return pl.pallas_call(
        body, grid=(T // tm,), in_specs=[row, vec, row, row] + [_DEP] * len(deps), out_specs=[row, row, vec],
        out_shape=[jax.ShapeDtypeStruct((T, D), F32), jax.ShapeDtypeStruct((T, D), BF16),
                   jax.ShapeDtypeStruct((1, D), F32)], name=name,
        compiler_params=_params(("arbitrary",), VMEM_MID))(h, g, dhn, dres, *deps)


def loss_head(h, g, tgt, *, name):
    T, D = h.shape
    tm = _pick(T, ROW_TILES)

    def body(h_ref, g_ref, t_ref, l_ref, dh_ref, dhb_ref, dg_ref):
        @pl.when(pl.program_id(0) == 0)
        def _():
            dg_ref[...] = jnp.zeros_like(dg_ref)
            l_ref[...] = jnp.zeros_like(l_ref)

        x = h_ref[...]
        gg = g_ref[...]
        r = lax.rsqrt(jnp.mean(x * x, axis=-1, keepdims=True) + EPS)
        xh = x * r
        e = xh * gg - t_ref[...]
        l_ref[...] += 0.5 * jnp.sum(jnp.mean(e * e, axis=-1, keepdims=True), axis=0, keepdims=True)
        dy = e * (1.0 / D)
        dxh = dy * gg
        dh = r * (dxh - xh * jnp.mean(dxh * xh, axis=-1, keepdims=True))
        dh_ref[...] = dh
        dhb_ref[...] = dh.astype(BF16)
        dg_ref[...] += jnp.sum(dy * xh, axis=0, keepdims=True)

    row = pl.BlockSpec((tm, D), lambda i: (i, 0))
    vec = pl.BlockSpec((1, D), lambda i: (0, 0))
    return pl.pallas_call(
        body, grid=(T // tm,), in_specs=[row, vec, row],
        out_specs=[pl.BlockSpec((SUBLANES, LANES), lambda i: (0, 0)), row, row, vec],
        out_shape=[jax.ShapeDtypeStruct((SUBLANES, LANES), F32), jax.ShapeDtypeStruct((T, D), F32),
                   jax.ShapeDtypeStruct((T, D), BF16), jax.ShapeDtypeStruct((1, D), F32)], name=name,
        compiler_params=_params(("arbitrary",), VMEM_MID))(h, g, tgt)


def _sgu_common(p, gv, w_ref, bexp, E, H, CH):
    Dg = E // H
    z = _gelu(p)
    u, v = z[:, :E], z[:, E:]
    r = lax.rsqrt(jnp.mean(v * v, axis=-1, keepdims=True) + EPS)
    vhat = v * r
    vn = (vhat * gv).astype(BF16)
    row = lax.broadcasted_iota(jnp.int32, (CH, CH), 0)
    col = lax.broadcasted_iota(jnp.int32, (CH, CH), 1)
    causal = row >= col
    ws = [jnp.where(causal, w_ref[hh], 0.0).astype(BF16) for hh in range(H)]
    s = jnp.concatenate([_dot_nn(ws[hh], vn[:, hh * Dg:(hh + 1) * Dg]) for hh in range(H)], axis=1) + bexp
    return u, r, vhat, vn, causal, ws, s


def sgu_fwd(p, g_v, w_s, bexp, *, name):
    T, E2 = p.shape
    E = E2 // 2
    H, CH, _ = w_s.shape

    def body(p_ref, gv_ref, w_ref, b_ref, o_ref):
        u, _, _, _, _, _, s = _sgu_common(p_ref[...], gv_ref[...], w_ref, b_ref[...], E, H, CH)
        o_ref[...] = (u * s).astype(BF16)

    return pl.pallas_call(
        body, grid=(T // CH,),
        in_specs=[pl.BlockSpec((CH, E2), lambda i: (i, 0)), pl.BlockSpec((1, E), lambda i: (0, 0)),
                  pl.BlockSpec((H, CH, CH), lambda i: (0, 0, 0)), pl.BlockSpec((CH, E), lambda i: (0, 0))],
        out_specs=pl.BlockSpec((CH, E), lambda i: (i, 0)),
        out_shape=jax.ShapeDtypeStruct((T, E), BF16), name=name,
        compiler_params=_params(("parallel",), VMEM_BIG))(p, g_v, w_s, bexp)


def sgu_bwd(p, d_us, g_v, w_s, bexp, *, name):
    T, E2 = p.shape
    E = E2 // 2
    H, CH, _ = w_s.shape
    Dg = E // H

    def body(p_ref, d_ref, gv_ref, w_ref, b_ref, dp_ref, dw_ref, db_ref, dg_ref):
        @pl.when(pl.program_id(0) == 0)
        def _():
            dw_ref[...] = jnp.zeros_like(dw_ref)
            db_ref[...] = jnp.zeros_like(db_ref)
            dg_ref[...] = jnp.zeros_like(dg_ref)

        p = p_ref[...]
        gv = gv_ref[...]
        u, r, vhat, vn, causal, ws, s = _sgu_common(p, gv, w_ref, b_ref[...], E, H, CH)
        d = d_ref[...]
        du = d * s
        ds = d * u
        lane = lax.broadcasted_iota(jnp.int32, (CH, LANES), 1)
        dvn_parts = []
        db = jnp.zeros((CH, LANES), F32)
        for hh in range(H):
            ds_h = ds[:, hh * Dg:(hh + 1) * Dg]
            ds_hb = ds_h.astype(BF16)
            dw_ref[hh] += jnp.where(causal, _dot_nt(ds_hb, vn[:, hh * Dg:(hh + 1) * Dg]), 0.0)
            dvn_parts.append(_dot_tn(ws[hh], ds_hb))
            db = db + jnp.where(lane == hh, jnp.sum(ds_h, axis=1, keepdims=True), 0.0)
        db_ref[...] += db
        dvn = jnp.concatenate(dvn_parts, axis=1)
        dg_ref[...] += jnp.sum(dvn * vhat, axis=0, keepdims=True)
        dvh = dvn * gv
        dv = r * (dvh - vhat * jnp.mean(dvh * vhat, axis=-1, keepdims=True))
        dp_ref[...] = (jnp.concatenate([du, dv], axis=1) * _gelu_grad(p)).astype(BF16)

    return pl.pallas_call(
        body, grid=(T // CH,),
        in_specs=[pl.BlockSpec((CH, E2), lambda i: (i, 0)), pl.BlockSpec((CH, E), lambda i: (i, 0)),
                  pl.BlockSpec((1, E), lambda i: (0, 0)), pl.BlockSpec((H, CH, CH), lambda i: (0, 0, 0)),
                  pl.BlockSpec((CH, E), lambda i: (0, 0))],
        out_specs=[pl.BlockSpec((CH, E2), lambda i: (i, 0)), pl.BlockSpec((H, CH, CH), lambda i: (0, 0, 0)),
                   pl.BlockSpec((CH, LANES), lambda i: (0, 0)), pl.BlockSpec((1, E), lambda i: (0, 0))],
        out_shape=[jax.ShapeDtypeStruct((T, E2), BF16), jax.ShapeDtypeStruct((H, CH, CH), F32),
                   jax.ShapeDtypeStruct((CH, LANES), F32), jax.ShapeDtypeStruct((1, E), F32)], name=name,
        compiler_params=_params(("arbitrary",), VMEM_BIG))(p, d_us, g_v, w_s, bexp)


def _s5_disc(a_re, a_im, log_dt, b_re, b_im):
    dt = jnp.exp(log_dt)
    mag = jnp.exp(dt * a_re)
    ar, ai = mag * jnp.cos(dt * a_im), mag * jnp.sin(dt * a_im)
    den = a_re * a_re + a_im * a_im
    qr = ((ar - 1.0) * a_re + ai * a_im) / den
    qi = (ai * a_re - (ar - 1.0) * a_im) / den
    return ar, ai, qr[None] * b_re - qi[None] * b_im, qr[None] * b_im + qi[None] * b_re


def s5_disc_fwd(a_re, a_im, log_dt, b_re, b_im, *, name):
    G, P = a_re.shape
    C = b_re.shape[0]

    def body(ar_ref, ai_ref, dt_ref, br_ref, bi_ref, o_ar, o_ai, o_br, o_bi):
        ar, ai, br, bi = _s5_disc(ar_ref[...], ai_ref[...], dt_ref[...], br_ref[...], bi_ref[...])
        o_ar[...] = ar
        o_ai[...] = ai
        o_br[...] = br
        o_bi[...] = bi

    gp = jax.ShapeDtypeStruct((G, P), F32)
    cgp = jax.ShapeDtypeStruct((C, G, P), F32)
    return pl.pallas_call(body, out_shape=[gp, gp, cgp, cgp], name=name)(a_re, a_im, log_dt, b_re, b_im)


def s5_disc_bwd(a_re, a_im, log_dt, b_re, b_im, d_ar, d_ai, d_br, d_bi, *, name):
    G, P = a_re.shape
    C = b_re.shape[0]

    def body(ar_ref, ai_ref, dt_ref, br_ref, bi_ref, g0, g1, g2, g3, o0, o1, o2, o3, o4):
        prim = (ar_ref[...], ai_ref[...], dt_ref[...], br_ref[...], bi_ref[...])
        _, vjp = jax.vjp(_s5_disc, *prim)
        outs = vjp((g0[...], g1[...], g2[...], g3[...]))
        for o, v in zip((o0, o1, o2, o3, o4), outs):
            o[...] = v

    gp = jax.ShapeDtypeStruct((G, P), F32)
    cgp = jax.ShapeDtypeStruct((C, G, P), F32)
    return pl.pallas_call(
        body, out_shape=[gp, gp, jax.ShapeDtypeStruct((G, 1), F32), cgp, cgp], name=name,
    )(a_re, a_im, log_dt, b_re, b_im, d_ar, d_ai, d_br, d_bi)


S5_KG = 8


def _planes(x):
    return [x[:, c * LANES:(c + 1) * LANES] for c in range(x.shape[1] // LANES)]


def _store_planes(ref, row0, val):
    for c, p in enumerate(_planes(val)):
        ref[c, pl.ds(row0, val.shape[0]), :] = p


def _load_planes(ref, row0, rows):
    return jnp.concatenate([ref[c, pl.ds(row0, rows), :] for c in range(ref.shape[0])], axis=1)


def _build_powers(a_r, a_i, S, pf_r, pf_i, pr_r=None, pr_i=None):
    ar, ai = _planes(a_r), _planes(a_i)
    NP = len(ar)

    def step(i, carry):
        out = []
        for c in range(NP):
            p_r, p_i = carry[2 * c], carry[2 * c + 1]
            pf_r[c, pl.ds(i, 1), :] = p_r
            pf_i[c, pl.ds(i, 1), :] = p_i
            if pr_r is not None:
                pr_r[c, pl.ds(S - 1 - i, 1), :] = p_r
                pr_i[c, pl.ds(S - 1 - i, 1), :] = -p_i
            out += [ar[c] * p_r - ai[c] * p_i, ar[c] * p_i + ai[c] * p_r]
        return tuple(out)

    init = []
    for c in range(NP):
        init += [ar[c], ai[c]]
    lax.fori_loop(0, S, step, tuple(init))


def _scan_seg(hr, hi, hrow0, tb, a_r, a_i, pw_r, pw_i, h0r, h0i, *, reverse):
    NP = hr.shape[0]
    S = tb // SUBLANES
    if reverse:
        a_i = -a_i
    ar, ai = _planes(a_r), _planes(a_i)

    def step(i, carry):
        j = (S - 1 - i) if reverse else i
        slab = pl.ds(pl.multiple_of(hrow0 + j * SUBLANES, SUBLANES), SUBLANES)
        out = []
        for c in range(NP):
            nr = ar[c] * carry[2 * c] - ai[c] * carry[2 * c + 1] + hr[c, slab, :]
            ni = ar[c] * carry[2 * c + 1] + ai[c] * carry[2 * c] + hi[c, slab, :]
            hr[c, slab, :] = nr
            hi[c, slab, :] = ni
            out += [nr, ni]
        return tuple(out)

    z = jnp.zeros((SUBLANES, LANES), F32)
    loc = lax.fori_loop(0, S, step, (z,) * (2 * NP), unroll=2)
    h0r_p, h0i_p = _planes(h0r), _planes(h0i)
    top = 0 if reverse else S - 1
    order = range(SUBLANES - 1, -1, -1) if reverse else range(SUBLANES)
    out_r, out_i, ent_r, ent_i = [], [], [], []
    for c in range(NP):
        s_r, s_i = pw_r[c, pl.ds(top, 1), :], pw_i[c, pl.ds(top, 1), :]
        c_r, c_i = h0r_p[c], h0i_p[c]
        in_r, in_i = [None] * SUBLANES, [None] * SUBLANES
        for seg in order:
            in_r[seg], in_i[seg] = c_r, c_i
            l_r, l_i = loc[2 * c][seg:seg + 1], loc[2 * c + 1][seg:seg + 1]
            c_r, c_i = s_r * c_r - s_i * c_i + l_r, s_r * c_i + s_i * c_r + l_i
        out_r.append(c_r)
        out_i.append(c_i)
        ent_r.append(jnp.concatenate(in_r, axis=0))
        ent_i.append(jnp.concatenate(in_i, axis=0))

    def fix(j, _):
        slab = pl.ds(pl.multiple_of(hrow0 + j * SUBLANES, SUBLANES), SUBLANES)
        for c in range(NP):
            p_r, p_i = pw_r[c, pl.ds(j, 1), :], pw_i[c, pl.ds(j, 1), :]
            hr[c, slab, :] += p_r * ent_r[c] - p_i * ent_i[c]
            hi[c, slab, :] += p_r * ent_i[c] + p_i * ent_r[c]
        return 0

    lax.fori_loop(0, S, fix, 0, unroll=2)
    return ent_r, ent_i, jnp.concatenate(out_r, axis=1), jnp.concatenate(out_i, axis=1)


def _perm_matrices(tb):
    r = jnp.arange(tb)
    pm = (r[None, :] == ((r % SUBLANES) * (tb // SUBLANES) + r // SUBLANES)[:, None]).astype(BF16)
    return pm, pm.T


def _split2(x):
    hi = x.astype(BF16)
    return hi, (x - hi.astype(F32)).astype(BF16)


def _s5_specs(tb, UC, SC, rev_nb=None):
    tmap = (lambda b: b) if rev_nb is None else (lambda b: rev_nb - 1 - b)
    row = pl.BlockSpec((tb, UC), lambda b, k: (tmap(b), k))
    wsp = pl.BlockSpec((None, UC, SC), lambda b, k: (k, 0, 0))
    vsc = pl.BlockSpec((None, 1, SC), lambda b, k: (k, 0, 0))
    vuc = pl.BlockSpec((None, 1, UC), lambda b, k: (k, 0, 0))
    st = pl.BlockSpec((None, None, 1, SC), lambda b, k: (tmap(b), k, 0, 0))
    return row, wsp, vsc, vuc, st


def s5_fwd(uu, pm, pmt, bd_r, bd_i, ct_r, ct_i, ab_r, ab_i, dd, *, tb, name):
    T, E = uu.shape
    KB, UC, SC = bd_r.shape
    NB = T // tb
    NP, S = SC // LANES, tb // SUBLANES

    def body(u_ref, pm_ref, pmt_ref, bdr, bdi, ctr, cti, ar_ref, ai_ref, dd_ref, y_ref, q_ref, sr_ref, si_ref,
             hr, hi, cr, ci, pf_r, pf_i):
        b, k = pl.program_id(0), pl.program_id(1)
        a_r, a_i = ar_ref[...], ai_ref[...]

        @pl.when(b == 0)
        def _():
            cr[k] = jnp.zeros((1, SC), F32)
            ci[k] = jnp.zeros((1, SC), F32)
            _build_powers(a_r, a_i, S, pf_r.at[k], pf_i.at[k])

        h0r, h0i = cr[k], ci[k]
        sr_ref[...] = h0r
        si_ref[...] = h0i
        u = u_ref[...]
        up = _dot_nn(pm_ref[...], u.astype(BF16)).astype(BF16)
        _store_planes(hr, 0, _dot_nn(up, bdr[...]))
        _store_planes(hi, 0, _dot_nn(up, bdi[...]))
        _, _, o_r, o_i = _scan_seg(hr, hi, 0, tb, a_r, a_i, pf_r.at[k], pf_i.at[k], h0r, h0i, reverse=False)
        cr[k] = o_r
        ci[k] = o_i
        ys = (_dot_nt(_load_planes(hr, 0, tb).astype(BF16), ctr[...])
              - _dot_nt(_load_planes(hi, 0, tb).astype(BF16), cti[...]))
        pmt_v = pmt_ref[...]
        y = sum(_dot_nn(pmt_v, part) for part in _split2(ys)) + dd_ref[...] * u
        y_ref[...] = y
        q_ref[...] = _gelu(y).astype(BF16)

    row, wsp, vsc, vuc, st = _s5_specs(tb, UC, SC)
    psp = pl.BlockSpec((tb, tb), lambda b, k: (0, 0))
    stsh = jax.ShapeDtypeStruct((NB, KB, 1, SC), F32)
    pw = pltpu.VMEM((KB, NP, S, LANES), F32)
    pln = pltpu.VMEM((NP, tb, LANES), F32)
    return pl.pallas_call(
        body, grid=(NB, KB), in_specs=[row, psp, psp, wsp, wsp, wsp, wsp, vsc, vsc, vuc],
        out_specs=[row, row, st, st],
        out_shape=[jax.ShapeDtypeStruct((T, E), F32), jax.ShapeDtypeStruct((T, E), BF16), stsh, stsh],
        scratch_shapes=[pln, pln, pltpu.VMEM((KB, 1, SC), F32), pltpu.VMEM((KB, 1, SC), F32), pw, pw],
        name=name, compiler_params=_params(("arbitrary", "arbitrary"), VMEM_MID),
    )(uu, pm, pmt, bd_r, bd_i, ct_r, ct_i, ab_r, ab_i, dd)


def s5_bwd(uu, y, dq, pm, pmt, st_r, st_i, bd_r, bd_i, ct_r, ct_i, ab_r, ab_i, dd, *, tb, name):
    T, E = uu.shape
    KB, UC, SC = bd_r.shape
    NB = T // tb
    HDR = SUBLANES
    NP, S = SC // LANES, tb // SUBLANES
    pw = pltpu.VMEM((KB, NP, S, LANES), F32)
    pln = pltpu.VMEM((NP, tb, LANES), F32)

    def body(u_ref, y_ref, dq_ref, pm_ref, pmt_ref, sr_ref, si_ref, bdr, bdi, ctr, cti, ar_ref, ai_ref, dd_ref,
             du_ref, obr, obi, ocr, oci, odar, odai, oddd,
             hr, hi, gr, gi, kr, ki, abr, abi, acr, aci, pf_r, pf_i, pr_r, pr_i):
        b, k = pl.program_id(0), pl.program_id(1)
        a_r, a_i = ar_ref[...], ai_ref[...]

        @pl.when(b == 0)
        def _():
            _build_powers(a_r, a_i, S, pf_r.at[k], pf_i.at[k], pr_r.at[k], pr_i.at[k])
            z1 = jnp.zeros((1, SC), F32)
            kr[k] = z1
            ki[k] = z1
            odar[k] = z1
            odai[k] = z1
            oddd[k] = jnp.zeros((1, UC), F32)
            zw = jnp.zeros((UC, SC), F32)
            abr[k] = zw
            abi[k] = zw
            acr[k] = zw
            aci[k] = zw

        u = u_ref[...]
        dy = dq_ref[...] * _gelu_grad(y_ref[...])
        oddd[k] += jnp.sum(dy * u, axis=0, keepdims=True)
        pm_v = pm_ref[...]
        ub = _dot_nn(pm_v, u.astype(BF16)).astype(BF16)
        dyb = _dot_nn(pm_v, dy.astype(BF16)).astype(BF16)
        s0r, s0i = sr_ref[...], si_ref[...]
        _store_planes(hr, HDR, _dot_nn(ub, bdr[...]))
        _store_planes(hi, HDR, _dot_nn(ub, bdi[...]))
        e_r, e_i, _, _ = _scan_seg(hr, hi, HDR, tb, a_r, a_i, pf_r.at[k], pf_i.at[k], s0r, s0i, reverse=False)
        for c in range(NP):
            hr[c, pl.ds(0, HDR), :] = e_r[c]
            hi[c, pl.ds(0, HDR), :] = e_i[c]
        _store_planes(gr, 0, _dot_nn(dyb, ctr[...]))
        _store_planes(gi, 0, -_dot_nn(dyb, cti[...]))
        _, _, g0r, g0i = _scan_seg(gr, gi, 0, tb, a_r, a_i, pr_r.at[k], pr_i.at[k], kr[k], ki[k], reverse=True)
        kr[k] = g0r
        ki[k] = g0i

        def slab(j, acc):
            o = pl.multiple_of(j * SUBLANES, SUBLANES)
            out = []
            for c in range(NP):
                p_r, p_i = hr[c, pl.ds(o, SUBLANES), :], hi[c, pl.ds(o, SUBLANES), :]
                g_r, g_i = gr[c, pl.ds(o, SUBLANES), :], gi[c, pl.ds(o, SUBLANES), :]
                out += [acc[2 * c] + g_r * p_r + g_i * p_i, acc[2 * c + 1] + g_i * p_r - g_r * p_i]
            return tuple(out)

        z8 = jnp.zeros((SUBLANES, LANES), F32)
        acc = lax.fori_loop(0, S, slab, (z8,) * (2 * NP), unroll=2)
        odar[k] += jnp.concatenate([jnp.sum(acc[2 * c], axis=0, keepdims=True) for c in range(NP)], axis=1)
        odai[k] += jnp.concatenate([jnp.sum(acc[2 * c + 1], axis=0, keepdims=True) for c in range(NP)], axis=1)
        g_rb = _load_planes(gr, 0, tb).astype(BF16)
        g_ib = _load_planes(gi, 0, tb).astype(BF16)
        h_rb = _load_planes(hr, HDR, tb).astype(BF16)
        h_ib = _load_planes(hi, HDR, tb).astype(BF16)
        dus = _dot_nt(g_rb, bdr[...]) + _dot_nt(g_ib, bdi[...])
        pmt_v = pmt_ref[...]
        du = sum(_dot_nn(pmt_v, part) for part in _split2(dus)) + dd_ref[...] * dy
        du_ref[...] = du.astype(BF16)
        abr[k] += _dot_tn(ub, g_rb)
        abi[k] += _dot_tn(ub, g_ib)
        acr[k] += _dot_tn(dyb, h_rb)
        aci[k] -= _dot_tn(dyb, h_ib)

        @pl.when(jnp.logical_and(b == NB - 1, k == KB - 1))
        def _():
            pltpu.sync_copy(abr, obr)
            pltpu.sync_copy(abi, obi)
            pltpu.sync_copy(acr, ocr)
            pltpu.sync_copy(aci, oci)

    row, wsp, vsc, vuc, st = _s5_specs(tb, UC, SC, rev_nb=NB)
    psp = pl.BlockSpec((tb, tb), lambda b, k: (0, 0))
    hbm = pl.BlockSpec(memory_space=pltpu.HBM)
    full_sc = pl.BlockSpec((KB, 1, SC), lambda b, k: (0, 0, 0))
    full_uc = pl.BlockSpec((KB, 1, UC), lambda b, k: (0, 0, 0))
    wsh = jax.ShapeDtypeStruct((KB, UC, SC), F32)
    acc = pltpu.VMEM((KB, UC, SC), F32)
    return pl.pallas_call(
        body, grid=(NB, KB),
        in_specs=[row, row, row, psp, psp, st, st, wsp, wsp, wsp, wsp, vsc, vsc, vuc],
        out_specs=[row, hbm, hbm, hbm, hbm, full_sc, full_sc, full_uc],
        out_shape=[jax.ShapeDtypeStruct((T, E), BF16), wsh, wsh, wsh, wsh,
                   jax.ShapeDtypeStruct((KB, 1, SC), F32), jax.ShapeDtypeStruct((KB, 1, SC), F32),
                   jax.ShapeDtypeStruct((KB, 1, UC), F32)],
        scratch_shapes=[pltpu.VMEM((NP, tb + HDR, LANES), F32), pltpu.VMEM((NP, tb + HDR, LANES), F32), pln, pln,
                        pltpu.VMEM((KB, 1, SC), F32), pltpu.VMEM((KB, 1, SC), F32), acc, acc, acc, acc,
                        pw, pw, pw, pw],
        name=name, compiler_params=_params(("arbitrary", "arbitrary"), VMEM_BIG),
    )(uu, y, dq, pm, pmt, st_r, st_i, bd_r, bd_i, ct_r, ct_i, ab_r, ab_i, dd)


def glu_fwd(h, pg, *, name):
    T, D = h.shape
    tm = _pick(T, ROW_TILES)

    def body(h_ref, a_ref, b_ref, o_ref):
        o_ref[...] = h_ref[...] + a_ref[...] * _sigmoid(b_ref[...])

    row = pl.BlockSpec((tm, D), lambda i: (i, 0))
    return pl.pallas_call(
        body, grid=(T // tm,), in_specs=[row, row, pl.BlockSpec((tm, D), lambda i: (i, 1))], out_specs=row,
        out_shape=jax.ShapeDtypeStruct((T, D), F32), name=name,
        compiler_params=_params(("parallel",), VMEM_MID))(h, pg, pg)


def glu_bwd(d, pg, *, name):
    T, D = d.shape
    tm = _pick(T, ROW_TILES)

    def body(d_ref, a_ref, b_ref, o_ref):
        dv = d_ref[...]
        sg = _sigmoid(b_ref[...])
        da = dv * sg
        db = dv * a_ref[...] * sg * (1.0 - sg)
        o_ref[...] = jnp.where(pl.program_id(1) == 0, da, db).astype(BF16)

    row = pl.BlockSpec((tm, D), lambda i, hf: (i, 0))
    return pl.pallas_call(
        body, grid=(T // tm, 2), in_specs=[row, row, pl.BlockSpec((tm, D), lambda i, hf: (i, 1))],
        out_specs=pl.BlockSpec((tm, D), lambda i, hf: (i, hf)),
        out_shape=jax.ShapeDtypeStruct((T, 2 * D), BF16), name=name,
        compiler_params=_params(("parallel", "arbitrary"), VMEM_MID))(d, pg, pg)


def _shift_down(x, halo, s):
    r = pltpu.roll(x, s, axis=0)
    hr = pltpu.roll(halo, s, axis=0)
    row = lax.broadcasted_iota(jnp.int32, halo.shape, 0)
    head = jnp.where(row < s, hr, r[:SUBLANES])
    return jnp.concatenate([head, r[SUBLANES:]], axis=0)


def _shift_up(x, halo, s):
    n = x.shape[0]
    r = pltpu.roll(x, n - s, axis=0)
    hr = pltpu.roll(halo, SUBLANES - s, axis=0)
    row = lax.broadcasted_iota(jnp.int32, halo.shape, 0)
    tail = jnp.where(row >= SUBLANES - s, hr, r[n - SUBLANES:])
    return jnp.concatenate([r[:n - SUBLANES], tail], axis=0)


def _conv_acc(z, zh, w, b, first):
    kw = w.shape[0]
    zh = jnp.where(first, 0.0, zh)
    acc = b + w[kw - 1:kw] * z
    shifted = []
    for k in range(kw - 1):
        zs = _shift_down(z, zh, kw - 1 - k)
        shifted.append(zs)
        acc = acc + w[k:k + 1] * zs
    return acc, shifted


def _conv_specs(T, F, tm, tc, KW):
    nfb = F // tc
    rb = tm // SUBLANES

    def main(off):
        return pl.BlockSpec((tm, tc), lambda i, c: (i, c + off))

    def halo(off):
        return pl.BlockSpec((SUBLANES, tc), lambda i, c: (jnp.maximum(i * rb - 1, 0), c + off))

    def wspec(off):
        return pl.BlockSpec((None, KW, tc), lambda i, c: (c + off, 0, 0))

    def bspec(off):
        return pl.BlockSpec((1, tc), lambda i, c: (0, c + off))

    return nfb, main, halo, wspec, bspec


def convglu_fwd(z, cw, cb, *, name):
    T, F2 = z.shape
    F = F2 // 2
    _, KW, tc = cw.shape
    tm = _pick(T, (256, 128))
    nfb, main, halo, wspec, bspec = _conv_specs(T, F, tm, tc, KW)

    def body(zg, zgh, zv, zvh, wg, wv, bg, bv, o_ref):
        first = pl.program_id(0) == 0
        g, _ = _conv_acc(zg[...], zgh[...], wg[...], bg[...], first)
        v, _ = _conv_acc(zv[...], zvh[...], wv[...], bv[...], first)
        o_ref[...] = (g * _sigmoid(g) * v).astype(BF16)

    return pl.pallas_call(
        body, grid=(T // tm, nfb),
        in_specs=[main(0), halo(0), main(nfb), halo(nfb), wspec(0), wspec(nfb), bspec(0), bspec(nfb)],
        out_specs=pl.BlockSpec((tm, tc), lambda i, c: (i, c)),
        out_shape=jax.ShapeDtypeStruct((T, F), BF16), name=name,
        compiler_params=_params(("parallel", "parallel"), VMEM_BIG))(z, z, z, z, cw, cw, cb, cb)


def convglu_bwd_acc(z, da, cw, cb, *, name):
    T, F2 = z.shape
    F = F2 // 2
    _, KW, tc = cw.shape
    tm = _pick(T, (256, 128))
    nfb, main, halo, wspec, bspec = _conv_specs(T, F, tm, tc, KW)

    def body(zg, zgh, zv, zvh, wg, wv, bg, bv, da_ref, o_ref, dwg, dwv, dbg, dbv):
        i = pl.program_id(1)
        first = i == 0

        @pl.when(first)
        def _():
            for o in (dwg, dwv, dbg, dbv):
                o[...] = jnp.zeros_like(o)

        zg_v, zv_v = zg[...], zv[...]
        g, sg_ = _conv_acc(zg_v, zgh[...], wg[...], bg[...], first)
        v, sv_ = _conv_acc(zv_v, zvh[...], wv[...], bv[...], first)
        d = da_ref[...]
        sig = _sigmoid(g)
        dg = d * v * sig * (1.0 + g * (1.0 - sig))
        dv = d * g * sig
        o_ref[0] = dg.astype(BF16)
        o_ref[1] = dv.astype(BF16)
        dbg[...] += jnp.sum(dg, axis=0, keepdims=True)
        dbv[...] += jnp.sum(dv, axis=0, keepdims=True)
        for k in range(KW):
            xg = zg_v if k == KW - 1 else sg_[k]
            xv = zv_v if k == KW - 1 else sv_[k]
            dwg[pl.ds(k, 1), :] += jnp.sum(dg * xg, axis=0, keepdims=True)
            dwv[pl.ds(k, 1), :] += jnp.sum(dv * xv, axis=0, keepdims=True)

    def sw(spec_fn, off):
        s = spec_fn(off)
        return pl.BlockSpec(s.block_shape, lambda c, i, f=s.index_map: f(i, c))

    both = jax.ShapeDtypeStruct((2, T, F), BF16)
    dwsh = jax.ShapeDtypeStruct((nfb, KW, tc), F32)
    dbsh = jax.ShapeDtypeStruct((1, F), F32)
    outs = pl.pallas_call(
        body, grid=(nfb, T // tm),
        in_specs=[sw(main, 0), sw(halo, 0), sw(main, nfb), sw(halo, nfb), sw(wspec, 0), sw(wspec, nfb),
                  sw(bspec, 0), sw(bspec, nfb), pl.BlockSpec((tm, tc), lambda c, i: (i, c))],
        out_specs=[pl.BlockSpec((2, tm, tc), lambda c, i: (0, i, c)),
                   pl.BlockSpec((None, KW, tc), lambda c, i: (c, 0, 0)), pl.BlockSpec((None, KW, tc), lambda c, i: (c, 0, 0)),
                   pl.BlockSpec((1, tc), lambda c, i: (0, c)), pl.BlockSpec((1, tc), lambda c, i: (0, c))],
        out_shape=[both, dwsh, dwsh, dbsh, dbsh], name=name,
        compiler_params=_params(("parallel", "arbitrary"), VMEM_BIG))(z, z, z, z, cw, cw, cb, cb, da)
    return outs


def conv_bwd_in(dacc, cw, *, name):
    _, T, F = dacc.shape
    _, KW, tc = cw.shape
    nfb = F // tc
    tm = _pick(T, (256, 128))
    rb = tm // (2 * SUBLANES)
    last_blk = T // (2 * SUBLANES) - 1

    def body(d_ref, dn_ref, w_ref, o_ref):
        last = pl.program_id(0) == pl.num_programs(0) - 1
        d = d_ref[...].astype(F32)
        dn = jnp.where(last, 0.0, dn_ref[...].astype(F32)[:SUBLANES])
        w = w_ref[...]
        out = w[KW - 1:KW] * d
        for k in range(KW - 1):
            out = out + w[k:k + 1] * _shift_up(d, dn, KW - 1 - k)
        o_ref[...] = out.astype(BF16)

    return pl.pallas_call(
        body, grid=(T // tm, 2, nfb),
        in_specs=[pl.BlockSpec((None, tm, tc), lambda i, hf, c: (hf, i, c)),
                  pl.BlockSpec((None, 2 * SUBLANES, tc), lambda i, hf, c: (hf, jnp.minimum((i + 1) * rb, last_blk), c)),
                  pl.BlockSpec((None, KW, tc), lambda i, hf, c: (hf * nfb + c, 0, 0))],
        out_specs=pl.BlockSpec((tm, tc), lambda i, hf, c: (i, hf * nfb + c)),
        out_shape=jax.ShapeDtypeStruct((T, 2 * F), BF16), name=name,
        compiler_params=_params(("parallel", "parallel", "parallel"), VMEM_BIG))(dacc, dacc, cw)


def _my_place():
    x, y, c = (lax.axis_index(a) for a in AXES)
    return x, y, c, 4 * x + 2 * y + c


def _peer(m, x, y, c):
    px = 1 - x if (m >> 2) & 1 else x
    py = 1 - y if (m >> 1) & 1 else y
    pc = 1 - c if m & 1 else c
    return (px, py, pc), 4 * px + 2 * py + pc


def _exchange(ins, out_shapes, plan, *, bcast, name):
    n_in, n_out, n = len(ins), len(out_shapes), len(plan)

    def body(*refs):
        in_refs, out_refs = refs[:n_in], refs[n_in:n_in + n_out]
        send_sems, recv_sems, loc_sems = refs[n_in + n_out:]
        x, y, c, me = _my_place()

        def src(f, who):
            r = in_refs[plan[f][0]]
            return r if bcast else r.at[who]

        def dst(f, who):
            r = out_refs[plan[f][1]]
            lay = plan[f][2]
            return r.at[who] if lay is None else r.at[lay, who]

        def remote(f, m, landing):
            dev, plin = _peer(m, x, y, c)
            return pltpu.make_async_remote_copy(
                src_ref=src(f, plin), dst_ref=dst(f, plin if landing else me), send_sem=send_sems.at[f, m - 1],
                recv_sem=recv_sems.at[f, m - 1], device_id=dev, device_id_type=MESH)

        locs = [pltpu.make_async_copy(src(f, me), dst(f, me), loc_sems.at[f]) for f in range(n)]
        for cp in locs:
            cp.start()
        for m in range(1, N_DEV):
            for f in range(n):
                remote(f, m, False).start()
        for m in range(1, N_DEV):
            for f in range(n):
                remote(f, m, True).wait()
        for cp in locs:
            cp.wait()

    hbm = pl.BlockSpec(memory_space=pltpu.HBM)
    return pl.pallas_call(
        body, in_specs=[hbm] * n_in, out_specs=[hbm] * n_out, out_shape=out_shapes,
        scratch_shapes=[pltpu.SemaphoreType.DMA((n, N_DEV - 1)), pltpu.SemaphoreType.DMA((n, N_DEV - 1)),
                        pltpu.SemaphoreType.DMA((n,))],
        name=name)(*ins)


def all_gather(shards, *, name):
    outs = [jax.ShapeDtypeStruct((N_DEV,) + s.shape, s.dtype) for s in shards]
    return _exchange(shards, outs, [(a, a, None) for a in range(len(shards))], bcast=True, name=name)


_HBM = pl.BlockSpec(memory_space=pltpu.HBM)
_SEM = pl.BlockSpec(memory_space=pltpu.SEMAPHORE)
_EFFECT = pltpu.SideEffectType.DATAFLOW_SIDE_EFFECTING


def _split_copy(in_refs, land_refs, send_sems, recv_sems, bcast, f, m, place, landing):
    x, y, c, me = place
    dev, plin = _peer(m, x, y, c)
    src = in_refs[f] if bcast else in_refs[f].at[plin]
    return pltpu.make_async_remote_copy(
        src_ref=src, dst_ref=land_refs[f].at[plin if landing else me],
        send_sem=send_sems.at[f * (N_DEV - 1) + m - 1], recv_sem=recv_sems.at[f * (N_DEV - 1) + m - 1],
        device_id=dev, device_id_type=MESH)


def _own_copy(in_refs, land_refs, own_sems, bcast, f, place):
    me = place[3]
    return pltpu.make_async_copy(in_refs[f] if bcast else in_refs[f].at[me], land_refs[f].at[me], own_sems.at[f])


PEERS_ALL = tuple(range(1, N_DEV))
PEERS_DIRECT = (1, 2, 4, 6)
SLOTS_PASSED = (2, 4, 6)


def exchange_start(ins, *, bcast, masks=PEERS_ALL, dep=None, name):
    n = len(ins)
    lands = [lax.empty(((N_DEV,) + a.shape) if bcast else a.shape, a.dtype) for a in ins]
    deps = [] if dep is None else [dep]
    nd = len(deps)

    def body(*refs):
        in_refs, land_refs = refs[:n], refs[n:2 * n]
        send_sems, recv_sems, own_sems = refs[2 * n + nd:2 * n + nd + 3]
        token = refs[-1]
        place = _my_place()
        for m in masks:
            for f in range(n):
                _split_copy(in_refs, land_refs, send_sems, recv_sems, bcast, f, m, place, False).start()
        for f in range(n):
            _own_copy(in_refs, land_refs, own_sems, bcast, f, place).start()
        token[...] = jnp.zeros_like(token)

    arrs = [pltpu.with_memory_space_constraint(a, pltpu.HBM) for a in (*ins, *lands)]
    sems = pltpu.SemaphoreType.DMA((n * (N_DEV - 1),))
    outs = pl.pallas_call(
        body, name=name,
        out_shape=(sems, sems, pltpu.SemaphoreType.DMA((n,)), *[pltpu.HBM(a.shape, a.dtype) for a in arrs],
                   jax.ShapeDtypeStruct((SUBLANES, LANES), F32)),
        in_specs=[_HBM] * (2 * n) + [_DEP] * nd,
        out_specs=(_SEM, _SEM, _SEM, *[_HBM] * (2 * n), pl.BlockSpec(memory_space=pltpu.VMEM)),
        input_output_aliases={i: 3 + i for i in range(2 * n)},
        compiler_params=pltpu.CompilerParams(has_side_effects=_EFFECT))(*arrs, *deps)
    return outs[0], outs[1], outs[2], list(outs[3:3 + 2 * n]), outs[-1]


def exchange_wait(started, after, *, bcast, masks=PEERS_ALL, name):
    send_sems, recv_sems, own_sems, thrus, _ = started
    n = len(thrus) // 2

    def body(*refs):
        in_refs, land_refs = refs[:n], refs[n:2 * n]
        send, recv, own = refs[2 * n:2 * n + 3]
        place = _my_place()
        for f in range(n):
            _own_copy(in_refs, land_refs, own, bcast, f, place).wait()
        for m in masks:
            for f in range(n):
                cp = _split_copy(in_refs, land_refs, send, recv, bcast, f, m, place, True)
                cp.wait_send()
                cp.wait_recv()

    outs = pl.pallas_call(
        body, name=name, out_shape=[pltpu.HBM(a.shape, a.dtype) for a in thrus],
        in_specs=[_HBM] * (2 * n) + [_SEM, _SEM, _SEM, pl.BlockSpec(memory_space=pl.ANY)], out_specs=[_HBM] * (2 * n),
        input_output_aliases={i: i for i in range(2 * n)},
        compiler_params=pltpu.CompilerParams(has_side_effects=_EFFECT))(
            *thrus, send_sems, recv_sems, own_sems, after)
    return list(outs[n:])


def _pass_copy(land_refs, send_sems, recv_sems, f, k, place, landing):
    x, y, c, _ = place
    m = SLOTS_PASSED[k]
    sib, _ = _peer(1, x, y, c)
    _, mine = _peer(m, x, y, c)
    _, theirs = _peer(m ^ 1, x, y, c)
    i = f * len(SLOTS_PASSED) + k
    return pltpu.make_async_remote_copy(
        src_ref=land_refs[f].at[mine], dst_ref=land_refs[f].at[theirs if landing else mine],
        send_sem=send_sems.at[i], recv_sem=recv_sems.at[i], device_id=sib, device_id_type=MESH)


def pass_start(lands, *, dep=None, name):
    n = len(lands)
    deps = [] if dep is None else [dep]
    nd = len(deps)

    def body(*refs):
        land_refs = refs[:n]
        send_sems, recv_sems, token = refs[n + nd], refs[n + nd + 1], refs[-1]
        place = _my_place()
        for k in range(len(SLOTS_PASSED)):
            for f in range(n):
                _pass_copy(land_refs, send_sems, recv_sems, f, k, place, False).start()
        token[...] = jnp.zeros_like(token)

    sems = pltpu.SemaphoreType.DMA((n * len(SLOTS_PASSED),))
    outs = pl.pallas_call(
        body, name=name,
        out_shape=(sems, sems, *[pltpu.HBM(a.shape, a.dtype) for a in lands], jax.ShapeDtypeStruct((SUBLANES, LANES), F32)),
        in_specs=[_HBM] * n + [_DEP] * nd,
        out_specs=(_SEM, _SEM, *[_HBM] * n, pl.BlockSpec(memory_space=pltpu.VMEM)),
        input_output_aliases={i: 2 + i for i in range(n)},
        compiler_params=pltpu.CompilerParams(has_side_effects=_EFFECT))(*lands, *deps)
    return outs[0], outs[1], list(outs[2:2 + n]), outs[-1]


def pass_wait(started, after, *, name):
    send_sems, recv_sems, thrus, _ = started
    n = len(thrus)

    def body(*refs):
        land_refs = refs[:n]
        send, recv = refs[n], refs[n + 1]
        place = _my_place()
        for k in range(len(SLOTS_PASSED)):
            for f in range(n):
                cp = _pass_copy(land_refs, send, recv, f, k, place, True)
                cp.wait_send()
                cp.wait_recv()

    outs = pl.pallas_call(
        body, name=name, out_shape=[pltpu.HBM(a.shape, a.dtype) for a in thrus],
        in_specs=[_HBM] * n + [_SEM, _SEM, pl.BlockSpec(memory_space=pl.ANY)], out_specs=[_HBM] * n,
        input_output_aliases={i: i for i in range(n)},
        compiler_params=pltpu.CompilerParams(has_side_effects=_EFFECT))(*thrus, send_sems, recv_sems, after)
    return list(outs)


def _adamw(w, g, m, v):
    m = ADAM_B1 * m + (1.0 - ADAM_B1) * g
    v = ADAM_B2 * v + (1.0 - ADAM_B2) * (g * g)
    m_hat = m / (1.0 - ADAM_B1 ** ADAM_STEP)
    v_hat = v / (1.0 - ADAM_B2 ** ADAM_STEP)
    delta = -ADAM_LR * (m_hat / (jnp.sqrt(v_hat) + ADAM_EPS) + ADAM_WD * w)
    return delta, m, v


def adam_reduce(recv, w, m, v, l, prev, *, name):
    _, R, C = recv.shape
    L = w.shape[0]
    budget = 4 * 1024 * 1024
    tr = R
    for cand in (1024, 512, 352, 256, 176, 128, 64, 32, 16):
        if R % cand == 0 and N_DEV * cand * C * recv.dtype.itemsize <= budget:
            tr = cand
            break

    def body(r_ref, w_ref, m_ref, v_ref, *rest):
        g_ref, d_ref, nm_ref, nv_ref = rest[-4:]
        g = r_ref[0].astype(F32)
        for s in range(1, N_DEV):
            g = g + r_ref[s].astype(F32)
        d, nm, nv = _adamw(w_ref[...], g, m_ref[...], v_ref[...])
        g_ref[...] = g
        d_ref[...] = d
        nm_ref[...] = nm
        nv_ref[...] = nv

    blk = pl.BlockSpec((None, tr, C), lambda r: (l, r, 0))
    sh = jax.ShapeDtypeStruct((L, R, C), F32)
    extra = [] if prev is None else list(prev)
    return pl.pallas_call(
        body, grid=(R // tr,),
        in_specs=[pl.BlockSpec((N_DEV, tr, C), lambda r: (0, r, 0)), blk, blk, blk]
        + [pl.BlockSpec(memory_space=pl.ANY)] * len(extra),
        out_specs=[blk] * 4, out_shape=[sh] * 4, name=name,
        input_output_aliases={4 + i: i for i in range(len(extra))},
        compiler_params=_params(("parallel",), VMEM_MID))(recv, w, m, v, *extra)


def sum_slots(recv, *, name):
    _, R, C = recv.shape
    tr = _pick(R, (512, 256, 128, 64, 32, 16, 8))

    def body(r_ref, o_ref):
        g = r_ref[0]
        for s in range(1, N_DEV):
            g = g + r_ref[s]
        o_ref[...] = g

    return pl.pallas_call(
        body, grid=(R // tr,), in_specs=[pl.BlockSpec((N_DEV, tr, C), lambda r: (0, r, 0))],
        out_specs=pl.BlockSpec((tr, C), lambda r: (r, 0)), out_shape=jax.ShapeDtypeStruct((R, C), F32),
        name=name, compiler_params=_params(("parallel",)))(recv)


def adam_flat(g, w, m, v, *, name):
    R, C = g.shape
    tr = _pick(R, (512, 256, 128, 64, 32, 16, 8))

    def body(g_ref, w_ref, m_ref, v_ref, d_ref, nm_ref, nv_ref):
        d, nm, nv = _adamw(w_ref[...], g_ref[...], m_ref[...], v_ref[...])
        d_ref[...] = d
        nm_ref[...] = nm
        nv_ref[...] = nv

    blk = pl.BlockSpec((tr, C), lambda r: (r, 0))
    sh = jax.ShapeDtypeStruct((R, C), F32)
    return pl.pallas_call(body, grid=(R // tr,), in_specs=[blk] * 4, out_specs=[blk] * 3, out_shape=[sh] * 3,
                          name=name, compiler_params=_params(("parallel",)))(g, w, m, v)


WEIGHTS = ("norm_mix_g", "norm_ffn_g", "a_w_in", "a_g_v", "a_w_s", "a_b_s", "a_w_out", "b_w_in", "b_a_re", "b_a_im",
           "b_log_dt", "b_b_re", "b_b_im", "b_c_re", "b_c_im", "b_d", "b_w_glu", "f_w_up", "f_conv_w", "f_conv_b",
           "f_w_down", "final_g")
BIG = ("a_w_in", "a_w_out", "b_w_in", "b_w_glu", "f_w_up", "f_w_down")
FLAT_CHUNK = N_DEV * SUBLANES * LANES


def _expand(blocks, eye):
    KB, KG, C, P = blocks.shape
    return (blocks[:, :, :, None, :] * eye[None, :, None, :, None]).reshape(KB, KG * C, KG * P)


def _diag_blocks(dense, KG, C, P, eye):
    KB = dense.shape[0]
    return jnp.einsum("kgchp,gh->kgcp", dense.reshape(KB, KG, C, KG, P), eye)


def _flatten_pack(parts):
    flat = jnp.concatenate([p.reshape(-1) for p in parts])
    pad = (-flat.shape[0]) % FLAT_CHUNK
    return jnp.pad(flat, (0, pad))


def _unpack(flat, like):
    out, o = [], 0
    for p in like:
        n = math.prod(p.shape)
        out.append(flat[o:o + n].reshape(p.shape))
        o += n
    return out


def kernel(x, norm_mix_g, norm_ffn_g, a_w_in, a_g_v, a_w_s, a_b_s, a_w_out, b_w_in, b_a_re, b_a_im, b_log_dt, b_b_re, b_b_im, b_c_re, b_c_im, b_d, b_w_glu, f_w_up, f_conv_w, f_conv_b, f_w_down, final_g, loss_target, m_norm_mix_g, m_norm_ffn_g, m_a_w_in, m_a_g_v, m_a_w_s, m_a_b_s, m_a_w_out, m_b_w_in, m_b_a_re, m_b_a_im, m_b_log_dt, m_b_b_re, m_b_b_im, m_b_c_re, m_b_c_im, m_b_d, m_b_w_glu, m_f_w_up, m_f_conv_w, m_f_conv_b, m_f_w_down, m_final_g, v_norm_mix_g, v_norm_ffn_g, v_a_w_in, v_a_g_v, v_a_w_s, v_a_b_s, v_a_w_out, v_b_w_in, v_b_a_re, v_b_a_im, v_b_log_dt, v_b_b_re, v_b_b_im, v_b_c_re, v_b_c_im, v_b_d, v_b_w_glu, v_f_w_up, v_f_conv_w, v_f_conv_b, v_f_w_down, v_final_g):
    env = dict(locals())
    W = {n: env[n] for n in WEIGHTS}
    Mo = {n: env["m_" + n] for n in WEIGHTS}
    Vo = {n: env["v_" + n] for n in WEIGHTS}

    _, T, D = x.shape
    depth = norm_mix_g.shape[0]
    E_A = a_g_v.shape[1]
    H = a_w_s.shape[1]
    G, P, C = b_b_re.shape[1], b_b_re.shape[2], b_b_re.shape[3]
    E_B = G * C
    KG = S5_KG
    KB = G // KG
    F2 = f_conv_b.shape[1]
    tb = _pick(T, (512, 256, 128))
    pm, pmt = _perm_matrices(tb)
    eye = jnp.eye(KG, dtype=F32)
    _, _, _, me = _my_place()

    wb = {n: W[n].astype(BF16) for n in BIG}

    def shards(i):
        j = i // 2
        if i % 2 == 0:
            return [wb["a_w_in"][j], wb["a_w_out"][j], wb["f_w_up"][i], wb["f_w_down"][i], f_conv_w[i]]
        return [wb["b_w_in"][j], wb["b_w_glu"][j], b_d[j][None], wb["f_w_up"][i], wb["f_w_down"][i], f_conv_w[i]]

    h = x[0]
    started = exchange_start(shards(0), bcast=True, masks=PEERS_DIRECT, name="gather_start0")
    saved = []
    for i in range(depth):
        j = i // 2
        s = {}
        lands = exchange_wait(started, h, bcast=True, masks=PEERS_DIRECT, name=f"gather_wait{i}")
        passing = pass_start(lands, name=f"gather_pass_start{i}")
        dep = passing[-1]
        if i + 1 < depth:
            started = exchange_start(shards(i + 1), bcast=True, masks=PEERS_DIRECT, dep=dep,
                                     name=f"gather_start{i + 1}")
            dep = started[-1]
        s["h"] = h
        s["hn"] = rms_fwd(h, norm_mix_g[i][None], dep=dep, name=f"rms_mix{i}")
        gathered = pass_wait(passing, s["hn"], name=f"gather_pass_wait{i}")
        if i % 2 == 0:
            g_in, g_out, g_up, g_dn, g_cw = gathered
            s["w_in"], s["w_out"] = g_in, g_out.reshape(E_A, D)
        else:
            g_in, g_glu, g_dd, g_up, g_dn, g_cw = gathered
            s["w_in"], s["w_glu"] = g_in.reshape(D, E_B), g_glu
            s["dd"] = g_dd.reshape(KB, 1, KG * C)
        s["w_up"], s["w_dn"], s["cw"] = g_up, g_dn.reshape(F2 // 2, D), g_cw
        if i % 2 == 0:
            s["p"] = mm_nn(s["hn"], s["w_in"], name=f"a_in{i}")
            s["bexp"] = jnp.repeat(a_b_s[j].T, E_A // H, axis=1)
            s["us"] = sgu_fwd(s["p"], a_g_v[j][None], a_w_s[j], s["bexp"], name=f"sgu_fwd{i}")
            h_mid = mm_nn(s["us"], s["w_out"], res=h, name=f"a_out{i}")
        else:
            s["uu"] = mm_nn(s["hn"], s["w_in"], name=f"b_in{i}")
            s["prm"] = (b_a_re[j], b_a_im[j], b_log_dt[j][:, None],
                        b_b_re[j].transpose(2, 0, 1), b_b_im[j].transpose(2, 0, 1))
            ar, ai, bbr, bbi = s5_disc_fwd(*s["prm"], name=f"s5_disc{i}")
            to_blocks = lambda t: t.reshape(C, KB, KG, P).transpose(1, 2, 0, 3)
            s["bd_r"] = _expand(to_blocks(bbr), eye).astype(BF16)
            s["bd_i"] = _expand(to_blocks(bbi), eye).astype(BF16)
            s["ct_r"] = _expand(b_c_re[j].reshape(KB, KG, C, P), eye).astype(BF16)
            s["ct_i"] = _expand(b_c_im[j].reshape(KB, KG, C, P), eye).astype(BF16)
            s["ab_r"], s["ab_i"] = ar.reshape(KB, 1, KG * P), ai.reshape(KB, 1, KG * P)
            s["y"], s["q"], s["st_r"], s["st_i"] = s5_fwd(
                s["uu"], pm, pmt, s["bd_r"], s["bd_i"], s["ct_r"], s["ct_i"], s["ab_r"], s["ab_i"], s["dd"],
                tb=tb, name=f"s5_fwd{i}")
            s["pg"] = mm_nn(s["q"], s["w_glu"], name=f"b_glu{i}")
            h_mid = glu_fwd(h, s["pg"], name=f"glu_fwd{i}")
        s["h_mid"] = h_mid
        s["hn2"] = rms_fwd(h_mid, norm_ffn_g[i][None], name=f"rms_ffn{i}")
        s["z"] = mm_nn(s["hn2"], s["w_up"], name=f"f_up{i}")
        s["a"] = convglu_fwd(s["z"], s["cw"], f_conv_b[i][None], name=f"convglu_fwd{i}")
        h = mm_nn(s["a"], s["w_dn"], res=h_mid, name=f"f_down{i}")
        saved.append(s)

    loss_tile, dh, dhb, dg_final = loss_head(h, final_g[None], loss_target[0], name="loss_head")
    loss = lax.psum(loss_tile[0, 0], AXES)

    gbig = {n: [None] * W[n].shape[0] for n in BIG}
    gs = {n: [None] * W[n].shape[0] for n in WEIGHTS if n not in BIG and n != "final_g"}
    pending = []
    for i in reversed(range(depth)):
        j = i // 2
        s = saved[i]
        gbig["f_w_down"][i] = mm_tn(s["a"], dhb, blocks=1, name=f"g_down{i}").reshape(N_DEV, F2 // 2 // N_DEV, D)
        da = mm_nt(dhb, s["w_dn"], name=f"d_a{i}")
        dacc, dwg, dwv, dbg, dbv = convglu_bwd_acc(s["z"], da, s["cw"], f_conv_b[i][None], name=f"convglu_bwd{i}")
        gs["f_conv_w"][i] = jnp.concatenate([dwg, dwv], axis=0)
        gs["f_conv_b"][i] = jnp.concatenate([dbg, dbv], axis=1)[0]
        dz = conv_bwd_in(dacc, s["cw"], name=f"conv_bwd_in{i}")
        gbig["f_w_up"][i] = mm_tn(s["hn2"], dz, blocks=N_DEV, name=f"g_up{i}")
        dhn2 = mm_nt(dz, s["w_up"], name=f"d_hn2{i}")
        st = exchange_start([gbig["f_w_down"][i], gbig["f_w_up"][i]], bcast=False, name=f"xchg_ffn_start{i}")
        pending.append((st, [("f_w_down", i), ("f_w_up", i)], f"ffn{i}"))
        dh_mid, dmb, dg = rms_bwd(s["h_mid"], norm_ffn_g[i][None], dhn2, dh, dep=st[-1], name=f"rms_ffn_bwd{i}")
        gs["norm_ffn_g"][i] = dg[0]
        if i % 2 == 0:
            gbig["a_w_out"][j] = mm_tn(s["us"], dmb, blocks=1, name=f"g_aout{i}").reshape(N_DEV, E_A // N_DEV, D)
            d_us = mm_nt(dmb, s["w_out"], name=f"d_us{i}")
            dp, dws, dbt, dgv = sgu_bwd(s["p"], d_us, a_g_v[j][None], a_w_s[j], s["bexp"], name=f"sgu_bwd{i}")
            gs["a_w_s"][j], gs["a_b_s"][j], gs["a_g_v"][j] = dws, dbt[:, :H].T, dgv[0]
            gbig["a_w_in"][j] = mm_tn(s["hn"], dp, blocks=N_DEV, name=f"g_ain{i}")
            dhn = mm_nt(dp, s["w_in"], name=f"d_hn_a{i}")
        else:
            dpg = glu_bwd(dh_mid, s["pg"], name=f"glu_bwd{i}")
            gbig["b_w_glu"][j] = mm_tn(s["q"], dpg, blocks=N_DEV, name=f"g_glu{i}")
            dq = mm_nt(dpg, s["w_glu"], name=f"d_q{i}")
            duu, dbr, dbi, dcr, dci, dar, dai, ddd = s5_bwd(
                s["uu"], s["y"], dq, pm, pmt, s["st_r"], s["st_i"], s["bd_r"], s["bd_i"], s["ct_r"], s["ct_i"],
                s["ab_r"], s["ab_i"], s["dd"], tb=tb, name=f"s5_bwd{i}")
            from_blocks = lambda t: _diag_blocks(t, KG, C, P, eye).transpose(2, 0, 1, 3).reshape(C, G, P)
            d_are, d_aim, d_ldt, d_bre, d_bim = s5_disc_bwd(
                *s["prm"], dar.reshape(G, P), dai.reshape(G, P), from_blocks(dbr), from_blocks(dbi),
                name=f"s5_disc_bwd{i}")
            gs["b_a_re"][j], gs["b_a_im"][j], gs["b_log_dt"][j] = d_are, d_aim, d_ldt[:, 0]
            gs["b_b_re"][j], gs["b_b_im"][j] = d_bre.transpose(1, 2, 0), d_bim.transpose(1, 2, 0)
            gs["b_c_re"][j] = _diag_blocks(dcr, KG, C, P, eye).reshape(G, C, P)
            gs["b_c_im"][j] = _diag_blocks(dci, KG, C, P, eye).reshape(G, C, P)
            gs["b_d"][j] = ddd.reshape(E_B)
            gbig["b_w_in"][j] = mm_tn(s["hn"], duu, blocks=1, name=f"g_bin{i}").reshape(N_DEV, D // N_DEV, E_B)
            dhn = mm_nt(duu, s["w_in"], name=f"d_hn_b{i}")
        mix = ("a_w_out", "a_w_in") if i % 2 == 0 else ("b_w_glu", "b_w_in")
        st = exchange_start([gbig[n][j] for n in mix], bcast=False, name=f"xchg_mix_start{i}")
        pending.append((st, [(n, j) for n in mix], f"mix{i}"))
        dh, dhb, dg = rms_bwd(s["h"], norm_mix_g[i][None], dhn, dh_mid, dep=st[-1], name=f"rms_mix_bwd{i}")
        gs["norm_mix_g"][i] = dg[0]
    grad_x = dh[None]

    grads, deltas, new_m, new_v = {}, {}, {}, {}
    small = [n for n in WEIGHTS if n not in BIG]
    full = {n: (dg_final[0] if n == "final_g" else jnp.stack(gs[n])) for n in small}
    flat = _flatten_pack([full[n] for n in small])
    rows = flat.shape[0] // N_DEV // LANES
    st_small = exchange_start([flat.reshape(N_DEV, rows, LANES)], bcast=False, name="xchg_small_start")

    done = {n: None for n in BIG}
    after = st_small[-1]
    for st, items, tag in pending:
        recvs = exchange_wait(st, after, bcast=False, name=f"xchg_wait_{tag}")
        for (n, l), recv in zip(items, recvs):
            done[n] = adam_reduce(recv, W[n], Mo[n], Vo[n], l, done[n], name=f"adam_{n}{l}")
            after = done[n][0]
    for n in BIG:
        grads[n], deltas[n], new_m[n], new_v[n] = done[n]

    recv = exchange_wait(st_small, after, bcast=False, name="xchg_small_wait")[0]
    part = sum_slots(recv, name="sum_small")
    tot = all_gather([part], name="gather_small")[0].reshape(-1)
    red = dict(zip(small, _unpack(tot, [full[n] for n in small])))
    red["f_conv_w"] = lax.dynamic_index_in_dim(red["f_conv_w"], me, axis=1, keepdims=False)
    red["b_d"] = lax.dynamic_slice_in_dim(red["b_d"], me * (E_B // N_DEV), E_B // N_DEV, axis=1)
    gflat = _flatten_pack([red[n] for n in small]).reshape(-1, LANES)
    d_f, m_f, v_f = adam_flat(
        gflat, _flatten_pack([W[n] for n in small]).reshape(-1, LANES),
        _flatten_pack([Mo[n] for n in small]).reshape(-1, LANES),
        _flatten_pack([Vo[n] for n in small]).reshape(-1, LANES), name="adam_small")
    like = [W[n] for n in small]
    for n, d_, m_, v_ in zip(small, _unpack(d_f.reshape(-1), like), _unpack(m_f.reshape(-1), like),
                             _unpack(v_f.reshape(-1), like)):
        grads[n], deltas[n], new_m[n], new_v[n] = red[n], d_, m_, v_

    return (loss, grad_x, *[grads[n] for n in WEIGHTS], *[deltas[n] for n in WEIGHTS],
            *[new_m[n] for n in WEIGHTS], *[new_v[n] for n in WEIGHTS])
```

```python
import math

import jax
import jax.numpy as jnp
from jax import lax
from jax.experimental import pallas as pl
from jax.experimental.pallas import tpu as pltpu

F32 = jnp.float32
BF16 = jnp.bfloat16
N_DEV = 8
AXES = ("x", "y", "c")
EPS = 1e-6
LANES = 128
SUBLANES = 8
VMEM_BIG = 56 * 1024 * 1024
VMEM_MID = 40 * 1024 * 1024
ADAM_LR, ADAM_B1, ADAM_B2, ADAM_EPS, ADAM_WD, ADAM_STEP = 0.001, 0.9, 0.999, 1e-08, 0.01, 10
MESH = pl.DeviceIdType.MESH
GELU_C = math.sqrt(2.0 / math.pi)
GELU_K = 0.044715


def _pick(n, prefs):
    for p in prefs:
        if n % p == 0:
            return p
    return n


def _params(sem, vmem=None):
    return pltpu.CompilerParams(dimension_semantics=sem, vmem_limit_bytes=vmem)


def _gelu(x):
    return 0.5 * x * (1.0 + jnp.tanh(GELU_C * (x + GELU_K * x * x * x)))


def _gelu_grad(x):
    x2 = x * x
    th = jnp.tanh(GELU_C * x * (1.0 + GELU_K * x2))
    return 0.5 * (1.0 + th) + 0.5 * x * (1.0 - th * th) * GELU_C * (1.0 + 3.0 * GELU_K * x2)


def _sigmoid(x):
    return 1.0 / (1.0 + jnp.exp(-x))


def _dot_nn(a, b):
    return jnp.dot(a, b, preferred_element_type=F32)


def _dot_nt(a, b):
    return lax.dot_general(a, b, (((1,), (1,)), ((), ())), preferred_element_type=F32)


def _dot_tn(a, b):
    return lax.dot_general(a, b, (((0,), (0,)), ((), ())), preferred_element_type=F32)


M_TILES = (1024, 512, 256, 128)
TN_M_TILES = (2048, 1024, 512, 256, 128)
FULL_K = 2048
NT_SPAN = 2816
N_TILES = (1408, 1024, 512, 384, 256, 128)
K_TILES = (1408, 1024, 512, 384, 256, 128)


def _as3(b):
    return b if b.ndim == 3 else b[None]


def mm_nn(a, b, *, res=None, out_dtype=F32, name):
    b3 = _as3(b)
    M, K = a.shape
    J, _, nb = b3.shape
    tm, tn = _pick(M, M_TILES), _pick(nb, N_TILES)
    tk = K if K <= FULL_K else _pick(K, (2816,) + K_TILES)
    per, nk = nb // tn, K // tk

    def body(*refs):
        if res is None:
            a_ref, b_ref, o_ref, acc = refs
        else:
            a_ref, b_ref, r_ref, o_ref, acc = refs
        k = pl.program_id(2)
        if nk == 1:
            r = _dot_nn(a_ref[...], b_ref[...])
            if res is not None:
                r = r + r_ref[...]
            o_ref[...] = r.astype(out_dtype)
            return

        @pl.when(k == 0)
        def _():
            acc[...] = jnp.zeros_like(acc)

        acc[...] += _dot_nn(a_ref[...], b_ref[...])

        @pl.when(k == nk - 1)
        def _():
            r = acc[...]
            if res is not None:
                r = r + r_ref[...]
            o_ref[...] = r.astype(out_dtype)

    in_specs = [pl.BlockSpec((tm, tk), lambda i, n, k: (i, k)),
                pl.BlockSpec((None, tk, tn), lambda i, n, k: (n // per, k, n % per))]
    args = [a, b3]
    if res is not None:
        in_specs.append(pl.BlockSpec((tm, tn), lambda i, n, k: (i, n)))
        args.append(res)
    return pl.pallas_call(
        body, grid=(M // tm, J * per, nk), in_specs=in_specs,
        out_specs=pl.BlockSpec((tm, tn), lambda i, n, k: (i, n)),
        out_shape=jax.ShapeDtypeStruct((M, J * nb), out_dtype),
        scratch_shapes=[pltpu.VMEM((tm, tn) if nk > 1 else (SUBLANES, LANES), F32)], name=name,
        compiler_params=_params(("parallel", "parallel", "arbitrary"), VMEM_BIG))(*args)


def mm_nt(dy, b, *, out_dtype=F32, name):
    b3 = _as3(b)
    M, N = dy.shape
    J, K, nb = b3.shape
    tm = _pick(M, M_TILES)
    tn = nb if (J == 1 and nb <= FULL_K) else _pick(nb, N_TILES)
    grp = max([g for g in (4, 2, 1) if J % g == 0 and g * tn <= NT_SPAN]) if tn == nb else 1
    tk = _pick(K, K_TILES) if (grp > 1 or K > FULL_K) else K
    per, nn = nb // tn, (J * nb) // (tn * grp)

    def body(d_ref, b_ref, o_ref, acc):
        if grp > 1:
            r = sum(_dot_nt(d_ref[:, g * tn:(g + 1) * tn], b_ref[g]) for g in range(grp))
        else:
            r = _dot_nt(d_ref[...], b_ref[...])
        if nn == 1:
            o_ref[...] = r.astype(out_dtype)
            return
        n = pl.program_id(2)

        @pl.when(n == 0)
        def _():
            acc[...] = jnp.zeros_like(acc)

        acc[...] += r

        @pl.when(n == nn - 1)
        def _():
            o_ref[...] = acc[...].astype(out_dtype)

    if grp > 1:
        b_spec = pl.BlockSpec((grp, tk, tn), lambda i, k, n: (n, k, 0))
    else:
        b_spec = pl.BlockSpec((None, tk, tn), lambda i, k, n: (n // per, k, n % per))
    return pl.pallas_call(
        body, grid=(M // tm, K // tk, nn),
        in_specs=[pl.BlockSpec((tm, tn * grp), lambda i, k, n: (i, n)), b_spec],
        out_specs=pl.BlockSpec((tm, tk), lambda i, k, n: (i, k)),
        out_shape=jax.ShapeDtypeStruct((M, K), out_dtype),
        scratch_shapes=[pltpu.VMEM((tm, tk) if nn > 1 else (SUBLANES, LANES), F32)], name=name,
        compiler_params=_params(("parallel", "parallel", "arbitrary"), VMEM_BIG))(dy, b3)


def mm_tn(x, dy, *, blocks, out_dtype=BF16, name):
    M, K = x.shape
    _, N = dy.shape
    nb = N // blocks
    tm, tn, tk = _pick(M, TN_M_TILES), _pick(nb, N_TILES), _pick(K, K_TILES)
    per, nm = nb // tn, M // tm

    def body(x_ref, d_ref, o_ref, acc):
        m = pl.program_id(2)

        @pl.when(m == 0)
        def _():
            acc[...] = jnp.zeros_like(acc)

        acc[...] += _dot_tn(x_ref[...], d_ref[...])

        @pl.when(m == nm - 1)
        def _():
            o_ref[...] = acc[...].astype(out_dtype)

    return pl.pallas_call(
        body, grid=(K // tk, N // tn, nm),
        in_specs=[pl.BlockSpec((tm, tk), lambda k, n, m: (m, k)),
                  pl.BlockSpec((tm, tn), lambda k, n, m: (m, n))],
        out_specs=pl.BlockSpec((None, tk, tn), lambda k, n, m: (n // per, k, n % per)),
        out_shape=jax.ShapeDtypeStruct((blocks, K, nb), out_dtype),
        scratch_shapes=[pltpu.VMEM((tk, tn), F32)], name=name,
        compiler_params=_params(("parallel", "parallel", "arbitrary"), VMEM_BIG))(x, dy)


ROW_TILES = (256, 128)


_DEP = pl.BlockSpec(memory_space=pl.ANY)


def rms_fwd(h, g, *, dep=None, name):
    T, D = h.shape
    tm = _pick(T, ROW_TILES)
    deps = [] if dep is None else [dep]

    def body(h_ref, g_ref, *rest):
        o_ref = rest[-1]
        x = h_ref[...]
        r = lax.rsqrt(jnp.mean(x * x, axis=-1, keepdims=True) + EPS)
        o_ref[...] = (x * r * g_ref[...]).astype(BF16)

    return pl.pallas_call(
        body, grid=(T // tm,),
        in_specs=[pl.BlockSpec((tm, D), lambda i: (i, 0)), pl.BlockSpec((1, D), lambda i: (0, 0))] + [_DEP] * len(deps),
        out_specs=pl.BlockSpec((tm, D), lambda i: (i, 0)),
        out_shape=jax.ShapeDtypeStruct((T, D), BF16), name=name,
        compiler_params=_params(("parallel",), VMEM_MID))(h, g, *deps)


def rms_bwd(h, g, dhn, dres, *, dep=None, name):
    T, D = h.shape
    tm = _pick(T, ROW_TILES)
    deps = [] if dep is None else [dep]

    def body(h_ref, g_ref, d_ref, r_ref, *rest):
        dh_ref, dhb_ref, dg_ref = rest[-3:]

        @pl.when(pl.program_id(0) == 0)
        def _():
            dg_ref[...] = jnp.zeros_like(dg_ref)

        x = h_ref[...]
        r = lax.rsqrt(jnp.mean(x * x, axis=-1, keepdims=True) + EPS)
        xh = x * r
        dy = d_ref[...]
        dxh = dy * g_ref[...]
        dh = r_ref[...] + r * (dxh - xh * jnp.mean(dxh * xh, axis=-1, keepdims=True))
        dh_ref[...] = dh
        dhb_ref[...] = dh.astype(BF16)
        dg_ref[...] += jnp.sum(dy * xh, axis=0, keepdims=True)

    row = pl.BlockSpec((tm, D), lambda i: (i, 0))
    vec = pl.BlockSpec((1, D), lambda i: (0, 0))
    return pl.pallas_call(
        body, grid=(T // tm,), in_specs=[row, vec, row, row] + [_DEP] * len(deps), out_specs=[row, row, vec],
        out_shape=[jax.ShapeDtypeStruct((T, D), F32), jax.ShapeDtypeStruct((T, D), BF16),
                   jax.ShapeDtypeStruct((1, D), F32)], name=name,
        compiler_params=_params(("arbitrary",), VMEM_MID))(h, g, dhn, dres, *deps)


def loss_head(h, g, tgt, *, name):
    T, D = h.shape
    tm = _pick(T, ROW_TILES)

    def body(h_ref, g_ref, t_ref, l_ref, dh_ref, dhb_ref, dg_ref):
        @pl.when(pl.program_id(0) == 0)
        def _():
            dg_ref[...] = jnp.zeros_like(dg_ref)
            l_ref[...] = jnp.zeros_like(l_ref)

        x = h_ref[...]
        gg = g_ref[...]
        r = lax.rsqrt(jnp.mean(x * x, axis=-1, keepdims=True) + EPS)
        xh = x * r
        e = xh * gg - t_ref[...]
        l_ref[...] += 0.5 * jnp.sum(jnp.mean(e * e, axis=-1, keepdims=True), axis=0, keepdims=True)
        dy = e * (1.0 / D)
        dxh = dy * gg
        dh = r * (dxh - xh * jnp.mean(dxh * xh, axis=-1, keepdims=True))
        dh_ref[...] = dh
        dhb_ref[...] = dh.astype(BF16)
        dg_ref[...] += jnp.sum(dy * xh, axis=0, keepdims=True)

    row = pl.BlockSpec((tm, D), lambda i: (i, 0))
    vec = pl.BlockSpec((1, D), lambda i: (0, 0))
    return pl.pallas_call(
        body, grid=(T // tm,), in_specs=[row, vec, row],
        out_specs=[pl.BlockSpec((SUBLANES, LANES), lambda i: (0, 0)), row, row, vec],
        out_shape=[jax.ShapeDtypeStruct((SUBLANES, LANES), F32), jax.ShapeDtypeStruct((T, D), F32),
                   jax.ShapeDtypeStruct((T, D), BF16), jax.ShapeDtypeStruct((1, D), F32)], name=name,
        compiler_params=_params(("arbitrary",), VMEM_MID))(h, g, tgt)


def _sgu_common(p, gv, w_ref, bexp, E, H, CH):
    Dg = E // H
    z = _gelu(p)
    u, v = z[:, :E], z[:, E:]
    r = lax.rsqrt(jnp.mean(v * v, axis=-1, keepdims=True) + EPS)
    vhat = v * r
    vn = (vhat * gv).astype(BF16)
    row = lax.broadcasted_iota(jnp.int32, (CH, CH), 0)
    col = lax.broadcasted_iota(jnp.int32, (CH, CH), 1)
    causal = row >= col
    ws = [jnp.where(causal, w_ref[hh], 0.0).astype(BF16) for hh in range(H)]
    s = jnp.concatenate([_dot_nn(ws[hh], vn[:, hh * Dg:(hh + 1) * Dg]) for hh in range(H)], axis=1) + bexp
    return u, r, vhat, vn, causal, ws, s


def sgu_fwd(p, g_v, w_s, bexp, *, name):
    T, E2 = p.shape
    E = E2 // 2
    H, CH, _ = w_s.shape

    def body(p_ref, gv_ref, w_ref, b_ref, o_ref):
        u, _, _, _, _, _, s = _sgu_common(p_ref[...], gv_ref[...], w_ref, b_ref[...], E, H, CH)
        o_ref[...] = (u * s).astype(BF16)

    return pl.pallas_call(
        body, grid=(T // CH,),
        in_specs=[pl.BlockSpec((CH, E2), lambda i: (i, 0)), pl.BlockSpec((1, E), lambda i: (0, 0)),
                  pl.BlockSpec((H, CH, CH), lambda i: (0, 0, 0)), pl.BlockSpec((CH, E), lambda i: (0, 0))],
        out_specs=pl.BlockSpec((CH, E), lambda i: (i, 0)),
        out_shape=jax.ShapeDtypeStruct((T, E), BF16), name=name,
        compiler_params=_params(("parallel",), VMEM_BIG))(p, g_v, w_s, bexp)


def sgu_bwd(p, d_us, g_v, w_s, bexp, *, name):
    T, E2 = p.shape
    E = E2 // 2
    H, CH, _ = w_s.shape
    Dg = E // H

    def body(p_ref, d_ref, gv_ref, w_ref, b_ref, dp_ref, dw_ref, db_ref, dg_ref):
        @pl.when(pl.program_id(0) == 0)
        def _():
            dw_ref[...] = jnp.zeros_like(dw_ref)
            db_ref[...] = jnp.zeros_like(db_ref)
            dg_ref[...] = jnp.zeros_like(dg_ref)

        p = p_ref[...]
        gv = gv_ref[...]
        u, r, vhat, vn, causal, ws, s = _sgu_common(p, gv, w_ref, b_ref[...], E, H, CH)
        d = d_ref[...]
        du = d * s
        ds = d * u
        lane = lax.broadcasted_iota(jnp.int32, (CH, LANES), 1)
        dvn_parts = []
        db = jnp.zeros((CH, LANES), F32)
        for hh in range(H):
            ds_h = ds[:, hh * Dg:(hh + 1) * Dg]
            ds_hb = ds_h.astype(BF16)
            dw_ref[hh] += jnp.where(causal, _dot_nt(ds_hb, vn[:, hh * Dg:(hh + 1) * Dg]), 0.0)
            dvn_parts.append(_dot_tn(ws[hh], ds_hb))
            db = db + jnp.where(lane == hh, jnp.sum(ds_h, axis=1, keepdims=True), 0.0)
        db_ref[...] += db
        dvn = jnp.concatenate(dvn_parts, axis=1)
        dg_ref[...] += jnp.sum(dvn * vhat, axis=0, keepdims=True)
        dvh = dvn * gv
        dv = r * (dvh - vhat * jnp.mean(dvh * vhat, axis=-1, keepdims=True))
        dp_ref[...] = (jnp.concatenate([du, dv], axis=1) * _gelu_grad(p)).astype(BF16)

    return pl.pallas_call(
        body, grid=(T // CH,),
        in_specs=[pl.BlockSpec((CH, E2), lambda i: (i, 0)), pl.BlockSpec((CH, E), lambda i: (i, 0)),
                  pl.BlockSpec((1, E), lambda i: (0, 0)), pl.BlockSpec((H, CH, CH), lambda i: (0, 0, 0)),
                  pl.BlockSpec((CH, E), lambda i: (0, 0))],
        out_specs=[pl.BlockSpec((CH, E2), lambda i: (i, 0)), pl.BlockSpec((H, CH, CH), lambda i: (0, 0, 0)),
                   pl.BlockSpec((CH, LANES), lambda i: (0, 0)), pl.BlockSpec((1, E), lambda i: (0, 0))],
        out_shape=[jax.ShapeDtypeStruct((T, E2), BF16), jax.ShapeDtypeStruct((H, CH, CH), F32),
                   jax.ShapeDtypeStruct((CH, LANES), F32), jax.ShapeDtypeStruct((1, E), F32)], name=name,
        compiler_params=_params(("arbitrary",), VMEM_BIG))(p, d_us, g_v, w_s, bexp)


def _s5_disc(a_re, a_im, log_dt, b_re, b_im):
    dt = jnp.exp(log_dt)
    mag = jnp.exp(dt * a_re)
    ar, ai = mag * jnp.cos(dt * a_im), mag * jnp.sin(dt * a_im)
    den = a_re * a_re + a_im * a_im
    qr = ((ar - 1.0) * a_re + ai * a_im) / den
    qi = (ai * a_re - (ar - 1.0) * a_im) / den
    return ar, ai, qr[None] * b_re - qi[None] * b_im, qr[None] * b_im + qi[None] * b_re


def s5_disc_fwd(a_re, a_im, log_dt, b_re, b_im, *, name):
    G, P = a_re.shape
    C = b_re.shape[0]

    def body(ar_ref, ai_ref, dt_ref, br_ref, bi_ref, o_ar, o_ai, o_br, o_bi):
        ar, ai, br, bi = _s5_disc(ar_ref[...], ai_ref[...], dt_ref[...], br_ref[...], bi_ref[...])
        o_ar[...] = ar
        o_ai[...] = ai
        o_br[...] = br
        o_bi[...] = bi

    gp = jax.ShapeDtypeStruct((G, P), F32)
    cgp = jax.ShapeDtypeStruct((C, G, P), F32)
    return pl.pallas_call(body, out_shape=[gp, gp, cgp, cgp], name=name)(a_re, a_im, log_dt, b_re, b_im)


def s5_disc_bwd(a_re, a_im, log_dt, b_re, b_im, d_ar, d_ai, d_br, d_bi, *, name):
    G, P = a_re.shape
    C = b_re.shape[0]

    def body(ar_ref, ai_ref, dt_ref, br_ref, bi_ref, g0, g1, g2, g3, o0, o1, o2, o3, o4):
        prim = (ar_ref[...], ai_ref[...], dt_ref[...], br_ref[...], bi_ref[...])
        _, vjp = jax.vjp(_s5_disc, *prim)
        outs = vjp((g0[...], g1[...], g2[...], g3[...]))
        for o, v in zip((o0, o1, o2, o3, o4), outs):
            o[...] = v

    gp = jax.ShapeDtypeStruct((G, P), F32)
    cgp = jax.ShapeDtypeStruct((C, G, P), F32)
    return pl.pallas_call(
        body, out_shape=[gp, gp, jax.ShapeDtypeStruct((G, 1), F32), cgp, cgp], name=name,
    )(a_re, a_im, log_dt, b_re, b_im, d_ar, d_ai, d_br, d_bi)


S5_KG = 8


def _planes(x):
    return [x[:, c * LANES:(c + 1) * LANES] for c in range(x.shape[1] // LANES)]


def _store_planes(ref, row0, val):
    for c, p in enumerate(_planes(val)):
        ref[c, pl.ds(row0, val.shape[0]), :] = p


def _load_planes(ref, row0, rows):
    return jnp.concatenate([ref[c, pl.ds(row0, rows), :] for c in range(ref.shape[0])], axis=1)


def _build_powers(a_r, a_i, S, pf_r, pf_i, pr_r=None, pr_i=None):
    ar, ai = _planes(a_r), _planes(a_i)
    NP = len(ar)

    def step(i, carry):
        out = []
        for c in range(NP):
            p_r, p_i = carry[2 * c], carry[2 * c + 1]
            pf_r[c, pl.ds(i, 1), :] = p_r
            pf_i[c, pl.ds(i, 1), :] = p_i
            if pr_r is not None:
                pr_r[c, pl.ds(S - 1 - i, 1), :] = p_r
                pr_i[c, pl.ds(S - 1 - i, 1), :] = -p_i
            out += [ar[c] * p_r - ai[c] * p_i, ar[c] * p_i + ai[c] * p_r]
        return tuple(out)

    init = []
    for c in range(NP):
        init += [ar[c], ai[c]]
    lax.fori_loop(0, S, step, tuple(init))


def _scan_seg(hr, hi, hrow0, tb, a_r, a_i, pw_r, pw_i, h0r, h0i, *, reverse):
    NP = hr.shape[0]
    S = tb // SUBLANES
    if reverse:
        a_i = -a_i
    ar, ai = _planes(a_r), _planes(a_i)

    def step(i, carry):
        j = (S - 1 - i) if reverse else i
        slab = pl.ds(pl.multiple_of(hrow0 + j * SUBLANES, SUBLANES), SUBLANES)
        out = []
        for c in range(NP):
            nr = ar[c] * carry[2 * c] - ai[c] * carry[2 * c + 1] + hr[c, slab, :]
            ni = ar[c] * carry[2 * c + 1] + ai[c] * carry[2 * c] + hi[c, slab, :]
            hr[c, slab, :] = nr
            hi[c, slab, :] = ni
            out += [nr, ni]
        return tuple(out)

    z = jnp.zeros((SUBLANES, LANES), F32)
    loc = lax.fori_loop(0, S, step, (z,) * (2 * NP), unroll=2)
    h0r_p, h0i_p = _planes(h0r), _planes(h0i)
    top = 0 if reverse else S - 1
    order = range(SUBLANES - 1, -1, -1) if reverse else range(SUBLANES)
    out_r, out_i, ent_r, ent_i = [], [], [], []
    for c in range(NP):
        s_r, s_i = pw_r[c, pl.ds(top, 1), :], pw_i[c, pl.ds(top, 1), :]
        c_r, c_i = h0r_p[c], h0i_p[c]
        in_r, in_i = [None] * SUBLANES, [None] * SUBLANES
        for seg in order:
            in_r[seg], in_i[seg] = c_r, c_i
            l_r, l_i = loc[2 * c][seg:seg + 1], loc[2 * c + 1][seg:seg + 1]
            c_r, c_i = s_r * c_r - s_i * c_i + l_r, s_r * c_i + s_i * c_r + l_i
        out_r.append(c_r)
        out_i.append(c_i)
        ent_r.append(jnp.concatenate(in_r, axis=0))
        ent_i.append(jnp.concatenate(in_i, axis=0))

    def fix(j, _):
        slab = pl.ds(pl.multiple_of(hrow0 + j * SUBLANES, SUBLANES), SUBLANES)
        for c in range(NP):
            p_r, p_i = pw_r[c, pl.ds(j, 1), :], pw_i[c, pl.ds(j, 1), :]
            hr[c, slab, :] += p_r * ent_r[c] - p_i * ent_i[c]
            hi[c, slab, :] += p_r * ent_i[c] + p_i * ent_r[c]
        return 0

    lax.fori_loop(0, S, fix, 0, unroll=2)
    return ent_r, ent_i, jnp.concatenate(out_r, axis=1), jnp.concatenate(out_i, axis=1)


def _perm_matrices(tb):
    r = jnp.arange(tb)
    pm = (r[None, :] == ((r % SUBLANES) * (tb // SUBLANES) + r // SUBLANES)[:, None]).astype(BF16)
    return pm, pm.T


def _unpermute(pmt, x):
    c = x.shape[1]
    hi = x.astype(BF16)
    lo = (x - hi.astype(F32)).astype(BF16)
    both = _dot_nn(pmt, jnp.concatenate([hi, lo], axis=1))
    return both[:, :c] + both[:, c:]


def _s5_specs(tb, UC, SC, rev_nb=None):
    tmap = (lambda b: b) if rev_nb is None else (lambda b: rev_nb - 1 - b)
    row = pl.BlockSpec((tb, UC), lambda b, k: (tmap(b), k))
    wsp = pl.BlockSpec((None, UC, SC), lambda b, k: (k, 0, 0))
    vsc = pl.BlockSpec((None, 1, SC), lambda b, k: (k, 0, 0))
    vuc = pl.BlockSpec((None, 1, UC), lambda b, k: (k, 0, 0))
    st = pl.BlockSpec((None, None, 1, SC), lambda b, k: (tmap(b), k, 0, 0))
    return row, wsp, vsc, vuc, st


def s5_fwd(uu, pm, pmt, bd_r, bd_i, ct_r, ct_i, ab_r, ab_i, dd, *, tb, name):
    T, E = uu.shape
    KB, UC, SC = bd_r.shape
    NB = T // tb
    NP, S = SC // LANES, tb // SUBLANES

    def body(u_ref, pm_ref, pmt_ref, bdr, bdi, ctr, cti, ar_ref, ai_ref, dd_ref, y_ref, q_ref, sr_ref, si_ref,
             hr, hi, cr, ci, pf_r, pf_i):
        b, k = pl.program_id(0), pl.program_id(1)
        a_r, a_i = ar_ref[...], ai_ref[...]

        @pl.when(b == 0)
        def _():
            cr[k] = jnp.zeros((1, SC), F32)
            ci[k] = jnp.zeros((1, SC), F32)
            _build_powers(a_r, a_i, S, pf_r.at[k], pf_i.at[k])

        h0r, h0i = cr[k], ci[k]
        sr_ref[...] = h0r
        si_ref[...] = h0i
        u = u_ref[...]
        up = _dot_nn(pm_ref[...], u.astype(BF16)).astype(BF16)
        _store_planes(hr, 0, _dot_nn(up, bdr[...]))
        _store_planes(hi, 0, _dot_nn(up, bdi[...]))
        _, _, o_r, o_i = _scan_seg(hr, hi, 0, tb, a_r, a_i, pf_r.at[k], pf_i.at[k], h0r, h0i, reverse=False)
        cr[k] = o_r
        ci[k] = o_i
        ys = (_dot_nt(_load_planes(hr, 0, tb).astype(BF16), ctr[...])
              - _dot_nt(_load_planes(hi, 0, tb).astype(BF16), cti[...]))
        pmt_v = pmt_ref[...]
        y = _unpermute(pmt_v, ys) + dd_ref[...] * u
        y_ref[...] = y
        q_ref[...] = _gelu(y).astype(BF16)

    row, wsp, vsc, vuc, st = _s5_specs(tb, UC, SC)
    psp = pl.BlockSpec((tb, tb), lambda b, k: (0, 0))
    stsh = jax.ShapeDtypeStruct((NB, KB, 1, SC), F32)
    pw = pltpu.VMEM((KB, NP, S, LANES), F32)
    pln = pltpu.VMEM((NP, tb, LANES), F32)
    return pl.pallas_call(
        body, grid=(NB, KB), in_specs=[row, psp, psp, wsp, wsp, wsp, wsp, vsc, vsc, vuc],
        out_specs=[row, row, st, st],
        out_shape=[jax.ShapeDtypeStruct((T, E), F32), jax.ShapeDtypeStruct((T, E), BF16), stsh, stsh],
        scratch_shapes=[pln, pln, pltpu.VMEM((KB, 1, SC), F32), pltpu.VMEM((KB, 1, SC), F32), pw, pw],
        name=name, compiler_params=_params(("arbitrary", "arbitrary"), VMEM_MID),
    )(uu, pm, pmt, bd_r, bd_i, ct_r, ct_i, ab_r, ab_i, dd)


def s5_bwd(uu, y, dq, pm, pmt, st_r, st_i, bd_r, bd_i, ct_r, ct_i, ab_r, ab_i, dd, *, tb, name):
    T, E = uu.shape
    KB, UC, SC = bd_r.shape
    NB = T // tb
    HDR = SUBLANES
    NP, S = SC // LANES, tb // SUBLANES
    pw = pltpu.VMEM((KB, NP, S, LANES), F32)
    pln = pltpu.VMEM((NP, tb, LANES), F32)

    def body(u_ref, y_ref, dq_ref, pm_ref, pmt_ref, sr_ref, si_ref, bdr, bdi, ctr, cti, ar_ref, ai_ref, dd_ref,
             du_ref, obr, obi, ocr, oci, odar, odai, oddd,
             hr, hi, gr, gi, kr, ki, abr, abi, acr, aci, pf_r, pf_i, pr_r, pr_i):
        b, k = pl.program_id(0), pl.program_id(1)
        a_r, a_i = ar_ref[...], ai_ref[...]

        @pl.when(b == 0)
        def _():
            _build_powers(a_r, a_i, S, pf_r.at[k], pf_i.at[k], pr_r.at[k], pr_i.at[k])
            z1 = jnp.zeros((1, SC), F32)
            kr[k] = z1
            ki[k] = z1
            odar[k] = z1
            odai[k] = z1
            oddd[k] = jnp.zeros((1, UC), F32)
            zw = jnp.zeros((UC, SC), F32)
            abr[k] = zw
            abi[k] = zw
            acr[k] = zw
            aci[k] = zw

        u = u_ref[...]
        dy = dq_ref[...] * _gelu_grad(y_ref[...])
        oddd[k] += jnp.sum(dy * u, axis=0, keepdims=True)
        pm_v = pm_ref[...]
        both = _dot_nn(pm_v, jnp.concatenate([u.astype(BF16), dy.astype(BF16)], axis=1))
        ub = both[:, :UC].astype(BF16)
        dyb = both[:, UC:].astype(BF16)
        s0r, s0i = sr_ref[...], si_ref[...]
        _store_planes(hr, HDR, _dot_nn(ub, bdr[...]))
        _store_planes(hi, HDR, _dot_nn(ub, bdi[...]))
        e_r, e_i, _, _ = _scan_seg(hr, hi, HDR, tb, a_r, a_i, pf_r.at[k], pf_i.at[k], s0r, s0i, reverse=False)
        for c in range(NP):
            hr[c, pl.ds(0, HDR), :] = e_r[c]
            hi[c, pl.ds(0, HDR), :] = e_i[c]
        _store_planes(gr, 0, _dot_nn(dyb, ctr[...]))
        _store_planes(gi, 0, -_dot_nn(dyb, cti[...]))
        _, _, g0r, g0i = _scan_seg(gr, gi, 0, tb, a_r, a_i, pr_r.at[k], pr_i.at[k], kr[k], ki[k], reverse=True)
        kr[k] = g0r
        ki[k] = g0i

        def slab(j, acc):
            o = pl.multiple_of(j * SUBLANES, SUBLANES)
            out = []
            for c in range(NP):
                p_r, p_i = hr[c, pl.ds(o, SUBLANES), :], hi[c, pl.ds(o, SUBLANES), :]
                g_r, g_i = gr[c, pl.ds(o, SUBLANES), :], gi[c, pl.ds(o, SUBLANES), :]
                out += [acc[2 * c] + g_r * p_r + g_i * p_i, acc[2 * c + 1] + g_i * p_r - g_r * p_i]
            return tuple(out)

        z8 = jnp.zeros((SUBLANES, LANES), F32)
        acc = lax.fori_loop(0, S, slab, (z8,) * (2 * NP), unroll=2)
        odar[k] += jnp.concatenate([jnp.sum(acc[2 * c], axis=0, keepdims=True) for c in range(NP)], axis=1)
        odai[k] += jnp.concatenate([jnp.sum(acc[2 * c + 1], axis=0, keepdims=True) for c in range(NP)], axis=1)
        g_rb = _load_planes(gr, 0, tb).astype(BF16)
        g_ib = _load_planes(gi, 0, tb).astype(BF16)
        h_rb = _load_planes(hr, HDR, tb).astype(BF16)
        h_ib = _load_planes(hi, HDR, tb).astype(BF16)
        dus = _dot_nt(g_rb, bdr[...]) + _dot_nt(g_ib, bdi[...])
        pmt_v = pmt_ref[...]
        du = _unpermute(pmt_v, dus) + dd_ref[...] * dy
        du_ref[...] = du.astype(BF16)
        abr[k] += _dot_tn(ub, g_rb)
        abi[k] += _dot_tn(ub, g_ib)
        acr[k] += _dot_tn(dyb, h_rb)
        aci[k] -= _dot_tn(dyb, h_ib)

        @pl.when(jnp.logical_and(b == NB - 1, k == KB - 1))
        def _():
            pltpu.sync_copy(abr, obr)
            pltpu.sync_copy(abi, obi)
            pltpu.sync_copy(acr, ocr)
            pltpu.sync_copy(aci, oci)

    row, wsp, vsc, vuc, st = _s5_specs(tb, UC, SC, rev_nb=NB)
    psp = pl.BlockSpec((tb, tb), lambda b, k: (0, 0))
    hbm = pl.BlockSpec(memory_space=pltpu.HBM)
    full_sc = pl.BlockSpec((KB, 1, SC), lambda b, k: (0, 0, 0))
    full_uc = pl.BlockSpec((KB, 1, UC), lambda b, k: (0, 0, 0))
    wsh = jax.ShapeDtypeStruct((KB, UC, SC), F32)
    acc = pltpu.VMEM((KB, UC, SC), F32)
    return pl.pallas_call(
        body, grid=(NB, KB),
        in_specs=[row, row, row, psp, psp, st, st, wsp, wsp, wsp, wsp, vsc, vsc, vuc],
        out_specs=[row, hbm, hbm, hbm, hbm, full_sc, full_sc, full_uc],
        out_shape=[jax.ShapeDtypeStruct((T, E), BF16), wsh, wsh, wsh, wsh,
                   jax.ShapeDtypeStruct((KB, 1, SC), F32), jax.ShapeDtypeStruct((KB, 1, SC), F32),
                   jax.ShapeDtypeStruct((KB, 1, UC), F32)],
        scratch_shapes=[pltpu.VMEM((NP, tb + HDR, LANES), F32), pltpu.VMEM((NP, tb + HDR, LANES), F32), pln, pln,
                        pltpu.VMEM((KB, 1, SC), F32), pltpu.VMEM((KB, 1, SC), F32), acc, acc, acc, acc,
                        pw, pw, pw, pw],
        name=name, compiler_params=_params(("arbitrary", "arbitrary"), VMEM_BIG),
    )(uu, y, dq, pm, pmt, st_r, st_i, bd_r, bd_i, ct_r, ct_i, ab_r, ab_i, dd)


def glu_fwd(h, pg, *, name):
    T, D = h.shape
    tm = _pick(T, ROW_TILES)

    def body(h_ref, a_ref, b_ref, o_ref):
        o_ref[...] = h_ref[...] + a_ref[...] * _sigmoid(b_ref[...])

    row = pl.BlockSpec((tm, D), lambda i: (i, 0))
    return pl.pallas_call(
        body, grid=(T // tm,), in_specs=[row, row, pl.BlockSpec((tm, D), lambda i: (i, 1))], out_specs=row,
        out_shape=jax.ShapeDtypeStruct((T, D), F32), name=name,
        compiler_params=_params(("parallel",), VMEM_MID))(h, pg, pg)


def glu_bwd(d, pg, *, name):
    T, D = d.shape
    tm = _pick(T, ROW_TILES)

    def body(d_ref, a_ref, b_ref, o_ref):
        dv = d_ref[...]
        sg = _sigmoid(b_ref[...])
        da = dv * sg
        db = dv * a_ref[...] * sg * (1.0 - sg)
        o_ref[...] = jnp.where(pl.program_id(1) == 0, da, db).astype(BF16)

    row = pl.BlockSpec((tm, D), lambda i, hf: (i, 0))
    return pl.pallas_call(
        body, grid=(T // tm, 2), in_specs=[row, row, pl.BlockSpec((tm, D), lambda i, hf: (i, 1))],
        out_specs=pl.BlockSpec((tm, D), lambda i, hf: (i, hf)),
        out_shape=jax.ShapeDtypeStruct((T, 2 * D), BF16), name=name,
        compiler_params=_params(("parallel", "arbitrary"), VMEM_MID))(d, pg, pg)


def _shift_down(x, halo, s):
    r = pltpu.roll(x, s, axis=0)
    hr = pltpu.roll(halo, s, axis=0)
    row = lax.broadcasted_iota(jnp.int32, halo.shape, 0)
    head = jnp.where(row < s, hr, r[:SUBLANES])
    return jnp.concatenate([head, r[SUBLANES:]], axis=0)


def _shift_up(x, halo, s):
    n = x.shape[0]
    r = pltpu.roll(x, n - s, axis=0)
    hr = pltpu.roll(halo, SUBLANES - s, axis=0)
    row = lax.broadcasted_iota(jnp.int32, halo.shape, 0)
    tail = jnp.where(row >= SUBLANES - s, hr, r[n - SUBLANES:])
    return jnp.concatenate([r[:n - SUBLANES], tail], axis=0)


def _conv_acc(z, zh, w, b, first):
    kw = w.shape[0]
    zh = jnp.where(first, 0.0, zh)
    acc = b + w[kw - 1:kw] * z
    shifted = []
    for k in range(kw - 1):
        zs = _shift_down(z, zh, kw - 1 - k)
        shifted.append(zs)
        acc = acc + w[k:k + 1] * zs
    return acc, shifted


def _conv_specs(T, F, tm, tc, KW):
    nfb = F // tc
    rb = tm // SUBLANES

    def main(off):
        return pl.BlockSpec((tm, tc), lambda i, c: (i, c + off))

    def halo(off):
        return pl.BlockSpec((SUBLANES, tc), lambda i, c: (jnp.maximum(i * rb - 1, 0), c + off))

    def wspec(off):
        return pl.BlockSpec((None, KW, tc), lambda i, c: (c + off, 0, 0))

    def bspec(off):
        return pl.BlockSpec((1, tc), lambda i, c: (0, c + off))

    return nfb, main, halo, wspec, bspec


def convglu_fwd(z, cw, cb, *, name):
    T, F2 = z.shape
    F = F2 // 2
    _, KW, tc = cw.shape
    tm = _pick(T, (256, 128))
    nfb, main, halo, wspec, bspec = _conv_specs(T, F, tm, tc, KW)

    def body(zg, zgh, zv, zvh, wg, wv, bg, bv, o_ref):
        first = pl.program_id(0) == 0
        g, _ = _conv_acc(zg[...], zgh[...], wg[...], bg[...], first)
        v, _ = _conv_acc(zv[...], zvh[...], wv[...], bv[...], first)
        o_ref[...] = (g * _sigmoid(g) * v).astype(BF16)

    return pl.pallas_call(
        body, grid=(T // tm, nfb),
        in_specs=[main(0), halo(0), main(nfb), halo(nfb), wspec(0), wspec(nfb), bspec(0), bspec(nfb)],
        out_specs=pl.BlockSpec((tm, tc), lambda i, c: (i, c)),
        out_shape=jax.ShapeDtypeStruct((T, F), BF16), name=name,
        compiler_params=_params(("parallel", "parallel"), VMEM_BIG))(z, z, z, z, cw, cw, cb, cb)


def convglu_bwd_acc(z, da, cw, cb, *, name):
    T, F2 = z.shape
    F = F2 // 2
    _, KW, tc = cw.shape
    tm = _pick(T, (256, 128))
    nfb, main, halo, wspec, bspec = _conv_specs(T, F, tm, tc, KW)

    def body(zg, zgh, zv, zvh, wg, wv, bg, bv, da_ref, o_ref, dwg, dwv, dbg, dbv):
        i = pl.program_id(1)
        first = i == 0

        @pl.when(first)
        def _():
            for o in (dwg, dwv, dbg, dbv):
                o[...] = jnp.zeros_like(o)

        zg_v, zv_v = zg[...], zv[...]
        g, sg_ = _conv_acc(zg_v, zgh[...], wg[...], bg[...], first)
        v, sv_ = _conv_acc(zv_v, zvh[...], wv[...], bv[...], first)
        d = da_ref[...]
        sig = _sigmoid(g)
        dg = d * v * sig * (1.0 + g * (1.0 - sig))
        dv = d * g * sig
        o_ref[0] = dg.astype(BF16)
        o_ref[1] = dv.astype(BF16)
        dbg[...] += jnp.sum(dg, axis=0, keepdims=True)
        dbv[...] += jnp.sum(dv, axis=0, keepdims=True)
        for k in range(KW):
            xg = zg_v if k == KW - 1 else sg_[k]
            xv = zv_v if k == KW - 1 else sv_[k]
            dwg[pl.ds(k, 1), :] += jnp.sum(dg * xg, axis=0, keepdims=True)
            dwv[pl.ds(k, 1), :] += jnp.sum(dv * xv, axis=0, keepdims=True)

    def sw(spec_fn, off):
        s = spec_fn(off)
        return pl.BlockSpec(s.block_shape, lambda c, i, f=s.index_map: f(i, c))

    both = jax.ShapeDtypeStruct((2, T, F), BF16)
    dwsh = jax.ShapeDtypeStruct((nfb, KW, tc), F32)
    dbsh = jax.ShapeDtypeStruct((1, F), F32)
    outs = pl.pallas_call(
        body, grid=(nfb, T // tm),
        in_specs=[sw(main, 0), sw(halo, 0), sw(main, nfb), sw(halo, nfb), sw(wspec, 0), sw(wspec, nfb),
                  sw(bspec, 0), sw(bspec, nfb), pl.BlockSpec((tm, tc), lambda c, i: (i, c))],
        out_specs=[pl.BlockSpec((2, tm, tc), lambda c, i: (0, i, c)),
                   pl.BlockSpec((None, KW, tc), lambda c, i: (c, 0, 0)), pl.BlockSpec((None, KW, tc), lambda c, i: (c, 0, 0)),
                   pl.BlockSpec((1, tc), lambda c, i: (0, c)), pl.BlockSpec((1, tc), lambda c, i: (0, c))],
        out_shape=[both, dwsh, dwsh, dbsh, dbsh], name=name,
        compiler_params=_params(("parallel", "arbitrary"), VMEM_BIG))(z, z, z, z, cw, cw, cb, cb, da)
    return outs


def conv_bwd_in(dacc, cw, *, name):
    _, T, F = dacc.shape
    _, KW, tc = cw.shape
    nfb = F // tc
    tm = _pick(T, (256, 128))
    rb = tm // (2 * SUBLANES)
    last_blk = T // (2 * SUBLANES) - 1

    def body(d_ref, dn_ref, w_ref, o_ref):
        last = pl.program_id(0) == pl.num_programs(0) - 1
        d = d_ref[...].astype(F32)
        dn = jnp.where(last, 0.0, dn_ref[...].astype(F32)[:SUBLANES])
        w = w_ref[...]
        out = w[KW - 1:KW] * d
        for k in range(KW - 1):
            out = out + w[k:k + 1] * _shift_up(d, dn, KW - 1 - k)
        o_ref[...] = out.astype(BF16)

    return pl.pallas_call(
        body, grid=(T // tm, 2, nfb),
        in_specs=[pl.BlockSpec((None, tm, tc), lambda i, hf, c: (hf, i, c)),
                  pl.BlockSpec((None, 2 * SUBLANES, tc), lambda i, hf, c: (hf, jnp.minimum((i + 1) * rb, last_blk), c)),
                  pl.BlockSpec((None, KW, tc), lambda i, hf, c: (hf * nfb + c, 0, 0))],
        out_specs=pl.BlockSpec((tm, tc), lambda i, hf, c: (i, hf * nfb + c)),
        out_shape=jax.ShapeDtypeStruct((T, 2 * F), BF16), name=name,
        compiler_params=_params(("parallel", "parallel", "parallel"), VMEM_BIG))(dacc, dacc, cw)


def _my_place():
    x, y, c = (lax.axis_index(a) for a in AXES)
    return x, y, c, 4 * x + 2 * y + c


def _peer(m, x, y, c):
    px = 1 - x if (m >> 2) & 1 else x
    py = 1 - y if (m >> 1) & 1 else y
    pc = 1 - c if m & 1 else c
    return (px, py, pc), 4 * px + 2 * py + pc


def _exchange(ins, out_shapes, plan, *, bcast, name):
    n_in, n_out, n = len(ins), len(out_shapes), len(plan)

    def body(*refs):
        in_refs, out_refs = refs[:n_in], refs[n_in:n_in + n_out]
        send_sems, recv_sems, loc_sems = refs[n_in + n_out:]
        x, y, c, me = _my_place()

        def src(f, who):
            r = in_refs[plan[f][0]]
            return r if bcast else r.at[who]

        def dst(f, who):
            r = out_refs[plan[f][1]]
            lay = plan[f][2]
            return r.at[who] if lay is None else r.at[lay, who]

        def remote(f, m, landing):
            dev, plin = _peer(m, x, y, c)
            return pltpu.make_async_remote_copy(
                src_ref=src(f, plin), dst_ref=dst(f, plin if landing else me), send_sem=send_sems.at[f, m - 1],
                recv_sem=recv_sems.at[f, m - 1], device_id=dev, device_id_type=MESH)

        locs = [pltpu.make_async_copy(src(f, me), dst(f, me), loc_sems.at[f]) for f in range(n)]
        for cp in locs:
            cp.start()
        for m in range(1, N_DEV):
            for f in range(n):
                remote(f, m, False).start()
        for m in range(1, N_DEV):
            for f in range(n):
                remote(f, m, True).wait()
        for cp in locs:
            cp.wait()

    hbm = pl.BlockSpec(memory_space=pltpu.HBM)
    return pl.pallas_call(
        body, in_specs=[hbm] * n_in, out_specs=[hbm] * n_out, out_shape=out_shapes,
        scratch_shapes=[pltpu.SemaphoreType.DMA((n, N_DEV - 1)), pltpu.SemaphoreType.DMA((n, N_DEV - 1)),
                        pltpu.SemaphoreType.DMA((n,))],
        name=name)(*ins)


def all_gather(shards, *, name):
    outs = [jax.ShapeDtypeStruct((N_DEV,) + s.shape, s.dtype) for s in shards]
    return _exchange(shards, outs, [(a, a, None) for a in range(len(shards))], bcast=True, name=name)


_HBM = pl.BlockSpec(memory_space=pltpu.HBM)
_SEM = pl.BlockSpec(memory_space=pltpu.SEMAPHORE)
_EFFECT = pltpu.SideEffectType.DATAFLOW_SIDE_EFFECTING


def _split_copy(in_refs, land_refs, send_sems, recv_sems, bcast, f, m, place, landing):
    x, y, c, me = place
    dev, plin = _peer(m, x, y, c)
    src = in_refs[f] if bcast else in_refs[f].at[plin]
    return pltpu.make_async_remote_copy(
        src_ref=src, dst_ref=land_refs[f].at[plin if landing else me],
        send_sem=send_sems.at[f * (N_DEV - 1) + m - 1], recv_sem=recv_sems.at[f * (N_DEV - 1) + m - 1],
        device_id=dev, device_id_type=MESH)


def _own_copy(in_refs, land_refs, own_sems, bcast, f, place):
    me = place[3]
    return pltpu.make_async_copy(in_refs[f] if bcast else in_refs[f].at[me], land_refs[f].at[me], own_sems.at[f])


PEERS_ALL = tuple(range(1, N_DEV))
PEERS_DIRECT = (1, 2, 4, 6)
SLOTS_PASSED = (2, 4, 6)


def exchange_start(ins, *, bcast, masks=PEERS_ALL, dep=None, name):
    n = len(ins)
    lands = [lax.empty(((N_DEV,) + a.shape) if bcast else a.shape, a.dtype) for a in ins]
    deps = [] if dep is None else [dep]
    nd = len(deps)

    def body(*refs):
        in_refs, land_refs = refs[:n], refs[n:2 * n]
        send_sems, recv_sems, own_sems = refs[2 * n + nd:2 * n + nd + 3]
        token = refs[-1]
        place = _my_place()
        for m in masks:
            for f in range(n):
                _split_copy(in_refs, land_refs, send_sems, recv_sems, bcast, f, m, place, False).start()
        for f in range(n):
            _own_copy(in_refs, land_refs, own_sems, bcast, f, place).start()
        token[...] = jnp.zeros_like(token)

    arrs = [pltpu.with_memory_space_constraint(a, pltpu.HBM) for a in (*ins, *lands)]
    sems = pltpu.SemaphoreType.DMA((n * (N_DEV - 1),))
    outs = pl.pallas_call(
        body, name=name,
        out_shape=(sems, sems, pltpu.SemaphoreType.DMA((n,)), *[pltpu.HBM(a.shape, a.dtype) for a in arrs],
                   jax.ShapeDtypeStruct((SUBLANES, LANES), F32)),
        in_specs=[_HBM] * (2 * n) + [_DEP] * nd,
        out_specs=(_SEM, _SEM, _SEM, *[_HBM] * (2 * n), pl.BlockSpec(memory_space=pltpu.VMEM)),
        input_output_aliases={i: 3 + i for i in range(2 * n)},
        compiler_params=pltpu.CompilerParams(has_side_effects=_EFFECT))(*arrs, *deps)
    return outs[0], outs[1], outs[2], list(outs[3:3 + 2 * n]), outs[-1]


def exchange_wait(started, after, *, bcast, masks=PEERS_ALL, name):
    send_sems, recv_sems, own_sems, thrus, _ = started
    n = len(thrus) // 2

    def body(*refs):
        in_refs, land_refs = refs[:n], refs[n:2 * n]
        send, recv, own = refs[2 * n:2 * n + 3]
        place = _my_place()
        for f in range(n):
            _own_copy(in_refs, land_refs, own, bcast, f, place).wait()
        for m in masks:
            for f in range(n):
                cp = _split_copy(in_refs, land_refs, send, recv, bcast, f, m, place, True)
                cp.wait_send()
                cp.wait_recv()

    outs = pl.pallas_call(
        body, name=name, out_shape=[pltpu.HBM(a.shape, a.dtype) for a in thrus],
        in_specs=[_HBM] * (2 * n) + [_SEM, _SEM, _SEM, pl.BlockSpec(memory_space=pl.ANY)], out_specs=[_HBM] * (2 * n),
        input_output_aliases={i: i for i in range(2 * n)},
        compiler_params=pltpu.CompilerParams(has_side_effects=_EFFECT))(
            *thrus, send_sems, recv_sems, own_sems, after)
    return list(outs[n:])


def _pass_copy(land_refs, send_sems, recv_sems, f, k, place, landing):
    x, y, c, _ = place
    m = SLOTS_PASSED[k]
    sib, _ = _peer(1, x, y, c)
    _, mine = _peer(m, x, y, c)
    _, theirs = _peer(m ^ 1, x, y, c)
    i = f * len(SLOTS_PASSED) + k
    return pltpu.make_async_remote_copy(
        src_ref=land_refs[f].at[mine], dst_ref=land_refs[f].at[theirs if landing else mine],
        send_sem=send_sems.at[i], recv_sem=recv_sems.at[i], device_id=sib, device_id_type=MESH)


def pass_start(lands, *, dep=None, name):
    n = len(lands)
    deps = [] if dep is None else [dep]
    nd = len(deps)

    def body(*refs):
        land_refs = refs[:n]
        send_sems, recv_sems, token = refs[n + nd], refs[n + nd + 1], refs[-1]
        place = _my_place()
        for k in range(len(SLOTS_PASSED)):
            for f in range(n):
                _pass_copy(land_refs, send_sems, recv_sems, f, k, place, False).start()
        token[...] = jnp.zeros_like(token)

    sems = pltpu.SemaphoreType.DMA((n * len(SLOTS_PASSED),))
    outs = pl.pallas_call(
        body, name=name,
        out_shape=(sems, sems, *[pltpu.HBM(a.shape, a.dtype) for a in lands], jax.ShapeDtypeStruct((SUBLANES, LANES), F32)),
        in_specs=[_HBM] * n + [_DEP] * nd,
        out_specs=(_SEM, _SEM, *[_HBM] * n, pl.BlockSpec(memory_space=pltpu.VMEM)),
        input_output_aliases={i: 2 + i for i in range(n)},
        compiler_params=pltpu.CompilerParams(has_side_effects=_EFFECT))(*lands, *deps)
    return outs[0], outs[1], list(outs[2:2 + n]), outs[-1]


def pass_wait(started, after, *, name):
    send_sems, recv_sems, thrus, _ = started
    n = len(thrus)

    def body(*refs):
        land_refs = refs[:n]
        send, recv = refs[n], refs[n + 1]
        place = _my_place()
        for k in range(len(SLOTS_PASSED)):
            for f in range(n):
                cp = _pass_copy(land_refs, send, recv, f, k, place, True)
                cp.wait_send()
                cp.wait_recv()

    outs = pl.pallas_call(
        body, name=name, out_shape=[pltpu.HBM(a.shape, a.dtype) for a in thrus],
        in_specs=[_HBM] * n + [_SEM, _SEM, pl.BlockSpec(memory_space=pl.ANY)], out_specs=[_HBM] * n,
        input_output_aliases={i: i for i in range(n)},
        compiler_params=pltpu.CompilerParams(has_side_effects=_EFFECT))(*thrus, send_sems, recv_sems, after)
    return list(outs)


def _adamw(w, g, m, v):
    m = ADAM_B1 * m + (1.0 - ADAM_B1) * g
    v = ADAM_B2 * v + (1.0 - ADAM_B2) * (g * g)
    m_hat = m / (1.0 - ADAM_B1 ** ADAM_STEP)
    v_hat = v / (1.0 - ADAM_B2 ** ADAM_STEP)
    delta = -ADAM_LR * (m_hat / (jnp.sqrt(v_hat) + ADAM_EPS) + ADAM_WD * w)
    return delta, m, v


def adam_reduce(recv, w, m, v, l, prev, *, name):
    _, R, C = recv.shape
    L = w.shape[0]
    budget = 4 * 1024 * 1024
    tr = R
    for cand in (1024, 512, 352, 256, 176, 128, 64, 32, 16):
        if R % cand == 0 and N_DEV * cand * C * recv.dtype.itemsize <= budget:
            tr = cand
            break

    def body(r_ref, w_ref, m_ref, v_ref, *rest):
        g_ref, d_ref, nm_ref, nv_ref = rest[-4:]
        g = r_ref[0].astype(F32)
        for s in range(1, N_DEV):
            g = g + r_ref[s].astype(F32)
        d, nm, nv = _adamw(w_ref[...], g, m_ref[...], v_ref[...])
        g_ref[...] = g
        d_ref[...] = d
        nm_ref[...] = nm
        nv_ref[...] = nv

    blk = pl.BlockSpec((None, tr, C), lambda r: (l, r, 0))
    sh = jax.ShapeDtypeStruct((L, R, C), F32)
    extra = [] if prev is None else list(prev)
    return pl.pallas_call(
        body, grid=(R // tr,),
        in_specs=[pl.BlockSpec((N_DEV, tr, C), lambda r: (0, r, 0)), blk, blk, blk]
        + [pl.BlockSpec(memory_space=pl.ANY)] * len(extra),
        out_specs=[blk] * 4, out_shape=[sh] * 4, name=name,
        input_output_aliases={4 + i: i for i in range(len(extra))},
        compiler_params=_params(("parallel",), VMEM_MID))(recv, w, m, v, *extra)


def sum_slots(recv, *, name):
    _, R, C = recv.shape
    tr = _pick(R, (512, 256, 128, 64, 32, 16, 8))

    def body(r_ref, o_ref):
        g = r_ref[0]
        for s in range(1, N_DEV):
            g = g + r_ref[s]
        o_ref[...] = g

    return pl.pallas_call(
        body, grid=(R // tr,), in_specs=[pl.BlockSpec((N_DEV, tr, C), lambda r: (0, r, 0))],
        out_specs=pl.BlockSpec((tr, C), lambda r: (r, 0)), out_shape=jax.ShapeDtypeStruct((R, C), F32),
        name=name, compiler_params=_params(("parallel",)))(recv)


def adam_flat(g, w, m, v, *, name):
    R, C = g.shape
    tr = _pick(R, (512, 256, 128, 64, 32, 16, 8))

    def body(g_ref, w_ref, m_ref, v_ref, d_ref, nm_ref, nv_ref):
        d, nm, nv = _adamw(w_ref[...], g_ref[...], m_ref[...], v_ref[...])
        d_ref[...] = d
        nm_ref[...] = nm
        nv_ref[...] = nv

    blk = pl.BlockSpec((tr, C), lambda r: (r, 0))
    sh = jax.ShapeDtypeStruct((R, C), F32)
    return pl.pallas_call(body, grid=(R // tr,), in_specs=[blk] * 4, out_specs=[blk] * 3, out_shape=[sh] * 3,
                          name=name, compiler_params=_params(("parallel",)))(g, w, m, v)


WEIGHTS = ("norm_mix_g", "norm_ffn_g", "a_w_in", "a_g_v", "a_w_s", "a_b_s", "a_w_out", "b_w_in", "b_a_re", "b_a_im",
           "b_log_dt", "b_b_re", "b_b_im", "b_c_re", "b_c_im", "b_d", "b_w_glu", "f_w_up", "f_conv_w", "f_conv_b",
           "f_w_down", "final_g")
BIG = ("a_w_in", "a_w_out", "b_w_in", "b_w_glu", "f_w_up", "f_w_down")
FLAT_CHUNK = N_DEV * SUBLANES * LANES


def _expand(blocks, eye):
    KB, KG, C, P = blocks.shape
    return (blocks[:, :, :, None, :] * eye[None, :, None, :, None]).reshape(KB, KG * C, KG * P)


def _diag_blocks(dense, KG, C, P, eye):
    KB = dense.shape[0]
    return jnp.einsum("kgchp,gh->kgcp", dense.reshape(KB, KG, C, KG, P), eye)


def _flatten_pack(parts):
    flat = jnp.concatenate([p.reshape(-1) for p in parts])
    pad = (-flat.shape[0]) % FLAT_CHUNK
    return jnp.pad(flat, (0, pad))


def _unpack(flat, like):
    out, o = [], 0
    for p in like:
        n = math.prod(p.shape)
        out.append(flat[o:o + n].reshape(p.shape))
        o += n
    return out


def kernel(x, norm_mix_g, norm_ffn_g, a_w_in, a_g_v, a_w_s, a_b_s, a_w_out, b_w_in, b_a_re, b_a_im, b_log_dt, b_b_re, b_b_im, b_c_re, b_c_im, b_d, b_w_glu, f_w_up, f_conv_w, f_conv_b, f_w_down, final_g, loss_target, m_norm_mix_g, m_norm_ffn_g, m_a_w_in, m_a_g_v, m_a_w_s, m_a_b_s, m_a_w_out, m_b_w_in, m_b_a_re, m_b_a_im, m_b_log_dt, m_b_b_re, m_b_b_im, m_b_c_re, m_b_c_im, m_b_d, m_b_w_glu, m_f_w_up, m_f_conv_w, m_f_conv_b, m_f_w_down, m_final_g, v_norm_mix_g, v_norm_ffn_g, v_a_w_in, v_a_g_v, v_a_w_s, v_a_b_s, v_a_w_out, v_b_w_in, v_b_a_re, v_b_a_im, v_b_log_dt, v_b_b_re, v_b_b_im, v_b_c_re, v_b_c_im, v_b_d, v_b_w_glu, v_f_w_up, v_f_conv_w, v_f_conv_b, v_f_w_down, v_final_g):
    env = dict(locals())
    W = {n: env[n] for n in WEIGHTS}
    Mo = {n: env["m_" + n] for n in WEIGHTS}
    Vo = {n: env["v_" + n] for n in WEIGHTS}

    _, T, D = x.shape
    depth = norm_mix_g.shape[0]
    E_A = a_g_v.shape[1]
    H = a_w_s.shape[1]
    G, P, C = b_b_re.shape[1], b_b_re.shape[2], b_b_re.shape[3]
    E_B = G * C
    KG = S5_KG
    KB = G // KG
    F2 = f_conv_b.shape[1]
    tb = _pick(T, (512, 256, 128))
    pm, pmt = _perm_matrices(tb)
    eye = jnp.eye(KG, dtype=F32)
    _, _, _, me = _my_place()

    wb = {n: W[n].astype(BF16) for n in BIG}

    def shards(i):
        j = i // 2
        if i % 2 == 0:
            return [wb["a_w_in"][j], wb["a_w_out"][j], wb["f_w_up"][i], wb["f_w_down"][i], f_conv_w[i]]
        return [wb["b_w_in"][j], wb["b_w_glu"][j], b_d[j][None], wb["f_w_up"][i], wb["f_w_down"][i], f_conv_w[i]]

    h = x[0]
    started = exchange_start(shards(0), bcast=True, masks=PEERS_DIRECT, name="gather_start0")
    saved = []
    for i in range(depth):
        j = i // 2
        s = {}
        lands = exchange_wait(started, h, bcast=True, masks=PEERS_DIRECT, name=f"gather_wait{i}")
        passing = pass_start(lands, name=f"gather_pass_start{i}")
        dep = passing[-1]
        if i + 1 < depth:
            started = exchange_start(shards(i + 1), bcast=True, masks=PEERS_DIRECT, dep=dep,
                                     name=f"gather_start{i + 1}")
            dep = started[-1]
        s["h"] = h
        s["hn"] = rms_fwd(h, norm_mix_g[i][None], dep=dep, name=f"rms_mix{i}")
        gathered = pass_wait(passing, s["hn"], name=f"gather_pass_wait{i}")
        if i % 2 == 0:
            g_in, g_out, g_up, g_dn, g_cw = gathered
            s["w_in"], s["w_out"] = g_in, g_out.reshape(E_A, D)
        else:
            g_in, g_glu, g_dd, g_up, g_dn, g_cw = gathered
            s["w_in"], s["w_glu"] = g_in.reshape(D, E_B), g_glu
            s["dd"] = g_dd.reshape(KB, 1, KG * C)
        s["w_up"], s["w_dn"], s["cw"] = g_up, g_dn.reshape(F2 // 2, D), g_cw
        if i % 2 == 0:
            s["p"] = mm_nn(s["hn"], s["w_in"], name=f"a_in{i}")
            s["bexp"] = jnp.repeat(a_b_s[j].T, E_A // H, axis=1)
            s["us"] = sgu_fwd(s["p"], a_g_v[j][None], a_w_s[j], s["bexp"], name=f"sgu_fwd{i}")
            h_mid = mm_nn(s["us"], s["w_out"], res=h, name=f"a_out{i}")
        else:
            s["uu"] = mm_nn(s["hn"], s["w_in"], name=f"b_in{i}")
            s["prm"] = (b_a_re[j], b_a_im[j], b_log_dt[j][:, None],
                        b_b_re[j].transpose(2, 0, 1), b_b_im[j].transpose(2, 0, 1))
            ar, ai, bbr, bbi = s5_disc_fwd(*s["prm"], name=f"s5_disc{i}")
            to_blocks = lambda t: t.reshape(C, KB, KG, P).transpose(1, 2, 0, 3)
            s["bd_r"] = _expand(to_blocks(bbr), eye).astype(BF16)
            s["bd_i"] = _expand(to_blocks(bbi), eye).astype(BF16)
            s["ct_r"] = _expand(b_c_re[j].reshape(KB, KG, C, P), eye).astype(BF16)
            s["ct_i"] = _expand(b_c_im[j].reshape(KB, KG, C, P), eye).astype(BF16)
            s["ab_r"], s["ab_i"] = ar.reshape(KB, 1, KG * P), ai.reshape(KB, 1, KG * P)
            s["y"], s["q"], s["st_r"], s["st_i"] = s5_fwd(
                s["uu"], pm, pmt, s["bd_r"], s["bd_i"], s["ct_r"], s["ct_i"], s["ab_r"], s["ab_i"], s["dd"],
                tb=tb, name=f"s5_fwd{i}")
            s["pg"] = mm_nn(s["q"], s["w_glu"], name=f"b_glu{i}")
            h_mid = glu_fwd(h, s["pg"], name=f"glu_fwd{i}")
        s["h_mid"] = h_mid
        s["hn2"] = rms_fwd(h_mid, norm_ffn_g[i][None], name=f"rms_ffn{i}")
        s["z"] = mm_nn(s["hn2"], s["w_up"], name=f"f_up{i}")
        s["a"] = convglu_fwd(s["z"], s["cw"], f_conv_b[i][None], name=f"convglu_fwd{i}")
        h = mm_nn(s["a"], s["w_dn"], res=h_mid, name=f"f_down{i}")
        saved.append(s)

    loss_tile, dh, dhb, dg_final = loss_head(h, final_g[None], loss_target[0], name="loss_head")
    loss = lax.psum(loss_tile[0, 0], AXES)

    gbig = {n: [None] * W[n].shape[0] for n in BIG}
    gs = {n: [None] * W[n].shape[0] for n in WEIGHTS if n not in BIG and n != "final_g"}
    pending = []
    for i in reversed(range(depth)):
        j = i // 2
        s = saved[i]
        gbig["f_w_down"][i] = mm_tn(s["a"], dhb, blocks=1, name=f"g_down{i}").reshape(N_DEV, F2 // 2 // N_DEV, D)
        da = mm_nt(dhb, s["w_dn"], name=f"d_a{i}")
        dacc, dwg, dwv, dbg, dbv = convglu_bwd_acc(s["z"], da, s["cw"], f_conv_b[i][None], name=f"convglu_bwd{i}")
        gs["f_conv_w"][i] = jnp.concatenate([dwg, dwv], axis=0)
        gs["f_conv_b"][i] = jnp.concatenate([dbg, dbv], axis=1)[0]
        dz = conv_bwd_in(dacc, s["cw"], name=f"conv_bwd_in{i}")
        gbig["f_w_up"][i] = mm_tn(s["hn2"], dz, blocks=N_DEV, name=f"g_up{i}")
        dhn2 = mm_nt(dz, s["w_up"], name=f"d_hn2{i}")
        st = exchange_start([gbig["f_w_down"][i], gbig["f_w_up"][i]], bcast=False, name=f"xchg_ffn_start{i}")
        pending.append((st, [("f_w_down", i), ("f_w_up", i)], f"ffn{i}"))
        dh_mid, dmb, dg = rms_bwd(s["h_mid"], norm_ffn_g[i][None], dhn2, dh, dep=st[-1], name=f"rms_ffn_bwd{i}")
        gs["norm_ffn_g"][i] = dg[0]
        if i % 2 == 0:
            gbig["a_w_out"][j] = mm_tn(s["us"], dmb, blocks=1, name=f"g_aout{i}").reshape(N_DEV, E_A // N_DEV, D)
            d_us = mm_nt(dmb, s["w_out"], name=f"d_us{i}")
            dp, dws, dbt, dgv = sgu_bwd(s["p"], d_us, a_g_v[j][None], a_w_s[j], s["bexp"], name=f"sgu_bwd{i}")
            gs["a_w_s"][j], gs["a_b_s"][j], gs["a_g_v"][j] = dws, dbt[:, :H].T, dgv[0]
            gbig["a_w_in"][j] = mm_tn(s["hn"], dp, blocks=N_DEV, name=f"g_ain{i}")
            dhn = mm_nt(dp, s["w_in"], name=f"d_hn_a{i}")
        else:
            dpg = glu_bwd(dh_mid, s["pg"], name=f"glu_bwd{i}")
            gbig["b_w_glu"][j] = mm_tn(s["q"], dpg, blocks=N_DEV, name=f"g_glu{i}")
            dq = mm_nt(dpg, s["w_glu"], name=f"d_q{i}")
            duu, dbr, dbi, dcr, dci, dar, dai, ddd = s5_bwd(
                s["uu"], s["y"], dq, pm, pmt, s["st_r"], s["st_i"], s["bd_r"], s["bd_i"], s["ct_r"], s["ct_i"],
                s["ab_r"], s["ab_i"], s["dd"], tb=tb, name=f"s5_bwd{i}")
            from_blocks = lambda t: _diag_blocks(t, KG, C, P, eye).transpose(2, 0, 1, 3).reshape(C, G, P)
            d_are, d_aim, d_ldt, d_bre, d_bim = s5_disc_bwd(
                *s["prm"], dar.reshape(G, P), dai.reshape(G, P), from_blocks(dbr), from_blocks(dbi),
                name=f"s5_disc_bwd{i}")
            gs["b_a_re"][j], gs["b_a_im"][j], gs["b_log_dt"][j] = d_are, d_aim, d_ldt[:, 0]
            gs["b_b_re"][j], gs["b_b_im"][j] = d_bre.transpose(1, 2, 0), d_bim.transpose(1, 2, 0)
            gs["b_c_re"][j] = _diag_blocks(dcr, KG, C, P, eye).reshape(G, C, P)
            gs["b_c_im"][j] = _diag_blocks(dci, KG, C, P, eye).reshape(G, C, P)
            gs["b_d"][j] = ddd.reshape(E_B)
            gbig["b_w_in"][j] = mm_tn(s["hn"], duu, blocks=1, name=f"g_bin{i}").reshape(N_DEV, D // N_DEV, E_B)
            dhn = mm_nt(duu, s["w_in"], name=f"d_hn_b{i}")
        mix = ("a_w_out", "a_w_in") if i % 2 == 0 else ("b_w_glu", "b_w_in")
        st = exchange_start([gbig[n][j] for n in mix], bcast=False, name=f"xchg_mix_start{i}")
        pending.append((st, [(n, j) for n in mix], f"mix{i}"))
        dh, dhb, dg = rms_bwd(s["h"], norm_mix_g[i][None], dhn, dh_mid, dep=st[-1], name=f"rms_mix_bwd{i}")
        gs["norm_mix_g"][i] = dg[0]
    grad_x = dh[None]

    grads, deltas, new_m, new_v = {}, {}, {}, {}
    small = [n for n in WEIGHTS if n not in BIG]
    full = {n: (dg_final[0] if n == "final_g" else jnp.stack(gs[n])) for n in small}
    flat = _flatten_pack([full[n] for n in small])
    rows = flat.shape[0] // N_DEV // LANES
    st_small = exchange_start([flat.reshape(N_DEV, rows, LANES)], bcast=False, name="xchg_small_start")

    done = {n: None for n in BIG}

    def finish(entries, after):
        for st, items, tag in entries:
            recvs = exchange_wait(st, after, bcast=False, name=f"xchg_wait_{tag}")
            for (n, l), recv in zip(items, recvs):
                done[n] = adam_reduce(recv, W[n], Mo[n], Vo[n], l, done[n], name=f"adam_{n}{l}")
                after = done[n][0]
        return after

    after = finish(pending[:-1], st_small[-1])
    recv = exchange_wait(st_small, after, bcast=False, name="xchg_small_wait")[0]
    part = sum_slots(recv, name="sum_small")
    tot = all_gather([part], name="gather_small")[0].reshape(-1)
    red = dict(zip(small, _unpack(tot, [full[n] for n in small])))
    red["f_conv_w"] = lax.dynamic_index_in_dim(red["f_conv_w"], me, axis=1, keepdims=False)
    red["b_d"] = lax.dynamic_slice_in_dim(red["b_d"], me * (E_B // N_DEV), E_B // N_DEV, axis=1)
    gflat = _flatten_pack([red[n] for n in small]).reshape(-1, LANES)
    d_f, m_f, v_f = adam_flat(
        gflat, _flatten_pack([W[n] for n in small]).reshape(-1, LANES),
        _flatten_pack([Mo[n] for n in small]).reshape(-1, LANES),
        _flatten_pack([Vo[n] for n in small]).reshape(-1, LANES), name="adam_small")
    like = [W[n] for n in small]
    for n, d_, m_, v_ in zip(small, _unpack(d_f.reshape(-1), like), _unpack(m_f.reshape(-1), like),
                             _unpack(v_f.reshape(-1), like)):
        grads[n], deltas[n], new_m[n], new_v[n] = red[n], d_, m_, v_

    finish(pending[-1:], d_f)
    for n in BIG:
        grads[n], deltas[n], new_m[n], new_v[n] = done[n]

    return (loss, grad_x, *[grads[n] for n in WEIGHTS], *[deltas[n] for n in WEIGHTS],
            *[new_m[n] for n in WEIGHTS], *[new_v[n] for n in WEIGHTS])
```

```python
import math

import jax
import jax.numpy as jnp
from jax import lax
from jax.experimental import pallas as pl
from jax.experimental.pallas import tpu as pltpu

F32 = jnp.float32
BF16 = jnp.bfloat16
N_DEV = 8
AXES = ("x", "y", "c")
EPS = 1e-6
LANES = 128
SUBLANES = 8
VMEM_BIG = 56 * 1024 * 1024
VMEM_MID = 40 * 1024 * 1024
ADAM_LR, ADAM_B1, ADAM_B2, ADAM_EPS, ADAM_WD, ADAM_STEP = 0.001, 0.9, 0.999, 1e-08, 0.01, 10
MESH = pl.DeviceIdType.MESH
GELU_C = math.sqrt(2.0 / math.pi)
GELU_K = 0.044715


def _pick(n, prefs):
    for p in prefs:
        if n % p == 0:
            return p
    return n


def _params(sem, vmem=None):
    return pltpu.CompilerParams(dimension_semantics=sem, vmem_limit_bytes=vmem)


def _gelu(x):
    return 0.5 * x * (1.0 + jnp.tanh(GELU_C * (x + GELU_K * x * x * x)))


def _gelu_grad(x):
    x2 = x * x
    th = jnp.tanh(GELU_C * x * (1.0 + GELU_K * x2))
    return 0.5 * (1.0 + th) + 0.5 * x * (1.0 - th * th) * GELU_C * (1.0 + 3.0 * GELU_K * x2)


def _sigmoid(x):
    return 1.0 / (1.0 + jnp.exp(-x))


def _dot_nn(a, b):
    return jnp.dot(a, b, preferred_element_type=F32)


def _dot_nt(a, b):
    return lax.dot_general(a, b, (((1,), (1,)), ((), ())), preferred_element_type=F32)


def _dot_tn(a, b):
    return lax.dot_general(a, b, (((0,), (0,)), ((), ())), preferred_element_type=F32)


M_TILES = (1024, 512, 256, 128)
TN_M_TILES = (2048, 1024, 512, 256, 128)
FULL_K = 2048
NT_SPAN = 2816
N_TILES = (1408, 1024, 512, 384, 256, 128)
K_TILES = (1408, 1024, 512, 384, 256, 128)


def _as3(b):
    return b if b.ndim == 3 else b[None]


def mm_nn(a, b, *, res=None, out_dtype=F32, name):
    b3 = _as3(b)
    M, K = a.shape
    J, _, nb = b3.shape
    tm, tn = _pick(M, M_TILES), _pick(nb, N_TILES)
    tk = K if K <= FULL_K else _pick(K, (2816,) + K_TILES)
    per, nk = nb // tn, K // tk

    def body(*refs):
        if res is None:
            a_ref, b_ref, o_ref, acc = refs
        else:
            a_ref, b_ref, r_ref, o_ref, acc = refs
        k = pl.program_id(2)
        if nk == 1:
            r = _dot_nn(a_ref[...], b_ref[...])
            if res is not None:
                r = r + r_ref[...]
            o_ref[...] = r.astype(out_dtype)
            return

        @pl.when(k == 0)
        def _():
            acc[...] = jnp.zeros_like(acc)

        acc[...] += _dot_nn(a_ref[...], b_ref[...])

        @pl.when(k == nk - 1)
        def _():
            r = acc[...]
            if res is not None:
                r = r + r_ref[...]
            o_ref[...] = r.astype(out_dtype)

    in_specs = [pl.BlockSpec((tm, tk), lambda i, n, k: (i, k)),
                pl.BlockSpec((None, tk, tn), lambda i, n, k: (n // per, k, n % per))]
    args = [a, b3]
    if res is not None:
        in_specs.append(pl.BlockSpec((tm, tn), lambda i, n, k: (i, n)))
        args.append(res)
    return pl.pallas_call(
        body, grid=(M // tm, J * per, nk), in_specs=in_specs,
        out_specs=pl.BlockSpec((tm, tn), lambda i, n, k: (i, n)),
        out_shape=jax.ShapeDtypeStruct((M, J * nb), out_dtype),
        scratch_shapes=[pltpu.VMEM((tm, tn) if nk > 1 else (SUBLANES, LANES), F32)], name=name,
        compiler_params=_params(("parallel", "parallel", "arbitrary"), VMEM_BIG))(*args)


def mm_nt(dy, b, *, out_dtype=F32, name):
    b3 = _as3(b)
    M, N = dy.shape
    J, K, nb = b3.shape
    tm = _pick(M, M_TILES)
    tn = nb if (J == 1 and nb <= FULL_K) else _pick(nb, N_TILES)
    grp = max([g for g in (4, 2, 1) if J % g == 0 and g * tn <= NT_SPAN]) if tn == nb else 1
    tk = _pick(K, K_TILES) if (grp > 1 or K > FULL_K) else K
    per, nn = nb // tn, (J * nb) // (tn * grp)

    def body(d_ref, b_ref, o_ref, acc):
        if grp > 1:
            r = sum(_dot_nt(d_ref[:, g * tn:(g + 1) * tn], b_ref[g]) for g in range(grp))
        else:
            r = _dot_nt(d_ref[...], b_ref[...])
        if nn == 1:
            o_ref[...] = r.astype(out_dtype)
            return
        n = pl.program_id(2)

        @pl.when(n == 0)
        def _():
            acc[...] = jnp.zeros_like(acc)

        acc[...] += r

        @pl.when(n == nn - 1)
        def _():
            o_ref[...] = acc[...].astype(out_dtype)

    if grp > 1:
        b_spec = pl.BlockSpec((grp, tk, tn), lambda i, k, n: (n, k, 0))
    else:
        b_spec = pl.BlockSpec((None, tk, tn), lambda i, k, n: (n // per, k, n % per))
    return pl.pallas_call(
        body, grid=(M // tm, K // tk, nn),
        in_specs=[pl.BlockSpec((tm, tn * grp), lambda i, k, n: (i, n)), b_spec],
        out_specs=pl.BlockSpec((tm, tk), lambda i, k, n: (i, k)),
        out_shape=jax.ShapeDtypeStruct((M, K), out_dtype),
        scratch_shapes=[pltpu.VMEM((tm, tk) if nn > 1 else (SUBLANES, LANES), F32)], name=name,
        compiler_params=_params(("parallel", "parallel", "arbitrary"), VMEM_BIG))(dy, b3)


def mm_tn(x, dy, *, blocks, out_dtype=BF16, name):
    M, K = x.shape
    _, N = dy.shape
    nb = N // blocks
    tm, tn, tk = _pick(M, TN_M_TILES), _pick(nb, N_TILES), _pick(K, K_TILES)
    per, nm = nb // tn, M // tm

    def body(x_ref, d_ref, o_ref, acc):
        m = pl.program_id(2)

        @pl.when(m == 0)
        def _():
            acc[...] = jnp.zeros_like(acc)

        acc[...] += _dot_tn(x_ref[...], d_ref[...])

        @pl.when(m == nm - 1)
        def _():
            o_ref[...] = acc[...].astype(out_dtype)

    return pl.pallas_call(
        body, grid=(K // tk, N // tn, nm),
        in_specs=[pl.BlockSpec((tm, tk), lambda k, n, m: (m, k)),
                  pl.BlockSpec((tm, tn), lambda k, n, m: (m, n))],
        out_specs=pl.BlockSpec((None, tk, tn), lambda k, n, m: (n // per, k, n % per)),
        out_shape=jax.ShapeDtypeStruct((blocks, K, nb), out_dtype),
        scratch_shapes=[pltpu.VMEM((tk, tn), F32)], name=name,
        compiler_params=_params(("parallel", "parallel", "arbitrary"), VMEM_BIG))(x, dy)


ROW_TILES = (256, 128)


_DEP = pl.BlockSpec(memory_space=pl.ANY)


def rms_fwd(h, g, *, dep=None, name):
    T, D = h.shape
    tm = _pick(T, ROW_TILES)
    deps = [] if dep is None else [dep]

    def body(h_ref, g_ref, *rest):
        o_ref = rest[-1]
        x = h_ref[...]
        r = lax.rsqrt(jnp.mean(x * x, axis=-1, keepdims=True) + EPS)
        o_ref[...] = (x * r * g_ref[...]).astype(BF16)

    return pl.pallas_call(
        body, grid=(T // tm,),
        in_specs=[pl.BlockSpec((tm, D), lambda i: (i, 0)), pl.BlockSpec((1, D), lambda i: (0, 0))] + [_DEP] * len(deps),
        out_specs=pl.BlockSpec((tm, D), lambda i: (i, 0)),
        out_shape=jax.ShapeDtypeStruct((T, D), BF16), name=name,
        compiler_params=_params(("parallel",), VMEM_MID))(h, g, *deps)


def rms_bwd(h, g, dhn, dres, *, dep=None, name):
    T, D = h.shape
    tm = _pick(T, ROW_TILES)
    deps = [] if dep is None else [dep]

    def body(h_ref, g_ref, d_ref, r_ref, *rest):
        dh_ref, dhb_ref, dg_ref = rest[-3:]

        @pl.when(pl.program_id(0) == 0)
        def _():
            dg_ref[...] = jnp.zeros_like(dg_ref)

        x = h_ref[...]
        r = lax.rsqrt(jnp.mean(x * x, axis=-1, keepdims=True) + EPS)
        xh = x * r
        dy = d_ref[...]
        dxh = dy * g_ref[...]
        dh = r_ref[...] + r * (dxh - xh * jnp.mean(dxh * xh, axis=-1, keepdims=True))
        dh_ref[...] = dh
        dhb_ref[...] = dh.astype(BF16)
        dg_ref[...] += jnp.sum(dy * xh, axis=0, keepdims=True)

    row = pl.BlockSpec((tm, D), lambda i: (i, 0))
    vec = pl.BlockSpec((1, D), lambda i: (0, 0))
    return pl.pallas_call(
        body, grid=(T // tm,), in_specs=[row, vec, row, row] + [_DEP] * len(deps), out_specs=[row, row, vec],
        out_shape=[jax.ShapeDtypeStruct((T, D), F32), jax.ShapeDtypeStruct((T, D), BF16),
                   jax.ShapeDtypeStruct((1, D), F32)], name=name,
        compiler_params=_params(("arbitrary",), VMEM_MID))(h, g, dhn, dres, *deps)


def loss_head(h, g, tgt, *, name):
    T, D = h.shape
    tm = _pick(T, ROW_TILES)

    def body(h_ref, g_ref, t_ref, l_ref, dh_ref, dhb_ref, dg_ref):
        @pl.when(pl.program_id(0) == 0)
        def _():
            dg_ref[...] = jnp.zeros_like(dg_ref)
            l_ref[...] = jnp.zeros_like(l_ref)

        x = h_ref[...]
        gg = g_ref[...]
        r = lax.rsqrt(jnp.mean(x * x, axis=-1, keepdims=True) + EPS)
        xh = x * r
        e = xh * gg - t_ref[...]
        l_ref[...] += 0.5 * jnp.sum(jnp.mean(e * e, axis=-1, keepdims=True), axis=0, keepdims=True)
        dy = e * (1.0 / D)
        dxh = dy * gg
        dh = r * (dxh - xh * jnp.mean(dxh * xh, axis=-1, keepdims=True))
        dh_ref[...] = dh
        dhb_ref[...] = dh.astype(BF16)
        dg_ref[...] += jnp.sum(dy * xh, axis=0, keepdims=True)

    row = pl.BlockSpec((tm, D), lambda i: (i, 0))
    vec = pl.BlockSpec((1, D), lambda i: (0, 0))
    return pl.pallas_call(
        body, grid=(T // tm,), in_specs=[row, vec, row],
        out_specs=[pl.BlockSpec((SUBLANES, LANES), lambda i: (0, 0)), row, row, vec],
        out_shape=[jax.ShapeDtypeStruct((SUBLANES, LANES), F32), jax.ShapeDtypeStruct((T, D), F32),
                   jax.ShapeDtypeStruct((T, D), BF16), jax.ShapeDtypeStruct((1, D), F32)], name=name,
        compiler_params=_params(("arbitrary",), VMEM_MID))(h, g, tgt)


def _sgu_common(p, gv, w_ref, bexp, E, H, CH):
    Dg = E // H
    z = _gelu(p)
    u, v = z[:, :E], z[:, E:]
    r = lax.rsqrt(jnp.mean(v * v, axis=-1, keepdims=True) + EPS)
    vhat = v * r
    vn = (vhat * gv).astype(BF16)
    row = lax.broadcasted_iota(jnp.int32, (CH, CH), 0)
    col = lax.broadcasted_iota(jnp.int32, (CH, CH), 1)
    causal = row >= col
    ws = [jnp.where(causal, w_ref[hh], 0.0).astype(BF16) for hh in range(H)]
    s = jnp.concatenate([_dot_nn(ws[hh], vn[:, hh * Dg:(hh + 1) * Dg]) for hh in range(H)], axis=1) + bexp
    return u, r, vhat, vn, causal, ws, s


def sgu_fwd(p, g_v, w_s, bexp, *, name):
    T, E2 = p.shape
    E = E2 // 2
    H, CH, _ = w_s.shape

    def body(p_ref, gv_ref, w_ref, b_ref, o_ref):
        u, _, _, _, _, _, s = _sgu_common(p_ref[...], gv_ref[...], w_ref, b_ref[...], E, H, CH)
        o_ref[...] = (u * s).astype(BF16)

    return pl.pallas_call(
        body, grid=(T // CH,),
        in_specs=[pl.BlockSpec((CH, E2), lambda i: (i, 0)), pl.BlockSpec((1, E), lambda i: (0, 0)),
                  pl.BlockSpec((H, CH, CH), lambda i: (0, 0, 0)), pl.BlockSpec((CH, E), lambda i: (0, 0))],
        out_specs=pl.BlockSpec((CH, E), lambda i: (i, 0)),
        out_shape=jax.ShapeDtypeStruct((T, E), BF16), name=name,
        compiler_params=_params(("parallel",), VMEM_BIG))(p, g_v, w_s, bexp)


def sgu_bwd(p, d_us, g_v, w_s, bexp, *, name):
    T, E2 = p.shape
    E = E2 // 2
    H, CH, _ = w_s.shape
    Dg = E // H

    def body(p_ref, d_ref, gv_ref, w_ref, b_ref, dp_ref, dw_ref, db_ref, dg_ref):
        @pl.when(pl.program_id(0) == 0)
        def _():
            dw_ref[...] = jnp.zeros_like(dw_ref)
            db_ref[...] = jnp.zeros_like(db_ref)
            dg_ref[...] = jnp.zeros_like(dg_ref)

        p = p_ref[...]
        gv = gv_ref[...]
        u, r, vhat, vn, causal, ws, s = _sgu_common(p, gv, w_ref, b_ref[...], E, H, CH)
        d = d_ref[...]
        du = d * s
        ds = d * u
        lane = lax.broadcasted_iota(jnp.int32, (CH, LANES), 1)
        dvn_parts = []
        db = jnp.zeros((CH, LANES), F32)
        for hh in range(H):
            ds_h = ds[:, hh * Dg:(hh + 1) * Dg]
            ds_hb = ds_h.astype(BF16)
            dw_ref[hh] += jnp.where(causal, _dot_nt(ds_hb, vn[:, hh * Dg:(hh + 1) * Dg]), 0.0)
            dvn_parts.append(_dot_tn(ws[hh], ds_hb))
            db = db + jnp.where(lane == hh, jnp.sum(ds_h, axis=1, keepdims=True), 0.0)
        db_ref[...] += db
        dvn = jnp.concatenate(dvn_parts, axis=1)
        dg_ref[...] += jnp.sum(dvn * vhat, axis=0, keepdims=True)
        dvh = dvn * gv
        dv = r * (dvh - vhat * jnp.mean(dvh * vhat, axis=-1, keepdims=True))
        dp_ref[...] = (jnp.concatenate([du, dv], axis=1) * _gelu_grad(p)).astype(BF16)

    return pl.pallas_call(
        body, grid=(T // CH,),
        in_specs=[pl.BlockSpec((CH, E2), lambda i: (i, 0)), pl.BlockSpec((CH, E), lambda i: (i, 0)),
                  pl.BlockSpec((1, E), lambda i: (0, 0)), pl.BlockSpec((H, CH, CH), lambda i: (0, 0, 0)),
                  pl.BlockSpec((CH, E), lambda i: (0, 0))],
        out_specs=[pl.BlockSpec((CH, E2), lambda i: (i, 0)), pl.BlockSpec((H, CH, CH), lambda i: (0, 0, 0)),
                   pl.BlockSpec((CH, LANES), lambda i: (0, 0)), pl.BlockSpec((1, E), lambda i: (0, 0))],
        out_shape=[jax.ShapeDtypeStruct((T, E2), BF16), jax.ShapeDtypeStruct((H, CH, CH), F32),
                   jax.ShapeDtypeStruct((CH, LANES), F32), jax.ShapeDtypeStruct((1, E), F32)], name=name,
        compiler_params=_params(("arbitrary",), VMEM_BIG))(p, d_us, g_v, w_s, bexp)


def _s5_disc(a_re, a_im, log_dt, b_re, b_im):
    dt = jnp.exp(log_dt)
    mag = jnp.exp(dt * a_re)
    ar, ai = mag * jnp.cos(dt * a_im), mag * jnp.sin(dt * a_im)
    den = a_re * a_re + a_im * a_im
    qr = ((ar - 1.0) * a_re + ai * a_im) / den
    qi = (ai * a_re - (ar - 1.0) * a_im) / den
    return ar, ai, qr[None] * b_re - qi[None] * b_im, qr[None] * b_im + qi[None] * b_re


def s5_disc_fwd(a_re, a_im, log_dt, b_re, b_im, *, name):
    G, P = a_re.shape
    C = b_re.shape[0]

    def body(ar_ref, ai_ref, dt_ref, br_ref, bi_ref, o_ar, o_ai, o_br, o_bi):
        ar, ai, br, bi = _s5_disc(ar_ref[...], ai_ref[...], dt_ref[...], br_ref[...], bi_ref[...])
        o_ar[...] = ar
        o_ai[...] = ai
        o_br[...] = br
        o_bi[...] = bi

    gp = jax.ShapeDtypeStruct((G, P), F32)
    cgp = jax.ShapeDtypeStruct((C, G, P), F32)
    return pl.pallas_call(body, out_shape=[gp, gp, cgp, cgp], name=name)(a_re, a_im, log_dt, b_re, b_im)


def s5_disc_bwd(a_re, a_im, log_dt, b_re, b_im, d_ar, d_ai, d_br, d_bi, *, name):
    G, P = a_re.shape
    C = b_re.shape[0]

    def body(ar_ref, ai_ref, dt_ref, br_ref, bi_ref, g0, g1, g2, g3, o0, o1, o2, o3, o4):
        prim = (ar_ref[...], ai_ref[...], dt_ref[...], br_ref[...], bi_ref[...])
        _, vjp = jax.vjp(_s5_disc, *prim)
        outs = vjp((g0[...], g1[...], g2[...], g3[...]))
        for o, v in zip((o0, o1, o2, o3, o4), outs):
            o[...] = v

    gp = jax.ShapeDtypeStruct((G, P), F32)
    cgp = jax.ShapeDtypeStruct((C, G, P), F32)
    return pl.pallas_call(
        body, out_shape=[gp, gp, jax.ShapeDtypeStruct((G, 1), F32), cgp, cgp], name=name,
    )(a_re, a_im, log_dt, b_re, b_im, d_ar, d_ai, d_br, d_bi)


S5_KG = 8


def _planes(x):
    return [x[:, c * LANES:(c + 1) * LANES] for c in range(x.shape[1] // LANES)]


def _store_planes(ref, row0, val):
    for c, p in enumerate(_planes(val)):
        ref[c, pl.ds(row0, val.shape[0]), :] = p


def _load_planes(ref, row0, rows):
    return jnp.concatenate([ref[c, pl.ds(row0, rows), :] for c in range(ref.shape[0])], axis=1)


def _build_powers(a_r, a_i, S, pf_r, pf_i, pr_r=None, pr_i=None):
    ar, ai = _planes(a_r), _planes(a_i)
    NP = len(ar)

    def step(i, carry):
        out = []
        for c in range(NP):
            p_r, p_i = carry[2 * c], carry[2 * c + 1]
            pf_r[c, pl.ds(i, 1), :] = p_r
            pf_i[c, pl.ds(i, 1), :] = p_i
            if pr_r is not None:
                pr_r[c, pl.ds(S - 1 - i, 1), :] = p_r
                pr_i[c, pl.ds(S - 1 - i, 1), :] = -p_i
            out += [ar[c] * p_r - ai[c] * p_i, ar[c] * p_i + ai[c] * p_r]
        return tuple(out)

    init = []
    for c in range(NP):
        init += [ar[c], ai[c]]
    lax.fori_loop(0, S, step, tuple(init))


def _scan_seg(hr, hi, hrow0, tb, a_r, a_i, pw_r, pw_i, h0r, h0i, *, reverse):
    NP = hr.shape[0]
    S = tb // SUBLANES
    if reverse:
        a_i = -a_i
    ar, ai = _planes(a_r), _planes(a_i)

    def step(i, carry):
        j = (S - 1 - i) if reverse else i
        slab = pl.ds(pl.multiple_of(hrow0 + j * SUBLANES, SUBLANES), SUBLANES)
        out = []
        for c in range(NP):
            nr = ar[c] * carry[2 * c] - ai[c] * carry[2 * c + 1] + hr[c, slab, :]
            ni = ar[c] * carry[2 * c + 1] + ai[c] * carry[2 * c] + hi[c, slab, :]
            hr[c, slab, :] = nr
            hi[c, slab, :] = ni
            out += [nr, ni]
        return tuple(out)

    z = jnp.zeros((SUBLANES, LANES), F32)
    loc = lax.fori_loop(0, S, step, (z,) * (2 * NP), unroll=2)
    h0r_p, h0i_p = _planes(h0r), _planes(h0i)
    top = 0 if reverse else S - 1
    order = range(SUBLANES - 1, -1, -1) if reverse else range(SUBLANES)
    out_r, out_i, ent_r, ent_i = [], [], [], []
    for c in range(NP):
        s_r, s_i = pw_r[c, pl.ds(top, 1), :], pw_i[c, pl.ds(top, 1), :]
        c_r, c_i = h0r_p[c], h0i_p[c]
        in_r, in_i = [None] * SUBLANES, [None] * SUBLANES
        for seg in order:
            in_r[seg], in_i[seg] = c_r, c_i
            l_r, l_i = loc[2 * c][seg:seg + 1], loc[2 * c + 1][seg:seg + 1]
            c_r, c_i = s_r * c_r - s_i * c_i + l_r, s_r * c_i + s_i * c_r + l_i
        out_r.append(c_r)
        out_i.append(c_i)
        ent_r.append(jnp.concatenate(in_r, axis=0))
        ent_i.append(jnp.concatenate(in_i, axis=0))

    def fix(j, _):
        slab = pl.ds(pl.multiple_of(hrow0 + j * SUBLANES, SUBLANES), SUBLANES)
        for c in range(NP):
            p_r, p_i = pw_r[c, pl.ds(j, 1), :], pw_i[c, pl.ds(j, 1), :]
            hr[c, slab, :] += p_r * ent_r[c] - p_i * ent_i[c]
            hi[c, slab, :] += p_r * ent_i[c] + p_i * ent_r[c]
        return 0

    lax.fori_loop(0, S, fix, 0, unroll=2)
    return ent_r, ent_i, jnp.concatenate(out_r, axis=1), jnp.concatenate(out_i, axis=1)


def _perm_matrices(tb):
    r = jnp.arange(tb)
    pm = (r[None, :] == ((r % SUBLANES) * (tb // SUBLANES) + r // SUBLANES)[:, None]).astype(BF16)
    return pm, pm.T


def _unpermute(pmt, x):
    c = x.shape[1]
    hi = x.astype(BF16)
    lo = (x - hi.astype(F32)).astype(BF16)
    both = _dot_nn(pmt, jnp.concatenate([hi, lo], axis=1))
    return both[:, :c] + both[:, c:]


def _s5_specs(tb, UC, SC, rev_nb=None):
    tmap = (lambda b: b) if rev_nb is None else (lambda b: rev_nb - 1 - b)
    row = pl.BlockSpec((tb, UC), lambda b, k: (tmap(b), k))
    wsp = pl.BlockSpec((None, UC, SC), lambda b, k: (k, 0, 0))
    vsc = pl.BlockSpec((None, 1, SC), lambda b, k: (k, 0, 0))
    vuc = pl.BlockSpec((None, 1, UC), lambda b, k: (k, 0, 0))
    st = pl.BlockSpec((None, None, 1, SC), lambda b, k: (tmap(b), k, 0, 0))
    return row, wsp, vsc, vuc, st


def s5_fwd(uu, pm, pmt, bd_r, bd_i, ct_r, ct_i, ab_r, ab_i, dd, *, tb, name):
    T, E = uu.shape
    KB, UC, SC = bd_r.shape
    NB = T // tb
    NP, S = SC // LANES, tb // SUBLANES

    def body(u_ref, pm_ref, pmt_ref, bdr, bdi, ctr, cti, ar_ref, ai_ref, dd_ref, y_ref, q_ref, sr_ref, si_ref,
             hr, hi, cr, ci, pf_r, pf_i):
        b, k = pl.program_id(0), pl.program_id(1)
        a_r, a_i = ar_ref[...], ai_ref[...]

        @pl.when(b == 0)
        def _():
            cr[k] = jnp.zeros((1, SC), F32)
            ci[k] = jnp.zeros((1, SC), F32)
            _build_powers(a_r, a_i, S, pf_r.at[k], pf_i.at[k])

        h0r, h0i = cr[k], ci[k]
        sr_ref[...] = h0r
        si_ref[...] = h0i
        u = u_ref[...]
        up = _dot_nn(pm_ref[...], u.astype(BF16)).astype(BF16)
        _store_planes(hr, 0, _dot_nn(up, bdr[...]))
        _store_planes(hi, 0, _dot_nn(up, bdi[...]))
        _, _, o_r, o_i = _scan_seg(hr, hi, 0, tb, a_r, a_i, pf_r.at[k], pf_i.at[k], h0r, h0i, reverse=False)
        cr[k] = o_r
        ci[k] = o_i
        ys = (_dot_nt(_load_planes(hr, 0, tb).astype(BF16), ctr[...])
              - _dot_nt(_load_planes(hi, 0, tb).astype(BF16), cti[...]))
        pmt_v = pmt_ref[...]
        y = _unpermute(pmt_v, ys) + dd_ref[...] * u
        y_ref[...] = y
        q_ref[...] = _gelu(y).astype(BF16)

    row, wsp, vsc, vuc, st = _s5_specs(tb, UC, SC)
    psp = pl.BlockSpec((tb, tb), lambda b, k: (0, 0))
    stsh = jax.ShapeDtypeStruct((NB, KB, 1, SC), F32)
    pw = pltpu.VMEM((KB, NP, S, LANES), F32)
    pln = pltpu.VMEM((NP, tb, LANES), F32)
    return pl.pallas_call(
        body, grid=(NB, KB), in_specs=[row, psp, psp, wsp, wsp, wsp, wsp, vsc, vsc, vuc],
        out_specs=[row, row, st, st],
        out_shape=[jax.ShapeDtypeStruct((T, E), F32), jax.ShapeDtypeStruct((T, E), BF16), stsh, stsh],
        scratch_shapes=[pln, pln, pltpu.VMEM((KB, 1, SC), F32), pltpu.VMEM((KB, 1, SC), F32), pw, pw],
        name=name, compiler_params=_params(("arbitrary", "arbitrary"), VMEM_MID),
    )(uu, pm, pmt, bd_r, bd_i, ct_r, ct_i, ab_r, ab_i, dd)


def s5_bwd(uu, y, dq, pm, pmt, st_r, st_i, bd_r, bd_i, ct_r, ct_i, ab_r, ab_i, dd, *, tb, name):
    T, E = uu.shape
    KB, UC, SC = bd_r.shape
    NB = T // tb
    HDR = SUBLANES
    NP, S = SC // LANES, tb // SUBLANES
    pw = pltpu.VMEM((KB, NP, S, LANES), F32)
    pln = pltpu.VMEM((NP, tb, LANES), F32)

    def body(u_ref, y_ref, dq_ref, pm_ref, pmt_ref, sr_ref, si_ref, bdr, bdi, ctr, cti, ar_ref, ai_ref, dd_ref,
             du_ref, obr, obi, ocr, oci, odar, odai, oddd,
             hr, hi, gr, gi, kr, ki, abr, abi, acr, aci, pf_r, pf_i, pr_r, pr_i):
        b, k = pl.program_id(0), pl.program_id(1)
        a_r, a_i = ar_ref[...], ai_ref[...]

        @pl.when(b == 0)
        def _():
            _build_powers(a_r, a_i, S, pf_r.at[k], pf_i.at[k], pr_r.at[k], pr_i.at[k])
            z1 = jnp.zeros((1, SC), F32)
            kr[k] = z1
            ki[k] = z1
            odar[k] = z1
            odai[k] = z1
            oddd[k] = jnp.zeros((1, UC), F32)
            zw = jnp.zeros((UC, SC), F32)
            abr[k] = zw
            abi[k] = zw
            acr[k] = zw
            aci[k] = zw

        u = u_ref[...]
        dy = dq_ref[...] * _gelu_grad(y_ref[...])
        oddd[k] += jnp.sum(dy * u, axis=0, keepdims=True)
        pm_v = pm_ref[...]
        both = _dot_nn(pm_v, jnp.concatenate([u.astype(BF16), dy.astype(BF16)], axis=1))
        ub = both[:, :UC].astype(BF16)
        dyb = both[:, UC:].astype(BF16)
        s0r, s0i = sr_ref[...], si_ref[...]
        _store_planes(hr, HDR, _dot_nn(ub, bdr[...]))
        _store_planes(hi, HDR, _dot_nn(ub, bdi[...]))
        e_r, e_i, _, _ = _scan_seg(hr, hi, HDR, tb, a_r, a_i, pf_r.at[k], pf_i.at[k], s0r, s0i, reverse=False)
        for c in range(NP):
            hr[c, pl.ds(0, HDR), :] = e_r[c]
            hi[c, pl.ds(0, HDR), :] = e_i[c]
        _store_planes(gr, 0, _dot_nn(dyb, ctr[...]))
        _store_planes(gi, 0, -_dot_nn(dyb, cti[...]))
        _, _, g0r, g0i = _scan_seg(gr, gi, 0, tb, a_r, a_i, pr_r.at[k], pr_i.at[k], kr[k], ki[k], reverse=True)
        kr[k] = g0r
        ki[k] = g0i

        def slab(j, acc):
            o = pl.multiple_of(j * SUBLANES, SUBLANES)
            out = []
            for c in range(NP):
                p_r, p_i = hr[c, pl.ds(o, SUBLANES), :], hi[c, pl.ds(o, SUBLANES), :]
                g_r, g_i = gr[c, pl.ds(o, SUBLANES), :], gi[c, pl.ds(o, SUBLANES), :]
                out += [acc[2 * c] + g_r * p_r + g_i * p_i, acc[2 * c + 1] + g_i * p_r - g_r * p_i]
            return tuple(out)

        z8 = jnp.zeros((SUBLANES, LANES), F32)
        acc = lax.fori_loop(0, S, slab, (z8,) * (2 * NP), unroll=2)
        odar[k] += jnp.concatenate([jnp.sum(acc[2 * c], axis=0, keepdims=True) for c in range(NP)], axis=1)
        odai[k] += jnp.concatenate([jnp.sum(acc[2 * c + 1], axis=0, keepdims=True) for c in range(NP)], axis=1)
        g_rb = _load_planes(gr, 0, tb).astype(BF16)
        g_ib = _load_planes(gi, 0, tb).astype(BF16)
        h_rb = _load_planes(hr, HDR, tb).astype(BF16)
        h_ib = _load_planes(hi, HDR, tb).astype(BF16)
        dus = _dot_nt(g_rb, bdr[...]) + _dot_nt(g_ib, bdi[...])
        pmt_v = pmt_ref[...]
        du = _unpermute(pmt_v, dus) + dd_ref[...] * dy
        du_ref[...] = du.astype(BF16)
        abr[k] += _dot_tn(ub, g_rb)
        abi[k] += _dot_tn(ub, g_ib)
        acr[k] += _dot_tn(dyb, h_rb)
        aci[k] -= _dot_tn(dyb, h_ib)

        @pl.when(jnp.logical_and(b == NB - 1, k == KB - 1))
        def _():
            pltpu.sync_copy(abr, obr)
            pltpu.sync_copy(abi, obi)
            pltpu.sync_copy(acr, ocr)
            pltpu.sync_copy(aci, oci)

    row, wsp, vsc, vuc, st = _s5_specs(tb, UC, SC, rev_nb=NB)
    psp = pl.BlockSpec((tb, tb), lambda b, k: (0, 0))
    hbm = pl.BlockSpec(memory_space=pltpu.HBM)
    full_sc = pl.BlockSpec((KB, 1, SC), lambda b, k: (0, 0, 0))
    full_uc = pl.BlockSpec((KB, 1, UC), lambda b, k: (0, 0, 0))
    wsh = jax.ShapeDtypeStruct((KB, UC, SC), F32)
    acc = pltpu.VMEM((KB, UC, SC), F32)
    return pl.pallas_call(
        body, grid=(NB, KB),
        in_specs=[row, row, row, psp, psp, st, st, wsp, wsp, wsp, wsp, vsc, vsc, vuc],
        out_specs=[row, hbm, hbm, hbm, hbm, full_sc, full_sc, full_uc],
        out_shape=[jax.ShapeDtypeStruct((T, E), BF16), wsh, wsh, wsh, wsh,
                   jax.ShapeDtypeStruct((KB, 1, SC), F32), jax.ShapeDtypeStruct((KB, 1, SC), F32),
                   jax.ShapeDtypeStruct((KB, 1, UC), F32)],
        scratch_shapes=[pltpu.VMEM((NP, tb + HDR, LANES), F32), pltpu.VMEM((NP, tb + HDR, LANES), F32), pln, pln,
                        pltpu.VMEM((KB, 1, SC), F32), pltpu.VMEM((KB, 1, SC), F32), acc, acc, acc, acc,
                        pw, pw, pw, pw],
        name=name, compiler_params=_params(("arbitrary", "arbitrary"), VMEM_BIG),
    )(uu, y, dq, pm, pmt, st_r, st_i, bd_r, bd_i, ct_r, ct_i, ab_r, ab_i, dd)


def glu_fwd(h, pg, *, name):
    T, D = h.shape
    tm = _pick(T, ROW_TILES)

    def body(h_ref, a_ref, b_ref, o_ref):
        o_ref[...] = h_ref[...] + a_ref[...] * _sigmoid(b_ref[...])

    row = pl.BlockSpec((tm, D), lambda i: (i, 0))
    return pl.pallas_call(
        body, grid=(T // tm,), in_specs=[row, row, pl.BlockSpec((tm, D), lambda i: (i, 1))], out_specs=row,
        out_shape=jax.ShapeDtypeStruct((T, D), F32), name=name,
        compiler_params=_params(("parallel",), VMEM_MID))(h, pg, pg)


def glu_bwd(d, pg, *, name):
    T, D = d.shape
    tm = _pick(T, ROW_TILES)

    def body(d_ref, a_ref, b_ref, o_ref):
        dv = d_ref[...]
        sg = _sigmoid(b_ref[...])
        da = dv * sg
        db = dv * a_ref[...] * sg * (1.0 - sg)
        o_ref[...] = jnp.where(pl.program_id(1) == 0, da, db).astype(BF16)

    row = pl.BlockSpec((tm, D), lambda i, hf: (i, 0))
    return pl.pallas_call(
        body, grid=(T // tm, 2), in_specs=[row, row, pl.BlockSpec((tm, D), lambda i, hf: (i, 1))],
        out_specs=pl.BlockSpec((tm, D), lambda i, hf: (i, hf)),
        out_shape=jax.ShapeDtypeStruct((T, 2 * D), BF16), name=name,
        compiler_params=_params(("parallel", "arbitrary"), VMEM_MID))(d, pg, pg)


def _shift_down(x, halo, s):
    r = pltpu.roll(x, s, axis=0)
    hr = pltpu.roll(halo, s, axis=0)
    row = lax.broadcasted_iota(jnp.int32, halo.shape, 0)
    head = jnp.where(row < s, hr, r[:SUBLANES])
    return jnp.concatenate([head, r[SUBLANES:]], axis=0)


def _shift_up(x, halo, s):
    n = x.shape[0]
    r = pltpu.roll(x, n - s, axis=0)
    hr = pltpu.roll(halo, SUBLANES - s, axis=0)
    row = lax.broadcasted_iota(jnp.int32, halo.shape, 0)
    tail = jnp.where(row >= SUBLANES - s, hr, r[n - SUBLANES:])
    return jnp.concatenate([r[:n - SUBLANES], tail], axis=0)


def _conv_acc(z, zh, w, b, first):
    kw = w.shape[0]
    zh = jnp.where(first, 0.0, zh)
    acc = b + w[kw - 1:kw] * z
    shifted = []
    for k in range(kw - 1):
        zs = _shift_down(z, zh, kw - 1 - k)
        shifted.append(zs)
        acc = acc + w[k:k + 1] * zs
    return acc, shifted


def _conv_specs(T, F, tm, tc, KW):
    nfb = F // tc
    rb = tm // SUBLANES

    def main(off):
        return pl.BlockSpec((tm, tc), lambda i, c: (i, c + off))

    def halo(off):
        return pl.BlockSpec((SUBLANES, tc), lambda i, c: (jnp.maximum(i * rb - 1, 0), c + off))

    def wspec(off):
        return pl.BlockSpec((None, KW, tc), lambda i, c: (c + off, 0, 0))

    def bspec(off):
        return pl.BlockSpec((1, tc), lambda i, c: (0, c + off))

    return nfb, main, halo, wspec, bspec


def convglu_fwd(z, cw, cb, *, name):
    T, F2 = z.shape
    F = F2 // 2
    _, KW, tc = cw.shape
    tm = _pick(T, (256, 128))
    nfb, main, halo, wspec, bspec = _conv_specs(T, F, tm, tc, KW)

    def body(zg, zgh, zv, zvh, wg, wv, bg, bv, o_ref):
        first = pl.program_id(0) == 0
        g, _ = _conv_acc(zg[...], zgh[...], wg[...], bg[...], first)
        v, _ = _conv_acc(zv[...], zvh[...], wv[...], bv[...], first)
        o_ref[...] = (g * _sigmoid(g) * v).astype(BF16)

    return pl.pallas_call(
        body, grid=(T // tm, nfb),
        in_specs=[main(0), halo(0), main(nfb), halo(nfb), wspec(0), wspec(nfb), bspec(0), bspec(nfb)],
        out_specs=pl.BlockSpec((tm, tc), lambda i, c: (i, c)),
        out_shape=jax.ShapeDtypeStruct((T, F), BF16), name=name,
        compiler_params=_params(("parallel", "parallel"), VMEM_BIG))(z, z, z, z, cw, cw, cb, cb)


def convglu_bwd_acc(z, da, cw, cb, *, name):
    T, F2 = z.shape
    F = F2 // 2
    _, KW, tc = cw.shape
    tm = _pick(T, (256, 128))
    nfb, main, halo, wspec, bspec = _conv_specs(T, F, tm, tc, KW)

    def body(zg, zgh, zv, zvh, wg, wv, bg, bv, da_ref, o_ref, dwg, dwv, dbg, dbv):
        i = pl.program_id(1)
        first = i == 0

        @pl.when(first)
        def _():
            for o in (dwg, dwv, dbg, dbv):
                o[...] = jnp.zeros_like(o)

        zg_v, zv_v = zg[...], zv[...]
        g, sg_ = _conv_acc(zg_v, zgh[...], wg[...], bg[...], first)
        v, sv_ = _conv_acc(zv_v, zvh[...], wv[...], bv[...], first)
        d = da_ref[...]
        sig = _sigmoid(g)
        dg = d * v * sig * (1.0 + g * (1.0 - sig))
        dv = d * g * sig
        o_ref[0] = dg.astype(BF16)
        o_ref[1] = dv.astype(BF16)
        dbg[...] += jnp.sum(dg, axis=0, keepdims=True)
        dbv[...] += jnp.sum(dv, axis=0, keepdims=True)
        for k in range(KW):
            xg = zg_v if k == KW - 1 else sg_[k]
            xv = zv_v if k == KW - 1 else sv_[k]
            dwg[pl.ds(k, 1), :] += jnp.sum(dg * xg, axis=0, keepdims=True)
            dwv[pl.ds(k, 1), :] += jnp.sum(dv * xv, axis=0, keepdims=True)

    def sw(spec_fn, off):
        s = spec_fn(off)
        return pl.BlockSpec(s.block_shape, lambda c, i, f=s.index_map: f(i, c))

    both = jax.ShapeDtypeStruct((2, T, F), BF16)
    dwsh = jax.ShapeDtypeStruct((nfb, KW, tc), F32)
    dbsh = jax.ShapeDtypeStruct((1, F), F32)
    outs = pl.pallas_call(
        body, grid=(nfb, T // tm),
        in_specs=[sw(main, 0), sw(halo, 0), sw(main, nfb), sw(halo, nfb), sw(wspec, 0), sw(wspec, nfb),
                  sw(bspec, 0), sw(bspec, nfb), pl.BlockSpec((tm, tc), lambda c, i: (i, c))],
        out_specs=[pl.BlockSpec((2, tm, tc), lambda c, i: (0, i, c)),
                   pl.BlockSpec((None, KW, tc), lambda c, i: (c, 0, 0)), pl.BlockSpec((None, KW, tc), lambda c, i: (c, 0, 0)),
                   pl.BlockSpec((1, tc), lambda c, i: (0, c)), pl.BlockSpec((1, tc), lambda c, i: (0, c))],
        out_shape=[both, dwsh, dwsh, dbsh, dbsh], name=name,
        compiler_params=_params(("parallel", "arbitrary"), VMEM_BIG))(z, z, z, z, cw, cw, cb, cb, da)
    return outs


def conv_bwd_in(dacc, cw, *, name):
    _, T, F = dacc.shape
    _, KW, tc = cw.shape
    nfb = F // tc
    tm = _pick(T, (256, 128))
    rb = tm // (2 * SUBLANES)
    last_blk = T // (2 * SUBLANES) - 1

    def body(d_ref, dn_ref, w_ref, o_ref):
        last = pl.program_id(0) == pl.num_programs(0) - 1
        d = d_ref[...].astype(F32)
        dn = jnp.where(last, 0.0, dn_ref[...].astype(F32)[:SUBLANES])
        w = w_ref[...]
        out = w[KW - 1:KW] * d
        for k in range(KW - 1):
            out = out + w[k:k + 1] * _shift_up(d, dn, KW - 1 - k)
        o_ref[...] = out.astype(BF16)

    return pl.pallas_call(
        body, grid=(T // tm, 2, nfb),
        in_specs=[pl.BlockSpec((None, tm, tc), lambda i, hf, c: (hf, i, c)),
                  pl.BlockSpec((None, 2 * SUBLANES, tc), lambda i, hf, c: (hf, jnp.minimum((i + 1) * rb, last_blk), c)),
                  pl.BlockSpec((None, KW, tc), lambda i, hf, c: (hf * nfb + c, 0, 0))],
        out_specs=pl.BlockSpec((tm, tc), lambda i, hf, c: (i, hf * nfb + c)),
        out_shape=jax.ShapeDtypeStruct((T, 2 * F), BF16), name=name,
        compiler_params=_params(("parallel", "parallel", "parallel"), VMEM_BIG))(dacc, dacc, cw)


def _my_place():
    x, y, c = (lax.axis_index(a) for a in AXES)
    return x, y, c, 4 * x + 2 * y + c


def _peer(m, x, y, c):
    px = 1 - x if (m >> 2) & 1 else x
    py = 1 - y if (m >> 1) & 1 else y
    pc = 1 - c if m & 1 else c
    return (px, py, pc), 4 * px + 2 * py + pc


def _exchange(ins, out_shapes, plan, *, bcast, name):
    n_in, n_out, n = len(ins), len(out_shapes), len(plan)

    def body(*refs):
        in_refs, out_refs = refs[:n_in], refs[n_in:n_in + n_out]
        send_sems, recv_sems, loc_sems = refs[n_in + n_out:]
        x, y, c, me = _my_place()

        def src(f, who):
            r = in_refs[plan[f][0]]
            return r if bcast else r.at[who]

        def dst(f, who):
            r = out_refs[plan[f][1]]
            lay = plan[f][2]
            return r.at[who] if lay is None else r.at[lay, who]

        def remote(f, m, landing):
            dev, plin = _peer(m, x, y, c)
            return pltpu.make_async_remote_copy(
                src_ref=src(f, plin), dst_ref=dst(f, plin if landing else me), send_sem=send_sems.at[f, m - 1],
                recv_sem=recv_sems.at[f, m - 1], device_id=dev, device_id_type=MESH)

        locs = [pltpu.make_async_copy(src(f, me), dst(f, me), loc_sems.at[f]) for f in range(n)]
        for cp in locs:
            cp.start()
        for m in range(1, N_DEV):
            for f in range(n):
                remote(f, m, False).start()
        for m in range(1, N_DEV):
            for f in range(n):
                remote(f, m, True).wait()
        for cp in locs:
            cp.wait()

    hbm = pl.BlockSpec(memory_space=pltpu.HBM)
    return pl.pallas_call(
        body, in_specs=[hbm] * n_in, out_specs=[hbm] * n_out, out_shape=out_shapes,
        scratch_shapes=[pltpu.SemaphoreType.DMA((n, N_DEV - 1)), pltpu.SemaphoreType.DMA((n, N_DEV - 1)),
                        pltpu.SemaphoreType.DMA((n,))],
        name=name)(*ins)


def all_gather(shards, *, name):
    outs = [jax.ShapeDtypeStruct((N_DEV,) + s.shape, s.dtype) for s in shards]
    return _exchange(shards, outs, [(a, a, None) for a in range(len(shards))], bcast=True, name=name)


_HBM = pl.BlockSpec(memory_space=pltpu.HBM)
_SEM = pl.BlockSpec(memory_space=pltpu.SEMAPHORE)
_EFFECT = pltpu.SideEffectType.DATAFLOW_SIDE_EFFECTING


def _split_copy(in_refs, land_refs, send_sems, recv_sems, bcast, f, m, place, landing):
    x, y, c, me = place
    dev, plin = _peer(m, x, y, c)
    src = in_refs[f] if bcast else in_refs[f].at[plin]
    return pltpu.make_async_remote_copy(
        src_ref=src, dst_ref=land_refs[f].at[plin if landing else me],
        send_sem=send_sems.at[f * (N_DEV - 1) + m - 1], recv_sem=recv_sems.at[f * (N_DEV - 1) + m - 1],
        device_id=dev, device_id_type=MESH)


def _own_copy(in_refs, land_refs, own_sems, bcast, f, place):
    me = place[3]
    return pltpu.make_async_copy(in_refs[f] if bcast else in_refs[f].at[me], land_refs[f].at[me], own_sems.at[f])


PEERS_ALL = tuple(range(1, N_DEV))
PEERS_DIRECT = (1, 2, 4, 6)
SLOTS_PASSED = (2, 4, 6)


def exchange_start(ins, *, bcast, masks=PEERS_ALL, dep=None, name):
    n = len(ins)
    lands = [lax.empty(((N_DEV,) + a.shape) if bcast else a.shape, a.dtype) for a in ins]
    deps = [] if dep is None else [dep]
    nd = len(deps)

    def body(*refs):
        in_refs, land_refs = refs[:n], refs[n:2 * n]
        send_sems, recv_sems, own_sems = refs[2 * n + nd:2 * n + nd + 3]
        token = refs[-1]
        place = _my_place()
        for m in masks:
            for f in range(n):
                _split_copy(in_refs, land_refs, send_sems, recv_sems, bcast, f, m, place, False).start()
        for f in range(n):
            _own_copy(in_refs, land_refs, own_sems, bcast, f, place).start()
        token[...] = jnp.zeros_like(token)

    arrs = [pltpu.with_memory_space_constraint(a, pltpu.HBM) for a in (*ins, *lands)]
    sems = pltpu.SemaphoreType.DMA((n * (N_DEV - 1),))
    outs = pl.pallas_call(
        body, name=name,
        out_shape=(sems, sems, pltpu.SemaphoreType.DMA((n,)), *[pltpu.HBM(a.shape, a.dtype) for a in arrs],
                   jax.ShapeDtypeStruct((SUBLANES, LANES), F32)),
        in_specs=[_HBM] * (2 * n) + [_DEP] * nd,
        out_specs=(_SEM, _SEM, _SEM, *[_HBM] * (2 * n), pl.BlockSpec(memory_space=pltpu.VMEM)),
        input_output_aliases={i: 3 + i for i in range(2 * n)},
        compiler_params=pltpu.CompilerParams(has_side_effects=_EFFECT))(*arrs, *deps)
    return outs[0], outs[1], outs[2], list(outs[3:3 + 2 * n]), outs[-1]


def exchange_wait(started, after, *, bcast, masks=PEERS_ALL, name):
    send_sems, recv_sems, own_sems, thrus, _ = started
    n = len(thrus) // 2

    def body(*refs):
        in_refs, land_refs = refs[:n], refs[n:2 * n]
        send, recv, own = refs[2 * n:2 * n + 3]
        place = _my_place()
        for f in range(n):
            _own_copy(in_refs, land_refs, own, bcast, f, place).wait()
        for m in masks:
            for f in range(n):
                cp = _split_copy(in_refs, land_refs, send, recv, bcast, f, m, place, True)
                cp.wait_send()
                cp.wait_recv()

    outs = pl.pallas_call(
        body, name=name, out_shape=[pltpu.HBM(a.shape, a.dtype) for a in thrus],
        in_specs=[_HBM] * (2 * n) + [_SEM, _SEM, _SEM, pl.BlockSpec(memory_space=pl.ANY)], out_specs=[_HBM] * (2 * n),
        input_output_aliases={i: i for i in range(2 * n)},
        compiler_params=pltpu.CompilerParams(has_side_effects=_EFFECT))(
            *thrus, send_sems, recv_sems, own_sems, after)
    return list(outs[n:])


def _pass_copy(land_refs, send_sems, recv_sems, f, k, place, landing):
    x, y, c, _ = place
    m = SLOTS_PASSED[k]
    sib, _ = _peer(1, x, y, c)
    _, mine = _peer(m, x, y, c)
    _, theirs = _peer(m ^ 1, x, y, c)
    i = f * len(SLOTS_PASSED) + k
    return pltpu.make_async_remote_copy(
        src_ref=land_refs[f].at[mine], dst_ref=land_refs[f].at[theirs if landing else mine],
        send_sem=send_sems.at[i], recv_sem=recv_sems.at[i], device_id=sib, device_id_type=MESH)


def pass_start(lands, *, dep=None, name):
    n = len(lands)
    deps = [] if dep is None else [dep]
    nd = len(deps)

    def body(*refs):
        land_refs = refs[:n]
        send_sems, recv_sems, token = refs[n + nd], refs[n + nd + 1], refs[-1]
        place = _my_place()
        for k in range(len(SLOTS_PASSED)):
            for f in range(n):
                _pass_copy(land_refs, send_sems, recv_sems, f, k, place, False).start()
        token[...] = jnp.zeros_like(token)

    sems = pltpu.SemaphoreType.DMA((n * len(SLOTS_PASSED),))
    outs = pl.pallas_call(
        body, name=name,
        out_shape=(sems, sems, *[pltpu.HBM(a.shape, a.dtype) for a in lands], jax.ShapeDtypeStruct((SUBLANES, LANES), F32)),
        in_specs=[_HBM] * n + [_DEP] * nd,
        out_specs=(_SEM, _SEM, *[_HBM] * n, pl.BlockSpec(memory_space=pltpu.VMEM)),
        input_output_aliases={i: 2 + i for i in range(n)},
        compiler_params=pltpu.CompilerParams(has_side_effects=_EFFECT))(*lands, *deps)
    return outs[0], outs[1], list(outs[2:2 + n]), outs[-1]


def pass_wait(started, after, *, name):
    send_sems, recv_sems, thrus, _ = started
    n = len(thrus)

    def body(*refs):
        land_refs = refs[:n]
        send, recv = refs[n], refs[n + 1]
        place = _my_place()
        for k in range(len(SLOTS_PASSED)):
            for f in range(n):
                cp = _pass_copy(land_refs, send, recv, f, k, place, True)
                cp.wait_send()
                cp.wait_recv()

    outs = pl.pallas_call(
        body, name=name, out_shape=[pltpu.HBM(a.shape, a.dtype) for a in thrus],
        in_specs=[_HBM] * n + [_SEM, _SEM, pl.BlockSpec(memory_space=pl.ANY)], out_specs=[_HBM] * n,
        input_output_aliases={i: i for i in range(n)},
        compiler_params=pltpu.CompilerParams(has_side_effects=_EFFECT))(*thrus, send_sems, recv_sems, after)
    return list(outs)


def _adamw(w, g, m, v):
    m = ADAM_B1 * m + (1.0 - ADAM_B1) * g
    v = ADAM_B2 * v + (1.0 - ADAM_B2) * (g * g)
    m_hat = m / (1.0 - ADAM_B1 ** ADAM_STEP)
    v_hat = v / (1.0 - ADAM_B2 ** ADAM_STEP)
    delta = -ADAM_LR * (m_hat / (jnp.sqrt(v_hat) + ADAM_EPS) + ADAM_WD * w)
    return delta, m, v


def adam_reduce(recv, w, m, v, l, prev, *, name):
    _, R, C = recv.shape
    L = w.shape[0]
    budget = 4 * 1024 * 1024
    tr = R
    for cand in (1024, 512, 352, 256, 176, 128, 64, 32, 16):
        if R % cand == 0 and N_DEV * cand * C * recv.dtype.itemsize <= budget:
            tr = cand
            break

    def body(r_ref, w_ref, m_ref, v_ref, *rest):
        g_ref, d_ref, nm_ref, nv_ref = rest[-4:]
        g = r_ref[0].astype(F32)
        for s in range(1, N_DEV):
            g = g + r_ref[s].astype(F32)
        d, nm, nv = _adamw(w_ref[...], g, m_ref[...], v_ref[...])
        g_ref[...] = g
        d_ref[...] = d
        nm_ref[...] = nm
        nv_ref[...] = nv

    blk = pl.BlockSpec((None, tr, C), lambda r: (l, r, 0))
    sh = jax.ShapeDtypeStruct((L, R, C), F32)
    extra = [] if prev is None else list(prev)
    return pl.pallas_call(
        body, grid=(R // tr,),
        in_specs=[pl.BlockSpec((N_DEV, tr, C), lambda r: (0, r, 0)), blk, blk, blk]
        + [pl.BlockSpec(memory_space=pl.ANY)] * len(extra),
        out_specs=[blk] * 4, out_shape=[sh] * 4, name=name,
        input_output_aliases={4 + i: i for i in range(len(extra))},
        compiler_params=_params(("parallel",), VMEM_MID))(recv, w, m, v, *extra)


def sum_slots(recv, *, name):
    _, R, C = recv.shape
    tr = _pick(R, (512, 256, 128, 64, 32, 16, 8))

    def body(r_ref, o_ref):
        g = r_ref[0]
        for s in range(1, N_DEV):
            g = g + r_ref[s]
        o_ref[...] = g

    return pl.pallas_call(
        body, grid=(R // tr,), in_specs=[pl.BlockSpec((N_DEV, tr, C), lambda r: (0, r, 0))],
        out_specs=pl.BlockSpec((tr, C), lambda r: (r, 0)), out_shape=jax.ShapeDtypeStruct((R, C), F32),
        name=name, compiler_params=_params(("parallel",)))(recv)


def adam_flat(g, w, m, v, *, name):
    R, C = g.shape
    tr = _pick(R, (512, 256, 128, 64, 32, 16, 8))

    def body(g_ref, w_ref, m_ref, v_ref, d_ref, nm_ref, nv_ref):
        d, nm, nv = _adamw(w_ref[...], g_ref[...], m_ref[...], v_ref[...])
        d_ref[...] = d
        nm_ref[...] = nm
        nv_ref[...] = nv

    blk = pl.BlockSpec((tr, C), lambda r: (r, 0))
    sh = jax.ShapeDtypeStruct((R, C), F32)
    return pl.pallas_call(body, grid=(R // tr,), in_specs=[blk] * 4, out_specs=[blk] * 3, out_shape=[sh] * 3,
                          name=name, compiler_params=_params(("parallel",)))(g, w, m, v)


WEIGHTS = ("norm_mix_g", "norm_ffn_g", "a_w_in", "a_g_v", "a_w_s", "a_b_s", "a_w_out", "b_w_in", "b_a_re", "b_a_im",
           "b_log_dt", "b_b_re", "b_b_im", "b_c_re", "b_c_im", "b_d", "b_w_glu", "f_w_up", "f_conv_w", "f_conv_b",
           "f_w_down", "final_g")
BIG = ("a_w_in", "a_w_out", "b_w_in", "b_w_glu", "f_w_up", "f_w_down")
FLAT_ROWS = 512
FLAT_CHUNK = N_DEV * FLAT_ROWS * LANES


def _expand(blocks, eye):
    KB, KG, C, P = blocks.shape
    return (blocks[:, :, :, None, :] * eye[None, :, None, :, None]).reshape(KB, KG * C, KG * P)


def _diag_blocks(dense, KG, C, P, eye):
    KB = dense.shape[0]
    return jnp.einsum("kgchp,gh->kgcp", dense.reshape(KB, KG, C, KG, P), eye)


def _flatten_pack(parts):
    flat = jnp.concatenate([p.reshape(-1) for p in parts])
    pad = (-flat.shape[0]) % FLAT_CHUNK
    return jnp.pad(flat, (0, pad))


def _unpack(flat, like):
    out, o = [], 0
    for p in like:
        n = math.prod(p.shape)
        out.append(flat[o:o + n].reshape(p.shape))
        o += n
    return out


def kernel(x, norm_mix_g, norm_ffn_g, a_w_in, a_g_v, a_w_s, a_b_s, a_w_out, b_w_in, b_a_re, b_a_im, b_log_dt, b_b_re, b_b_im, b_c_re, b_c_im, b_d, b_w_glu, f_w_up, f_conv_w, f_conv_b, f_w_down, final_g, loss_target, m_norm_mix_g, m_norm_ffn_g, m_a_w_in, m_a_g_v, m_a_w_s, m_a_b_s, m_a_w_out, m_b_w_in, m_b_a_re, m_b_a_im, m_b_log_dt, m_b_b_re, m_b_b_im, m_b_c_re, m_b_c_im, m_b_d, m_b_w_glu, m_f_w_up, m_f_conv_w, m_f_conv_b, m_f_w_down, m_final_g, v_norm_mix_g, v_norm_ffn_g, v_a_w_in, v_a_g_v, v_a_w_s, v_a_b_s, v_a_w_out, v_b_w_in, v_b_a_re, v_b_a_im, v_b_log_dt, v_b_b_re, v_b_b_im, v_b_c_re, v_b_c_im, v_b_d, v_b_w_glu, v_f_w_up, v_f_conv_w, v_f_conv_b, v_f_w_down, v_final_g):
    env = dict(locals())
    W = {n: env[n] for n in WEIGHTS}
    Mo = {n: env["m_" + n] for n in WEIGHTS}
    Vo = {n: env["v_" + n] for n in WEIGHTS}

    _, T, D = x.shape
    depth = norm_mix_g.shape[0]
    E_A = a_g_v.shape[1]
    H = a_w_s.shape[1]
    G, P, C = b_b_re.shape[1], b_b_re.shape[2], b_b_re.shape[3]
    E_B = G * C
    KG = S5_KG
    KB = G // KG
    F2 = f_conv_b.shape[1]
    tb = _pick(T, (512, 256, 128))
    pm, pmt = _perm_matrices(tb)
    eye = jnp.eye(KG, dtype=F32)
    _, _, _, me = _my_place()

    def shards(i):
        j = i // 2
        ffn = [f_w_up[i].astype(BF16), f_w_down[i].astype(BF16), f_conv_w[i]]
        if i % 2 == 0:
            return [a_w_in[j].astype(BF16), a_w_out[j].astype(BF16)] + ffn
        return [b_w_in[j].astype(BF16), b_w_glu[j].astype(BF16), b_d[j][None]] + ffn

    h = x[0]
    started = exchange_start(shards(0), bcast=True, masks=PEERS_DIRECT, name="gather_start0")
    saved = []
    for i in range(depth):
        j = i // 2
        s = {}
        lands = exchange_wait(started, h, bcast=True, masks=PEERS_DIRECT, name=f"gather_wait{i}")
        passing = pass_start(lands, name=f"gather_pass_start{i}")
        dep = passing[-1]
        if i + 1 < depth:
            started = exchange_start(shards(i + 1), bcast=True, masks=PEERS_DIRECT, dep=dep,
                                     name=f"gather_start{i + 1}")
            dep = started[-1]
        s["h"] = h
        s["hn"] = rms_fwd(h, norm_mix_g[i][None], dep=dep, name=f"rms_mix{i}")
        gathered = pass_wait(passing, s["hn"], name=f"gather_pass_wait{i}")
        if i % 2 == 0:
            g_in, g_out, g_up, g_dn, g_cw = gathered
            s["w_in"], s["w_out"] = g_in, g_out.reshape(E_A, D)
        else:
            g_in, g_glu, g_dd, g_up, g_dn, g_cw = gathered
            s["w_in"], s["w_glu"] = g_in.reshape(D, E_B), g_glu
            s["dd"] = g_dd.reshape(KB, 1, KG * C)
        s["w_up"], s["w_dn"], s["cw"] = g_up, g_dn.reshape(F2 // 2, D), g_cw
        if i % 2 == 0:
            s["p"] = mm_nn(s["hn"], s["w_in"], name=f"a_in{i}")
            s["bexp"] = jnp.repeat(a_b_s[j].T, E_A // H, axis=1)
            s["us"] = sgu_fwd(s["p"], a_g_v[j][None], a_w_s[j], s["bexp"], name=f"sgu_fwd{i}")
            h_mid = mm_nn(s["us"], s["w_out"], res=h, name=f"a_out{i}")
        else:
            s["uu"] = mm_nn(s["hn"], s["w_in"], name=f"b_in{i}")
            s["prm"] = (b_a_re[j], b_a_im[j], b_log_dt[j][:, None],
                        b_b_re[j].transpose(2, 0, 1), b_b_im[j].transpose(2, 0, 1))
            ar, ai, bbr, bbi = s5_disc_fwd(*s["prm"], name=f"s5_disc{i}")
            to_blocks = lambda t: t.reshape(C, KB, KG, P).transpose(1, 2, 0, 3)
            s["bd_r"] = _expand(to_blocks(bbr), eye).astype(BF16)
            s["bd_i"] = _expand(to_blocks(bbi), eye).astype(BF16)
            s["ct_r"] = _expand(b_c_re[j].reshape(KB, KG, C, P), eye).astype(BF16)
            s["ct_i"] = _expand(b_c_im[j].reshape(KB, KG, C, P), eye).astype(BF16)
            s["ab_r"], s["ab_i"] = ar.reshape(KB, 1, KG * P), ai.reshape(KB, 1, KG * P)
            s["y"], s["q"], s["st_r"], s["st_i"] = s5_fwd(
                s["uu"], pm, pmt, s["bd_r"], s["bd_i"], s["ct_r"], s["ct_i"], s["ab_r"], s["ab_i"], s["dd"],
                tb=tb, name=f"s5_fwd{i}")
            s["pg"] = mm_nn(s["q"], s["w_glu"], name=f"b_glu{i}")
            h_mid = glu_fwd(h, s["pg"], name=f"glu_fwd{i}")
        s["h_mid"] = h_mid
        s["hn2"] = rms_fwd(h_mid, norm_ffn_g[i][None], name=f"rms_ffn{i}")
        s["z"] = mm_nn(s["hn2"], s["w_up"], name=f"f_up{i}")
        s["a"] = convglu_fwd(s["z"], s["cw"], f_conv_b[i][None], name=f"convglu_fwd{i}")
        h = mm_nn(s["a"], s["w_dn"], res=h_mid, name=f"f_down{i}")
        saved.append(s)

    loss_tile, dh, dhb, dg_final = loss_head(h, final_g[None], loss_target[0], name="loss_head")
    loss = lax.psum(loss_tile[0, 0], AXES)

    gbig = {n: [None] * W[n].shape[0] for n in BIG}
    gs = {n: [None] * W[n].shape[0] for n in WEIGHTS if n not in BIG and n != "final_g"}
    pending = []
    for i in reversed(range(depth)):
        j = i // 2
        s = saved[i]
        gbig["f_w_down"][i] = mm_tn(s["a"], dhb, blocks=1, name=f"g_down{i}").reshape(N_DEV, F2 // 2 // N_DEV, D)
        da = mm_nt(dhb, s["w_dn"], name=f"d_a{i}")
        dacc, dwg, dwv, dbg, dbv = convglu_bwd_acc(s["z"], da, s["cw"], f_conv_b[i][None], name=f"convglu_bwd{i}")
        gs["f_conv_w"][i] = jnp.concatenate([dwg, dwv], axis=0)
        gs["f_conv_b"][i] = jnp.concatenate([dbg, dbv], axis=1)[0]
        dz = conv_bwd_in(dacc, s["cw"], name=f"conv_bwd_in{i}")
        gbig["f_w_up"][i] = mm_tn(s["hn2"], dz, blocks=N_DEV, name=f"g_up{i}")
        dhn2 = mm_nt(dz, s["w_up"], name=f"d_hn2{i}")
        st = exchange_start([gbig["f_w_down"][i], gbig["f_w_up"][i]], bcast=False, name=f"xchg_ffn_start{i}")
        pending.append((st, [("f_w_down", i), ("f_w_up", i)], f"ffn{i}"))
        dh_mid, dmb, dg = rms_bwd(s["h_mid"], norm_ffn_g[i][None], dhn2, dh, dep=st[-1], name=f"rms_ffn_bwd{i}")
        gs["norm_ffn_g"][i] = dg[0]
        if i % 2 == 0:
            gbig["a_w_out"][j] = mm_tn(s["us"], dmb, blocks=1, name=f"g_aout{i}").reshape(N_DEV, E_A // N_DEV, D)
            d_us = mm_nt(dmb, s["w_out"], name=f"d_us{i}")
            dp, dws, dbt, dgv = sgu_bwd(s["p"], d_us, a_g_v[j][None], a_w_s[j], s["bexp"], name=f"sgu_bwd{i}")
            gs["a_w_s"][j], gs["a_b_s"][j], gs["a_g_v"][j] = dws, dbt[:, :H].T, dgv[0]
            gbig["a_w_in"][j] = mm_tn(s["hn"], dp, blocks=N_DEV, name=f"g_ain{i}")
            dhn = mm_nt(dp, s["w_in"], name=f"d_hn_a{i}")
        else:
            dpg = glu_bwd(dh_mid, s["pg"], name=f"glu_bwd{i}")
            gbig["b_w_glu"][j] = mm_tn(s["q"], dpg, blocks=N_DEV, name=f"g_glu{i}")
            dq = mm_nt(dpg, s["w_glu"], name=f"d_q{i}")
            duu, dbr, dbi, dcr, dci, dar, dai, ddd = s5_bwd(
                s["uu"], s["y"], dq, pm, pmt, s["st_r"], s["st_i"], s["bd_r"], s["bd_i"], s["ct_r"], s["ct_i"],
                s["ab_r"], s["ab_i"], s["dd"], tb=tb, name=f"s5_bwd{i}")
            from_blocks = lambda t: _diag_blocks(t, KG, C, P, eye).transpose(2, 0, 1, 3).reshape(C, G, P)
            d_are, d_aim, d_ldt, d_bre, d_bim = s5_disc_bwd(
                *s["prm"], dar.reshape(G, P), dai.reshape(G, P), from_blocks(dbr), from_blocks(dbi),
                name=f"s5_disc_bwd{i}")
            gs["b_a_re"][j], gs["b_a_im"][j], gs["b_log_dt"][j] = d_are, d_aim, d_ldt[:, 0]
            gs["b_b_re"][j], gs["b_b_im"][j] = d_bre.transpose(1, 2, 0), d_bim.transpose(1, 2, 0)
            gs["b_c_re"][j] = _diag_blocks(dcr, KG, C, P, eye).reshape(G, C, P)
            gs["b_c_im"][j] = _diag_blocks(dci, KG, C, P, eye).reshape(G, C, P)
            gs["b_d"][j] = ddd.reshape(E_B)
            gbig["b_w_in"][j] = mm_tn(s["hn"], duu, blocks=1, name=f"g_bin{i}").reshape(N_DEV, D // N_DEV, E_B)
            dhn = mm_nt(duu, s["w_in"], name=f"d_hn_b{i}")
        mix = ("a_w_out", "a_w_in") if i % 2 == 0 else ("b_w_glu", "b_w_in")
        st = exchange_start([gbig[n][j] for n in mix], bcast=False, name=f"xchg_mix_start{i}")
        pending.append((st, [(n, j) for n in mix], f"mix{i}"))
        dh, dhb, dg = rms_bwd(s["h"], norm_mix_g[i][None], dhn, dh_mid, dep=st[-1], name=f"rms_mix_bwd{i}")
        gs["norm_mix_g"][i] = dg[0]
    grad_x = dh[None]

    grads, deltas, new_m, new_v = {}, {}, {}, {}
    small = [n for n in WEIGHTS if n not in BIG]
    full = {n: (dg_final[0] if n == "final_g" else jnp.stack(gs[n])) for n in small}
    flat = _flatten_pack([full[n] for n in small])
    rows = flat.shape[0] // N_DEV // LANES
    st_small = exchange_start([flat.reshape(N_DEV, rows, LANES)], bcast=False, name="xchg_small_start")

    done = {n: None for n in BIG}

    def finish(entries, after):
        for st, items, tag in entries:
            recvs = exchange_wait(st, after, bcast=False, name=f"xchg_wait_{tag}")
            for (n, l), recv in zip(items, recvs):
                done[n] = adam_reduce(recv, W[n], Mo[n], Vo[n], l, done[n], name=f"adam_{n}{l}")
                after = done[n][0]
        return after

    after = finish(pending[:-1], st_small[-1])
    recv = exchange_wait(st_small, after, bcast=False, name="xchg_small_wait")[0]
    part = sum_slots(recv, name="sum_small")
    tot = all_gather([part], name="gather_small")[0].reshape(-1)
    red = dict(zip(small, _unpack(tot, [full[n] for n in small])))
    red["f_conv_w"] = lax.dynamic_index_in_dim(red["f_conv_w"], me, axis=1, keepdims=False)
    red["b_d"] = lax.dynamic_slice_in_dim(red["b_d"], me * (E_B // N_DEV), E_B // N_DEV, axis=1)
    gflat = _flatten_pack([red[n] for n in small]).reshape(-1, LANES)
    d_f, m_f, v_f = adam_flat(
        gflat, _flatten_pack([W[n] for n in small]).reshape(-1, LANES),
        _flatten_pack([Mo[n] for n in small]).reshape(-1, LANES),
        _flatten_pack([Vo[n] for n in small]).reshape(-1, LANES), name="adam_small")
    like = [W[n] for n in small]
    for n, d_, m_, v_ in zip(small, _unpack(d_f.reshape(-1), like), _unpack(m_f.reshape(-1), like),
                             _unpack(v_f.reshape(-1), like)):
        grads[n], deltas[n], new_m[n], new_v[n] = red[n], d_, m_, v_

    finish(pending[-1:], d_f)
    for n in BIG:
        grads[n], deltas[n], new_m[n], new_v[n] = done[n]

    return (loss, grad_x, *[grads[n] for n in WEIGHTS], *[deltas[n] for n in WEIGHTS],
            *[new_m[n] for n in WEIGHTS], *[new_v[n] for n in WEIGHTS])
```

```python
import math

import jax
import jax.numpy as jnp
from jax import lax
from jax.experimental import pallas as pl
from jax.experimental.pallas import tpu as pltpu

F32 = jnp.float32
BF16 = jnp.bfloat16
N_DEV = 8
AXES = ("x", "y", "c")
EPS = 1e-6
LANES = 128
SUBLANES = 8
VMEM_BIG = 56 * 1024 * 1024
VMEM_MID = 40 * 1024 * 1024
ADAM_LR, ADAM_B1, ADAM_B2, ADAM_EPS, ADAM_WD, ADAM_STEP = 0.001, 0.9, 0.999, 1e-08, 0.01, 10
MESH = pl.DeviceIdType.MESH
GELU_C = math.sqrt(2.0 / math.pi)
GELU_K = 0.044715


def _pick(n, prefs):
    for p in prefs:
        if n % p == 0:
            return p
    return n


def _params(sem, vmem=None):
    return pltpu.CompilerParams(dimension_semantics=sem, vmem_limit_bytes=vmem)


def _gelu(x):
    return 0.5 * x * (1.0 + jnp.tanh(GELU_C * (x + GELU_K * x * x * x)))


def _gelu_grad(x):
    x2 = x * x
    th = jnp.tanh(GELU_C * x * (1.0 + GELU_K * x2))
    return 0.5 * (1.0 + th) + 0.5 * x * (1.0 - th * th) * GELU_C * (1.0 + 3.0 * GELU_K * x2)


def _sigmoid(x):
    return 1.0 / (1.0 + jnp.exp(-x))


def _dot_nn(a, b):
    return jnp.dot(a, b, preferred_element_type=F32)


def _dot_nt(a, b):
    return lax.dot_general(a, b, (((1,), (1,)), ((), ())), preferred_element_type=F32)


def _dot_tn(a, b):
    return lax.dot_general(a, b, (((0,), (0,)), ((), ())), preferred_element_type=F32)


M_TILES = (1024, 512, 256, 128)
TN_M_TILES = (2048, 1024, 512, 256, 128)
FULL_K = 2048
NT_SPAN = 2816
N_TILES = (1408, 1024, 512, 384, 256, 128)
K_TILES = (1408, 1024, 512, 384, 256, 128)


def _as3(b):
    return b if b.ndim == 3 else b[None]


def mm_nn(a, b, *, res=None, out_dtype=F32, name):
    b3 = _as3(b)
    M, K = a.shape
    J, _, nb = b3.shape
    tm, tn = _pick(M, M_TILES), _pick(nb, N_TILES)
    tk = K if K <= FULL_K else _pick(K, (2816,) + K_TILES)
    per, nk = nb // tn, K // tk

    def body(*refs):
        if res is None:
            a_ref, b_ref, o_ref, acc = refs
        else:
            a_ref, b_ref, r_ref, o_ref, acc = refs
        k = pl.program_id(2)
        if nk == 1:
            r = _dot_nn(a_ref[...], b_ref[...])
            if res is not None:
                r = r + r_ref[...]
            o_ref[...] = r.astype(out_dtype)
            return

        @pl.when(k == 0)
        def _():
            acc[...] = jnp.zeros_like(acc)

        acc[...] += _dot_nn(a_ref[...], b_ref[...])

        @pl.when(k == nk - 1)
        def _():
            r = acc[...]
            if res is not None:
                r = r + r_ref[...]
            o_ref[...] = r.astype(out_dtype)

    in_specs = [pl.BlockSpec((tm, tk), lambda i, n, k: (i, k)),
                pl.BlockSpec((None, tk, tn), lambda i, n, k: (n // per, k, n % per))]
    args = [a, b3]
    if res is not None:
        in_specs.append(pl.BlockSpec((tm, tn), lambda i, n, k: (i, n)))
        args.append(res)
    return pl.pallas_call(
        body, grid=(M // tm, J * per, nk), in_specs=in_specs,
        out_specs=pl.BlockSpec((tm, tn), lambda i, n, k: (i, n)),
        out_shape=jax.ShapeDtypeStruct((M, J * nb), out_dtype),
        scratch_shapes=[pltpu.VMEM((tm, tn) if nk > 1 else (SUBLANES, LANES), F32)], name=name,
        compiler_params=_params(("parallel", "parallel", "arbitrary"), VMEM_BIG))(*args)


def mm_nt(dy, b, *, out_dtype=F32, dep=None, name):
    b3 = _as3(b)
    deps = [] if dep is None else [dep]
    M, N = dy.shape
    J, K, nb = b3.shape
    tm = _pick(M, M_TILES)
    tn = nb if (J == 1 and nb <= FULL_K) else _pick(nb, N_TILES)
    grp = max([g for g in (4, 2, 1) if J % g == 0 and g * tn <= NT_SPAN]) if tn == nb else 1
    tk = _pick(K, K_TILES) if (grp > 1 or K > FULL_K) else K
    per, nn = nb // tn, (J * nb) // (tn * grp)

    def body(d_ref, b_ref, *rest):
        o_ref, acc = rest[-2:]
        if grp > 1:
            r = sum(_dot_nt(d_ref[:, g * tn:(g + 1) * tn], b_ref[g]) for g in range(grp))
        else:
            r = _dot_nt(d_ref[...], b_ref[...])
        if nn == 1:
            o_ref[...] = r.astype(out_dtype)
            return
        n = pl.program_id(2)

        @pl.when(n == 0)
        def _():
            acc[...] = jnp.zeros_like(acc)

        acc[...] += r

        @pl.when(n == nn - 1)
        def _():
            o_ref[...] = acc[...].astype(out_dtype)

    if grp > 1:
        b_spec = pl.BlockSpec((grp, tk, tn), lambda i, k, n: (n, k, 0))
    else:
        b_spec = pl.BlockSpec((None, tk, tn), lambda i, k, n: (n // per, k, n % per))
    return pl.pallas_call(
        body, grid=(M // tm, K // tk, nn),
        in_specs=[pl.BlockSpec((tm, tn * grp), lambda i, k, n: (i, n)), b_spec]
        + [pl.BlockSpec(memory_space=pl.ANY)] * len(deps),
        out_specs=pl.BlockSpec((tm, tk), lambda i, k, n: (i, k)),
        out_shape=jax.ShapeDtypeStruct((M, K), out_dtype),
        scratch_shapes=[pltpu.VMEM((tm, tk) if nn > 1 else (SUBLANES, LANES), F32)], name=name,
        compiler_params=_params(("parallel", "parallel", "arbitrary"), VMEM_BIG))(dy, b3, *deps)


def mm_tn(x, dy, *, blocks, out_dtype=BF16, name):
    M, K = x.shape
    _, N = dy.shape
    nb = N // blocks
    tm, tn, tk = _pick(M, TN_M_TILES), _pick(nb, N_TILES), _pick(K, K_TILES)
    per, nm = nb // tn, M // tm

    def body(x_ref, d_ref, o_ref, acc):
        m = pl.program_id(2)

        @pl.when(m == 0)
        def _():
            acc[...] = jnp.zeros_like(acc)

        acc[...] += _dot_tn(x_ref[...], d_ref[...])

        @pl.when(m == nm - 1)
        def _():
            o_ref[...] = acc[...].astype(out_dtype)

    return pl.pallas_call(
        body, grid=(K // tk, N // tn, nm),
        in_specs=[pl.BlockSpec((tm, tk), lambda k, n, m: (m, k)),
                  pl.BlockSpec((tm, tn), lambda k, n, m: (m, n))],
        out_specs=pl.BlockSpec((None, tk, tn), lambda k, n, m: (n // per, k, n % per)),
        out_shape=jax.ShapeDtypeStruct((blocks, K, nb), out_dtype),
        scratch_shapes=[pltpu.VMEM((tk, tn), F32)], name=name,
        compiler_params=_params(("parallel", "parallel", "arbitrary"), VMEM_BIG))(x, dy)


ROW_TILES = (256, 128)


_DEP = pl.BlockSpec(memory_space=pl.ANY)


def rms_fwd(h, g, *, dep=None, name):
    T, D = h.shape
    tm = _pick(T, ROW_TILES)
    deps = [] if dep is None else [dep]

    def body(h_ref, g_ref, *rest):
        o_ref = rest[-1]
        x = h_ref[...]
        r = lax.rsqrt(jnp.mean(x * x, axis=-1, keepdims=True) + EPS)
        o_ref[...] = (x * r * g_ref[...]).astype(BF16)

    return pl.pallas_call(
        body, grid=(T // tm,),
        in_specs=[pl.BlockSpec((tm, D), lambda i: (i, 0)), pl.BlockSpec((1, D), lambda i: (0, 0))] + [_DEP] * len(deps),
        out_specs=pl.BlockSpec((tm, D), lambda i: (i, 0)),
        out_shape=jax.ShapeDtypeStruct((T, D), BF16), name=name,
        compiler_params=_params(("parallel",), VMEM_MID))(h, g, *deps)


def rms_bwd(h, g, dhn, dres, *, dep=None, name):
    T, D = h.shape
    tm = _pick(T, ROW_TILES)
    deps = [] if dep is None else [dep]

    def body(h_ref, g_ref, d_ref, r_ref, *rest):
        dh_ref, dhb_ref, dg_ref = rest[-3:]

        @pl.when(pl.program_id(0) == 0)
        def _():
            dg_ref[...] = jnp.zeros_like(dg_ref)

        x = h_ref[...]
        r = lax.rsqrt(jnp.mean(x * x, axis=-1, keepdims=True) + EPS)
        xh = x * r
        dy = d_ref[...]
        dxh = dy * g_ref[...]
        dh = r_ref[...] + r * (dxh - xh * jnp.mean(dxh * xh, axis=-1, keepdims=True))
        dh_ref[...] = dh
        dhb_ref[...] = dh.astype(BF16)
        dg_ref[...] += jnp.sum(dy * xh, axis=0, keepdims=True)

    row = pl.BlockSpec((tm, D), lambda i: (i, 0))
    vec = pl.BlockSpec((1, D), lambda i: (0, 0))
    return pl.pallas_call(
        body, grid=(T // tm,), in_specs=[row, vec, row, row] + [_DEP] * len(deps), out_specs=[row, row, vec],
        out_shape=[jax.ShapeDtypeStruct((T, D), F32), jax.ShapeDtypeStruct((T, D), BF16),
                   jax.ShapeDtypeStruct((1, D), F32)], name=name,
        compiler_params=_params(("arbitrary",), VMEM_MID))(h, g, dhn, dres, *deps)


def loss_head(h, g, tgt, *, name):
    T, D = h.shape
    tm = _pick(T, ROW_TILES)

    def body(h_ref, g_ref, t_ref, l_ref, dh_ref, dhb_ref, dg_ref):
        @pl.when(pl.program_id(0) == 0)
        def _():
            dg_ref[...] = jnp.zeros_like(dg_ref)
            l_ref[...] = jnp.zeros_like(l_ref)

        x = h_ref[...]
        gg = g_ref[...]
        r = lax.rsqrt(jnp.mean(x * x, axis=-1, keepdims=True) + EPS)
        xh = x * r
        e = xh * gg - t_ref[...]
        l_ref[...] += 0.5 * jnp.sum(jnp.mean(e * e, axis=-1, keepdims=True), axis=0, keepdims=True)
        dy = e * (1.0 / D)
        dxh = dy * gg
        dh = r * (dxh - xh * jnp.mean(dxh * xh, axis=-1, keepdims=True))
        dh_ref[...] = dh
        dhb_ref[...] = dh.astype(BF16)
        dg_ref[...] += jnp.sum(dy * xh, axis=0, keepdims=True)

    row = pl.BlockSpec((tm, D), lambda i: (i, 0))
    vec = pl.BlockSpec((1, D), lambda i: (0, 0))
    return pl.pallas_call(
        body, grid=(T // tm,), in_specs=[row, vec, row],
        out_specs=[pl.BlockSpec((SUBLANES, LANES), lambda i: (0, 0)), row, row, vec],
        out_shape=[jax.ShapeDtypeStruct((SUBLANES, LANES), F32), jax.ShapeDtypeStruct((T, D), F32),
                   jax.ShapeDtypeStruct((T, D), BF16), jax.ShapeDtypeStruct((1, D), F32)], name=name,
        compiler_params=_params(("arbitrary",), VMEM_MID))(h, g, tgt)


def _sgu_common(p, gv, w_ref, bexp, E, H, CH):
    Dg = E // H
    z = _gelu(p)
    u, v = z[:, :E], z[:, E:]
    r = lax.rsqrt(jnp.mean(v * v, axis=-1, keepdims=True) + EPS)
    vhat = v * r
    vn = (vhat * gv).astype(BF16)
    row = lax.broadcasted_iota(jnp.int32, (CH, CH), 0)
    col = lax.broadcasted_iota(jnp.int32, (CH, CH), 1)
    causal = row >= col
    ws = [jnp.where(causal, w_ref[hh], 0.0).astype(BF16) for hh in range(H)]
    s = jnp.concatenate([_dot_nn(ws[hh], vn[:, hh * Dg:(hh + 1) * Dg]) for hh in range(H)], axis=1) + bexp
    return u, r, vhat, vn, causal, ws, s


def sgu_fwd(p, g_v, w_s, bexp, *, name):
    T, E2 = p.shape
    E = E2 // 2
    H, CH, _ = w_s.shape

    def body(p_ref, gv_ref, w_ref, b_ref, o_ref):
        u, _, _, _, _, _, s = _sgu_common(p_ref[...], gv_ref[...], w_ref, b_ref[...], E, H, CH)
        o_ref[...] = (u * s).astype(BF16)

    return pl.pallas_call(
        body, grid=(T // CH,),
        in_specs=[pl.BlockSpec((CH, E2), lambda i: (i, 0)), pl.BlockSpec((1, E), lambda i: (0, 0)),
                  pl.BlockSpec((H, CH, CH), lambda i: (0, 0, 0)), pl.BlockSpec((CH, E), lambda i: (0, 0))],
        out_specs=pl.BlockSpec((CH, E), lambda i: (i, 0)),
        out_shape=jax.ShapeDtypeStruct((T, E), BF16), name=name,
        compiler_params=_params(("parallel",), VMEM_BIG))(p, g_v, w_s, bexp)


def sgu_bwd(p, d_us, g_v, w_s, bexp, *, name):
    T, E2 = p.shape
    E = E2 // 2
    H, CH, _ = w_s.shape
    Dg = E // H

    def body(p_ref, d_ref, gv_ref, w_ref, b_ref, dp_ref, dw_ref, db_ref, dg_ref):
        @pl.when(pl.program_id(0) == 0)
        def _():
            dw_ref[...] = jnp.zeros_like(dw_ref)
            db_ref[...] = jnp.zeros_like(db_ref)
            dg_ref[...] = jnp.zeros_like(dg_ref)

        p = p_ref[...]
        gv = gv_ref[...]
        u, r, vhat, vn, causal, ws, s = _sgu_common(p, gv, w_ref, b_ref[...], E, H, CH)
        d = d_ref[...]
        du = d * s
        ds = d * u
        lane = lax.broadcasted_iota(jnp.int32, (CH, LANES), 1)
        dvn_parts = []
        db = jnp.zeros((CH, LANES), F32)
        for hh in range(H):
            ds_h = ds[:, hh * Dg:(hh + 1) * Dg]
            ds_hb = ds_h.astype(BF16)
            dw_ref[hh] += jnp.where(causal, _dot_nt(ds_hb, vn[:, hh * Dg:(hh + 1) * Dg]), 0.0)
            dvn_parts.append(_dot_tn(ws[hh], ds_hb))
            db = db + jnp.where(lane == hh, jnp.sum(ds_h, axis=1, keepdims=True), 0.0)
        db_ref[...] += db
        dvn = jnp.concatenate(dvn_parts, axis=1)
        dg_ref[...] += jnp.sum(dvn * vhat, axis=0, keepdims=True)
        dvh = dvn * gv
        dv = r * (dvh - vhat * jnp.mean(dvh * vhat, axis=-1, keepdims=True))
        dp_ref[...] = (jnp.concatenate([du, dv], axis=1) * _gelu_grad(p)).astype(BF16)

    return pl.pallas_call(
        body, grid=(T // CH,),
        in_specs=[pl.BlockSpec((CH, E2), lambda i: (i, 0)), pl.BlockSpec((CH, E), lambda i: (i, 0)),
                  pl.BlockSpec((1, E), lambda i: (0, 0)), pl.BlockSpec((H, CH, CH), lambda i: (0, 0, 0)),
                  pl.BlockSpec((CH, E), lambda i: (0, 0))],
        out_specs=[pl.BlockSpec((CH, E2), lambda i: (i, 0)), pl.BlockSpec((H, CH, CH), lambda i: (0, 0, 0)),
                   pl.BlockSpec((CH, LANES), lambda i: (0, 0)), pl.BlockSpec((1, E), lambda i: (0, 0))],
        out_shape=[jax.ShapeDtypeStruct((T, E2), BF16), jax.ShapeDtypeStruct((H, CH, CH), F32),
                   jax.ShapeDtypeStruct((CH, LANES), F32), jax.ShapeDtypeStruct((1, E), F32)], name=name,
        compiler_params=_params(("arbitrary",), VMEM_BIG))(p, d_us, g_v, w_s, bexp)


def _s5_disc(a_re, a_im, log_dt, b_re, b_im):
    dt = jnp.exp(log_dt)
    mag = jnp.exp(dt * a_re)
    ar, ai = mag * jnp.cos(dt * a_im), mag * jnp.sin(dt * a_im)
    den = a_re * a_re + a_im * a_im
    qr = ((ar - 1.0) * a_re + ai * a_im) / den
    qi = (ai * a_re - (ar - 1.0) * a_im) / den
    return ar, ai, qr[None] * b_re - qi[None] * b_im, qr[None] * b_im + qi[None] * b_re


def s5_disc_fwd(a_re, a_im, log_dt, b_re, b_im, *, name):
    G, P = a_re.shape
    C = b_re.shape[0]

    def body(ar_ref, ai_ref, dt_ref, br_ref, bi_ref, o_ar, o_ai, o_br, o_bi):
        ar, ai, br, bi = _s5_disc(ar_ref[...], ai_ref[...], dt_ref[...], br_ref[...], bi_ref[...])
        o_ar[...] = ar
        o_ai[...] = ai
        o_br[...] = br
        o_bi[...] = bi

    gp = jax.ShapeDtypeStruct((G, P), F32)
    cgp = jax.ShapeDtypeStruct((C, G, P), F32)
    return pl.pallas_call(body, out_shape=[gp, gp, cgp, cgp], name=name)(a_re, a_im, log_dt, b_re, b_im)


def s5_disc_bwd(a_re, a_im, log_dt, b_re, b_im, d_ar, d_ai, d_br, d_bi, *, name):
    G, P = a_re.shape
    C = b_re.shape[0]

    def body(ar_ref, ai_ref, dt_ref, br_ref, bi_ref, g0, g1, g2, g3, o0, o1, o2, o3, o4):
        prim = (ar_ref[...], ai_ref[...], dt_ref[...], br_ref[...], bi_ref[...])
        _, vjp = jax.vjp(_s5_disc, *prim)
        outs = vjp((g0[...], g1[...], g2[...], g3[...]))
        for o, v in zip((o0, o1, o2, o3, o4), outs):
            o[...] = v

    gp = jax.ShapeDtypeStruct((G, P), F32)
    cgp = jax.ShapeDtypeStruct((C, G, P), F32)
    return pl.pallas_call(
        body, out_shape=[gp, gp, jax.ShapeDtypeStruct((G, 1), F32), cgp, cgp], name=name,
    )(a_re, a_im, log_dt, b_re, b_im, d_ar, d_ai, d_br, d_bi)


S5_KG = 8


def _planes(x):
    return [x[:, c * LANES:(c + 1) * LANES] for c in range(x.shape[1] // LANES)]


def _store_planes(ref, row0, val):
    for c, p in enumerate(_planes(val)):
        ref[c, pl.ds(row0, val.shape[0]), :] = p


def _load_planes(ref, row0, rows):
    return jnp.concatenate([ref[c, pl.ds(row0, rows), :] for c in range(ref.shape[0])], axis=1)


def _build_powers(a_r, a_i, S, pf_r, pf_i, pr_r=None, pr_i=None):
    ar, ai = _planes(a_r), _planes(a_i)
    NP = len(ar)

    def step(i, carry):
        out = []
        for c in range(NP):
            p_r, p_i = carry[2 * c], carry[2 * c + 1]
            pf_r[c, pl.ds(i, 1), :] = p_r
            pf_i[c, pl.ds(i, 1), :] = p_i
            if pr_r is not None:
                pr_r[c, pl.ds(S - 1 - i, 1), :] = p_r
                pr_i[c, pl.ds(S - 1 - i, 1), :] = -p_i
            out += [ar[c] * p_r - ai[c] * p_i, ar[c] * p_i + ai[c] * p_r]
        return tuple(out)

    init = []
    for c in range(NP):
        init += [ar[c], ai[c]]
    lax.fori_loop(0, S, step, tuple(init))


def _scan_seg(hr, hi, hrow0, tb, a_r, a_i, pw_r, pw_i, h0r, h0i, *, reverse):
    NP = hr.shape[0]
    S = tb // SUBLANES
    if reverse:
        a_i = -a_i
    ar, ai = _planes(a_r), _planes(a_i)

    def step(i, carry):
        j = (S - 1 - i) if reverse else i
        slab = pl.ds(pl.multiple_of(hrow0 + j * SUBLANES, SUBLANES), SUBLANES)
        out = []
        for c in range(NP):
            nr = ar[c] * carry[2 * c] - ai[c] * carry[2 * c + 1] + hr[c, slab, :]
            ni = ar[c] * carry[2 * c + 1] + ai[c] * carry[2 * c] + hi[c, slab, :]
            hr[c, slab, :] = nr
            hi[c, slab, :] = ni
            out += [nr, ni]
        return tuple(out)

    z = jnp.zeros((SUBLANES, LANES), F32)
    loc = lax.fori_loop(0, S, step, (z,) * (2 * NP), unroll=2)
    h0r_p, h0i_p = _planes(h0r), _planes(h0i)
    top = 0 if reverse else S - 1
    order = range(SUBLANES - 1, -1, -1) if reverse else range(SUBLANES)
    out_r, out_i, ent_r, ent_i = [], [], [], []
    for c in range(NP):
        s_r, s_i = pw_r[c, pl.ds(top, 1), :], pw_i[c, pl.ds(top, 1), :]
        c_r, c_i = h0r_p[c], h0i_p[c]
        in_r, in_i = [None] * SUBLANES, [None] * SUBLANES
        for seg in order:
            in_r[seg], in_i[seg] = c_r, c_i
            l_r, l_i = loc[2 * c][seg:seg + 1], loc[2 * c + 1][seg:seg + 1]
            c_r, c_i = s_r * c_r - s_i * c_i + l_r, s_r * c_i + s_i * c_r + l_i
        out_r.append(c_r)
        out_i.append(c_i)
        ent_r.append(jnp.concatenate(in_r, axis=0))
        ent_i.append(jnp.concatenate(in_i, axis=0))

    def fix(j, _):
        slab = pl.ds(pl.multiple_of(hrow0 + j * SUBLANES, SUBLANES), SUBLANES)
        for c in range(NP):
            p_r, p_i = pw_r[c, pl.ds(j, 1), :], pw_i[c, pl.ds(j, 1), :]
            hr[c, slab, :] += p_r * ent_r[c] - p_i * ent_i[c]
            hi[c, slab, :] += p_r * ent_i[c] + p_i * ent_r[c]
        return 0

    lax.fori_loop(0, S, fix, 0, unroll=2)
    return ent_r, ent_i, jnp.concatenate(out_r, axis=1), jnp.concatenate(out_i, axis=1)


def _perm_matrices(tb):
    r = jnp.arange(tb)
    pm = (r[None, :] == ((r % SUBLANES) * (tb // SUBLANES) + r // SUBLANES)[:, None]).astype(BF16)
    return pm, pm.T


def _unpermute(pmt, x):
    c = x.shape[1]
    hi = x.astype(BF16)
    lo = (x - hi.astype(F32)).astype(BF16)
    both = _dot_nn(pmt, jnp.concatenate([hi, lo], axis=1))
    return both[:, :c] + both[:, c:]


def _s5_specs(tb, UC, SC, rev_nb=None):
    tmap = (lambda b: b) if rev_nb is None else (lambda b: rev_nb - 1 - b)
    row = pl.BlockSpec((tb, UC), lambda b, k: (tmap(b), k))
    wsp = pl.BlockSpec((None, UC, SC), lambda b, k: (k, 0, 0))
    vsc = pl.BlockSpec((None, 1, SC), lambda b, k: (k, 0, 0))
    vuc = pl.BlockSpec((None, 1, UC), lambda b, k: (k, 0, 0))
    st = pl.BlockSpec((None, None, 1, SC), lambda b, k: (tmap(b), k, 0, 0))
    return row, wsp, vsc, vuc, st


def s5_fwd(uu, pm, pmt, bd_r, bd_i, ct_r, ct_i, ab_r, ab_i, dd, *, tb, name):
    T, E = uu.shape
    KB, UC, SC = bd_r.shape
    NB = T // tb
    NP, S = SC // LANES, tb // SUBLANES

    def body(u_ref, pm_ref, pmt_ref, bdr, bdi, ctr, cti, ar_ref, ai_ref, dd_ref, y_ref, q_ref, sr_ref, si_ref,
             hr, hi, cr, ci, pf_r, pf_i):
        b, k = pl.program_id(0), pl.program_id(1)
        a_r, a_i = ar_ref[...], ai_ref[...]

        @pl.when(b == 0)
        def _():
            cr[k] = jnp.zeros((1, SC), F32)
            ci[k] = jnp.zeros((1, SC), F32)
            _build_powers(a_r, a_i, S, pf_r.at[k], pf_i.at[k])

        h0r, h0i = cr[k], ci[k]
        sr_ref[...] = h0r
        si_ref[...] = h0i
        u = u_ref[...]
        up = _dot_nn(pm_ref[...], u.astype(BF16)).astype(BF16)
        _store_planes(hr, 0, _dot_nn(up, bdr[...]))
        _store_planes(hi, 0, _dot_nn(up, bdi[...]))
        _, _, o_r, o_i = _scan_seg(hr, hi, 0, tb, a_r, a_i, pf_r.at[k], pf_i.at[k], h0r, h0i, reverse=False)
        cr[k] = o_r
        ci[k] = o_i
        ys = (_dot_nt(_load_planes(hr, 0, tb).astype(BF16), ctr[...])
              - _dot_nt(_load_planes(hi, 0, tb).astype(BF16), cti[...]))
        pmt_v = pmt_ref[...]
        y = _unpermute(pmt_v, ys) + dd_ref[...] * u
        y_ref[...] = y
        q_ref[...] = _gelu(y).astype(BF16)

    row, wsp, vsc, vuc, st = _s5_specs(tb, UC, SC)
    psp = pl.BlockSpec((tb, tb), lambda b, k: (0, 0))
    stsh = jax.ShapeDtypeStruct((NB, KB, 1, SC), F32)
    pw = pltpu.VMEM((KB, NP, S, LANES), F32)
    pln = pltpu.VMEM((NP, tb, LANES), F32)
    return pl.pallas_call(
        body, grid=(NB, KB), in_specs=[row, psp, psp, wsp, wsp, wsp, wsp, vsc, vsc, vuc],
        out_specs=[row, row, st, st],
        out_shape=[jax.ShapeDtypeStruct((T, E), F32), jax.ShapeDtypeStruct((T, E), BF16), stsh, stsh],
        scratch_shapes=[pln, pln, pltpu.VMEM((KB, 1, SC), F32), pltpu.VMEM((KB, 1, SC), F32), pw, pw],
        name=name, compiler_params=_params(("arbitrary", "arbitrary"), VMEM_MID),
    )(uu, pm, pmt, bd_r, bd_i, ct_r, ct_i, ab_r, ab_i, dd)


def s5_bwd(uu, y, dq, pm, pmt, st_r, st_i, bd_r, bd_i, ct_r, ct_i, ab_r, ab_i, dd, *, tb, name):
    T, E = uu.shape
    KB, UC, SC = bd_r.shape
    NB = T // tb
    HDR = SUBLANES
    NP, S = SC // LANES, tb // SUBLANES
    pw = pltpu.VMEM((KB, NP, S, LANES), F32)
    pln = pltpu.VMEM((NP, tb, LANES), F32)

    def body(u_ref, y_ref, dq_ref, pm_ref, pmt_ref, sr_ref, si_ref, bdr, bdi, ctr, cti, ar_ref, ai_ref, dd_ref,
             du_ref, obr, obi, ocr, oci, odar, odai, oddd,
             hr, hi, gr, gi, kr, ki, abr, abi, acr, aci, pf_r, pf_i, pr_r, pr_i):
        b, k = pl.program_id(0), pl.program_id(1)
        a_r, a_i = ar_ref[...], ai_ref[...]

        @pl.when(b == 0)
        def _():
            _build_powers(a_r, a_i, S, pf_r.at[k], pf_i.at[k], pr_r.at[k], pr_i.at[k])
            z1 = jnp.zeros((1, SC), F32)
            kr[k] = z1
            ki[k] = z1
            odar[k] = z1
            odai[k] = z1
            oddd[k] = jnp.zeros((1, UC), F32)
            zw = jnp.zeros((UC, SC), F32)
            abr[k] = zw
            abi[k] = zw
            acr[k] = zw
            aci[k] = zw

        u = u_ref[...]
        dy = dq_ref[...] * _gelu_grad(y_ref[...])
        oddd[k] += jnp.sum(dy * u, axis=0, keepdims=True)
        pm_v = pm_ref[...]
        both = _dot_nn(pm_v, jnp.concatenate([u.astype(BF16), dy.astype(BF16)], axis=1))
        ub = both[:, :UC].astype(BF16)
        dyb = both[:, UC:].astype(BF16)
        s0r, s0i = sr_ref[...], si_ref[...]
        _store_planes(hr, HDR, _dot_nn(ub, bdr[...]))
        _store_planes(hi, HDR, _dot_nn(ub, bdi[...]))
        e_r, e_i, _, _ = _scan_seg(hr, hi, HDR, tb, a_r, a_i, pf_r.at[k], pf_i.at[k], s0r, s0i, reverse=False)
        for c in range(NP):
            hr[c, pl.ds(0, HDR), :] = e_r[c]
            hi[c, pl.ds(0, HDR), :] = e_i[c]
        _store_planes(gr, 0, _dot_nn(dyb, ctr[...]))
        _store_planes(gi, 0, -_dot_nn(dyb, cti[...]))
        _, _, g0r, g0i = _scan_seg(gr, gi, 0, tb, a_r, a_i, pr_r.at[k], pr_i.at[k], kr[k], ki[k], reverse=True)
        kr[k] = g0r
        ki[k] = g0i

        def slab(j, acc):
            o = pl.multiple_of(j * SUBLANES, SUBLANES)
            out = []
            for c in range(NP):
                p_r, p_i = hr[c, pl.ds(o, SUBLANES), :], hi[c, pl.ds(o, SUBLANES), :]
                g_r, g_i = gr[c, pl.ds(o, SUBLANES), :], gi[c, pl.ds(o, SUBLANES), :]
                out += [acc[2 * c] + g_r * p_r + g_i * p_i, acc[2 * c + 1] + g_i * p_r - g_r * p_i]
            return tuple(out)

        z8 = jnp.zeros((SUBLANES, LANES), F32)
        acc = lax.fori_loop(0, S, slab, (z8,) * (2 * NP), unroll=2)
        odar[k] += jnp.concatenate([jnp.sum(acc[2 * c], axis=0, keepdims=True) for c in range(NP)], axis=1)
        odai[k] += jnp.concatenate([jnp.sum(acc[2 * c + 1], axis=0, keepdims=True) for c in range(NP)], axis=1)
        g_rb = _load_planes(gr, 0, tb).astype(BF16)
        g_ib = _load_planes(gi, 0, tb).astype(BF16)
        h_rb = _load_planes(hr, HDR, tb).astype(BF16)
        h_ib = _load_planes(hi, HDR, tb).astype(BF16)
        dus = _dot_nt(g_rb, bdr[...]) + _dot_nt(g_ib, bdi[...])
        pmt_v = pmt_ref[...]
        du = _unpermute(pmt_v, dus) + dd_ref[...] * dy
        du_ref[...] = du.astype(BF16)
        abr[k] += _dot_tn(ub, g_rb)
        abi[k] += _dot_tn(ub, g_ib)
        acr[k] += _dot_tn(dyb, h_rb)
        aci[k] -= _dot_tn(dyb, h_ib)

        @pl.when(jnp.logical_and(b == NB - 1, k == KB - 1))
        def _():
            pltpu.sync_copy(abr, obr)
            pltpu.sync_copy(abi, obi)
            pltpu.sync_copy(acr, ocr)
            pltpu.sync_copy(aci, oci)

    row, wsp, vsc, vuc, st = _s5_specs(tb, UC, SC, rev_nb=NB)
    psp = pl.BlockSpec((tb, tb), lambda b, k: (0, 0))
    hbm = pl.BlockSpec(memory_space=pltpu.HBM)
    full_sc = pl.BlockSpec((KB, 1, SC), lambda b, k: (0, 0, 0))
    full_uc = pl.BlockSpec((KB, 1, UC), lambda b, k: (0, 0, 0))
    wsh = jax.ShapeDtypeStruct((KB, UC, SC), F32)
    acc = pltpu.VMEM((KB, UC, SC), F32)
    return pl.pallas_call(
        body, grid=(NB, KB),
        in_specs=[row, row, row, psp, psp, st, st, wsp, wsp, wsp, wsp, vsc, vsc, vuc],
        out_specs=[row, hbm, hbm, hbm, hbm, full_sc, full_sc, full_uc],
        out_shape=[jax.ShapeDtypeStruct((T, E), BF16), wsh, wsh, wsh, wsh,
                   jax.ShapeDtypeStruct((KB, 1, SC), F32), jax.ShapeDtypeStruct((KB, 1, SC), F32),
                   jax.ShapeDtypeStruct((KB, 1, UC), F32)],
        scratch_shapes=[pltpu.VMEM((NP, tb + HDR, LANES), F32), pltpu.VMEM((NP, tb + HDR, LANES), F32), pln, pln,
                        pltpu.VMEM((KB, 1, SC), F32), pltpu.VMEM((KB, 1, SC), F32), acc, acc, acc, acc,
                        pw, pw, pw, pw],
        name=name, compiler_params=_params(("arbitrary", "arbitrary"), VMEM_BIG),
    )(uu, y, dq, pm, pmt, st_r, st_i, bd_r, bd_i, ct_r, ct_i, ab_r, ab_i, dd)


def glu_fwd(h, pg, *, name):
    T, D = h.shape
    tm = _pick(T, ROW_TILES)

    def body(h_ref, a_ref, b_ref, o_ref):
        o_ref[...] = h_ref[...] + a_ref[...] * _sigmoid(b_ref[...])

    row = pl.BlockSpec((tm, D), lambda i: (i, 0))
    return pl.pallas_call(
        body, grid=(T // tm,), in_specs=[row, row, pl.BlockSpec((tm, D), lambda i: (i, 1))], out_specs=row,
        out_shape=jax.ShapeDtypeStruct((T, D), F32), name=name,
        compiler_params=_params(("parallel",), VMEM_MID))(h, pg, pg)


def glu_bwd(d, pg, *, name):
    T, D = d.shape
    tm = _pick(T, ROW_TILES)

    def body(d_ref, a_ref, b_ref, o_ref):
        dv = d_ref[...]
        sg = _sigmoid(b_ref[...])
        da = dv * sg
        db = dv * a_ref[...] * sg * (1.0 - sg)
        o_ref[...] = jnp.where(pl.program_id(1) == 0, da, db).astype(BF16)

    row = pl.BlockSpec((tm, D), lambda i, hf: (i, 0))
    return pl.pallas_call(
        body, grid=(T // tm, 2), in_specs=[row, row, pl.BlockSpec((tm, D), lambda i, hf: (i, 1))],
        out_specs=pl.BlockSpec((tm, D), lambda i, hf: (i, hf)),
        out_shape=jax.ShapeDtypeStruct((T, 2 * D), BF16), name=name,
        compiler_params=_params(("parallel", "arbitrary"), VMEM_MID))(d, pg, pg)


def _shift_down(x, halo, s):
    r = pltpu.roll(x, s, axis=0)
    hr = pltpu.roll(halo, s, axis=0)
    row = lax.broadcasted_iota(jnp.int32, halo.shape, 0)
    head = jnp.where(row < s, hr, r[:SUBLANES])
    return jnp.concatenate([head, r[SUBLANES:]], axis=0)


def _shift_up(x, halo, s):
    n = x.shape[0]
    r = pltpu.roll(x, n - s, axis=0)
    hr = pltpu.roll(halo, SUBLANES - s, axis=0)
    row = lax.broadcasted_iota(jnp.int32, halo.shape, 0)
    tail = jnp.where(row >= SUBLANES - s, hr, r[n - SUBLANES:])
    return jnp.concatenate([r[:n - SUBLANES], tail], axis=0)


def _conv_acc(z, zh, w, b, first):
    kw = w.shape[0]
    zh = jnp.where(first, 0.0, zh)
    acc = b + w[kw - 1:kw] * z
    shifted = []
    for k in range(kw - 1):
        zs = _shift_down(z, zh, kw - 1 - k)
        shifted.append(zs)
        acc = acc + w[k:k + 1] * zs
    return acc, shifted


def _conv_specs(T, F, tm, tc, KW):
    nfb = F // tc
    rb = tm // SUBLANES

    def main(off):
        return pl.BlockSpec((tm, tc), lambda i, c: (i, c + off))

    def halo(off):
        return pl.BlockSpec((SUBLANES, tc), lambda i, c: (jnp.maximum(i * rb - 1, 0), c + off))

    def wspec(off):
        return pl.BlockSpec((None, KW, tc), lambda i, c: (c + off, 0, 0))

    def bspec(off):
        return pl.BlockSpec((1, tc), lambda i, c: (0, c + off))

    return nfb, main, halo, wspec, bspec


def convglu_fwd(z, cw, cb, *, name):
    T, F2 = z.shape
    F = F2 // 2
    _, KW, tc = cw.shape
    tm = _pick(T, (256, 128))
    nfb, main, halo, wspec, bspec = _conv_specs(T, F, tm, tc, KW)

    def body(zg, zgh, zv, zvh, wg, wv, bg, bv, o_ref):
        first = pl.program_id(0) == 0
        g, _ = _conv_acc(zg[...], zgh[...], wg[...], bg[...], first)
        v, _ = _conv_acc(zv[...], zvh[...], wv[...], bv[...], first)
        o_ref[...] = (g * _sigmoid(g) * v).astype(BF16)

    return pl.pallas_call(
        body, grid=(T // tm, nfb),
        in_specs=[main(0), halo(0), main(nfb), halo(nfb), wspec(0), wspec(nfb), bspec(0), bspec(nfb)],
        out_specs=pl.BlockSpec((tm, tc), lambda i, c: (i, c)),
        out_shape=jax.ShapeDtypeStruct((T, F), BF16), name=name,
        compiler_params=_params(("parallel", "parallel"), VMEM_BIG))(z, z, z, z, cw, cw, cb, cb)


def convglu_bwd_acc(z, da, cw, cb, *, name):
    T, F2 = z.shape
    F = F2 // 2
    _, KW, tc = cw.shape
    tm = _pick(T, (256, 128))
    nfb, main, halo, wspec, bspec = _conv_specs(T, F, tm, tc, KW)

    def body(zg, zgh, zv, zvh, wg, wv, bg, bv, da_ref, o_ref, dwg, dwv, dbg, dbv):
        i = pl.program_id(1)
        first = i == 0

        @pl.when(first)
        def _():
            for o in (dwg, dwv, dbg, dbv):
                o[...] = jnp.zeros_like(o)

        zg_v, zv_v = zg[...], zv[...]
        g, sg_ = _conv_acc(zg_v, zgh[...], wg[...], bg[...], first)
        v, sv_ = _conv_acc(zv_v, zvh[...], wv[...], bv[...], first)
        d = da_ref[...]
        sig = _sigmoid(g)
        dg = d * v * sig * (1.0 + g * (1.0 - sig))
        dv = d * g * sig
        o_ref[0] = dg.astype(BF16)
        o_ref[1] = dv.astype(BF16)
        dbg[...] += jnp.sum(dg, axis=0, keepdims=True)
        dbv[...] += jnp.sum(dv, axis=0, keepdims=True)
        for k in range(KW):
            xg = zg_v if k == KW - 1 else sg_[k]
            xv = zv_v if k == KW - 1 else sv_[k]
            dwg[pl.ds(k, 1), :] += jnp.sum(dg * xg, axis=0, keepdims=True)
            dwv[pl.ds(k, 1), :] += jnp.sum(dv * xv, axis=0, keepdims=True)

    def sw(spec_fn, off):
        s = spec_fn(off)
        return pl.BlockSpec(s.block_shape, lambda c, i, f=s.index_map: f(i, c))

    both = jax.ShapeDtypeStruct((2, T, F), BF16)
    dwsh = jax.ShapeDtypeStruct((nfb, KW, tc), F32)
    dbsh = jax.ShapeDtypeStruct((1, F), F32)
    outs = pl.pallas_call(
        body, grid=(nfb, T // tm),
        in_specs=[sw(main, 0), sw(halo, 0), sw(main, nfb), sw(halo, nfb), sw(wspec, 0), sw(wspec, nfb),
                  sw(bspec, 0), sw(bspec, nfb), pl.BlockSpec((tm, tc), lambda c, i: (i, c))],
        out_specs=[pl.BlockSpec((2, tm, tc), lambda c, i: (0, i, c)),
                   pl.BlockSpec((None, KW, tc), lambda c, i: (c, 0, 0)), pl.BlockSpec((None, KW, tc), lambda c, i: (c, 0, 0)),
                   pl.BlockSpec((1, tc), lambda c, i: (0, c)), pl.BlockSpec((1, tc), lambda c, i: (0, c))],
        out_shape=[both, dwsh, dwsh, dbsh, dbsh], name=name,
        compiler_params=_params(("parallel", "arbitrary"), VMEM_BIG))(z, z, z, z, cw, cw, cb, cb, da)
    return outs


def conv_bwd_in(dacc, cw, *, name):
    _, T, F = dacc.shape
    _, KW, tc = cw.shape
    nfb = F // tc
    tm = _pick(T, (256, 128))
    rb = tm // (2 * SUBLANES)
    last_blk = T // (2 * SUBLANES) - 1

    def body(d_ref, dn_ref, w_ref, o_ref):
        last = pl.program_id(0) == pl.num_programs(0) - 1
        d = d_ref[...].astype(F32)
        dn = jnp.where(last, 0.0, dn_ref[...].astype(F32)[:SUBLANES])
        w = w_ref[...]
        out = w[KW - 1:KW] * d
        for k in range(KW - 1):
            out = out + w[k:k + 1] * _shift_up(d, dn, KW - 1 - k)
        o_ref[...] = out.astype(BF16)

    return pl.pallas_call(
        body, grid=(T // tm, 2, nfb),
        in_specs=[pl.BlockSpec((None, tm, tc), lambda i, hf, c: (hf, i, c)),
                  pl.BlockSpec((None, 2 * SUBLANES, tc), lambda i, hf, c: (hf, jnp.minimum((i + 1) * rb, last_blk), c)),
                  pl.BlockSpec((None, KW, tc), lambda i, hf, c: (hf * nfb + c, 0, 0))],
        out_specs=pl.BlockSpec((tm, tc), lambda i, hf, c: (i, hf * nfb + c)),
        out_shape=jax.ShapeDtypeStruct((T, 2 * F), BF16), name=name,
        compiler_params=_params(("parallel", "parallel", "parallel"), VMEM_BIG))(dacc, dacc, cw)


def _my_place():
    x, y, c = (lax.axis_index(a) for a in AXES)
    return x, y, c, 4 * x + 2 * y + c


def _peer(m, x, y, c):
    px = 1 - x if (m >> 2) & 1 else x
    py = 1 - y if (m >> 1) & 1 else y
    pc = 1 - c if m & 1 else c
    return (px, py, pc), 4 * px + 2 * py + pc


_HBM = pl.BlockSpec(memory_space=pltpu.HBM)
_SEM = pl.BlockSpec(memory_space=pltpu.SEMAPHORE)
_EFFECT = pltpu.SideEffectType.DATAFLOW_SIDE_EFFECTING


def _split_copy(in_refs, land_refs, send_sems, recv_sems, bcast, f, m, place, landing):
    x, y, c, me = place
    dev, plin = _peer(m, x, y, c)
    src = in_refs[f] if bcast else in_refs[f].at[plin]
    return pltpu.make_async_remote_copy(
        src_ref=src, dst_ref=land_refs[f].at[plin if landing else me],
        send_sem=send_sems.at[f * (N_DEV - 1) + m - 1], recv_sem=recv_sems.at[f * (N_DEV - 1) + m - 1],
        device_id=dev, device_id_type=MESH)


def _own_copy(in_refs, land_refs, own_sems, bcast, f, place):
    me = place[3]
    return pltpu.make_async_copy(in_refs[f] if bcast else in_refs[f].at[me], land_refs[f].at[me], own_sems.at[f])


PEERS_ALL = tuple(range(1, N_DEV))
PEERS_DIRECT = (1, 2, 4, 6)
SLOTS_PASSED = (2, 4, 6)


def exchange_start(ins, *, bcast, masks=PEERS_ALL, dep=None, name):
    n = len(ins)
    lands = [lax.empty(((N_DEV,) + a.shape) if bcast else a.shape, a.dtype) for a in ins]
    deps = [] if dep is None else [dep]
    nd = len(deps)

    def body(*refs):
        in_refs, land_refs = refs[:n], refs[n:2 * n]
        send_sems, recv_sems, own_sems = refs[2 * n + nd:2 * n + nd + 3]
        token = refs[-1]
        place = _my_place()
        for m in masks:
            for f in range(n):
                _split_copy(in_refs, land_refs, send_sems, recv_sems, bcast, f, m, place, False).start()
        for f in range(n):
            _own_copy(in_refs, land_refs, own_sems, bcast, f, place).start()
        token[...] = jnp.zeros_like(token)

    arrs = [pltpu.with_memory_space_constraint(a, pltpu.HBM) for a in (*ins, *lands)]
    sems = pltpu.SemaphoreType.DMA((n * (N_DEV - 1),))
    outs = pl.pallas_call(
        body, name=name,
        out_shape=(sems, sems, pltpu.SemaphoreType.DMA((n,)), *[pltpu.HBM(a.shape, a.dtype) for a in arrs],
                   jax.ShapeDtypeStruct((SUBLANES, LANES), F32)),
        in_specs=[_HBM] * (2 * n) + [_DEP] * nd,
        out_specs=(_SEM, _SEM, _SEM, *[_HBM] * (2 * n), pl.BlockSpec(memory_space=pltpu.VMEM)),
        input_output_aliases={i: 3 + i for i in range(2 * n)},
        compiler_params=pltpu.CompilerParams(has_side_effects=_EFFECT))(*arrs, *deps)
    return outs[0], outs[1], outs[2], list(outs[3:3 + 2 * n]), outs[-1]


def exchange_wait(started, after, *, bcast, masks=PEERS_ALL, name):
    send_sems, recv_sems, own_sems, thrus, _ = started
    n = len(thrus) // 2

    def body(*refs):
        in_refs, land_refs = refs[:n], refs[n:2 * n]
        send, recv, own = refs[2 * n:2 * n + 3]
        place = _my_place()
        for f in range(n):
            _own_copy(in_refs, land_refs, own, bcast, f, place).wait()
        for m in masks:
            for f in range(n):
                cp = _split_copy(in_refs, land_refs, send, recv, bcast, f, m, place, True)
                cp.wait_send()
                cp.wait_recv()

    outs = pl.pallas_call(
        body, name=name, out_shape=[pltpu.HBM(a.shape, a.dtype) for a in thrus],
        in_specs=[_HBM] * (2 * n) + [_SEM, _SEM, _SEM, pl.BlockSpec(memory_space=pl.ANY)], out_specs=[_HBM] * (2 * n),
        input_output_aliases={i: i for i in range(2 * n)},
        compiler_params=pltpu.CompilerParams(has_side_effects=_EFFECT))(
            *thrus, send_sems, recv_sems, own_sems, after)
    return list(outs[n:])


def _pass_copy(land_refs, send_sems, recv_sems, f, k, place, landing):
    x, y, c, _ = place
    m = SLOTS_PASSED[k]
    sib, _ = _peer(1, x, y, c)
    _, mine = _peer(m, x, y, c)
    _, theirs = _peer(m ^ 1, x, y, c)
    i = f * len(SLOTS_PASSED) + k
    return pltpu.make_async_remote_copy(
        src_ref=land_refs[f].at[mine], dst_ref=land_refs[f].at[theirs if landing else mine],
        send_sem=send_sems.at[i], recv_sem=recv_sems.at[i], device_id=sib, device_id_type=MESH)


def pass_start(lands, *, dep=None, name):
    n = len(lands)
    deps = [] if dep is None else [dep]
    nd = len(deps)

    def body(*refs):
        land_refs = refs[:n]
        send_sems, recv_sems, token = refs[n + nd], refs[n + nd + 1], refs[-1]
        place = _my_place()
        for k in range(len(SLOTS_PASSED)):
            for f in range(n):
                _pass_copy(land_refs, send_sems, recv_sems, f, k, place, False).start()
        token[...] = jnp.zeros_like(token)

    sems = pltpu.SemaphoreType.DMA((n * len(SLOTS_PASSED),))
    outs = pl.pallas_call(
        body, name=name,
        out_shape=(sems, sems, *[pltpu.HBM(a.shape, a.dtype) for a in lands], jax.ShapeDtypeStruct((SUBLANES, LANES), F32)),
        in_specs=[_HBM] * n + [_DEP] * nd,
        out_specs=(_SEM, _SEM, *[_HBM] * n, pl.BlockSpec(memory_space=pltpu.VMEM)),
        input_output_aliases={i: 2 + i for i in range(n)},
        compiler_params=pltpu.CompilerParams(has_side_effects=_EFFECT))(*lands, *deps)
    return outs[0], outs[1], list(outs[2:2 + n]), outs[-1]


def pass_wait(started, after, *, name):
    send_sems, recv_sems, thrus, _ = started
    n = len(thrus)

    def body(*refs):
        land_refs = refs[:n]
        send, recv = refs[n], refs[n + 1]
        place = _my_place()
        for k in range(len(SLOTS_PASSED)):
            for f in range(n):
                cp = _pass_copy(land_refs, send, recv, f, k, place, True)
                cp.wait_send()
                cp.wait_recv()

    outs = pl.pallas_call(
        body, name=name, out_shape=[pltpu.HBM(a.shape, a.dtype) for a in thrus],
        in_specs=[_HBM] * n + [_SEM, _SEM, pl.BlockSpec(memory_space=pl.ANY)], out_specs=[_HBM] * n,
        input_output_aliases={i: i for i in range(n)},
        compiler_params=pltpu.CompilerParams(has_side_effects=_EFFECT))(*thrus, send_sems, recv_sems, after)
    return list(outs)


def _adamw(w, g, m, v):
    m = ADAM_B1 * m + (1.0 - ADAM_B1) * g
    v = ADAM_B2 * v + (1.0 - ADAM_B2) * (g * g)
    m_hat = m / (1.0 - ADAM_B1 ** ADAM_STEP)
    v_hat = v / (1.0 - ADAM_B2 ** ADAM_STEP)
    delta = -ADAM_LR * (m_hat / (jnp.sqrt(v_hat) + ADAM_EPS) + ADAM_WD * w)
    return delta, m, v


def adam_reduce(recv, w, m, v, l, prev, *, name):
    _, R, C = recv.shape
    L = w.shape[0]
    budget = 4 * 1024 * 1024
    tr = R
    for cand in (1024, 512, 352, 256, 176, 128, 64, 32, 16):
        if R % cand == 0 and N_DEV * cand * C * recv.dtype.itemsize <= budget:
            tr = cand
            break

    def body(r_ref, w_ref, m_ref, v_ref, *rest):
        g_ref, d_ref, nm_ref, nv_ref = rest[-4:]
        g = r_ref[0].astype(F32)
        for s in range(1, N_DEV):
            g = g + r_ref[s].astype(F32)
        d, nm, nv = _adamw(w_ref[...], g, m_ref[...], v_ref[...])
        g_ref[...] = g
        d_ref[...] = d
        nm_ref[...] = nm
        nv_ref[...] = nv

    blk = pl.BlockSpec((None, tr, C), lambda r: (l, r, 0))
    sh = jax.ShapeDtypeStruct((L, R, C), F32)
    extra = [] if prev is None else list(prev)
    return pl.pallas_call(
        body, grid=(R // tr,),
        in_specs=[pl.BlockSpec((N_DEV, tr, C), lambda r: (0, r, 0)), blk, blk, blk]
        + [pl.BlockSpec(memory_space=pl.ANY)] * len(extra),
        out_specs=[blk] * 4, out_shape=[sh] * 4, name=name,
        input_output_aliases={4 + i: i for i in range(len(extra))},
        compiler_params=_params(("parallel",), VMEM_MID))(recv, w, m, v, *extra)


def sum_slots(recv, *, name):
    _, R, C = recv.shape
    tr = _pick(R, (512, 256, 128, 64, 32, 16, 8))

    def body(r_ref, o_ref):
        g = r_ref[0]
        for s in range(1, N_DEV):
            g = g + r_ref[s]
        o_ref[...] = g

    return pl.pallas_call(
        body, grid=(R // tr,), in_specs=[pl.BlockSpec((N_DEV, tr, C), lambda r: (0, r, 0))],
        out_specs=pl.BlockSpec((tr, C), lambda r: (r, 0)), out_shape=jax.ShapeDtypeStruct((R, C), F32),
        name=name, compiler_params=_params(("parallel",)))(recv)


def adam_flat(g, w, m, v, *, name):
    R, C = g.shape
    tr = _pick(R, (512, 256, 128, 64, 32, 16, 8))

    def body(g_ref, w_ref, m_ref, v_ref, d_ref, nm_ref, nv_ref):
        d, nm, nv = _adamw(w_ref[...], g_ref[...], m_ref[...], v_ref[...])
        d_ref[...] = d
        nm_ref[...] = nm
        nv_ref[...] = nv

    blk = pl.BlockSpec((tr, C), lambda r: (r, 0))
    sh = jax.ShapeDtypeStruct((R, C), F32)
    return pl.pallas_call(body, grid=(R // tr,), in_specs=[blk] * 4, out_specs=[blk] * 3, out_shape=[sh] * 3,
                          name=name, compiler_params=_params(("parallel",)))(g, w, m, v)


WEIGHTS = ("norm_mix_g", "norm_ffn_g", "a_w_in", "a_g_v", "a_w_s", "a_b_s", "a_w_out", "b_w_in", "b_a_re", "b_a_im",
           "b_log_dt", "b_b_re", "b_b_im", "b_c_re", "b_c_im", "b_d", "b_w_glu", "f_w_up", "f_conv_w", "f_conv_b",
           "f_w_down", "final_g")
BIG = ("a_w_in", "a_w_out", "b_w_in", "b_w_glu", "f_w_up", "f_w_down")
FLAT_ROWS = 512
FLAT_CHUNK = N_DEV * FLAT_ROWS * LANES


def _expand(blocks, eye):
    KB, KG, C, P = blocks.shape
    return (blocks[:, :, :, None, :] * eye[None, :, None, :, None]).reshape(KB, KG * C, KG * P)


def _diag_blocks(dense, KG, C, P, eye):
    KB = dense.shape[0]
    return jnp.einsum("kgchp,gh->kgcp", dense.reshape(KB, KG, C, KG, P), eye)


def _flatten_pack(parts):
    flat = jnp.concatenate([p.reshape(-1) for p in parts])
    pad = (-flat.shape[0]) % FLAT_CHUNK
    return jnp.pad(flat, (0, pad))


def _unpack(flat, like):
    out, o = [], 0
    for p in like:
        n = math.prod(p.shape)
        out.append(flat[o:o + n].reshape(p.shape))
        o += n
    return out


def kernel(x, norm_mix_g, norm_ffn_g, a_w_in, a_g_v, a_w_s, a_b_s, a_w_out, b_w_in, b_a_re, b_a_im, b_log_dt, b_b_re, b_b_im, b_c_re, b_c_im, b_d, b_w_glu, f_w_up, f_conv_w, f_conv_b, f_w_down, final_g, loss_target, m_norm_mix_g, m_norm_ffn_g, m_a_w_in, m_a_g_v, m_a_w_s, m_a_b_s, m_a_w_out, m_b_w_in, m_b_a_re, m_b_a_im, m_b_log_dt, m_b_b_re, m_b_b_im, m_b_c_re, m_b_c_im, m_b_d, m_b_w_glu, m_f_w_up, m_f_conv_w, m_f_conv_b, m_f_w_down, m_final_g, v_norm_mix_g, v_norm_ffn_g, v_a_w_in, v_a_g_v, v_a_w_s, v_a_b_s, v_a_w_out, v_b_w_in, v_b_a_re, v_b_a_im, v_b_log_dt, v_b_b_re, v_b_b_im, v_b_c_re, v_b_c_im, v_b_d, v_b_w_glu, v_f_w_up, v_f_conv_w, v_f_conv_b, v_f_w_down, v_final_g):
    env = dict(locals())
    W = {n: env[n] for n in WEIGHTS}
    Mo = {n: env["m_" + n] for n in WEIGHTS}
    Vo = {n: env["v_" + n] for n in WEIGHTS}

    _, T, D = x.shape
    depth = norm_mix_g.shape[0]
    E_A = a_g_v.shape[1]
    H = a_w_s.shape[1]
    G, P, C = b_b_re.shape[1], b_b_re.shape[2], b_b_re.shape[3]
    E_B = G * C
    KG = S5_KG
    KB = G // KG
    F2 = f_conv_b.shape[1]
    tb = _pick(T, (512, 256, 128))
    pm, pmt = _perm_matrices(tb)
    eye = jnp.eye(KG, dtype=F32)
    _, _, _, me = _my_place()

    def shards(i):
        j = i // 2
        ffn = [f_w_up[i].astype(BF16), f_w_down[i].astype(BF16), f_conv_w[i]]
        if i % 2 == 0:
            return [a_w_in[j].astype(BF16), a_w_out[j].astype(BF16)] + ffn
        return [b_w_in[j].astype(BF16), b_w_glu[j].astype(BF16), b_d[j][None]] + ffn

    h = x[0]
    started = exchange_start(shards(0), bcast=True, masks=PEERS_DIRECT, name="gather_start0")
    saved = []
    for i in range(depth):
        j = i // 2
        s = {}
        lands = exchange_wait(started, h, bcast=True, masks=PEERS_DIRECT, name=f"gather_wait{i}")
        passing = pass_start(lands, name=f"gather_pass_start{i}")
        dep = passing[-1]
        if i + 1 < depth:
            started = exchange_start(shards(i + 1), bcast=True, masks=PEERS_DIRECT, dep=dep,
                                     name=f"gather_start{i + 1}")
            dep = started[-1]
        s["h"] = h
        s["hn"] = rms_fwd(h, norm_mix_g[i][None], dep=dep, name=f"rms_mix{i}")
        gathered = pass_wait(passing, s["hn"], name=f"gather_pass_wait{i}")
        if i % 2 == 0:
            g_in, g_out, g_up, g_dn, g_cw = gathered
            s["w_in"], s["w_out"] = g_in, g_out.reshape(E_A, D)
        else:
            g_in, g_glu, g_dd, g_up, g_dn, g_cw = gathered
            s["w_in"], s["w_glu"] = g_in.reshape(D, E_B), g_glu
            s["dd"] = g_dd.reshape(KB, 1, KG * C)
        s["w_up"], s["w_dn"], s["cw"] = g_up, g_dn.reshape(F2 // 2, D), g_cw
        if i % 2 == 0:
            s["p"] = mm_nn(s["hn"], s["w_in"], name=f"a_in{i}")
            s["bexp"] = jnp.repeat(a_b_s[j].T, E_A // H, axis=1)
            s["us"] = sgu_fwd(s["p"], a_g_v[j][None], a_w_s[j], s["bexp"], name=f"sgu_fwd{i}")
            h_mid = mm_nn(s["us"], s["w_out"], res=h, name=f"a_out{i}")
        else:
            s["uu"] = mm_nn(s["hn"], s["w_in"], name=f"b_in{i}")
            s["prm"] = (b_a_re[j], b_a_im[j], b_log_dt[j][:, None],
                        b_b_re[j].transpose(2, 0, 1), b_b_im[j].transpose(2, 0, 1))
            ar, ai, bbr, bbi = s5_disc_fwd(*s["prm"], name=f"s5_disc{i}")
            to_blocks = lambda t: t.reshape(C, KB, KG, P).transpose(1, 2, 0, 3)
            s["bd_r"] = _expand(to_blocks(bbr), eye).astype(BF16)
            s["bd_i"] = _expand(to_blocks(bbi), eye).astype(BF16)
            s["ct_r"] = _expand(b_c_re[j].reshape(KB, KG, C, P), eye).astype(BF16)
            s["ct_i"] = _expand(b_c_im[j].reshape(KB, KG, C, P), eye).astype(BF16)
            s["ab_r"], s["ab_i"] = ar.reshape(KB, 1, KG * P), ai.reshape(KB, 1, KG * P)
            s["y"], s["q"], s["st_r"], s["st_i"] = s5_fwd(
                s["uu"], pm, pmt, s["bd_r"], s["bd_i"], s["ct_r"], s["ct_i"], s["ab_r"], s["ab_i"], s["dd"],
                tb=tb, name=f"s5_fwd{i}")
            s["pg"] = mm_nn(s["q"], s["w_glu"], name=f"b_glu{i}")
            h_mid = glu_fwd(h, s["pg"], name=f"glu_fwd{i}")
        s["h_mid"] = h_mid
        s["hn2"] = rms_fwd(h_mid, norm_ffn_g[i][None], name=f"rms_ffn{i}")
        s["z"] = mm_nn(s["hn2"], s["w_up"], name=f"f_up{i}")
        s["a"] = convglu_fwd(s["z"], s["cw"], f_conv_b[i][None], name=f"convglu_fwd{i}")
        h = mm_nn(s["a"], s["w_dn"], res=h_mid, name=f"f_down{i}")
        saved.append(s)

    loss_tile, dh, dhb, dg_final = loss_head(h, final_g[None], loss_target[0], name="loss_head")
    loss = lax.psum(loss_tile[0, 0], AXES)

    gbig = {n: [None] * W[n].shape[0] for n in BIG}
    gs = {n: [None] * W[n].shape[0] for n in WEIGHTS if n not in BIG and n != "final_g"}
    pending = []
    for i in reversed(range(depth)):
        j = i // 2
        s = saved[i]
        gbig["f_w_down"][i] = mm_tn(s["a"], dhb, blocks=1, name=f"g_down{i}").reshape(N_DEV, F2 // 2 // N_DEV, D)
        st = exchange_start([gbig["f_w_down"][i]], bcast=False, name=f"xchg_down_start{i}")
        pending.append((st, [("f_w_down", i)], f"down{i}"))
        da = mm_nt(dhb, s["w_dn"], dep=st[-1], name=f"d_a{i}")
        dacc, dwg, dwv, dbg, dbv = convglu_bwd_acc(s["z"], da, s["cw"], f_conv_b[i][None], name=f"convglu_bwd{i}")
        gs["f_conv_w"][i] = jnp.concatenate([dwg, dwv], axis=0)
        gs["f_conv_b"][i] = jnp.concatenate([dbg, dbv], axis=1)[0]
        dz = conv_bwd_in(dacc, s["cw"], name=f"conv_bwd_in{i}")
        gbig["f_w_up"][i] = mm_tn(s["hn2"], dz, blocks=N_DEV, name=f"g_up{i}")
        st = exchange_start([gbig["f_w_up"][i]], bcast=False, name=f"xchg_up_start{i}")
        pending.append((st, [("f_w_up", i)], f"up{i}"))
        dhn2 = mm_nt(dz, s["w_up"], dep=st[-1], name=f"d_hn2{i}")
        dh_mid, dmb, dg = rms_bwd(s["h_mid"], norm_ffn_g[i][None], dhn2, dh, name=f"rms_ffn_bwd{i}")
        gs["norm_ffn_g"][i] = dg[0]
        if i % 2 == 0:
            gbig["a_w_out"][j] = mm_tn(s["us"], dmb, blocks=1, name=f"g_aout{i}").reshape(N_DEV, E_A // N_DEV, D)
            d_us = mm_nt(dmb, s["w_out"], name=f"d_us{i}")
            dp, dws, dbt, dgv = sgu_bwd(s["p"], d_us, a_g_v[j][None], a_w_s[j], s["bexp"], name=f"sgu_bwd{i}")
            gs["a_w_s"][j], gs["a_b_s"][j], gs["a_g_v"][j] = dws, dbt[:, :H].T, dgv[0]
            gbig["a_w_in"][j] = mm_tn(s["hn"], dp, blocks=N_DEV, name=f"g_ain{i}")
            dhn = mm_nt(dp, s["w_in"], name=f"d_hn_a{i}")
        else:
            dpg = glu_bwd(dh_mid, s["pg"], name=f"glu_bwd{i}")
            gbig["b_w_glu"][j] = mm_tn(s["q"], dpg, blocks=N_DEV, name=f"g_glu{i}")
            dq = mm_nt(dpg, s["w_glu"], name=f"d_q{i}")
            duu, dbr, dbi, dcr, dci, dar, dai, ddd = s5_bwd(
                s["uu"], s["y"], dq, pm, pmt, s["st_r"], s["st_i"], s["bd_r"], s["bd_i"], s["ct_r"], s["ct_i"],
                s["ab_r"], s["ab_i"], s["dd"], tb=tb, name=f"s5_bwd{i}")
            from_blocks = lambda t: _diag_blocks(t, KG, C, P, eye).transpose(2, 0, 1, 3).reshape(C, G, P)
            d_are, d_aim, d_ldt, d_bre, d_bim = s5_disc_bwd(
                *s["prm"], dar.reshape(G, P), dai.reshape(G, P), from_blocks(dbr), from_blocks(dbi),
                name=f"s5_disc_bwd{i}")
            gs["b_a_re"][j], gs["b_a_im"][j], gs["b_log_dt"][j] = d_are, d_aim, d_ldt[:, 0]
            gs["b_b_re"][j], gs["b_b_im"][j] = d_bre.transpose(1, 2, 0), d_bim.transpose(1, 2, 0)
            gs["b_c_re"][j] = _diag_blocks(dcr, KG, C, P, eye).reshape(G, C, P)
            gs["b_c_im"][j] = _diag_blocks(dci, KG, C, P, eye).reshape(G, C, P)
            gs["b_d"][j] = ddd.reshape(E_B)
            gbig["b_w_in"][j] = mm_tn(s["hn"], duu, blocks=1, name=f"g_bin{i}").reshape(N_DEV, D // N_DEV, E_B)
            dhn = mm_nt(duu, s["w_in"], name=f"d_hn_b{i}")
        mix = ("a_w_out", "a_w_in") if i % 2 == 0 else ("b_w_glu", "b_w_in")
        st = exchange_start([gbig[n][j] for n in mix], bcast=False, name=f"xchg_mix_start{i}")
        pending.append((st, [(n, j) for n in mix], f"mix{i}"))
        dh, dhb, dg = rms_bwd(s["h"], norm_mix_g[i][None], dhn, dh_mid, dep=st[-1], name=f"rms_mix_bwd{i}")
        gs["norm_mix_g"][i] = dg[0]
    grad_x = dh[None]

    grads, deltas, new_m, new_v = {}, {}, {}, {}
    small = [n for n in WEIGHTS if n not in BIG]
    full = {n: (dg_final[0] if n == "final_g" else jnp.stack(gs[n])) for n in small}
    flat = _flatten_pack([full[n] for n in small])
    rows = flat.shape[0] // N_DEV // LANES
    st_small = exchange_start([flat.reshape(N_DEV, rows, LANES)], bcast=False, name="xchg_small_start")

    done = {n: None for n in BIG}

    def finish(entries, after):
        for st, items, tag in entries:
            recvs = exchange_wait(st, after, bcast=False, name=f"xchg_wait_{tag}")
            for (n, l), recv in zip(items, recvs):
                done[n] = adam_reduce(recv, W[n], Mo[n], Vo[n], l, done[n], name=f"adam_{n}{l}")
                after = done[n][0]
        return after

    after = finish(pending[:-1], st_small[-1])
    recv = exchange_wait(st_small, after, bcast=False, name="xchg_small_wait")[0]
    part = sum_slots(recv, name="sum_small")
    st_tot = exchange_start([part], bcast=True, name="gather_small_start")
    after = finish(pending[-1:], st_tot[-1])
    tot = exchange_wait(st_tot, after, bcast=True, name="gather_small_wait")[0].reshape(-1)
    red = dict(zip(small, _unpack(tot, [full[n] for n in small])))
    red["f_conv_w"] = lax.dynamic_index_in_dim(red["f_conv_w"], me, axis=1, keepdims=False)
    red["b_d"] = lax.dynamic_slice_in_dim(red["b_d"], me * (E_B // N_DEV), E_B // N_DEV, axis=1)
    gflat = _flatten_pack([red[n] for n in small]).reshape(-1, LANES)
    d_f, m_f, v_f = adam_flat(
        gflat, _flatten_pack([W[n] for n in small]).reshape(-1, LANES),
        _flatten_pack([Mo[n] for n in small]).reshape(-1, LANES),
        _flatten_pack([Vo[n] for n in small]).reshape(-1, LANES), name="adam_small")
    like = [W[n] for n in small]
    for n, d_, m_, v_ in zip(small, _unpack(d_f.reshape(-1), like), _unpack(m_f.reshape(-1), like),
                             _unpack(v_f.reshape(-1), like)):
        grads[n], deltas[n], new_m[n], new_v[n] = red[n], d_, m_, v_

    for n in BIG:
        grads[n], deltas[n], new_m[n], new_v[n] = done[n]

    return (loss, grad_x, *[grads[n] for n in WEIGHTS], *[deltas[n] for n in WEIGHTS],
            *[new_m[n] for n in WEIGHTS], *[new_v[n] for n in WEIGHTS])
```

```python
import math

import jax
import jax.numpy as jnp
from jax import lax
from jax.experimental import pallas as pl
from jax.experimental.pallas import tpu as pltpu

F32 = jnp.float32
BF16 = jnp.bfloat16
N_DEV = 8
AXES = ("x", "y", "c")
EPS = 1e-6
LANES = 128
SUBLANES = 8
VMEM_BIG = 56 * 1024 * 1024
VMEM_MID = 48 * 1024 * 1024
ADAM_LR, ADAM_B1, ADAM_B2, ADAM_EPS, ADAM_WD, ADAM_STEP = 0.001, 0.9, 0.999, 1e-08, 0.01, 10
MESH = pl.DeviceIdType.MESH
GELU_C = math.sqrt(2.0 / math.pi)
GELU_K = 0.044715


def _pick(n, prefs):
    for p in prefs:
        if n % p == 0:
            return p
    return n


def _params(sem, vmem=None):
    return pltpu.CompilerParams(dimension_semantics=sem, vmem_limit_bytes=vmem)


def _gelu(x):
    return 0.5 * x * (1.0 + jnp.tanh(GELU_C * (x + GELU_K * x * x * x)))


def _gelu_grad(x):
    x2 = x * x
    th = jnp.tanh(GELU_C * x * (1.0 + GELU_K * x2))
    return 0.5 * (1.0 + th) + 0.5 * x * (1.0 - th * th) * GELU_C * (1.0 + 3.0 * GELU_K * x2)


def _sigmoid(x):
    return 1.0 / (1.0 + jnp.exp(-x))


def _dot_nn(a, b):
    return jnp.dot(a, b, preferred_element_type=F32)


def _dot_nt(a, b):
    return lax.dot_general(a, b, (((1,), (1,)), ((), ())), preferred_element_type=F32)


def _dot_tn(a, b):
    return lax.dot_general(a, b, (((0,), (0,)), ((), ())), preferred_element_type=F32)


M_TILES = (1024, 512, 256, 128)
TN_M_TILES = (2048, 1024, 512, 256, 128)
FULL_K = 2048
NT_SPAN = 2816
N_TILES = (1408, 1024, 512, 384, 256, 128)
K_TILES = (1408, 1024, 512, 384, 256, 128)


def _as3(b):
    return b if b.ndim == 3 else b[None]


def mm_nn(a, b, *, res=None, out_dtype=F32, name):
    b3 = _as3(b)
    M, K = a.shape
    J, _, nb = b3.shape
    tm, tn = _pick(M, M_TILES), _pick(nb, N_TILES)
    tk = K if K <= FULL_K else _pick(K, (2816,) + K_TILES)
    per, nk = nb // tn, K // tk

    def body(*refs):
        if res is None:
            a_ref, b_ref, o_ref, acc = refs
        else:
            a_ref, b_ref, r_ref, o_ref, acc = refs
        k = pl.program_id(2)
        if nk == 1:
            r = _dot_nn(a_ref[...], b_ref[...])
            if res is not None:
                r = r + r_ref[...]
            o_ref[...] = r.astype(out_dtype)
            return

        @pl.when(k == 0)
        def _():
            acc[...] = jnp.zeros_like(acc)

        acc[...] += _dot_nn(a_ref[...], b_ref[...])

        @pl.when(k == nk - 1)
        def _():
            r = acc[...]
            if res is not None:
                r = r + r_ref[...]
            o_ref[...] = r.astype(out_dtype)

    in_specs = [pl.BlockSpec((tm, tk), lambda i, n, k: (i, k)),
                pl.BlockSpec((None, tk, tn), lambda i, n, k: (n // per, k, n % per))]
    args = [a, b3]
    if res is not None:
        in_specs.append(pl.BlockSpec((tm, tn), lambda i, n, k: (i, n)))
        args.append(res)
    return pl.pallas_call(
        body, grid=(M // tm, J * per, nk), in_specs=in_specs,
        out_specs=pl.BlockSpec((tm, tn), lambda i, n, k: (i, n)),
        out_shape=jax.ShapeDtypeStruct((M, J * nb), out_dtype),
        scratch_shapes=[pltpu.VMEM((tm, tn) if nk > 1 else (SUBLANES, LANES), F32)], name=name,
        compiler_params=_params(("parallel", "parallel", "arbitrary"), VMEM_BIG))(*args)


def mm_nt(dy, b, *, out_dtype=F32, dep=None, name):
    b3 = _as3(b)
    deps = [] if dep is None else [dep]
    M, N = dy.shape
    J, K, nb = b3.shape
    tm = _pick(M, M_TILES)
    tn = nb if (J == 1 and nb <= FULL_K) else _pick(nb, N_TILES)
    grp = max([g for g in (4, 2, 1) if J % g == 0 and g * tn <= NT_SPAN]) if tn == nb else 1
    tk = _pick(K, K_TILES) if (grp > 1 or K > FULL_K) else K
    per, nn = nb // tn, (J * nb) // (tn * grp)

    def body(d_ref, b_ref, *rest):
        o_ref, acc = rest[-2:]
        if grp > 1:
            r = sum(_dot_nt(d_ref[:, g * tn:(g + 1) * tn], b_ref[g]) for g in range(grp))
        else:
            r = _dot_nt(d_ref[...], b_ref[...])
        if nn == 1:
            o_ref[...] = r.astype(out_dtype)
            return
        n = pl.program_id(2)

        @pl.when(n == 0)
        def _():
            acc[...] = jnp.zeros_like(acc)

        acc[...] += r

        @pl.when(n == nn - 1)
        def _():
            o_ref[...] = acc[...].astype(out_dtype)

    if grp > 1:
        b_spec = pl.BlockSpec((grp, tk, tn), lambda i, k, n: (n, k, 0))
    else:
        b_spec = pl.BlockSpec((None, tk, tn), lambda i, k, n: (n // per, k, n % per))
    return pl.pallas_call(
        body, grid=(M // tm, K // tk, nn),
        in_specs=[pl.BlockSpec((tm, tn * grp), lambda i, k, n: (i, n)), b_spec]
        + [pl.BlockSpec(memory_space=pl.ANY)] * len(deps),
        out_specs=pl.BlockSpec((tm, tk), lambda i, k, n: (i, k)),
        out_shape=jax.ShapeDtypeStruct((M, K), out_dtype),
        scratch_shapes=[pltpu.VMEM((tm, tk) if nn > 1 else (SUBLANES, LANES), F32)], name=name,
        compiler_params=_params(("parallel", "parallel", "arbitrary"), VMEM_BIG))(dy, b3, *deps)


def mm_tn(x, dy, *, blocks, out_dtype=BF16, name):
    M, K = x.shape
    _, N = dy.shape
    nb = N // blocks
    tm, tn, tk = _pick(M, TN_M_TILES), _pick(nb, N_TILES), _pick(K, K_TILES)
    per, nm = nb // tn, M // tm

    def body(x_ref, d_ref, o_ref, acc):
        m = pl.program_id(2)

        @pl.when(m == 0)
        def _():
            acc[...] = jnp.zeros_like(acc)

        acc[...] += _dot_tn(x_ref[...], d_ref[...])

        @pl.when(m == nm - 1)
        def _():
            o_ref[...] = acc[...].astype(out_dtype)

    return pl.pallas_call(
        body, grid=(K // tk, N // tn, nm),
        in_specs=[pl.BlockSpec((tm, tk), lambda k, n, m: (m, k)),
                  pl.BlockSpec((tm, tn), lambda k, n, m: (m, n))],
        out_specs=pl.BlockSpec((None, tk, tn), lambda k, n, m: (n // per, k, n % per)),
        out_shape=jax.ShapeDtypeStruct((blocks, K, nb), out_dtype),
        scratch_shapes=[pltpu.VMEM((tk, tn), F32)], name=name,
        compiler_params=_params(("parallel", "parallel", "arbitrary"), VMEM_BIG))(x, dy)


ROW_TILES = (512, 256, 128)


_DEP = pl.BlockSpec(memory_space=pl.ANY)


def rms_fwd(h, g, *, dep=None, name):
    T, D = h.shape
    tm = _pick(T, ROW_TILES)
    deps = [] if dep is None else [dep]

    def body(h_ref, g_ref, *rest):
        o_ref = rest[-1]
        x = h_ref[...]
        r = lax.rsqrt(jnp.mean(x * x, axis=-1, keepdims=True) + EPS)
        o_ref[...] = (x * r * g_ref[...]).astype(BF16)

    return pl.pallas_call(
        body, grid=(T // tm,),
        in_specs=[pl.BlockSpec((tm, D), lambda i: (i, 0)), pl.BlockSpec((1, D), lambda i: (0, 0))] + [_DEP] * len(deps),
        out_specs=pl.BlockSpec((tm, D), lambda i: (i, 0)),
        out_shape=jax.ShapeDtypeStruct((T, D), BF16), name=name,
        compiler_params=_params(("parallel",), VMEM_MID))(h, g, *deps)


def rms_bwd(h, g, dhn, dres, *, dep=None, name):
    T, D = h.shape
    tm = _pick(T, ROW_TILES[1:])
    deps = [] if dep is None else [dep]

    def body(h_ref, g_ref, d_ref, r_ref, *rest):
        dh_ref, dhb_ref, dg_ref = rest[-3:]

        @pl.when(pl.program_id(0) == 0)
        def _():
            dg_ref[...] = jnp.zeros_like(dg_ref)

        x = h_ref[...]
        r = lax.rsqrt(jnp.mean(x * x, axis=-1, keepdims=True) + EPS)
        xh = x * r
        dy = d_ref[...]
        dxh = dy * g_ref[...]
        dh = r_ref[...] + r * (dxh - xh * jnp.mean(dxh * xh, axis=-1, keepdims=True))
        dh_ref[...] = dh
        dhb_ref[...] = dh.astype(BF16)
        dg_ref[...] += jnp.sum(dy * xh, axis=0, keepdims=True)

    row = pl.BlockSpec((tm, D), lambda i: (i, 0))
    vec = pl.BlockSpec((1, D), lambda i: (0, 0))
    return pl.pallas_call(
        body, grid=(T // tm,), in_specs=[row, vec, row, row] + [_DEP] * len(deps), out_specs=[row, row, vec],
        out_shape=[jax.ShapeDtypeStruct((T, D), F32), jax.ShapeDtypeStruct((T, D), BF16),
                   jax.ShapeDtypeStruct((1, D), F32)], name=name,
        compiler_params=_params(("arbitrary",), VMEM_MID))(h, g, dhn, dres, *deps)


def loss_head(h, g, tgt, *, name):
    T, D = h.shape
    tm = _pick(T, ROW_TILES)

    def body(h_ref, g_ref, t_ref, l_ref, dh_ref, dhb_ref, dg_ref):
        @pl.when(pl.program_id(0) == 0)
        def _():
            dg_ref[...] = jnp.zeros_like(dg_ref)
            l_ref[...] = jnp.zeros_like(l_ref)

        x = h_ref[...]
        gg = g_ref[...]
        r = lax.rsqrt(jnp.mean(x * x, axis=-1, keepdims=True) + EPS)
        xh = x * r
        e = xh * gg - t_ref[...]
        l_ref[...] += 0.5 * jnp.sum(jnp.mean(e * e, axis=-1, keepdims=True), axis=0, keepdims=True)
        dy = e * (1.0 / D)
        dxh = dy * gg
        dh = r * (dxh - xh * jnp.mean(dxh * xh, axis=-1, keepdims=True))
        dh_ref[...] = dh
        dhb_ref[...] = dh.astype(BF16)
        dg_ref[...] += jnp.sum(dy * xh, axis=0, keepdims=True)

    row = pl.BlockSpec((tm, D), lambda i: (i, 0))
    vec = pl.BlockSpec((1, D), lambda i: (0, 0))
    return pl.pallas_call(
        body, grid=(T // tm,), in_specs=[row, vec, row],
        out_specs=[pl.BlockSpec((SUBLANES, LANES), lambda i: (0, 0)), row, row, vec],
        out_shape=[jax.ShapeDtypeStruct((SUBLANES, LANES), F32), jax.ShapeDtypeStruct((T, D), F32),
                   jax.ShapeDtypeStruct((T, D), BF16), jax.ShapeDtypeStruct((1, D), F32)], name=name,
        compiler_params=_params(("arbitrary",), VMEM_MID))(h, g, tgt)


def _sgu_common(p, gv, w_ref, bexp, E, H, CH):
    Dg = E // H
    z = _gelu(p)
    u, v = z[:, :E], z[:, E:]
    r = lax.rsqrt(jnp.mean(v * v, axis=-1, keepdims=True) + EPS)
    vhat = v * r
    vn = (vhat * gv).astype(BF16)
    row = lax.broadcasted_iota(jnp.int32, (CH, CH), 0)
    col = lax.broadcasted_iota(jnp.int32, (CH, CH), 1)
    causal = row >= col
    ws = [jnp.where(causal, w_ref[hh], 0.0).astype(BF16) for hh in range(H)]
    s = jnp.concatenate([_dot_nn(ws[hh], vn[:, hh * Dg:(hh + 1) * Dg]) for hh in range(H)], axis=1) + bexp
    return u, r, vhat, vn, causal, ws, s


def sgu_fwd(p, g_v, w_s, bexp, *, name):
    T, E2 = p.shape
    E = E2 // 2
    H, CH, _ = w_s.shape

    def body(p_ref, gv_ref, w_ref, b_ref, o_ref):
        u, _, _, _, _, _, s = _sgu_common(p_ref[...], gv_ref[...], w_ref, b_ref[...], E, H, CH)
        o_ref[...] = (u * s).astype(BF16)

    return pl.pallas_call(
        body, grid=(T // CH,),
        in_specs=[pl.BlockSpec((CH, E2), lambda i: (i, 0)), pl.BlockSpec((1, E), lambda i: (0, 0)),
                  pl.BlockSpec((H, CH, CH), lambda i: (0, 0, 0)), pl.BlockSpec((CH, E), lambda i: (0, 0))],
        out_specs=pl.BlockSpec((CH, E), lambda i: (i, 0)),
        out_shape=jax.ShapeDtypeStruct((T, E), BF16), name=name,
        compiler_params=_params(("parallel",), VMEM_BIG))(p, g_v, w_s, bexp)


def sgu_bwd(p, d_us, g_v, w_s, bexp, *, name):
    T, E2 = p.shape
    E = E2 // 2
    H, CH, _ = w_s.shape
    Dg = E // H

    def body(p_ref, d_ref, gv_ref, w_ref, b_ref, dp_ref, dw_ref, db_ref, dg_ref):
        @pl.when(pl.program_id(0) == 0)
        def _():
            dw_ref[...] = jnp.zeros_like(dw_ref)
            db_ref[...] = jnp.zeros_like(db_ref)
            dg_ref[...] = jnp.zeros_like(dg_ref)

        p = p_ref[...]
        gv = gv_ref[...]
        u, r, vhat, vn, causal, ws, s = _sgu_common(p, gv, w_ref, b_ref[...], E, H, CH)
        d = d_ref[...]
        du = d * s
        ds = d * u
        lane = lax.broadcasted_iota(jnp.int32, (CH, LANES), 1)
        dvn_parts = []
        db = jnp.zeros((CH, LANES), F32)
        for hh in range(H):
            ds_h = ds[:, hh * Dg:(hh + 1) * Dg]
            ds_hb = ds_h.astype(BF16)
            dw_ref[hh] += jnp.where(causal, _dot_nt(ds_hb, vn[:, hh * Dg:(hh + 1) * Dg]), 0.0)
            dvn_parts.append(_dot_tn(ws[hh], ds_hb))
            db = db + jnp.where(lane == hh, jnp.sum(ds_h, axis=1, keepdims=True), 0.0)
        db_ref[...] += db
        dvn = jnp.concatenate(dvn_parts, axis=1)
        dg_ref[...] += jnp.sum(dvn * vhat, axis=0, keepdims=True)
        dvh = dvn * gv
        dv = r * (dvh - vhat * jnp.mean(dvh * vhat, axis=-1, keepdims=True))
        dp_ref[...] = (jnp.concatenate([du, dv], axis=1) * _gelu_grad(p)).astype(BF16)

    return pl.pallas_call(
        body, grid=(T // CH,),
        in_specs=[pl.BlockSpec((CH, E2), lambda i: (i, 0)), pl.BlockSpec((CH, E), lambda i: (i, 0)),
                  pl.BlockSpec((1, E), lambda i: (0, 0)), pl.BlockSpec((H, CH, CH), lambda i: (0, 0, 0)),
                  pl.BlockSpec((CH, E), lambda i: (0, 0))],
        out_specs=[pl.BlockSpec((CH, E2), lambda i: (i, 0)), pl.BlockSpec((H, CH, CH), lambda i: (0, 0, 0)),
                   pl.BlockSpec((CH, LANES), lambda i: (0, 0)), pl.BlockSpec((1, E), lambda i: (0, 0))],
        out_shape=[jax.ShapeDtypeStruct((T, E2), BF16), jax.ShapeDtypeStruct((H, CH, CH), F32),
                   jax.ShapeDtypeStruct((CH, LANES), F32), jax.ShapeDtypeStruct((1, E), F32)], name=name,
        compiler_params=_params(("arbitrary",), VMEM_BIG))(p, d_us, g_v, w_s, bexp)


def _s5_disc(a_re, a_im, log_dt, b_re, b_im):
    dt = jnp.exp(log_dt)
    mag = jnp.exp(dt * a_re)
    ar, ai = mag * jnp.cos(dt * a_im), mag * jnp.sin(dt * a_im)
    den = a_re * a_re + a_im * a_im
    qr = ((ar - 1.0) * a_re + ai * a_im) / den
    qi = (ai * a_re - (ar - 1.0) * a_im) / den
    return ar, ai, qr[None] * b_re - qi[None] * b_im, qr[None] * b_im + qi[None] * b_re


def s5_disc_fwd(a_re, a_im, log_dt, b_re, b_im, *, name):
    G, P = a_re.shape
    C = b_re.shape[0]

    def body(ar_ref, ai_ref, dt_ref, br_ref, bi_ref, o_ar, o_ai, o_br, o_bi):
        ar, ai, br, bi = _s5_disc(ar_ref[...], ai_ref[...], dt_ref[...], br_ref[...], bi_ref[...])
        o_ar[...] = ar
        o_ai[...] = ai
        o_br[...] = br
        o_bi[...] = bi

    gp = jax.ShapeDtypeStruct((G, P), F32)
    cgp = jax.ShapeDtypeStruct((C, G, P), F32)
    return pl.pallas_call(body, out_shape=[gp, gp, cgp, cgp], name=name)(a_re, a_im, log_dt, b_re, b_im)


def s5_disc_bwd(a_re, a_im, log_dt, b_re, b_im, d_ar, d_ai, d_br, d_bi, *, name):
    G, P = a_re.shape
    C = b_re.shape[0]

    def body(ar_ref, ai_ref, dt_ref, br_ref, bi_ref, g0, g1, g2, g3, o0, o1, o2, o3, o4):
        prim = (ar_ref[...], ai_ref[...], dt_ref[...], br_ref[...], bi_ref[...])
        _, vjp = jax.vjp(_s5_disc, *prim)
        outs = vjp((g0[...], g1[...], g2[...], g3[...]))
        for o, v in zip((o0, o1, o2, o3, o4), outs):
            o[...] = v

    gp = jax.ShapeDtypeStruct((G, P), F32)
    cgp = jax.ShapeDtypeStruct((C, G, P), F32)
    return pl.pallas_call(
        body, out_shape=[gp, gp, jax.ShapeDtypeStruct((G, 1), F32), cgp, cgp], name=name,
    )(a_re, a_im, log_dt, b_re, b_im, d_ar, d_ai, d_br, d_bi)


S5_KG = 8


def _planes(x):
    return [x[:, c * LANES:(c + 1) * LANES] for c in range(x.shape[1] // LANES)]


def _store_planes(ref, row0, val):
    for c, p in enumerate(_planes(val)):
        ref[c, pl.ds(row0, val.shape[0]), :] = p


def _load_planes(ref, row0, rows):
    return jnp.concatenate([ref[c, pl.ds(row0, rows), :] for c in range(ref.shape[0])], axis=1)


def _build_powers(a_r, a_i, S, pf_r, pf_i, pr_r=None, pr_i=None):
    ar, ai = _planes(a_r), _planes(a_i)
    NP = len(ar)

    def step(i, carry):
        out = []
        for c in range(NP):
            p_r, p_i = carry[2 * c], carry[2 * c + 1]
            pf_r[c, pl.ds(i, 1), :] = p_r
            pf_i[c, pl.ds(i, 1), :] = p_i
            if pr_r is not None:
                pr_r[c, pl.ds(S - 1 - i, 1), :] = p_r
                pr_i[c, pl.ds(S - 1 - i, 1), :] = -p_i
            out += [ar[c] * p_r - ai[c] * p_i, ar[c] * p_i + ai[c] * p_r]
        return tuple(out)

    init = []
    for c in range(NP):
        init += [ar[c], ai[c]]
    lax.fori_loop(0, S, step, tuple(init))


def _scan_seg(hr, hi, hrow0, tb, a_r, a_i, pw_r, pw_i, h0r, h0i, *, reverse):
    NP = hr.shape[0]
    S = tb // SUBLANES
    if reverse:
        a_i = -a_i
    ar, ai = _planes(a_r), _planes(a_i)

    def step(i, carry):
        j = (S - 1 - i) if reverse else i
        slab = pl.ds(pl.multiple_of(hrow0 + j * SUBLANES, SUBLANES), SUBLANES)
        out = []
        for c in range(NP):
            nr = ar[c] * carry[2 * c] - ai[c] * carry[2 * c + 1] + hr[c, slab, :]
            ni = ar[c] * carry[2 * c + 1] + ai[c] * carry[2 * c] + hi[c, slab, :]
            hr[c, slab, :] = nr
            hi[c, slab, :] = ni
            out += [nr, ni]
        return tuple(out)

    z = jnp.zeros((SUBLANES, LANES), F32)
    loc = lax.fori_loop(0, S, step, (z,) * (2 * NP), unroll=2)
    h0r_p, h0i_p = _planes(h0r), _planes(h0i)
    top = 0 if reverse else S - 1
    order = range(SUBLANES - 1, -1, -1) if reverse else range(SUBLANES)
    out_r, out_i, ent_r, ent_i = [], [], [], []
    for c in range(NP):
        s_r, s_i = pw_r[c, pl.ds(top, 1), :], pw_i[c, pl.ds(top, 1), :]
        c_r, c_i = h0r_p[c], h0i_p[c]
        in_r, in_i = [None] * SUBLANES, [None] * SUBLANES
        for seg in order:
            in_r[seg], in_i[seg] = c_r, c_i
            l_r, l_i = loc[2 * c][seg:seg + 1], loc[2 * c + 1][seg:seg + 1]
            c_r, c_i = s_r * c_r - s_i * c_i + l_r, s_r * c_i + s_i * c_r + l_i
        out_r.append(c_r)
        out_i.append(c_i)
        ent_r.append(jnp.concatenate(in_r, axis=0))
        ent_i.append(jnp.concatenate(in_i, axis=0))

    def fix(j, _):
        slab = pl.ds(pl.multiple_of(hrow0 + j * SUBLANES, SUBLANES), SUBLANES)
        for c in range(NP):
            p_r, p_i = pw_r[c, pl.ds(j, 1), :], pw_i[c, pl.ds(j, 1), :]
            hr[c, slab, :] += p_r * ent_r[c] - p_i * ent_i[c]
            hi[c, slab, :] += p_r * ent_i[c] + p_i * ent_r[c]
        return 0

    lax.fori_loop(0, S, fix, 0, unroll=2)
    return ent_r, ent_i, jnp.concatenate(out_r, axis=1), jnp.concatenate(out_i, axis=1)


def _perm_matrices(tb):
    r = jnp.arange(tb)
    pm = (r[None, :] == ((r % SUBLANES) * (tb // SUBLANES) + r // SUBLANES)[:, None]).astype(BF16)
    return pm, pm.T


def _unpermute(pmt, x):
    c = x.shape[1]
    hi = x.astype(BF16)
    lo = (x - hi.astype(F32)).astype(BF16)
    both = _dot_nn(pmt, jnp.concatenate([hi, lo], axis=1))
    return both[:, :c] + both[:, c:]


def _s5_specs(tb, UC, SC, rev_nb=None):
    tmap = (lambda b: b) if rev_nb is None else (lambda b: rev_nb - 1 - b)
    row = pl.BlockSpec((tb, UC), lambda b, k: (tmap(b), k))
    wsp = pl.BlockSpec((None, UC, SC), lambda b, k: (k, 0, 0))
    vsc = pl.BlockSpec((None, 1, SC), lambda b, k: (k, 0, 0))
    vuc = pl.BlockSpec((None, 1, UC), lambda b, k: (k, 0, 0))
    st = pl.BlockSpec((None, None, 1, SC), lambda b, k: (tmap(b), k, 0, 0))
    return row, wsp, vsc, vuc, st


def s5_fwd(uu, pm, pmt, bd_r, bd_i, ct_r, ct_i, ab_r, ab_i, dd, *, tb, name):
    T, E = uu.shape
    KB, UC, SC = bd_r.shape
    NB = T // tb
    NP, S = SC // LANES, tb // SUBLANES

    def body(u_ref, pm_ref, pmt_ref, bdr, bdi, ctr, cti, ar_ref, ai_ref, dd_ref, y_ref, q_ref, sr_ref, si_ref,
             hr, hi, cr, ci, pf_r, pf_i):
        b, k = pl.program_id(0), pl.program_id(1)
        a_r, a_i = ar_ref[...], ai_ref[...]

        @pl.when(b == 0)
        def _():
            cr[k] = jnp.zeros((1, SC), F32)
            ci[k] = jnp.zeros((1, SC), F32)
            _build_powers(a_r, a_i, S, pf_r.at[k], pf_i.at[k])

        h0r, h0i = cr[k], ci[k]
        sr_ref[...] = h0r
        si_ref[...] = h0i
        u = u_ref[...]
        up = _dot_nn(pm_ref[...], u.astype(BF16)).astype(BF16)
        _store_planes(hr, 0, _dot_nn(up, bdr[...]))
        _store_planes(hi, 0, _dot_nn(up, bdi[...]))
        _, _, o_r, o_i = _scan_seg(hr, hi, 0, tb, a_r, a_i, pf_r.at[k], pf_i.at[k], h0r, h0i, reverse=False)
        cr[k] = o_r
        ci[k] = o_i
        ys = (_dot_nt(_load_planes(hr, 0, tb).astype(BF16), ctr[...])
              - _dot_nt(_load_planes(hi, 0, tb).astype(BF16), cti[...]))
        pmt_v = pmt_ref[...]
        y = _unpermute(pmt_v, ys) + dd_ref[...] * u
        y_ref[...] = y
        q_ref[...] = _gelu(y).astype(BF16)

    row, wsp, vsc, vuc, st = _s5_specs(tb, UC, SC)
    psp = pl.BlockSpec((tb, tb), lambda b, k: (0, 0))
    stsh = jax.ShapeDtypeStruct((NB, KB, 1, SC), F32)
    pw = pltpu.VMEM((KB, NP, S, LANES), F32)
    pln = pltpu.VMEM((NP, tb, LANES), F32)
    return pl.pallas_call(
        body, grid=(NB, KB), in_specs=[row, psp, psp, wsp, wsp, wsp, wsp, vsc, vsc, vuc],
        out_specs=[row, row, st, st],
        out_shape=[jax.ShapeDtypeStruct((T, E), F32), jax.ShapeDtypeStruct((T, E), BF16), stsh, stsh],
        scratch_shapes=[pln, pln, pltpu.VMEM((KB, 1, SC), F32), pltpu.VMEM((KB, 1, SC), F32), pw, pw],
        name=name, compiler_params=_params(("arbitrary", "arbitrary"), VMEM_MID),
    )(uu, pm, pmt, bd_r, bd_i, ct_r, ct_i, ab_r, ab_i, dd)


def s5_bwd(uu, y, dq, pm, pmt, st_r, st_i, bd_r, bd_i, ct_r, ct_i, ab_r, ab_i, dd, *, tb, name):
    T, E = uu.shape
    KB, UC, SC = bd_r.shape
    NB = T // tb
    HDR = SUBLANES
    NP, S = SC // LANES, tb // SUBLANES
    pw = pltpu.VMEM((KB, NP, S, LANES), F32)
    pln = pltpu.VMEM((NP, tb, LANES), F32)

    def body(u_ref, y_ref, dq_ref, pm_ref, pmt_ref, sr_ref, si_ref, bdr, bdi, ctr, cti, ar_ref, ai_ref, dd_ref,
             du_ref, obr, obi, ocr, oci, odar, odai, oddd,
             hr, hi, gr, gi, kr, ki, abr, abi, acr, aci, pf_r, pf_i, pr_r, pr_i):
        b, k = pl.program_id(0), pl.program_id(1)
        a_r, a_i = ar_ref[...], ai_ref[...]

        @pl.when(b == 0)
        def _():
            _build_powers(a_r, a_i, S, pf_r.at[k], pf_i.at[k], pr_r.at[k], pr_i.at[k])
            z1 = jnp.zeros((1, SC), F32)
            kr[k] = z1
            ki[k] = z1
            odar[k] = z1
            odai[k] = z1
            oddd[k] = jnp.zeros((1, UC), F32)
            zw = jnp.zeros((UC, SC), F32)
            abr[k] = zw
            abi[k] = zw
            acr[k] = zw
            aci[k] = zw

        u = u_ref[...]
        dy = dq_ref[...] * _gelu_grad(y_ref[...])
        oddd[k] += jnp.sum(dy * u, axis=0, keepdims=True)
        pm_v = pm_ref[...]
        both = _dot_nn(pm_v, jnp.concatenate([u.astype(BF16), dy.astype(BF16)], axis=1))
        ub = both[:, :UC].astype(BF16)
        dyb = both[:, UC:].astype(BF16)
        s0r, s0i = sr_ref[...], si_ref[...]
        _store_planes(hr, HDR, _dot_nn(ub, bdr[...]))
        _store_planes(hi, HDR, _dot_nn(ub, bdi[...]))
        e_r, e_i, _, _ = _scan_seg(hr, hi, HDR, tb, a_r, a_i, pf_r.at[k], pf_i.at[k], s0r, s0i, reverse=False)
        for c in range(NP):
            hr[c, pl.ds(0, HDR), :] = e_r[c]
            hi[c, pl.ds(0, HDR), :] = e_i[c]
        _store_planes(gr, 0, _dot_nn(dyb, ctr[...]))
        _store_planes(gi, 0, -_dot_nn(dyb, cti[...]))
        _, _, g0r, g0i = _scan_seg(gr, gi, 0, tb, a_r, a_i, pr_r.at[k], pr_i.at[k], kr[k], ki[k], reverse=True)
        kr[k] = g0r
        ki[k] = g0i

        def slab(j, acc):
            o = pl.multiple_of(j * SUBLANES, SUBLANES)
            out = []
            for c in range(NP):
                p_r, p_i = hr[c, pl.ds(o, SUBLANES), :], hi[c, pl.ds(o, SUBLANES), :]
                g_r, g_i = gr[c, pl.ds(o, SUBLANES), :], gi[c, pl.ds(o, SUBLANES), :]
                out += [acc[2 * c] + g_r * p_r + g_i * p_i, acc[2 * c + 1] + g_i * p_r - g_r * p_i]
            return tuple(out)

        z8 = jnp.zeros((SUBLANES, LANES), F32)
        acc = lax.fori_loop(0, S, slab, (z8,) * (2 * NP), unroll=2)
        odar[k] += jnp.concatenate([jnp.sum(acc[2 * c], axis=0, keepdims=True) for c in range(NP)], axis=1)
        odai[k] += jnp.concatenate([jnp.sum(acc[2 * c + 1], axis=0, keepdims=True) for c in range(NP)], axis=1)
        g_rb = _load_planes(gr, 0, tb).astype(BF16)
        g_ib = _load_planes(gi, 0, tb).astype(BF16)
        h_rb = _load_planes(hr, HDR, tb).astype(BF16)
        h_ib = _load_planes(hi, HDR, tb).astype(BF16)
        dus = _dot_nt(g_rb, bdr[...]) + _dot_nt(g_ib, bdi[...])
        pmt_v = pmt_ref[...]
        du = _unpermute(pmt_v, dus) + dd_ref[...] * dy
        du_ref[...] = du.astype(BF16)
        abr[k] += _dot_tn(ub, g_rb)
        abi[k] += _dot_tn(ub, g_ib)
        acr[k] += _dot_tn(dyb, h_rb)
        aci[k] -= _dot_tn(dyb, h_ib)

        @pl.when(jnp.logical_and(b == NB - 1, k == KB - 1))
        def _():
            pltpu.sync_copy(abr, obr)
            pltpu.sync_copy(abi, obi)
            pltpu.sync_copy(acr, ocr)
            pltpu.sync_copy(aci, oci)

    row, wsp, vsc, vuc, st = _s5_specs(tb, UC, SC, rev_nb=NB)
    psp = pl.BlockSpec((tb, tb), lambda b, k: (0, 0))
    hbm = pl.BlockSpec(memory_space=pltpu.HBM)
    full_sc = pl.BlockSpec((KB, 1, SC), lambda b, k: (0, 0, 0))
    full_uc = pl.BlockSpec((KB, 1, UC), lambda b, k: (0, 0, 0))
    wsh = jax.ShapeDtypeStruct((KB, UC, SC), F32)
    acc = pltpu.VMEM((KB, UC, SC), F32)
    return pl.pallas_call(
        body, grid=(NB, KB),
        in_specs=[row, row, row, psp, psp, st, st, wsp, wsp, wsp, wsp, vsc, vsc, vuc],
        out_specs=[row, hbm, hbm, hbm, hbm, full_sc, full_sc, full_uc],
        out_shape=[jax.ShapeDtypeStruct((T, E), BF16), wsh, wsh, wsh, wsh,
                   jax.ShapeDtypeStruct((KB, 1, SC), F32), jax.ShapeDtypeStruct((KB, 1, SC), F32),
                   jax.ShapeDtypeStruct((KB, 1, UC), F32)],
        scratch_shapes=[pltpu.VMEM((NP, tb + HDR, LANES), F32), pltpu.VMEM((NP, tb + HDR, LANES), F32), pln, pln,
                        pltpu.VMEM((KB, 1, SC), F32), pltpu.VMEM((KB, 1, SC), F32), acc, acc, acc, acc,
                        pw, pw, pw, pw],
        name=name, compiler_params=_params(("arbitrary", "arbitrary"), VMEM_BIG),
    )(uu, y, dq, pm, pmt, st_r, st_i, bd_r, bd_i, ct_r, ct_i, ab_r, ab_i, dd)


def glu_fwd(h, pg, *, name):
    T, D = h.shape
    tm = _pick(T, ROW_TILES)

    def body(h_ref, a_ref, b_ref, o_ref):
        o_ref[...] = h_ref[...] + a_ref[...] * _sigmoid(b_ref[...])

    row = pl.BlockSpec((tm, D), lambda i: (i, 0))
    return pl.pallas_call(
        body, grid=(T // tm,), in_specs=[row, row, pl.BlockSpec((tm, D), lambda i: (i, 1))], out_specs=row,
        out_shape=jax.ShapeDtypeStruct((T, D), F32), name=name,
        compiler_params=_params(("parallel",), VMEM_MID))(h, pg, pg)


def glu_bwd(d, pg, *, name):
    T, D = d.shape
    tm = _pick(T, ROW_TILES)

    def body(d_ref, a_ref, b_ref, o_ref):
        dv = d_ref[...]
        sg = _sigmoid(b_ref[...])
        da = dv * sg
        db = dv * a_ref[...] * sg * (1.0 - sg)
        o_ref[...] = jnp.where(pl.program_id(1) == 0, da, db).astype(BF16)

    row = pl.BlockSpec((tm, D), lambda i, hf: (i, 0))
    return pl.pallas_call(
        body, grid=(T // tm, 2), in_specs=[row, row, pl.BlockSpec((tm, D), lambda i, hf: (i, 1))],
        out_specs=pl.BlockSpec((tm, D), lambda i, hf: (i, hf)),
        out_shape=jax.ShapeDtypeStruct((T, 2 * D), BF16), name=name,
        compiler_params=_params(("parallel", "arbitrary"), VMEM_MID))(d, pg, pg)


def _shift_down(x, halo, s):
    r = pltpu.roll(x, s, axis=0)
    hr = pltpu.roll(halo, s, axis=0)
    row = lax.broadcasted_iota(jnp.int32, halo.shape, 0)
    head = jnp.where(row < s, hr, r[:SUBLANES])
    return jnp.concatenate([head, r[SUBLANES:]], axis=0)


def _shift_up(x, halo, s):
    n = x.shape[0]
    r = pltpu.roll(x, n - s, axis=0)
    hr = pltpu.roll(halo, SUBLANES - s, axis=0)
    row = lax.broadcasted_iota(jnp.int32, halo.shape, 0)
    tail = jnp.where(row >= SUBLANES - s, hr, r[n - SUBLANES:])
    return jnp.concatenate([r[:n - SUBLANES], tail], axis=0)


def _conv_acc(z, zh, w, b, first):
    kw = w.shape[0]
    zh = jnp.where(first, 0.0, zh)
    acc = b + w[kw - 1:kw] * z
    shifted = []
    for k in range(kw - 1):
        zs = _shift_down(z, zh, kw - 1 - k)
        shifted.append(zs)
        acc = acc + w[k:k + 1] * zs
    return acc, shifted


def _conv_specs(T, F, tm, tc, KW):
    nfb = F // tc
    rb = tm // SUBLANES

    def main(off):
        return pl.BlockSpec((tm, tc), lambda i, c: (i, c + off))

    def halo(off):
        return pl.BlockSpec((SUBLANES, tc), lambda i, c: (jnp.maximum(i * rb - 1, 0), c + off))

    def wspec(off):
        return pl.BlockSpec((None, KW, tc), lambda i, c: (c + off, 0, 0))

    def bspec(off):
        return pl.BlockSpec((1, tc), lambda i, c: (0, c + off))

    return nfb, main, halo, wspec, bspec


def convglu_fwd(z, cw, cb, *, name):
    T, F2 = z.shape
    F = F2 // 2
    _, KW, tc = cw.shape
    tm = _pick(T, (256, 128))
    nfb, main, halo, wspec, bspec = _conv_specs(T, F, tm, tc, KW)

    def body(zg, zgh, zv, zvh, wg, wv, bg, bv, o_ref):
        first = pl.program_id(0) == 0
        g, _ = _conv_acc(zg[...], zgh[...], wg[...], bg[...], first)
        v, _ = _conv_acc(zv[...], zvh[...], wv[...], bv[...], first)
        o_ref[...] = (g * _sigmoid(g) * v).astype(BF16)

    return pl.pallas_call(
        body, grid=(T // tm, nfb),
        in_specs=[main(0), halo(0), main(nfb), halo(nfb), wspec(0), wspec(nfb), bspec(0), bspec(nfb)],
        out_specs=pl.BlockSpec((tm, tc), lambda i, c: (i, c)),
        out_shape=jax.ShapeDtypeStruct((T, F), BF16), name=name,
        compiler_params=_params(("parallel", "parallel"), VMEM_BIG))(z, z, z, z, cw, cw, cb, cb)


def convglu_bwd_acc(z, da, cw, cb, *, name):
    T, F2 = z.shape
    F = F2 // 2
    _, KW, tc = cw.shape
    tm = _pick(T, (256, 128))
    nfb, main, halo, wspec, bspec = _conv_specs(T, F, tm, tc, KW)

    def body(zg, zgh, zv, zvh, wg, wv, bg, bv, da_ref, o_ref, dwg, dwv, dbg, dbv):
        i = pl.program_id(1)
        first = i == 0

        @pl.when(first)
        def _():
            for o in (dwg, dwv, dbg, dbv):
                o[...] = jnp.zeros_like(o)

        zg_v, zv_v = zg[...], zv[...]
        g, sg_ = _conv_acc(zg_v, zgh[...], wg[...], bg[...], first)
        v, sv_ = _conv_acc(zv_v, zvh[...], wv[...], bv[...], first)
        d = da_ref[...]
        sig = _sigmoid(g)
        dg = d * v * sig * (1.0 + g * (1.0 - sig))
        dv = d * g * sig
        o_ref[0] = dg.astype(BF16)
        o_ref[1] = dv.astype(BF16)
        dbg[...] += jnp.sum(dg, axis=0, keepdims=True)
        dbv[...] += jnp.sum(dv, axis=0, keepdims=True)
        for k in range(KW):
            xg = zg_v if k == KW - 1 else sg_[k]
            xv = zv_v if k == KW - 1 else sv_[k]
            dwg[pl.ds(k, 1), :] += jnp.sum(dg * xg, axis=0, keepdims=True)
            dwv[pl.ds(k, 1), :] += jnp.sum(dv * xv, axis=0, keepdims=True)

    def sw(spec_fn, off):
        s = spec_fn(off)
        return pl.BlockSpec(s.block_shape, lambda c, i, f=s.index_map: f(i, c))

    both = jax.ShapeDtypeStruct((2, T, F), BF16)
    dwsh = jax.ShapeDtypeStruct((nfb, KW, tc), F32)
    dbsh = jax.ShapeDtypeStruct((1, F), F32)
    outs = pl.pallas_call(
        body, grid=(nfb, T // tm),
        in_specs=[sw(main, 0), sw(halo, 0), sw(main, nfb), sw(halo, nfb), sw(wspec, 0), sw(wspec, nfb),
                  sw(bspec, 0), sw(bspec, nfb), pl.BlockSpec((tm, tc), lambda c, i: (i, c))],
        out_specs=[pl.BlockSpec((2, tm, tc), lambda c, i: (0, i, c)),
                   pl.BlockSpec((None, KW, tc), lambda c, i: (c, 0, 0)), pl.BlockSpec((None, KW, tc), lambda c, i: (c, 0, 0)),
                   pl.BlockSpec((1, tc), lambda c, i: (0, c)), pl.BlockSpec((1, tc), lambda c, i: (0, c))],
        out_shape=[both, dwsh, dwsh, dbsh, dbsh], name=name,
        compiler_params=_params(("parallel", "arbitrary"), VMEM_BIG))(z, z, z, z, cw, cw, cb, cb, da)
    return outs


def conv_bwd_in(dacc, cw, *, name):
    _, T, F = dacc.shape
    _, KW, tc = cw.shape
    nfb = F // tc
    tm = _pick(T, (256, 128))
    rb = tm // (2 * SUBLANES)
    last_blk = T // (2 * SUBLANES) - 1

    def body(d_ref, dn_ref, w_ref, o_ref):
        last = pl.program_id(0) == pl.num_programs(0) - 1
        d = d_ref[...].astype(F32)
        dn = jnp.where(last, 0.0, dn_ref[...].astype(F32)[:SUBLANES])
        w = w_ref[...]
        out = w[KW - 1:KW] * d
        for k in range(KW - 1):
            out = out + w[k:k + 1] * _shift_up(d, dn, KW - 1 - k)
        o_ref[...] = out.astype(BF16)

    return pl.pallas_call(
        body, grid=(T // tm, 2, nfb),
        in_specs=[pl.BlockSpec((None, tm, tc), lambda i, hf, c: (hf, i, c)),
                  pl.BlockSpec((None, 2 * SUBLANES, tc), lambda i, hf, c: (hf, jnp.minimum((i + 1) * rb, last_blk), c)),
                  pl.BlockSpec((None, KW, tc), lambda i, hf, c: (hf * nfb + c, 0, 0))],
        out_specs=pl.BlockSpec((tm, tc), lambda i, hf, c: (i, hf * nfb + c)),
        out_shape=jax.ShapeDtypeStruct((T, 2 * F), BF16), name=name,
        compiler_params=_params(("parallel", "parallel", "parallel"), VMEM_BIG))(dacc, dacc, cw)


def _my_place():
    x, y, c = (lax.axis_index(a) for a in AXES)
    return x, y, c, 4 * x + 2 * y + c


def _peer(m, x, y, c):
    px = 1 - x if (m >> 2) & 1 else x
    py = 1 - y if (m >> 1) & 1 else y
    pc = 1 - c if m & 1 else c
    return (px, py, pc), 4 * px + 2 * py + pc


_HBM = pl.BlockSpec(memory_space=pltpu.HBM)
_SEM = pl.BlockSpec(memory_space=pltpu.SEMAPHORE)
_EFFECT = pltpu.SideEffectType.DATAFLOW_SIDE_EFFECTING


def _split_copy(in_refs, land_refs, send_sems, recv_sems, bcast, f, m, place, landing):
    x, y, c, me = place
    dev, plin = _peer(m, x, y, c)
    src = in_refs[f] if bcast else in_refs[f].at[plin]
    return pltpu.make_async_remote_copy(
        src_ref=src, dst_ref=land_refs[f].at[plin if landing else me],
        send_sem=send_sems.at[f * (N_DEV - 1) + m - 1], recv_sem=recv_sems.at[f * (N_DEV - 1) + m - 1],
        device_id=dev, device_id_type=MESH)


def _own_copy(in_refs, land_refs, own_sems, bcast, f, place):
    me = place[3]
    return pltpu.make_async_copy(in_refs[f] if bcast else in_refs[f].at[me], land_refs[f].at[me], own_sems.at[f])


PEERS_ALL = tuple(range(1, N_DEV))
PEERS_DIRECT = (1, 2, 4, 6)
SLOTS_PASSED = (2, 4, 6)


def exchange_start(ins, *, bcast, masks=PEERS_ALL, dep=None, name):
    n = len(ins)
    lands = [lax.empty(((N_DEV,) + a.shape) if bcast else a.shape, a.dtype) for a in ins]
    deps = [] if dep is None else [dep]
    nd = len(deps)

    def body(*refs):
        in_refs, land_refs = refs[:n], refs[n:2 * n]
        send_sems, recv_sems, own_sems = refs[2 * n + nd:2 * n + nd + 3]
        token = refs[-1]
        place = _my_place()
        for m in masks:
            for f in range(n):
                _split_copy(in_refs, land_refs, send_sems, recv_sems, bcast, f, m, place, False).start()
        for f in range(n):
            _own_copy(in_refs, land_refs, own_sems, bcast, f, place).start()
        token[...] = jnp.zeros_like(token)

    arrs = [pltpu.with_memory_space_constraint(a, pltpu.HBM) for a in (*ins, *lands)]
    sems = pltpu.SemaphoreType.DMA((n * (N_DEV - 1),))
    outs = pl.pallas_call(
        body, name=name,
        out_shape=(sems, sems, pltpu.SemaphoreType.DMA((n,)), *[pltpu.HBM(a.shape, a.dtype) for a in arrs],
                   jax.ShapeDtypeStruct((SUBLANES, LANES), F32)),
        in_specs=[_HBM] * (2 * n) + [_DEP] * nd,
        out_specs=(_SEM, _SEM, _SEM, *[_HBM] * (2 * n), pl.BlockSpec(memory_space=pltpu.VMEM)),
        input_output_aliases={i: 3 + i for i in range(2 * n)},
        compiler_params=pltpu.CompilerParams(has_side_effects=_EFFECT))(*arrs, *deps)
    return outs[0], outs[1], outs[2], list(outs[3:3 + 2 * n]), outs[-1]


def exchange_wait(started, after, *, bcast, masks=PEERS_ALL, name):
    send_sems, recv_sems, own_sems, thrus, _ = started
    n = len(thrus) // 2

    def body(*refs):
        in_refs, land_refs = refs[:n], refs[n:2 * n]
        send, recv, own = refs[2 * n:2 * n + 3]
        place = _my_place()
        for f in range(n):
            _own_copy(in_refs, land_refs, own, bcast, f, place).wait()
        for m in masks:
            for f in range(n):
                cp = _split_copy(in_refs, land_refs, send, recv, bcast, f, m, place, True)
                cp.wait_send()
                cp.wait_recv()

    outs = pl.pallas_call(
        body, name=name, out_shape=[pltpu.HBM(a.shape, a.dtype) for a in thrus],
        in_specs=[_HBM] * (2 * n) + [_SEM, _SEM, _SEM, pl.BlockSpec(memory_space=pl.ANY)], out_specs=[_HBM] * (2 * n),
        input_output_aliases={i: i for i in range(2 * n)},
        compiler_params=pltpu.CompilerParams(has_side_effects=_EFFECT))(
            *thrus, send_sems, recv_sems, own_sems, after)
    return list(outs[n:])


def _pass_copy(land_refs, send_sems, recv_sems, f, k, place, landing):
    x, y, c, _ = place
    m = SLOTS_PASSED[k]
    sib, _ = _peer(1, x, y, c)
    _, mine = _peer(m, x, y, c)
    _, theirs = _peer(m ^ 1, x, y, c)
    i = f * len(SLOTS_PASSED) + k
    return pltpu.make_async_remote_copy(
        src_ref=land_refs[f].at[mine], dst_ref=land_refs[f].at[theirs if landing else mine],
        send_sem=send_sems.at[i], recv_sem=recv_sems.at[i], device_id=sib, device_id_type=MESH)


def pass_start(lands, *, dep=None, name):
    n = len(lands)
    deps = [] if dep is None else [dep]
    nd = len(deps)

    def body(*refs):
        land_refs = refs[:n]
        send_sems, recv_sems, token = refs[n + nd], refs[n + nd + 1], refs[-1]
        place = _my_place()
        for k in range(len(SLOTS_PASSED)):
            for f in range(n):
                _pass_copy(land_refs, send_sems, recv_sems, f, k, place, False).start()
        token[...] = jnp.zeros_like(token)

    sems = pltpu.SemaphoreType.DMA((n * len(SLOTS_PASSED),))
    outs = pl.pallas_call(
        body, name=name,
        out_shape=(sems, sems, *[pltpu.HBM(a.shape, a.dtype) for a in lands], jax.ShapeDtypeStruct((SUBLANES, LANES), F32)),
        in_specs=[_HBM] * n + [_DEP] * nd,
        out_specs=(_SEM, _SEM, *[_HBM] * n, pl.BlockSpec(memory_space=pltpu.VMEM)),
        input_output_aliases={i: 2 + i for i in range(n)},
        compiler_params=pltpu.CompilerParams(has_side_effects=_EFFECT))(*lands, *deps)
    return outs[0], outs[1], list(outs[2:2 + n]), outs[-1]


def pass_wait(started, after, *, name):
    send_sems, recv_sems, thrus, _ = started
    n = len(thrus)

    def body(*refs):
        land_refs = refs[:n]
        send, recv = refs[n], refs[n + 1]
        place = _my_place()
        for k in range(len(SLOTS_PASSED)):
            for f in range(n):
                cp = _pass_copy(land_refs, send, recv, f, k, place, True)
                cp.wait_send()
                cp.wait_recv()

    outs = pl.pallas_call(
        body, name=name, out_shape=[pltpu.HBM(a.shape, a.dtype) for a in thrus],
        in_specs=[_HBM] * n + [_SEM, _SEM, pl.BlockSpec(memory_space=pl.ANY)], out_specs=[_HBM] * n,
        input_output_aliases={i: i for i in range(n)},
        compiler_params=pltpu.CompilerParams(has_side_effects=_EFFECT))(*thrus, send_sems, recv_sems, after)
    return list(outs)


def _adamw(w, g, m, v):
    m = ADAM_B1 * m + (1.0 - ADAM_B1) * g
    v = ADAM_B2 * v + (1.0 - ADAM_B2) * (g * g)
    m_hat = m / (1.0 - ADAM_B1 ** ADAM_STEP)
    v_hat = v / (1.0 - ADAM_B2 ** ADAM_STEP)
    delta = -ADAM_LR * (m_hat / (jnp.sqrt(v_hat) + ADAM_EPS) + ADAM_WD * w)
    return delta, m, v


def adam_reduce(recv, w, m, v, l, prev, *, name):
    _, R, C = recv.shape
    L = w.shape[0]
    budget = 4 * 1024 * 1024
    tr = R
    for cand in (1024, 512, 352, 256, 176, 128, 64, 32, 16):
        if R % cand == 0 and N_DEV * cand * C * recv.dtype.itemsize <= budget:
            tr = cand
            break

    def body(r_ref, w_ref, m_ref, v_ref, *rest):
        g_ref, d_ref, nm_ref, nv_ref = rest[-4:]
        g = r_ref[0].astype(F32)
        for s in range(1, N_DEV):
            g = g + r_ref[s].astype(F32)
        d, nm, nv = _adamw(w_ref[...], g, m_ref[...], v_ref[...])
        g_ref[...] = g
        d_ref[...] = d
        nm_ref[...] = nm
        nv_ref[...] = nv

    blk = pl.BlockSpec((None, tr, C), lambda r: (l, r, 0))
    sh = jax.ShapeDtypeStruct((L, R, C), F32)
    extra = [] if prev is None else list(prev)
    return pl.pallas_call(
        body, grid=(R // tr,),
        in_specs=[pl.BlockSpec((N_DEV, tr, C), lambda r: (0, r, 0)), blk, blk, blk]
        + [pl.BlockSpec(memory_space=pl.ANY)] * len(extra),
        out_specs=[blk] * 4, out_shape=[sh] * 4, name=name,
        input_output_aliases={4 + i: i for i in range(len(extra))},
        compiler_params=_params(("parallel",), VMEM_MID))(recv, w, m, v, *extra)


def sum_slots(recv, *, name):
    _, R, C = recv.shape
    tr = _pick(R, (512, 256, 128, 64, 32, 16, 8))

    def body(r_ref, o_ref):
        g = r_ref[0]
        for s in range(1, N_DEV):
            g = g + r_ref[s]
        o_ref[...] = g

    return pl.pallas_call(
        body, grid=(R // tr,), in_specs=[pl.BlockSpec((N_DEV, tr, C), lambda r: (0, r, 0))],
        out_specs=pl.BlockSpec((tr, C), lambda r: (r, 0)), out_shape=jax.ShapeDtypeStruct((R, C), F32),
        name=name, compiler_params=_params(("parallel",)))(recv)


def adam_flat(g, w, m, v, *, name):
    R, C = g.shape
    tr = _pick(R, (512, 256, 128, 64, 32, 16, 8))

    def body(g_ref, w_ref, m_ref, v_ref, d_ref, nm_ref, nv_ref):
        d, nm, nv = _adamw(w_ref[...], g_ref[...], m_ref[...], v_ref[...])
        d_ref[...] = d
        nm_ref[...] = nm
        nv_ref[...] = nv

    blk = pl.BlockSpec((tr, C), lambda r: (r, 0))
    sh = jax.ShapeDtypeStruct((R, C), F32)
    return pl.pallas_call(body, grid=(R // tr,), in_specs=[blk] * 4, out_specs=[blk] * 3, out_shape=[sh] * 3,
                          name=name, compiler_params=_params(("parallel",)))(g, w, m, v)


WEIGHTS = ("norm_mix_g", "norm_ffn_g", "a_w_in", "a_g_v", "a_w_s", "a_b_s", "a_w_out", "b_w_in", "b_a_re", "b_a_im",
           "b_log_dt", "b_b_re", "b_b_im", "b_c_re", "b_c_im", "b_d", "b_w_glu", "f_w_up", "f_conv_w", "f_conv_b",
           "f_w_down", "final_g")
BIG = ("a_w_in", "a_w_out", "b_w_in", "b_w_glu", "f_w_up", "f_w_down")
FLAT_ROWS = 512
FLAT_CHUNK = N_DEV * FLAT_ROWS * LANES


def _expand(blocks, eye):
    KB, KG, C, P = blocks.shape
    return (blocks[:, :, :, None, :] * eye[None, :, None, :, None]).reshape(KB, KG * C, KG * P)


def _diag_blocks(dense, KG, C, P, eye):
    KB = dense.shape[0]
    return jnp.einsum("kgchp,gh->kgcp", dense.reshape(KB, KG, C, KG, P), eye)


def _flatten_pack(parts):
    flat = jnp.concatenate([p.reshape(-1) for p in parts])
    pad = (-flat.shape[0]) % FLAT_CHUNK
    return jnp.pad(flat, (0, pad))


def _unpack(flat, like):
    out, o = [], 0
    for p in like:
        n = math.prod(p.shape)
        out.append(flat[o:o + n].reshape(p.shape))
        o += n
    return out


def kernel(x, norm_mix_g, norm_ffn_g, a_w_in, a_g_v, a_w_s, a_b_s, a_w_out, b_w_in, b_a_re, b_a_im, b_log_dt, b_b_re, b_b_im, b_c_re, b_c_im, b_d, b_w_glu, f_w_up, f_conv_w, f_conv_b, f_w_down, final_g, loss_target, m_norm_mix_g, m_norm_ffn_g, m_a_w_in, m_a_g_v, m_a_w_s, m_a_b_s, m_a_w_out, m_b_w_in, m_b_a_re, m_b_a_im, m_b_log_dt, m_b_b_re, m_b_b_im, m_b_c_re, m_b_c_im, m_b_d, m_b_w_glu, m_f_w_up, m_f_conv_w, m_f_conv_b, m_f_w_down, m_final_g, v_norm_mix_g, v_norm_ffn_g, v_a_w_in, v_a_g_v, v_a_w_s, v_a_b_s, v_a_w_out, v_b_w_in, v_b_a_re, v_b_a_im, v_b_log_dt, v_b_b_re, v_b_b_im, v_b_c_re, v_b_c_im, v_b_d, v_b_w_glu, v_f_w_up, v_f_conv_w, v_f_conv_b, v_f_w_down, v_final_g):
    env = dict(locals())
    W = {n: env[n] for n in WEIGHTS}
    Mo = {n: env["m_" + n] for n in WEIGHTS}
    Vo = {n: env["v_" + n] for n in WEIGHTS}

    _, T, D = x.shape
    depth = norm_mix_g.shape[0]
    E_A = a_g_v.shape[1]
    H = a_w_s.shape[1]
    G, P, C = b_b_re.shape[1], b_b_re.shape[2], b_b_re.shape[3]
    E_B = G * C
    KG = S5_KG
    KB = G // KG
    F2 = f_conv_b.shape[1]
    tb = _pick(T, (512, 256, 128))
    pm, pmt = _perm_matrices(tb)
    eye = jnp.eye(KG, dtype=F32)
    _, _, _, me = _my_place()

    def shards(i):
        j = i // 2
        ffn = [f_w_up[i].astype(BF16), f_w_down[i].astype(BF16), f_conv_w[i]]
        if i % 2 == 0:
            return [a_w_in[j].astype(BF16), a_w_out[j].astype(BF16)] + ffn
        return [b_w_in[j].astype(BF16), b_w_glu[j].astype(BF16), b_d[j][None]] + ffn

    h = x[0]
    started = exchange_start(shards(0), bcast=True, masks=PEERS_DIRECT, name="gather_start0")
    saved = []
    for i in range(depth):
        j = i // 2
        s = {}
        lands = exchange_wait(started, h, bcast=True, masks=PEERS_DIRECT, name=f"gather_wait{i}")
        passing = pass_start(lands, name=f"gather_pass_start{i}")
        dep = passing[-1]
        if i + 1 < depth:
            started = exchange_start(shards(i + 1), bcast=True, masks=PEERS_DIRECT, dep=dep,
                                     name=f"gather_start{i + 1}")
            dep = started[-1]
        s["h"] = h
        s["hn"] = rms_fwd(h, norm_mix_g[i][None], dep=dep, name=f"rms_mix{i}")
        gathered = pass_wait(passing, s["hn"], name=f"gather_pass_wait{i}")
        if i % 2 == 0:
            g_in, g_out, g_up, g_dn, g_cw = gathered
            s["w_in"], s["w_out"] = g_in, g_out.reshape(E_A, D)
        else:
            g_in, g_glu, g_dd, g_up, g_dn, g_cw = gathered
            s["w_in"], s["w_glu"] = g_in.reshape(D, E_B), g_glu
            s["dd"] = g_dd.reshape(KB, 1, KG * C)
        s["w_up"], s["w_dn"], s["cw"] = g_up, g_dn.reshape(F2 // 2, D), g_cw
        if i % 2 == 0:
            s["p"] = mm_nn(s["hn"], s["w_in"], name=f"a_in{i}")
            s["bexp"] = jnp.repeat(a_b_s[j].T, E_A // H, axis=1)
            s["us"] = sgu_fwd(s["p"], a_g_v[j][None], a_w_s[j], s["bexp"], name=f"sgu_fwd{i}")
            h_mid = mm_nn(s["us"], s["w_out"], res=h, name=f"a_out{i}")
        else:
            s["uu"] = mm_nn(s["hn"], s["w_in"], name=f"b_in{i}")
            s["prm"] = (b_a_re[j], b_a_im[j], b_log_dt[j][:, None],
                        b_b_re[j].transpose(2, 0, 1), b_b_im[j].transpose(2, 0, 1))
            ar, ai, bbr, bbi = s5_disc_fwd(*s["prm"], name=f"s5_disc{i}")
            to_blocks = lambda t: t.reshape(C, KB, KG, P).transpose(1, 2, 0, 3)
            s["bd_r"] = _expand(to_blocks(bbr), eye).astype(BF16)
            s["bd_i"] = _expand(to_blocks(bbi), eye).astype(BF16)
            s["ct_r"] = _expand(b_c_re[j].reshape(KB, KG, C, P), eye).astype(BF16)
            s["ct_i"] = _expand(b_c_im[j].reshape(KB, KG, C, P), eye).astype(BF16)
            s["ab_r"], s["ab_i"] = ar.reshape(KB, 1, KG * P), ai.reshape(KB, 1, KG * P)
            s["y"], s["q"], s["st_r"], s["st_i"] = s5_fwd(
                s["uu"], pm, pmt, s["bd_r"], s["bd_i"], s["ct_r"], s["ct_i"], s["ab_r"], s["ab_i"], s["dd"],
                tb=tb, name=f"s5_fwd{i}")
            s["pg"] = mm_nn(s["q"], s["w_glu"], name=f"b_glu{i}")
            h_mid = glu_fwd(h, s["pg"], name=f"glu_fwd{i}")
        s["h_mid"] = h_mid
        s["hn2"] = rms_fwd(h_mid, norm_ffn_g[i][None], name=f"rms_ffn{i}")
        s["z"] = mm_nn(s["hn2"], s["w_up"], name=f"f_up{i}")
        s["a"] = convglu_fwd(s["z"], s["cw"], f_conv_b[i][None], name=f"convglu_fwd{i}")
        h = mm_nn(s["a"], s["w_dn"], res=h_mid, name=f"f_down{i}")
        saved.append(s)

    loss_tile, dh, dhb, dg_final = loss_head(h, final_g[None], loss_target[0], name="loss_head")
    loss = lax.psum(loss_tile[0, 0], AXES)

    gbig = {n: [None] * W[n].shape[0] for n in BIG}
    gs = {n: [None] * W[n].shape[0] for n in WEIGHTS if n not in BIG and n != "final_g"}
    pending = []
    for i in reversed(range(depth)):
        j = i // 2
        s = saved[i]
        gbig["f_w_down"][i] = mm_tn(s["a"], dhb, blocks=1, name=f"g_down{i}").reshape(N_DEV, F2 // 2 // N_DEV, D)
        st = exchange_start([gbig["f_w_down"][i]], bcast=False, name=f"xchg_down_start{i}")
        pending.append((st, [("f_w_down", i)], f"down{i}"))
        da = mm_nt(dhb, s["w_dn"], dep=st[-1], name=f"d_a{i}")
        dacc, dwg, dwv, dbg, dbv = convglu_bwd_acc(s["z"], da, s["cw"], f_conv_b[i][None], name=f"convglu_bwd{i}")
        gs["f_conv_w"][i] = jnp.concatenate([dwg, dwv], axis=0)
        gs["f_conv_b"][i] = jnp.concatenate([dbg, dbv], axis=1)[0]
        dz = conv_bwd_in(dacc, s["cw"], name=f"conv_bwd_in{i}")
        gbig["f_w_up"][i] = mm_tn(s["hn2"], dz, blocks=N_DEV, name=f"g_up{i}")
        st = exchange_start([gbig["f_w_up"][i]], bcast=False, name=f"xchg_up_start{i}")
        pending.append((st, [("f_w_up", i)], f"up{i}"))
        dhn2 = mm_nt(dz, s["w_up"], dep=st[-1], name=f"d_hn2{i}")
        dh_mid, dmb, dg = rms_bwd(s["h_mid"], norm_ffn_g[i][None], dhn2, dh, name=f"rms_ffn_bwd{i}")
        gs["norm_ffn_g"][i] = dg[0]
        if i % 2 == 0:
            gbig["a_w_out"][j] = mm_tn(s["us"], dmb, blocks=1, name=f"g_aout{i}").reshape(N_DEV, E_A // N_DEV, D)
            d_us = mm_nt(dmb, s["w_out"], name=f"d_us{i}")
            dp, dws, dbt, dgv = sgu_bwd(s["p"], d_us, a_g_v[j][None], a_w_s[j], s["bexp"], name=f"sgu_bwd{i}")
            gs["a_w_s"][j], gs["a_b_s"][j], gs["a_g_v"][j] = dws, dbt[:, :H].T, dgv[0]
            gbig["a_w_in"][j] = mm_tn(s["hn"], dp, blocks=N_DEV, name=f"g_ain{i}")
            dhn = mm_nt(dp, s["w_in"], name=f"d_hn_a{i}")
        else:
            dpg = glu_bwd(dh_mid, s["pg"], name=f"glu_bwd{i}")
            gbig["b_w_glu"][j] = mm_tn(s["q"], dpg, blocks=N_DEV, name=f"g_glu{i}")
            dq = mm_nt(dpg, s["w_glu"], name=f"d_q{i}")
            duu, dbr, dbi, dcr, dci, dar, dai, ddd = s5_bwd(
                s["uu"], s["y"], dq, pm, pmt, s["st_r"], s["st_i"], s["bd_r"], s["bd_i"], s["ct_r"], s["ct_i"],
                s["ab_r"], s["ab_i"], s["dd"], tb=tb, name=f"s5_bwd{i}")
            from_blocks = lambda t: _diag_blocks(t, KG, C, P, eye).transpose(2, 0, 1, 3).reshape(C, G, P)
            d_are, d_aim, d_ldt, d_bre, d_bim = s5_disc_bwd(
                *s["prm"], dar.reshape(G, P), dai.reshape(G, P), from_blocks(dbr), from_blocks(dbi),
                name=f"s5_disc_bwd{i}")
            gs["b_a_re"][j], gs["b_a_im"][j], gs["b_log_dt"][j] = d_are, d_aim, d_ldt[:, 0]
            gs["b_b_re"][j], gs["b_b_im"][j] = d_bre.transpose(1, 2, 0), d_bim.transpose(1, 2, 0)
            gs["b_c_re"][j] = _diag_blocks(dcr, KG, C, P, eye).reshape(G, C, P)
            gs["b_c_im"][j] = _diag_blocks(dci, KG, C, P, eye).reshape(G, C, P)
            gs["b_d"][j] = ddd.reshape(E_B)
            gbig["b_w_in"][j] = mm_tn(s["hn"], duu, blocks=1, name=f"g_bin{i}").reshape(N_DEV, D // N_DEV, E_B)
            dhn = mm_nt(duu, s["w_in"], name=f"d_hn_b{i}")
        mix = ("a_w_out", "a_w_in") if i % 2 == 0 else ("b_w_glu", "b_w_in")
        st = exchange_start([gbig[n][j] for n in mix], bcast=False, name=f"xchg_mix_start{i}")
        pending.append((st, [(n, j) for n in mix], f"mix{i}"))
        dh, dhb, dg = rms_bwd(s["h"], norm_mix_g[i][None], dhn, dh_mid, dep=st[-1], name=f"rms_mix_bwd{i}")
        gs["norm_mix_g"][i] = dg[0]
    grad_x = dh[None]

    grads, deltas, new_m, new_v = {}, {}, {}, {}
    small = [n for n in WEIGHTS if n not in BIG]
    full = {n: (dg_final[0] if n == "final_g" else jnp.stack(gs[n])) for n in small}
    flat = _flatten_pack([full[n] for n in small])
    rows = flat.shape[0] // N_DEV // LANES
    st_small = exchange_start([flat.reshape(N_DEV, rows, LANES)], bcast=False, name="xchg_small_start")

    done = {n: None for n in BIG}

    def finish(entries, after):
        for st, items, tag in entries:
            recvs = exchange_wait(st, after, bcast=False, name=f"xchg_wait_{tag}")
            for (n, l), recv in zip(items, recvs):
                done[n] = adam_reduce(recv, W[n], Mo[n], Vo[n], l, done[n], name=f"adam_{n}{l}")
                after = done[n][0]
        return after

    after = finish(pending[:-1], st_small[-1])
    recv = exchange_wait(st_small, after, bcast=False, name="xchg_small_wait")[0]
    part = sum_slots(recv, name="sum_small")
    st_tot = exchange_start([part], bcast=True, name="gather_small_start")
    after = finish(pending[-1:], st_tot[-1])
    tot = exchange_wait(st_tot, after, bcast=True, name="gather_small_wait")[0].reshape(-1)
    red = dict(zip(small, _unpack(tot, [full[n] for n in small])))
    red["f_conv_w"] = lax.dynamic_index_in_dim(red["f_conv_w"], me, axis=1, keepdims=False)
    red["b_d"] = lax.dynamic_slice_in_dim(red["b_d"], me * (E_B // N_DEV), E_B // N_DEV, axis=1)
    gflat = _flatten_pack([red[n] for n in small]).reshape(-1, LANES)
    d_f, m_f, v_f = adam_flat(
        gflat, _flatten_pack([W[n] for n in small]).reshape(-1, LANES),
        _flatten_pack([Mo[n] for n in small]).reshape(-1, LANES),
        _flatten_pack([Vo[n] for n in small]).reshape(-1, LANES), name="adam_small")
    like = [W[n] for n in small]
    for n, d_, m_, v_ in zip(small, _unpack(d_f.reshape(-1), like), _unpack(m_f.reshape(-1), like),
                             _unpack(v_f.reshape(-1), like)):
        grads[n], deltas[n], new_m[n], new_v[n] = red[n], d_, m_, v_

    for n in BIG:
        grads[n], deltas[n], new_m[n], new_v[n] = done[n]

    return (loss, grad_x, *[grads[n] for n in WEIGHTS], *[deltas[n] for n in WEIGHTS],
            *[new_m[n] for n in WEIGHTS], *[new_v[n] for n in WEIGHTS])
```

```python
import math

import jax
import jax.numpy as jnp
from jax import lax
from jax.experimental import pallas as pl
from jax.experimental.pallas import tpu as pltpu

F32 = jnp.float32
BF16 = jnp.bfloat16
N_DEV = 8
AXES = ("x", "y", "c")
EPS = 1e-6
LANES = 128
SUBLANES = 8
VMEM_BIG = 56 * 1024 * 1024
VMEM_MID = 48 * 1024 * 1024
ADAM_LR, ADAM_B1, ADAM_B2, ADAM_EPS, ADAM_WD, ADAM_STEP = 0.001, 0.9, 0.999, 1e-08, 0.01, 10
MESH = pl.DeviceIdType.MESH
GELU_C = math.sqrt(2.0 / math.pi)
GELU_K = 0.044715


def _pick(n, prefs):
    for p in prefs:
        if n % p == 0:
            return p
    return n


def _params(sem, vmem=None):
    return pltpu.CompilerParams(dimension_semantics=sem, vmem_limit_bytes=vmem)


def _gelu(x):
    return 0.5 * x * (1.0 + jnp.tanh(GELU_C * (x + GELU_K * x * x * x)))


def _gelu_grad(x):
    x2 = x * x
    th = jnp.tanh(GELU_C * x * (1.0 + GELU_K * x2))
    return 0.5 * (1.0 + th) + 0.5 * x * (1.0 - th * th) * GELU_C * (1.0 + 3.0 * GELU_K * x2)


def _sigmoid(x):
    return 1.0 / (1.0 + jnp.exp(-x))


def _dot_nn(a, b):
    return jnp.dot(a, b, preferred_element_type=F32)


def _dot_nt(a, b):
    return lax.dot_general(a, b, (((1,), (1,)), ((), ())), preferred_element_type=F32)


def _dot_tn(a, b):
    return lax.dot_general(a, b, (((0,), (0,)), ((), ())), preferred_element_type=F32)


M_TILES = (1024, 512, 256, 128)
TN_M_TILES = (2048, 1024, 512, 256, 128)
FULL_K = 2048
NT_SPAN = 2816
N_TILES = (1408, 1024, 512, 384, 256, 128)
K_TILES = (1408, 1024, 512, 384, 256, 128)


def _as3(b):
    return b if b.ndim == 3 else b[None]


def mm_nn(a, b, *, res=None, out_dtype=F32, name):
    b3 = _as3(b)
    M, K = a.shape
    J, _, nb = b3.shape
    tm, tn = _pick(M, M_TILES), _pick(nb, N_TILES)
    tk = K if K <= FULL_K else _pick(K, (2816,) + K_TILES)
    per, nk = nb // tn, K // tk

    def body(*refs):
        if res is None:
            a_ref, b_ref, o_ref, acc = refs
        else:
            a_ref, b_ref, r_ref, o_ref, acc = refs
        k = pl.program_id(2)
        if nk == 1:
            r = _dot_nn(a_ref[...], b_ref[...])
            if res is not None:
                r = r + r_ref[...]
            o_ref[...] = r.astype(out_dtype)
            return

        @pl.when(k == 0)
        def _():
            acc[...] = jnp.zeros_like(acc)

        acc[...] += _dot_nn(a_ref[...], b_ref[...])

        @pl.when(k == nk - 1)
        def _():
            r = acc[...]
            if res is not None:
                r = r + r_ref[...]
            o_ref[...] = r.astype(out_dtype)

    in_specs = [pl.BlockSpec((tm, tk), lambda i, n, k: (i, k)),
                pl.BlockSpec((None, tk, tn), lambda i, n, k: (n // per, k, n % per))]
    args = [a, b3]
    if res is not None:
        in_specs.append(pl.BlockSpec((tm, tn), lambda i, n, k: (i, n)))
        args.append(res)
    return pl.pallas_call(
        body, grid=(M // tm, J * per, nk), in_specs=in_specs,
        out_specs=pl.BlockSpec((tm, tn), lambda i, n, k: (i, n)),
        out_shape=jax.ShapeDtypeStruct((M, J * nb), out_dtype),
        scratch_shapes=[pltpu.VMEM((tm, tn) if nk > 1 else (SUBLANES, LANES), F32)], name=name,
        compiler_params=_params(("parallel", "parallel", "arbitrary"), VMEM_BIG))(*args)


def mm_nt(dy, b, *, out_dtype=F32, dep=None, name):
    b3 = _as3(b)
    deps = [] if dep is None else [dep]
    M, N = dy.shape
    J, K, nb = b3.shape
    tm = _pick(M, M_TILES)
    tn = nb if (J == 1 and nb <= FULL_K) else _pick(nb, N_TILES)
    grp = max([g for g in (4, 2, 1) if J % g == 0 and g * tn <= NT_SPAN]) if tn == nb else 1
    tk = _pick(K, K_TILES) if (grp > 1 or K > FULL_K) else K
    per, nn = nb // tn, (J * nb) // (tn * grp)

    def body(d_ref, b_ref, *rest):
        o_ref, acc = rest[-2:]
        if grp > 1:
            r = sum(_dot_nt(d_ref[:, g * tn:(g + 1) * tn], b_ref[g]) for g in range(grp))
        else:
            r = _dot_nt(d_ref[...], b_ref[...])
        if nn == 1:
            o_ref[...] = r.astype(out_dtype)
            return
        n = pl.program_id(2)

        @pl.when(n == 0)
        def _():
            acc[...] = jnp.zeros_like(acc)

        acc[...] += r

        @pl.when(n == nn - 1)
        def _():
            o_ref[...] = acc[...].astype(out_dtype)

    if grp > 1:
        b_spec = pl.BlockSpec((grp, tk, tn), lambda i, k, n: (n, k, 0))
    else:
        b_spec = pl.BlockSpec((None, tk, tn), lambda i, k, n: (n // per, k, n % per))
    return pl.pallas_call(
        body, grid=(M // tm, K // tk, nn),
        in_specs=[pl.BlockSpec((tm, tn * grp), lambda i, k, n: (i, n)), b_spec]
        + [pl.BlockSpec(memory_space=pl.ANY)] * len(deps),
        out_specs=pl.BlockSpec((tm, tk), lambda i, k, n: (i, k)),
        out_shape=jax.ShapeDtypeStruct((M, K), out_dtype),
        scratch_shapes=[pltpu.VMEM((tm, tk) if nn > 1 else (SUBLANES, LANES), F32)], name=name,
        compiler_params=_params(("parallel", "parallel", "arbitrary"), VMEM_BIG))(dy, b3, *deps)


def mm_tn(x, dy, *, blocks, out_dtype=BF16, name):
    M, K = x.shape
    _, N = dy.shape
    nb = N // blocks
    tm, tn, tk = _pick(M, TN_M_TILES), _pick(nb, N_TILES), _pick(K, K_TILES)
    per, nm = nb // tn, M // tm

    def body(x_ref, d_ref, o_ref, acc):
        m = pl.program_id(2)

        @pl.when(m == 0)
        def _():
            acc[...] = jnp.zeros_like(acc)

        acc[...] += _dot_tn(x_ref[...], d_ref[...])

        @pl.when(m == nm - 1)
        def _():
            o_ref[...] = acc[...].astype(out_dtype)

    return pl.pallas_call(
        body, grid=(K // tk, N // tn, nm),
        in_specs=[pl.BlockSpec((tm, tk), lambda k, n, m: (m, k)),
                  pl.BlockSpec((tm, tn), lambda k, n, m: (m, n))],
        out_specs=pl.BlockSpec((None, tk, tn), lambda k, n, m: (n // per, k, n % per)),
        out_shape=jax.ShapeDtypeStruct((blocks, K, nb), out_dtype),
        scratch_shapes=[pltpu.VMEM((tk, tn), F32)], name=name,
        compiler_params=_params(("parallel", "parallel", "arbitrary"), VMEM_BIG))(x, dy)


ROW_TILES = (512, 256, 128)


_DEP = pl.BlockSpec(memory_space=pl.ANY)


def rms_fwd(h, g, *, dep=None, name):
    T, D = h.shape
    tm = _pick(T, ROW_TILES)
    deps = [] if dep is None else [dep]

    def body(h_ref, g_ref, *rest):
        o_ref = rest[-1]
        x = h_ref[...]
        r = lax.rsqrt(jnp.mean(x * x, axis=-1, keepdims=True) + EPS)
        o_ref[...] = (x * r * g_ref[...]).astype(BF16)

    return pl.pallas_call(
        body, grid=(T // tm,),
        in_specs=[pl.BlockSpec((tm, D), lambda i: (i, 0)), pl.BlockSpec((1, D), lambda i: (0, 0))] + [_DEP] * len(deps),
        out_specs=pl.BlockSpec((tm, D), lambda i: (i, 0)),
        out_shape=jax.ShapeDtypeStruct((T, D), BF16), name=name,
        compiler_params=_params(("parallel",), VMEM_MID))(h, g, *deps)


def rms_bwd(h, g, dhn, dres, *, dep=None, name):
    T, D = h.shape
    tm = _pick(T, ROW_TILES[1:])
    deps = [] if dep is None else [dep]

    def body(h_ref, g_ref, d_ref, r_ref, *rest):
        dh_ref, dhb_ref, dg_ref = rest[-3:]

        @pl.when(pl.program_id(0) == 0)
        def _():
            dg_ref[...] = jnp.zeros_like(dg_ref)

        x = h_ref[...]
        r = lax.rsqrt(jnp.mean(x * x, axis=-1, keepdims=True) + EPS)
        xh = x * r
        dy = d_ref[...]
        dxh = dy * g_ref[...]
        dh = r_ref[...] + r * (dxh - xh * jnp.mean(dxh * xh, axis=-1, keepdims=True))
        dh_ref[...] = dh
        dhb_ref[...] = dh.astype(BF16)
        dg_ref[...] += jnp.sum(dy * xh, axis=0, keepdims=True)

    row = pl.BlockSpec((tm, D), lambda i: (i, 0))
    vec = pl.BlockSpec((1, D), lambda i: (0, 0))
    return pl.pallas_call(
        body, grid=(T // tm,), in_specs=[row, vec, row, row] + [_DEP] * len(deps), out_specs=[row, row, vec],
        out_shape=[jax.ShapeDtypeStruct((T, D), F32), jax.ShapeDtypeStruct((T, D), BF16),
                   jax.ShapeDtypeStruct((1, D), F32)], name=name,
        compiler_params=_params(("arbitrary",), VMEM_MID))(h, g, dhn, dres, *deps)


def loss_head(h, g, tgt, *, name):
    T, D = h.shape
    tm = _pick(T, ROW_TILES)

    def body(h_ref, g_ref, t_ref, l_ref, dh_ref, dhb_ref, dg_ref):
        @pl.when(pl.program_id(0) == 0)
        def _():
            dg_ref[...] = jnp.zeros_like(dg_ref)
            l_ref[...] = jnp.zeros_like(l_ref)

        x = h_ref[...]
        gg = g_ref[...]
        r = lax.rsqrt(jnp.mean(x * x, axis=-1, keepdims=True) + EPS)
        xh = x * r
        e = xh * gg - t_ref[...]
        l_ref[...] += 0.5 * jnp.sum(jnp.mean(e * e, axis=-1, keepdims=True), axis=0, keepdims=True)
        dy = e * (1.0 / D)
        dxh = dy * gg
        dh = r * (dxh - xh * jnp.mean(dxh * xh, axis=-1, keepdims=True))
        dh_ref[...] = dh
        dhb_ref[...] = dh.astype(BF16)
        dg_ref[...] += jnp.sum(dy * xh, axis=0, keepdims=True)

    row = pl.BlockSpec((tm, D), lambda i: (i, 0))
    vec = pl.BlockSpec((1, D), lambda i: (0, 0))
    return pl.pallas_call(
        body, grid=(T // tm,), in_specs=[row, vec, row],
        out_specs=[pl.BlockSpec((SUBLANES, LANES), lambda i: (0, 0)), row, row, vec],
        out_shape=[jax.ShapeDtypeStruct((SUBLANES, LANES), F32), jax.ShapeDtypeStruct((T, D), F32),
                   jax.ShapeDtypeStruct((T, D), BF16), jax.ShapeDtypeStruct((1, D), F32)], name=name,
        compiler_params=_params(("arbitrary",), VMEM_MID))(h, g, tgt)


def _sgu_common(p, gv, w_ref, bexp, E, H, CH):
    Dg = E // H
    z = _gelu(p)
    u, v = z[:, :E], z[:, E:]
    r = lax.rsqrt(jnp.mean(v * v, axis=-1, keepdims=True) + EPS)
    vhat = v * r
    vn = (vhat * gv).astype(BF16)
    row = lax.broadcasted_iota(jnp.int32, (CH, CH), 0)
    col = lax.broadcasted_iota(jnp.int32, (CH, CH), 1)
    causal = row >= col
    ws = [jnp.where(causal, w_ref[hh], 0.0).astype(BF16) for hh in range(H)]
    s = jnp.concatenate([_dot_nn(ws[hh], vn[:, hh * Dg:(hh + 1) * Dg]) for hh in range(H)], axis=1) + bexp
    return u, r, vhat, vn, causal, ws, s


def sgu_fwd(p, g_v, w_s, bexp, *, name):
    T, E2 = p.shape
    E = E2 // 2
    H, CH, _ = w_s.shape

    def body(p_ref, gv_ref, w_ref, b_ref, o_ref):
        u, _, _, _, _, _, s = _sgu_common(p_ref[...], gv_ref[...], w_ref, b_ref[...], E, H, CH)
        o_ref[...] = (u * s).astype(BF16)

    return pl.pallas_call(
        body, grid=(T // CH,),
        in_specs=[pl.BlockSpec((CH, E2), lambda i: (i, 0)), pl.BlockSpec((1, E), lambda i: (0, 0)),
                  pl.BlockSpec((H, CH, CH), lambda i: (0, 0, 0)), pl.BlockSpec((CH, E), lambda i: (0, 0))],
        out_specs=pl.BlockSpec((CH, E), lambda i: (i, 0)),
        out_shape=jax.ShapeDtypeStruct((T, E), BF16), name=name,
        compiler_params=_params(("parallel",), VMEM_BIG))(p, g_v, w_s, bexp)


def sgu_bwd(p, d_us, g_v, w_s, bexp, *, name):
    T, E2 = p.shape
    E = E2 // 2
    H, CH, _ = w_s.shape
    Dg = E // H

    def body(p_ref, d_ref, gv_ref, w_ref, b_ref, dp_ref, dw_ref, db_ref, dg_ref):
        @pl.when(pl.program_id(0) == 0)
        def _():
            dw_ref[...] = jnp.zeros_like(dw_ref)
            db_ref[...] = jnp.zeros_like(db_ref)
            dg_ref[...] = jnp.zeros_like(dg_ref)

        p = p_ref[...]
        gv = gv_ref[...]
        u, r, vhat, vn, causal, ws, s = _sgu_common(p, gv, w_ref, b_ref[...], E, H, CH)
        d = d_ref[...]
        du = d * s
        ds = d * u
        lane = lax.broadcasted_iota(jnp.int32, (CH, LANES), 1)
        dvn_parts = []
        db = jnp.zeros((CH, LANES), F32)
        for hh in range(H):
            ds_h = ds[:, hh * Dg:(hh + 1) * Dg]
            ds_hb = ds_h.astype(BF16)
            dw_ref[hh] += jnp.where(causal, _dot_nt(ds_hb, vn[:, hh * Dg:(hh + 1) * Dg]), 0.0)
            dvn_parts.append(_dot_tn(ws[hh], ds_hb))
            db = db + jnp.where(lane == hh, jnp.sum(ds_h, axis=1, keepdims=True), 0.0)
        db_ref[...] += db
        dvn = jnp.concatenate(dvn_parts, axis=1)
        dg_ref[...] += jnp.sum(dvn * vhat, axis=0, keepdims=True)
        dvh = dvn * gv
        dv = r * (dvh - vhat * jnp.mean(dvh * vhat, axis=-1, keepdims=True))
        dp_ref[...] = (jnp.concatenate([du, dv], axis=1) * _gelu_grad(p)).astype(BF16)

    return pl.pallas_call(
        body, grid=(T // CH,),
        in_specs=[pl.BlockSpec((CH, E2), lambda i: (i, 0)), pl.BlockSpec((CH, E), lambda i: (i, 0)),
                  pl.BlockSpec((1, E), lambda i: (0, 0)), pl.BlockSpec((H, CH, CH), lambda i: (0, 0, 0)),
                  pl.BlockSpec((CH, E), lambda i: (0, 0))],
        out_specs=[pl.BlockSpec((CH, E2), lambda i: (i, 0)), pl.BlockSpec((H, CH, CH), lambda i: (0, 0, 0)),
                   pl.BlockSpec((CH, LANES), lambda i: (0, 0)), pl.BlockSpec((1, E), lambda i: (0, 0))],
        out_shape=[jax.ShapeDtypeStruct((T, E2), BF16), jax.ShapeDtypeStruct((H, CH, CH), F32),
                   jax.ShapeDtypeStruct((CH, LANES), F32), jax.ShapeDtypeStruct((1, E), F32)], name=name,
        compiler_params=_params(("arbitrary",), VMEM_BIG))(p, d_us, g_v, w_s, bexp)


def _s5_disc(a_re, a_im, log_dt, b_re, b_im):
    dt = jnp.exp(log_dt)
    mag = jnp.exp(dt * a_re)
    ar, ai = mag * jnp.cos(dt * a_im), mag * jnp.sin(dt * a_im)
    den = a_re * a_re + a_im * a_im
    qr = ((ar - 1.0) * a_re + ai * a_im) / den
    qi = (ai * a_re - (ar - 1.0) * a_im) / den
    return ar, ai, qr[None] * b_re - qi[None] * b_im, qr[None] * b_im + qi[None] * b_re


def s5_disc_fwd(a_re, a_im, log_dt, b_re, b_im, *, name):
    G, P = a_re.shape
    C = b_re.shape[0]

    def body(ar_ref, ai_ref, dt_ref, br_ref, bi_ref, o_ar, o_ai, o_br, o_bi):
        ar, ai, br, bi = _s5_disc(ar_ref[...], ai_ref[...], dt_ref[...], br_ref[...], bi_ref[...])
        o_ar[...] = ar
        o_ai[...] = ai
        o_br[...] = br
        o_bi[...] = bi

    gp = jax.ShapeDtypeStruct((G, P), F32)
    cgp = jax.ShapeDtypeStruct((C, G, P), F32)
    return pl.pallas_call(body, out_shape=[gp, gp, cgp, cgp], name=name)(a_re, a_im, log_dt, b_re, b_im)


def s5_disc_bwd(a_re, a_im, log_dt, b_re, b_im, d_ar, d_ai, d_br, d_bi, *, name):
    G, P = a_re.shape
    C = b_re.shape[0]

    def body(ar_ref, ai_ref, dt_ref, br_ref, bi_ref, g0, g1, g2, g3, o0, o1, o2, o3, o4):
        prim = (ar_ref[...], ai_ref[...], dt_ref[...], br_ref[...], bi_ref[...])
        _, vjp = jax.vjp(_s5_disc, *prim)
        outs = vjp((g0[...], g1[...], g2[...], g3[...]))
        for o, v in zip((o0, o1, o2, o3, o4), outs):
            o[...] = v

    gp = jax.ShapeDtypeStruct((G, P), F32)
    cgp = jax.ShapeDtypeStruct((C, G, P), F32)
    return pl.pallas_call(
        body, out_shape=[gp, gp, jax.ShapeDtypeStruct((G, 1), F32), cgp, cgp], name=name,
    )(a_re, a_im, log_dt, b_re, b_im, d_ar, d_ai, d_br, d_bi)


S5_KG = 8


def _planes(x):
    return [x[:, c * LANES:(c + 1) * LANES] for c in range(x.shape[1] // LANES)]


def _store_planes(ref, row0, val):
    for c, p in enumerate(_planes(val)):
        ref[c, pl.ds(row0, val.shape[0]), :] = p


def _load_planes(ref, row0, rows):
    return jnp.concatenate([ref[c, pl.ds(row0, rows), :] for c in range(ref.shape[0])], axis=1)


def _build_powers(a_r, a_i, S, pf_r, pf_i, pr_r=None, pr_i=None):
    ar, ai = _planes(a_r), _planes(a_i)
    NP = len(ar)

    def step(i, carry):
        out = []
        for c in range(NP):
            p_r, p_i = carry[2 * c], carry[2 * c + 1]
            pf_r[c, pl.ds(i, 1), :] = p_r
            pf_i[c, pl.ds(i, 1), :] = p_i
            if pr_r is not None:
                pr_r[c, pl.ds(S - 1 - i, 1), :] = p_r
                pr_i[c, pl.ds(S - 1 - i, 1), :] = -p_i
            out += [ar[c] * p_r - ai[c] * p_i, ar[c] * p_i + ai[c] * p_r]
        return tuple(out)

    init = []
    for c in range(NP):
        init += [ar[c], ai[c]]
    lax.fori_loop(0, S, step, tuple(init))


def _scan_seg(hr, hi, hrow0, tb, a_r, a_i, pw_r, pw_i, h0r, h0i, *, reverse):
    NP = hr.shape[0]
    S = tb // SUBLANES
    if reverse:
        a_i = -a_i
    ar, ai = _planes(a_r), _planes(a_i)

    def step(i, carry):
        j = (S - 1 - i) if reverse else i
        slab = pl.ds(pl.multiple_of(hrow0 + j * SUBLANES, SUBLANES), SUBLANES)
        out = []
        for c in range(NP):
            nr = ar[c] * carry[2 * c] - ai[c] * carry[2 * c + 1] + hr[c, slab, :]
            ni = ar[c] * carry[2 * c + 1] + ai[c] * carry[2 * c] + hi[c, slab, :]
            hr[c, slab, :] = nr
            hi[c, slab, :] = ni
            out += [nr, ni]
        return tuple(out)

    z = jnp.zeros((SUBLANES, LANES), F32)
    loc = lax.fori_loop(0, S, step, (z,) * (2 * NP), unroll=2)
    h0r_p, h0i_p = _planes(h0r), _planes(h0i)
    top = 0 if reverse else S - 1
    order = range(SUBLANES - 1, -1, -1) if reverse else range(SUBLANES)
    out_r, out_i, ent_r, ent_i = [], [], [], []
    for c in range(NP):
        s_r, s_i = pw_r[c, pl.ds(top, 1), :], pw_i[c, pl.ds(top, 1), :]
        c_r, c_i = h0r_p[c], h0i_p[c]
        in_r, in_i = [None] * SUBLANES, [None] * SUBLANES
        for seg in order:
            in_r[seg], in_i[seg] = c_r, c_i
            l_r, l_i = loc[2 * c][seg:seg + 1], loc[2 * c + 1][seg:seg + 1]
            c_r, c_i = s_r * c_r - s_i * c_i + l_r, s_r * c_i + s_i * c_r + l_i
        out_r.append(c_r)
        out_i.append(c_i)
        ent_r.append(jnp.concatenate(in_r, axis=0))
        ent_i.append(jnp.concatenate(in_i, axis=0))

    def fix(j, _):
        slab = pl.ds(pl.multiple_of(hrow0 + j * SUBLANES, SUBLANES), SUBLANES)
        for c in range(NP):
            p_r, p_i = pw_r[c, pl.ds(j, 1), :], pw_i[c, pl.ds(j, 1), :]
            hr[c, slab, :] += p_r * ent_r[c] - p_i * ent_i[c]
            hi[c, slab, :] += p_r * ent_i[c] + p_i * ent_r[c]
        return 0

    lax.fori_loop(0, S, fix, 0, unroll=2)
    return ent_r, ent_i, jnp.concatenate(out_r, axis=1), jnp.concatenate(out_i, axis=1)


def _perm_matrices(tb):
    r = jnp.arange(tb)
    pm = (r[None, :] == ((r % SUBLANES) * (tb // SUBLANES) + r // SUBLANES)[:, None]).astype(BF16)
    return pm, pm.T


def _unpermute(pmt, x):
    c = x.shape[1]
    hi = x.astype(BF16)
    lo = (x - hi.astype(F32)).astype(BF16)
    both = _dot_nn(pmt, jnp.concatenate([hi, lo], axis=1))
    return both[:, :c] + both[:, c:]


def _s5_specs(tb, UC, SC, rev_nb=None):
    tmap = (lambda b: b) if rev_nb is None else (lambda b: rev_nb - 1 - b)
    row = pl.BlockSpec((tb, UC), lambda b, k: (tmap(b), k))
    wsp = pl.BlockSpec((None, UC, SC), lambda b, k: (k, 0, 0))
    vsc = pl.BlockSpec((None, 1, SC), lambda b, k: (k, 0, 0))
    vuc = pl.BlockSpec((None, 1, UC), lambda b, k: (k, 0, 0))
    st = pl.BlockSpec((None, None, 1, SC), lambda b, k: (tmap(b), k, 0, 0))
    return row, wsp, vsc, vuc, st


def s5_fwd(uu, pm, pmt, bd_r, bd_i, ct_r, ct_i, ab_r, ab_i, dd, *, tb, name):
    T, E = uu.shape
    KB, UC, SC = bd_r.shape
    NB = T // tb
    NP, S = SC // LANES, tb // SUBLANES

    def body(u_ref, pm_ref, pmt_ref, bdr, bdi, ctr, cti, ar_ref, ai_ref, dd_ref, y_ref, q_ref, sr_ref, si_ref,
             hr, hi, cr, ci, pf_r, pf_i):
        b, k = pl.program_id(0), pl.program_id(1)
        a_r, a_i = ar_ref[...], ai_ref[...]

        @pl.when(b == 0)
        def _():
            cr[k] = jnp.zeros((1, SC), F32)
            ci[k] = jnp.zeros((1, SC), F32)
            _build_powers(a_r, a_i, S, pf_r.at[k], pf_i.at[k])

        h0r, h0i = cr[k], ci[k]
        sr_ref[...] = h0r
        si_ref[...] = h0i
        u = u_ref[...]
        up = _dot_nn(pm_ref[...], u.astype(BF16)).astype(BF16)
        _store_planes(hr, 0, _dot_nn(up, bdr[...]))
        _store_planes(hi, 0, _dot_nn(up, bdi[...]))
        _, _, o_r, o_i = _scan_seg(hr, hi, 0, tb, a_r, a_i, pf_r.at[k], pf_i.at[k], h0r, h0i, reverse=False)
        cr[k] = o_r
        ci[k] = o_i
        ys = (_dot_nt(_load_planes(hr, 0, tb).astype(BF16), ctr[...])
              - _dot_nt(_load_planes(hi, 0, tb).astype(BF16), cti[...]))
        pmt_v = pmt_ref[...]
        y = _unpermute(pmt_v, ys) + dd_ref[...] * u
        y_ref[...] = y
        q_ref[...] = _gelu(y).astype(BF16)

    row, wsp, vsc, vuc, st = _s5_specs(tb, UC, SC)
    psp = pl.BlockSpec((tb, tb), lambda b, k: (0, 0))
    stsh = jax.ShapeDtypeStruct((NB, KB, 1, SC), F32)
    pw = pltpu.VMEM((KB, NP, S, LANES), F32)
    pln = pltpu.VMEM((NP, tb, LANES), F32)
    return pl.pallas_call(
        body, grid=(NB, KB), in_specs=[row, psp, psp, wsp, wsp, wsp, wsp, vsc, vsc, vuc],
        out_specs=[row, row, st, st],
        out_shape=[jax.ShapeDtypeStruct((T, E), F32), jax.ShapeDtypeStruct((T, E), BF16), stsh, stsh],
        scratch_shapes=[pln, pln, pltpu.VMEM((KB, 1, SC), F32), pltpu.VMEM((KB, 1, SC), F32), pw, pw],
        name=name, compiler_params=_params(("arbitrary", "arbitrary"), VMEM_MID),
    )(uu, pm, pmt, bd_r, bd_i, ct_r, ct_i, ab_r, ab_i, dd)


def s5_bwd(uu, y, dq, pm, pmt, st_r, st_i, bd_r, bd_i, ct_r, ct_i, ab_r, ab_i, dd, *, tb, name):
    T, E = uu.shape
    KB, UC, SC = bd_r.shape
    NB = T // tb
    HDR = SUBLANES
    NP, S = SC // LANES, tb // SUBLANES
    pw = pltpu.VMEM((KB, NP, S, LANES), F32)
    pln = pltpu.VMEM((NP, tb, LANES), F32)

    def body(u_ref, y_ref, dq_ref, pm_ref, pmt_ref, sr_ref, si_ref, bdr, bdi, ctr, cti, ar_ref, ai_ref, dd_ref,
             du_ref, obr, obi, ocr, oci, odar, odai, oddd,
             hr, hi, gr, gi, kr, ki, abr, abi, acr, aci, pf_r, pf_i, pr_r, pr_i):
        b, k = pl.program_id(0), pl.program_id(1)
        a_r, a_i = ar_ref[...], ai_ref[...]

        @pl.when(b == 0)
        def _():
            _build_powers(a_r, a_i, S, pf_r.at[k], pf_i.at[k], pr_r.at[k], pr_i.at[k])
            z1 = jnp.zeros((1, SC), F32)
            kr[k] = z1
            ki[k] = z1
            odar[k] = z1
            odai[k] = z1
            oddd[k] = jnp.zeros((1, UC), F32)
            zw = jnp.zeros((UC, SC), F32)
            abr[k] = zw
            abi[k] = zw
            acr[k] = zw
            aci[k] = zw

        u = u_ref[...]
        dy = dq_ref[...] * _gelu_grad(y_ref[...])
        oddd[k] += jnp.sum(dy * u, axis=0, keepdims=True)
        pm_v = pm_ref[...]
        both = _dot_nn(pm_v, jnp.concatenate([u.astype(BF16), dy.astype(BF16)], axis=1))
        ub = both[:, :UC].astype(BF16)
        dyb = both[:, UC:].astype(BF16)
        s0r, s0i = sr_ref[...], si_ref[...]
        _store_planes(hr, HDR, _dot_nn(ub, bdr[...]))
        _store_planes(hi, HDR, _dot_nn(ub, bdi[...]))
        e_r, e_i, _, _ = _scan_seg(hr, hi, HDR, tb, a_r, a_i, pf_r.at[k], pf_i.at[k], s0r, s0i, reverse=False)
        for c in range(NP):
            hr[c, pl.ds(0, HDR), :] = e_r[c]
            hi[c, pl.ds(0, HDR), :] = e_i[c]
        _store_planes(gr, 0, _dot_nn(dyb, ctr[...]))
        _store_planes(gi, 0, -_dot_nn(dyb, cti[...]))
        _, _, g0r, g0i = _scan_seg(gr, gi, 0, tb, a_r, a_i, pr_r.at[k], pr_i.at[k], kr[k], ki[k], reverse=True)
        kr[k] = g0r
        ki[k] = g0i

        def slab(j, acc):
            o = pl.multiple_of(j * SUBLANES, SUBLANES)
            out = []
            for c in range(NP):
                p_r, p_i = hr[c, pl.ds(o, SUBLANES), :], hi[c, pl.ds(o, SUBLANES), :]
                g_r, g_i = gr[c, pl.ds(o, SUBLANES), :], gi[c, pl.ds(o, SUBLANES), :]
                out += [acc[2 * c] + g_r * p_r + g_i * p_i, acc[2 * c + 1] + g_i * p_r - g_r * p_i]
            return tuple(out)

        z8 = jnp.zeros((SUBLANES, LANES), F32)
        acc = lax.fori_loop(0, S, slab, (z8,) * (2 * NP), unroll=2)
        odar[k] += jnp.concatenate([jnp.sum(acc[2 * c], axis=0, keepdims=True) for c in range(NP)], axis=1)
        odai[k] += jnp.concatenate([jnp.sum(acc[2 * c + 1], axis=0, keepdims=True) for c in range(NP)], axis=1)
        g_rb = _load_planes(gr, 0, tb).astype(BF16)
        g_ib = _load_planes(gi, 0, tb).astype(BF16)
        h_rb = _load_planes(hr, HDR, tb).astype(BF16)
        h_ib = _load_planes(hi, HDR, tb).astype(BF16)
        dus = _dot_nt(g_rb, bdr[...]) + _dot_nt(g_ib, bdi[...])
        pmt_v = pmt_ref[...]
        du = _unpermute(pmt_v, dus) + dd_ref[...] * dy
        du_ref[...] = du.astype(BF16)
        abr[k] += _dot_tn(ub, g_rb)
        abi[k] += _dot_tn(ub, g_ib)
        acr[k] += _dot_tn(dyb, h_rb)
        aci[k] -= _dot_tn(dyb, h_ib)

        @pl.when(jnp.logical_and(b == NB - 1, k == KB - 1))
        def _():
            pltpu.sync_copy(abr, obr)
            pltpu.sync_copy(abi, obi)
            pltpu.sync_copy(acr, ocr)
            pltpu.sync_copy(aci, oci)

    row, wsp, vsc, vuc, st = _s5_specs(tb, UC, SC, rev_nb=NB)
    psp = pl.BlockSpec((tb, tb), lambda b, k: (0, 0))
    hbm = pl.BlockSpec(memory_space=pltpu.HBM)
    full_sc = pl.BlockSpec((KB, 1, SC), lambda b, k: (0, 0, 0))
    full_uc = pl.BlockSpec((KB, 1, UC), lambda b, k: (0, 0, 0))
    wsh = jax.ShapeDtypeStruct((KB, UC, SC), F32)
    acc = pltpu.VMEM((KB, UC, SC), F32)
    return pl.pallas_call(
        body, grid=(NB, KB),
        in_specs=[row, row, row, psp, psp, st, st, wsp, wsp, wsp, wsp, vsc, vsc, vuc],
        out_specs=[row, hbm, hbm, hbm, hbm, full_sc, full_sc, full_uc],
        out_shape=[jax.ShapeDtypeStruct((T, E), BF16), wsh, wsh, wsh, wsh,
                   jax.ShapeDtypeStruct((KB, 1, SC), F32), jax.ShapeDtypeStruct((KB, 1, SC), F32),
                   jax.ShapeDtypeStruct((KB, 1, UC), F32)],
        scratch_shapes=[pltpu.VMEM((NP, tb + HDR, LANES), F32), pltpu.VMEM((NP, tb + HDR, LANES), F32), pln, pln,
                        pltpu.VMEM((KB, 1, SC), F32), pltpu.VMEM((KB, 1, SC), F32), acc, acc, acc, acc,
                        pw, pw, pw, pw],
        name=name, compiler_params=_params(("arbitrary", "arbitrary"), VMEM_BIG),
    )(uu, y, dq, pm, pmt, st_r, st_i, bd_r, bd_i, ct_r, ct_i, ab_r, ab_i, dd)


def glu_fwd(h, pg, *, name):
    T, D = h.shape
    tm = _pick(T, ROW_TILES)

    def body(h_ref, a_ref, b_ref, o_ref):
        o_ref[...] = h_ref[...] + a_ref[...] * _sigmoid(b_ref[...])

    row = pl.BlockSpec((tm, D), lambda i: (i, 0))
    return pl.pallas_call(
        body, grid=(T // tm,), in_specs=[row, row, pl.BlockSpec((tm, D), lambda i: (i, 1))], out_specs=row,
        out_shape=jax.ShapeDtypeStruct((T, D), F32), name=name,
        compiler_params=_params(("parallel",), VMEM_MID))(h, pg, pg)


def glu_bwd(d, pg, *, name):
    T, D = d.shape
    tm = _pick(T, ROW_TILES)

    def body(d_ref, a_ref, b_ref, o_ref):
        dv = d_ref[...]
        sg = _sigmoid(b_ref[...])
        da = dv * sg
        db = dv * a_ref[...] * sg * (1.0 - sg)
        o_ref[...] = jnp.where(pl.program_id(1) == 0, da, db).astype(BF16)

    row = pl.BlockSpec((tm, D), lambda i, hf: (i, 0))
    return pl.pallas_call(
        body, grid=(T // tm, 2), in_specs=[row, row, pl.BlockSpec((tm, D), lambda i, hf: (i, 1))],
        out_specs=pl.BlockSpec((tm, D), lambda i, hf: (i, hf)),
        out_shape=jax.ShapeDtypeStruct((T, 2 * D), BF16), name=name,
        compiler_params=_params(("parallel", "arbitrary"), VMEM_MID))(d, pg, pg)


def _shift_down(x, halo, s):
    r = pltpu.roll(x, s, axis=0)
    hr = pltpu.roll(halo, s, axis=0)
    row = lax.broadcasted_iota(jnp.int32, halo.shape, 0)
    head = jnp.where(row < s, hr, r[:SUBLANES])
    return jnp.concatenate([head, r[SUBLANES:]], axis=0)


def _shift_up(x, halo, s):
    n = x.shape[0]
    r = pltpu.roll(x, n - s, axis=0)
    hr = pltpu.roll(halo, SUBLANES - s, axis=0)
    row = lax.broadcasted_iota(jnp.int32, halo.shape, 0)
    tail = jnp.where(row >= SUBLANES - s, hr, r[n - SUBLANES:])
    return jnp.concatenate([r[:n - SUBLANES], tail], axis=0)


def _conv_acc(z, zh, w, b, first):
    kw = w.shape[0]
    zh = jnp.where(first, 0.0, zh)
    acc = b + w[kw - 1:kw] * z
    shifted = []
    for k in range(kw - 1):
        zs = _shift_down(z, zh, kw - 1 - k)
        shifted.append(zs)
        acc = acc + w[k:k + 1] * zs
    return acc, shifted


def _conv_specs(T, F, tm, tc, KW):
    nfb = F // tc
    rb = tm // SUBLANES

    def main(off):
        return pl.BlockSpec((tm, tc), lambda i, c: (i, c + off))

    def halo(off):
        return pl.BlockSpec((SUBLANES, tc), lambda i, c: (jnp.maximum(i * rb - 1, 0), c + off))

    def wspec(off):
        return pl.BlockSpec((None, KW, tc), lambda i, c: (c + off, 0, 0))

    def bspec(off):
        return pl.BlockSpec((1, tc), lambda i, c: (0, c + off))

    return nfb, main, halo, wspec, bspec


def convglu_fwd(z, cw, cb, *, name):
    T, F2 = z.shape
    F = F2 // 2
    _, KW, tc = cw.shape
    tm = _pick(T, (256, 128))
    nfb, main, halo, wspec, bspec = _conv_specs(T, F, tm, tc, KW)

    def body(zg, zgh, zv, zvh, wg, wv, bg, bv, o_ref):
        first = pl.program_id(0) == 0
        g, _ = _conv_acc(zg[...], zgh[...], wg[...], bg[...], first)
        v, _ = _conv_acc(zv[...], zvh[...], wv[...], bv[...], first)
        o_ref[...] = (g * _sigmoid(g) * v).astype(BF16)

    return pl.pallas_call(
        body, grid=(T // tm, nfb),
        in_specs=[main(0), halo(0), main(nfb), halo(nfb), wspec(0), wspec(nfb), bspec(0), bspec(nfb)],
        out_specs=pl.BlockSpec((tm, tc), lambda i, c: (i, c)),
        out_shape=jax.ShapeDtypeStruct((T, F), BF16), name=name,
        compiler_params=_params(("parallel", "parallel"), VMEM_BIG))(z, z, z, z, cw, cw, cb, cb)


def convglu_bwd_acc(z, da, cw, cb, *, name):
    T, F2 = z.shape
    F = F2 // 2
    _, KW, tc = cw.shape
    tm = _pick(T, (256, 128))
    nfb, main, halo, wspec, bspec = _conv_specs(T, F, tm, tc, KW)

    def body(zg, zgh, zv, zvh, wg, wv, bg, bv, da_ref, o_ref, dwg, dwv, dbg, dbv):
        i = pl.program_id(1)
        first = i == 0

        @pl.when(first)
        def _():
            for o in (dwg, dwv, dbg, dbv):
                o[...] = jnp.zeros_like(o)

        zg_v, zv_v = zg[...], zv[...]
        g, sg_ = _conv_acc(zg_v, zgh[...], wg[...], bg[...], first)
        v, sv_ = _conv_acc(zv_v, zvh[...], wv[...], bv[...], first)
        d = da_ref[...]
        sig = _sigmoid(g)
        dg = d * v * sig * (1.0 + g * (1.0 - sig))
        dv = d * g * sig
        o_ref[0] = dg.astype(BF16)
        o_ref[1] = dv.astype(BF16)
        dbg[...] += jnp.sum(dg, axis=0, keepdims=True)
        dbv[...] += jnp.sum(dv, axis=0, keepdims=True)
        for k in range(KW):
            xg = zg_v if k == KW - 1 else sg_[k]
            xv = zv_v if k == KW - 1 else sv_[k]
            dwg[pl.ds(k, 1), :] += jnp.sum(dg * xg, axis=0, keepdims=True)
            dwv[pl.ds(k, 1), :] += jnp.sum(dv * xv, axis=0, keepdims=True)

    def sw(spec_fn, off):
        s = spec_fn(off)
        return pl.BlockSpec(s.block_shape, lambda c, i, f=s.index_map: f(i, c))

    both = jax.ShapeDtypeStruct((2, T, F), BF16)
    dwsh = jax.ShapeDtypeStruct((nfb, KW, tc), F32)
    dbsh = jax.ShapeDtypeStruct((1, F), F32)
    outs = pl.pallas_call(
        body, grid=(nfb, T // tm),
        in_specs=[sw(main, 0), sw(halo, 0), sw(main, nfb), sw(halo, nfb), sw(wspec, 0), sw(wspec, nfb),
                  sw(bspec, 0), sw(bspec, nfb), pl.BlockSpec((tm, tc), lambda c, i: (i, c))],
        out_specs=[pl.BlockSpec((2, tm, tc), lambda c, i: (0, i, c)),
                   pl.BlockSpec((None, KW, tc), lambda c, i: (c, 0, 0)), pl.BlockSpec((None, KW, tc), lambda c, i: (c, 0, 0)),
                   pl.BlockSpec((1, tc), lambda c, i: (0, c)), pl.BlockSpec((1, tc), lambda c, i: (0, c))],
        out_shape=[both, dwsh, dwsh, dbsh, dbsh], name=name,
        compiler_params=_params(("parallel", "arbitrary"), VMEM_BIG))(z, z, z, z, cw, cw, cb, cb, da)
    return outs


def conv_bwd_in(dacc, cw, *, name):
    _, T, F = dacc.shape
    _, KW, tc = cw.shape
    nfb = F // tc
    tm = _pick(T, (256, 128))
    rb = tm // (2 * SUBLANES)
    last_blk = T // (2 * SUBLANES) - 1

    def body(d_ref, dn_ref, w_ref, o_ref):
        last = pl.program_id(0) == pl.num_programs(0) - 1
        d = d_ref[...].astype(F32)
        dn = jnp.where(last, 0.0, dn_ref[...].astype(F32)[:SUBLANES])
        w = w_ref[...]
        out = w[KW - 1:KW] * d
        for k in range(KW - 1):
            out = out + w[k:k + 1] * _shift_up(d, dn, KW - 1 - k)
        o_ref[...] = out.astype(BF16)

    return pl.pallas_call(
        body, grid=(T // tm, 2, nfb),
        in_specs=[pl.BlockSpec((None, tm, tc), lambda i, hf, c: (hf, i, c)),
                  pl.BlockSpec((None, 2 * SUBLANES, tc), lambda i, hf, c: (hf, jnp.minimum((i + 1) * rb, last_blk), c)),
                  pl.BlockSpec((None, KW, tc), lambda i, hf, c: (hf * nfb + c, 0, 0))],
        out_specs=pl.BlockSpec((tm, tc), lambda i, hf, c: (i, hf * nfb + c)),
        out_shape=jax.ShapeDtypeStruct((T, 2 * F), BF16), name=name,
        compiler_params=_params(("parallel", "parallel", "parallel"), VMEM_BIG))(dacc, dacc, cw)


def _my_place():
    x, y, c = (lax.axis_index(a) for a in AXES)
    return x, y, c, 4 * x + 2 * y + c


def _peer(m, x, y, c):
    px = 1 - x if (m >> 2) & 1 else x
    py = 1 - y if (m >> 1) & 1 else y
    pc = 1 - c if m & 1 else c
    return (px, py, pc), 4 * px + 2 * py + pc


_HBM = pl.BlockSpec(memory_space=pltpu.HBM)
_SEM = pl.BlockSpec(memory_space=pltpu.SEMAPHORE)
_EFFECT = pltpu.SideEffectType.DATAFLOW_SIDE_EFFECTING


def _split_copy(in_refs, land_refs, send_sems, recv_sems, bcast, f, m, place, landing):
    x, y, c, me = place
    dev, plin = _peer(m, x, y, c)
    src = in_refs[f] if bcast else in_refs[f].at[plin]
    return pltpu.make_async_remote_copy(
        src_ref=src, dst_ref=land_refs[f].at[plin if landing else me],
        send_sem=send_sems.at[f * (N_DEV - 1) + m - 1], recv_sem=recv_sems.at[f * (N_DEV - 1) + m - 1],
        device_id=dev, device_id_type=MESH)


def _own_copy(in_refs, land_refs, own_sems, bcast, f, place):
    me = place[3]
    return pltpu.make_async_copy(in_refs[f] if bcast else in_refs[f].at[me], land_refs[f].at[me], own_sems.at[f])


PEERS_ALL = tuple(range(1, N_DEV))
PEERS_DIRECT = (1, 2, 4, 6)
SLOTS_PASSED = (2, 4, 6)


def exchange_start(ins, *, bcast, masks=PEERS_ALL, dep=None, name):
    n = len(ins)
    lands = [lax.empty(((N_DEV,) + a.shape) if bcast else a.shape, a.dtype) for a in ins]
    deps = [] if dep is None else [dep]
    nd = len(deps)

    def body(*refs):
        in_refs, land_refs = refs[:n], refs[n:2 * n]
        send_sems, recv_sems, own_sems = refs[2 * n + nd:2 * n + nd + 3]
        token = refs[-1]
        place = _my_place()
        for m in masks:
            for f in range(n):
                _split_copy(in_refs, land_refs, send_sems, recv_sems, bcast, f, m, place, False).start()
        for f in range(n):
            _own_copy(in_refs, land_refs, own_sems, bcast, f, place).start()
        token[...] = jnp.zeros_like(token)

    arrs = [pltpu.with_memory_space_constraint(a, pltpu.HBM) for a in (*ins, *lands)]
    sems = pltpu.SemaphoreType.DMA((n * (N_DEV - 1),))
    outs = pl.pallas_call(
        body, name=name,
        out_shape=(sems, sems, pltpu.SemaphoreType.DMA((n,)), *[pltpu.HBM(a.shape, a.dtype) for a in arrs],
                   jax.ShapeDtypeStruct((SUBLANES, LANES), F32)),
        in_specs=[_HBM] * (2 * n) + [_DEP] * nd,
        out_specs=(_SEM, _SEM, _SEM, *[_HBM] * (2 * n), pl.BlockSpec(memory_space=pltpu.VMEM)),
        input_output_aliases={i: 3 + i for i in range(2 * n)},
        compiler_params=pltpu.CompilerParams(has_side_effects=_EFFECT))(*arrs, *deps)
    return outs[0], outs[1], outs[2], list(outs[3:3 + 2 * n]), outs[-1]


def exchange_wait(started, after, *, bcast, masks=PEERS_ALL, name):
    send_sems, recv_sems, own_sems, thrus, _ = started
    n = len(thrus) // 2

    def body(*refs):
        in_refs, land_refs = refs[:n], refs[n:2 * n]
        send, recv, own = refs[2 * n:2 * n + 3]
        place = _my_place()
        for f in range(n):
            _own_copy(in_refs, land_refs, own, bcast, f, place).wait()
        for m in masks:
            for f in range(n):
                cp = _split_copy(in_refs, land_refs, send, recv, bcast, f, m, place, True)
                cp.wait_send()
                cp.wait_recv()

    outs = pl.pallas_call(
        body, name=name, out_shape=[pltpu.HBM(a.shape, a.dtype) for a in thrus],
        in_specs=[_HBM] * (2 * n) + [_SEM, _SEM, _SEM, pl.BlockSpec(memory_space=pl.ANY)], out_specs=[_HBM] * (2 * n),
        input_output_aliases={i: i for i in range(2 * n)},
        compiler_params=pltpu.CompilerParams(has_side_effects=_EFFECT))(
            *thrus, send_sems, recv_sems, own_sems, after)
    return list(outs[n:])


def _pass_copy(land_refs, send_sems, recv_sems, f, k, place, landing):
    x, y, c, _ = place
    m = SLOTS_PASSED[k]
    sib, _ = _peer(1, x, y, c)
    _, mine = _peer(m, x, y, c)
    _, theirs = _peer(m ^ 1, x, y, c)
    i = f * len(SLOTS_PASSED) + k
    return pltpu.make_async_remote_copy(
        src_ref=land_refs[f].at[mine], dst_ref=land_refs[f].at[theirs if landing else mine],
        send_sem=send_sems.at[i], recv_sem=recv_sems.at[i], device_id=sib, device_id_type=MESH)


def pass_start(lands, *, dep=None, name):
    n = len(lands)
    deps = [] if dep is None else [dep]
    nd = len(deps)

    def body(*refs):
        land_refs = refs[:n]
        send_sems, recv_sems, token = refs[n + nd], refs[n + nd + 1], refs[-1]
        place = _my_place()
        for k in range(len(SLOTS_PASSED)):
            for f in range(n):
                _pass_copy(land_refs, send_sems, recv_sems, f, k, place, False).start()
        token[...] = jnp.zeros_like(token)

    sems = pltpu.SemaphoreType.DMA((n * len(SLOTS_PASSED),))
    outs = pl.pallas_call(
        body, name=name,
        out_shape=(sems, sems, *[pltpu.HBM(a.shape, a.dtype) for a in lands], jax.ShapeDtypeStruct((SUBLANES, LANES), F32)),
        in_specs=[_HBM] * n + [_DEP] * nd,
        out_specs=(_SEM, _SEM, *[_HBM] * n, pl.BlockSpec(memory_space=pltpu.VMEM)),
        input_output_aliases={i: 2 + i for i in range(n)},
        compiler_params=pltpu.CompilerParams(has_side_effects=_EFFECT))(*lands, *deps)
    return outs[0], outs[1], list(outs[2:2 + n]), outs[-1]


def pass_wait(started, after, *, name):
    send_sems, recv_sems, thrus, _ = started
    n = len(thrus)

    def body(*refs):
        land_refs = refs[:n]
        send, recv = refs[n], refs[n + 1]
        place = _my_place()
        for k in range(len(SLOTS_PASSED)):
            for f in range(n):
                cp = _pass_copy(land_refs, send, recv, f, k, place, True)
                cp.wait_send()
                cp.wait_recv()

    outs = pl.pallas_call(
        body, name=name, out_shape=[pltpu.HBM(a.shape, a.dtype) for a in thrus],
        in_specs=[_HBM] * n + [_SEM, _SEM, pl.BlockSpec(memory_space=pl.ANY)], out_specs=[_HBM] * n,
        input_output_aliases={i: i for i in range(n)},
        compiler_params=pltpu.CompilerParams(has_side_effects=_EFFECT))(*thrus, send_sems, recv_sems, after)
    return list(outs)


def _adamw(w, g, m, v):
    m = ADAM_B1 * m + (1.0 - ADAM_B1) * g
    v = ADAM_B2 * v + (1.0 - ADAM_B2) * (g * g)
    m_hat = m / (1.0 - ADAM_B1 ** ADAM_STEP)
    v_hat = v / (1.0 - ADAM_B2 ** ADAM_STEP)
    delta = -ADAM_LR * (m_hat / (jnp.sqrt(v_hat) + ADAM_EPS) + ADAM_WD * w)
    return delta, m, v


def adam_reduce(recv, w, m, v, l, prev, *, name):
    _, R, C = recv.shape
    L = w.shape[0]
    budget = 4 * 1024 * 1024
    tr = R
    for cand in (1024, 512, 352, 256, 176, 128, 64, 32, 16):
        if R % cand == 0 and N_DEV * cand * C * recv.dtype.itemsize <= budget:
            tr = cand
            break

    def body(r_ref, w_ref, m_ref, v_ref, *rest):
        g_ref, d_ref, nm_ref, nv_ref = rest[-4:]
        g = r_ref[0].astype(F32)
        for s in range(1, N_DEV):
            g = g + r_ref[s].astype(F32)
        d, nm, nv = _adamw(w_ref[...], g, m_ref[...], v_ref[...])
        g_ref[...] = g
        d_ref[...] = d
        nm_ref[...] = nm
        nv_ref[...] = nv

    blk = pl.BlockSpec((None, tr, C), lambda r: (l, r, 0))
    sh = jax.ShapeDtypeStruct((L, R, C), F32)
    extra = [] if prev is None else list(prev)
    return pl.pallas_call(
        body, grid=(R // tr,),
        in_specs=[pl.BlockSpec((N_DEV, tr, C), lambda r: (0, r, 0)), blk, blk, blk]
        + [pl.BlockSpec(memory_space=pl.ANY)] * len(extra),
        out_specs=[blk] * 4, out_shape=[sh] * 4, name=name,
        input_output_aliases={4 + i: i for i in range(len(extra))},
        compiler_params=_params(("parallel",), VMEM_MID))(recv, w, m, v, *extra)


def sum_slots(recv, *, name):
    _, R, C = recv.shape
    tr = _pick(R, (512, 256, 128, 64, 32, 16, 8))

    def body(r_ref, o_ref):
        g = r_ref[0]
        for s in range(1, N_DEV):
            g = g + r_ref[s]
        o_ref[...] = g

    return pl.pallas_call(
        body, grid=(R // tr,), in_specs=[pl.BlockSpec((N_DEV, tr, C), lambda r: (0, r, 0))],
        out_specs=pl.BlockSpec((tr, C), lambda r: (r, 0)), out_shape=jax.ShapeDtypeStruct((R, C), F32),
        name=name, compiler_params=_params(("parallel",)))(recv)


def adam_flat(g, w, m, v, *, name):
    R, C = g.shape
    tr = _pick(R, (512, 256, 128, 64, 32, 16, 8))

    def body(g_ref, w_ref, m_ref, v_ref, d_ref, nm_ref, nv_ref):
        d, nm, nv = _adamw(w_ref[...], g_ref[...], m_ref[...], v_ref[...])
        d_ref[...] = d
        nm_ref[...] = nm
        nv_ref[...] = nv

    blk = pl.BlockSpec((tr, C), lambda r: (r, 0))
    sh = jax.ShapeDtypeStruct((R, C), F32)
    return pl.pallas_call(body, grid=(R // tr,), in_specs=[blk] * 4, out_specs=[blk] * 3, out_shape=[sh] * 3,
                          name=name, compiler_params=_params(("parallel",)))(g, w, m, v)


WEIGHTS = ("norm_mix_g", "norm_ffn_g", "a_w_in", "a_g_v", "a_w_s", "a_b_s", "a_w_out", "b_w_in", "b_a_re", "b_a_im",
           "b_log_dt", "b_b_re", "b_b_im", "b_c_re", "b_c_im", "b_d", "b_w_glu", "f_w_up", "f_conv_w", "f_conv_b",
           "f_w_down", "final_g")
BIG = ("a_w_in", "a_w_out", "b_w_in", "b_w_glu", "f_w_up", "f_w_down")
FLAT_ROWS = 512
FLAT_CHUNK = N_DEV * FLAT_ROWS * LANES


def _expand(blocks, eye):
    KB, KG, C, P = blocks.shape
    return (blocks[:, :, :, None, :] * eye[None, :, None, :, None]).reshape(KB, KG * C, KG * P)


def _diag_blocks(dense, KG, C, P, eye):
    KB = dense.shape[0]
    return jnp.einsum("kgchp,gh->kgcp", dense.reshape(KB, KG, C, KG, P), eye)


def _flatten_pack(parts):
    flat = jnp.concatenate([p.reshape(-1) for p in parts])
    pad = (-flat.shape[0]) % FLAT_CHUNK
    return jnp.pad(flat, (0, pad))


def _unpack(flat, like):
    out, o = [], 0
    for p in like:
        n = math.prod(p.shape)
        out.append(flat[o:o + n].reshape(p.shape))
        o += n
    return out


def kernel(x, norm_mix_g, norm_ffn_g, a_w_in, a_g_v, a_w_s, a_b_s, a_w_out, b_w_in, b_a_re, b_a_im, b_log_dt, b_b_re, b_b_im, b_c_re, b_c_im, b_d, b_w_glu, f_w_up, f_conv_w, f_conv_b, f_w_down, final_g, loss_target, m_norm_mix_g, m_norm_ffn_g, m_a_w_in, m_a_g_v, m_a_w_s, m_a_b_s, m_a_w_out, m_b_w_in, m_b_a_re, m_b_a_im, m_b_log_dt, m_b_b_re, m_b_b_im, m_b_c_re, m_b_c_im, m_b_d, m_b_w_glu, m_f_w_up, m_f_conv_w, m_f_conv_b, m_f_w_down, m_final_g, v_norm_mix_g, v_norm_ffn_g, v_a_w_in, v_a_g_v, v_a_w_s, v_a_b_s, v_a_w_out, v_b_w_in, v_b_a_re, v_b_a_im, v_b_log_dt, v_b_b_re, v_b_b_im, v_b_c_re, v_b_c_im, v_b_d, v_b_w_glu, v_f_w_up, v_f_conv_w, v_f_conv_b, v_f_w_down, v_final_g):
    env = dict(locals())
    W = {n: env[n] for n in WEIGHTS}
    Mo = {n: env["m_" + n] for n in WEIGHTS}
    Vo = {n: env["v_" + n] for n in WEIGHTS}

    _, T, D = x.shape
    depth = norm_mix_g.shape[0]
    E_A = a_g_v.shape[1]
    H = a_w_s.shape[1]
    G, P, C = b_b_re.shape[1], b_b_re.shape[2], b_b_re.shape[3]
    E_B = G * C
    KG = S5_KG
    KB = G // KG
    F2 = f_conv_b.shape[1]
    tb = _pick(T, (512, 256, 128))
    pm, pmt = _perm_matrices(tb)
    eye = jnp.eye(KG, dtype=F32)
    _, _, _, me = _my_place()

    def shards(i):
        j = i // 2
        ffn = [f_w_up[i].astype(BF16), f_w_down[i].astype(BF16), f_conv_w[i]]
        if i % 2 == 0:
            return [a_w_in[j].astype(BF16), a_w_out[j].astype(BF16)], ffn
        return [b_w_in[j].astype(BF16), b_w_glu[j].astype(BF16), b_d[j][None]], ffn

    h = x[0]
    mix0, ffn0 = shards(0)
    started = exchange_start(mix0, bcast=True, masks=PEERS_DIRECT, name="gather_start0")
    started_ffn0 = exchange_start(ffn0, bcast=True, masks=PEERS_DIRECT, dep=started[-1], name="gather_ffn_start0")
    saved = []
    for i in range(depth):
        j = i // 2
        s = {}
        lands = exchange_wait(started, h, bcast=True, masks=PEERS_DIRECT, name=f"gather_wait{i}")
        passing = pass_start(lands, name=f"gather_pass_start{i}")
        dep = passing[-1]
        if i + 1 < depth:
            mix, ffn = shards(i + 1)
            started = exchange_start(mix + ffn, bcast=True, masks=PEERS_DIRECT, dep=dep,
                                     name=f"gather_start{i + 1}")
            dep = started[-1]
        s["h"] = h
        s["hn"] = rms_fwd(h, norm_mix_g[i][None], dep=dep, name=f"rms_mix{i}")
        gathered = pass_wait(passing, s["hn"], name=f"gather_pass_wait{i}")
        if i % 2 == 0:
            g_in, g_out = gathered[:2]
            s["w_in"], s["w_out"] = g_in, g_out.reshape(E_A, D)
        else:
            g_in, g_glu, g_dd = gathered[:3]
            s["w_in"], s["w_glu"] = g_in.reshape(D, E_B), g_glu
            s["dd"] = g_dd.reshape(KB, 1, KG * C)
        if i % 2 == 0:
            s["p"] = mm_nn(s["hn"], s["w_in"], name=f"a_in{i}")
            s["bexp"] = jnp.repeat(a_b_s[j].T, E_A // H, axis=1)
            s["us"] = sgu_fwd(s["p"], a_g_v[j][None], a_w_s[j], s["bexp"], name=f"sgu_fwd{i}")
            h_mid = mm_nn(s["us"], s["w_out"], res=h, name=f"a_out{i}")
        else:
            s["uu"] = mm_nn(s["hn"], s["w_in"], name=f"b_in{i}")
            s["prm"] = (b_a_re[j], b_a_im[j], b_log_dt[j][:, None],
                        b_b_re[j].transpose(2, 0, 1), b_b_im[j].transpose(2, 0, 1))
            ar, ai, bbr, bbi = s5_disc_fwd(*s["prm"], name=f"s5_disc{i}")
            to_blocks = lambda t: t.reshape(C, KB, KG, P).transpose(1, 2, 0, 3)
            s["bd_r"] = _expand(to_blocks(bbr), eye).astype(BF16)
            s["bd_i"] = _expand(to_blocks(bbi), eye).astype(BF16)
            s["ct_r"] = _expand(b_c_re[j].reshape(KB, KG, C, P), eye).astype(BF16)
            s["ct_i"] = _expand(b_c_im[j].reshape(KB, KG, C, P), eye).astype(BF16)
            s["ab_r"], s["ab_i"] = ar.reshape(KB, 1, KG * P), ai.reshape(KB, 1, KG * P)
            s["y"], s["q"], s["st_r"], s["st_i"] = s5_fwd(
                s["uu"], pm, pmt, s["bd_r"], s["bd_i"], s["ct_r"], s["ct_i"], s["ab_r"], s["ab_i"], s["dd"],
                tb=tb, name=f"s5_fwd{i}")
            s["pg"] = mm_nn(s["q"], s["w_glu"], name=f"b_glu{i}")
            h_mid = glu_fwd(h, s["pg"], name=f"glu_fwd{i}")
        s["h_mid"] = h_mid
        if i == 0:
            lands = exchange_wait(started_ffn0, h_mid, bcast=True, masks=PEERS_DIRECT, name="gather_ffn_wait0")
            passing = pass_start(lands, name="gather_ffn_pass_start0")
            s["hn2"] = rms_fwd(h_mid, norm_ffn_g[i][None], dep=passing[-1], name=f"rms_ffn{i}")
            g_up, g_dn, g_cw = pass_wait(passing, s["hn2"], name="gather_ffn_pass_wait0")
        else:
            s["hn2"] = rms_fwd(h_mid, norm_ffn_g[i][None], name=f"rms_ffn{i}")
            g_up, g_dn, g_cw = gathered[-3:]
        s["w_up"], s["w_dn"], s["cw"] = g_up, g_dn.reshape(F2 // 2, D), g_cw
        s["z"] = mm_nn(s["hn2"], s["w_up"], name=f"f_up{i}")
        s["a"] = convglu_fwd(s["z"], s["cw"], f_conv_b[i][None], name=f"convglu_fwd{i}")
        h = mm_nn(s["a"], s["w_dn"], res=h_mid, name=f"f_down{i}")
        saved.append(s)

    loss_tile, dh, dhb, dg_final = loss_head(h, final_g[None], loss_target[0], name="loss_head")
    loss = lax.psum(loss_tile[0, 0], AXES)

    gbig = {n: [None] * W[n].shape[0] for n in BIG}
    gs = {n: [None] * W[n].shape[0] for n in WEIGHTS if n not in BIG and n != "final_g"}
    pending = []
    for i in reversed(range(depth)):
        j = i // 2
        s = saved[i]
        gbig["f_w_down"][i] = mm_tn(s["a"], dhb, blocks=1, name=f"g_down{i}").reshape(N_DEV, F2 // 2 // N_DEV, D)
        st = exchange_start([gbig["f_w_down"][i]], bcast=False, name=f"xchg_down_start{i}")
        pending.append((st, [("f_w_down", i)], f"down{i}"))
        da = mm_nt(dhb, s["w_dn"], dep=st[-1], name=f"d_a{i}")
        dacc, dwg, dwv, dbg, dbv = convglu_bwd_acc(s["z"], da, s["cw"], f_conv_b[i][None], name=f"convglu_bwd{i}")
        gs["f_conv_w"][i] = jnp.concatenate([dwg, dwv], axis=0)
        gs["f_conv_b"][i] = jnp.concatenate([dbg, dbv], axis=1)[0]
        dz = conv_bwd_in(dacc, s["cw"], name=f"conv_bwd_in{i}")
        gbig["f_w_up"][i] = mm_tn(s["hn2"], dz, blocks=N_DEV, name=f"g_up{i}")
        st = exchange_start([gbig["f_w_up"][i]], bcast=False, name=f"xchg_up_start{i}")
        pending.append((st, [("f_w_up", i)], f"up{i}"))
        dhn2 = mm_nt(dz, s["w_up"], dep=st[-1], name=f"d_hn2{i}")
        dh_mid, dmb, dg = rms_bwd(s["h_mid"], norm_ffn_g[i][None], dhn2, dh, name=f"rms_ffn_bwd{i}")
        gs["norm_ffn_g"][i] = dg[0]
        if i % 2 == 0:
            gbig["a_w_out"][j] = mm_tn(s["us"], dmb, blocks=1, name=f"g_aout{i}").reshape(N_DEV, E_A // N_DEV, D)
            d_us = mm_nt(dmb, s["w_out"], name=f"d_us{i}")
            dp, dws, dbt, dgv = sgu_bwd(s["p"], d_us, a_g_v[j][None], a_w_s[j], s["bexp"], name=f"sgu_bwd{i}")
            gs["a_w_s"][j], gs["a_b_s"][j], gs["a_g_v"][j] = dws, dbt[:, :H].T, dgv[0]
            gbig["a_w_in"][j] = mm_tn(s["hn"], dp, blocks=N_DEV, name=f"g_ain{i}")
            dhn = mm_nt(dp, s["w_in"], name=f"d_hn_a{i}")
        else:
            dpg = glu_bwd(dh_mid, s["pg"], name=f"glu_bwd{i}")
            gbig["b_w_glu"][j] = mm_tn(s["q"], dpg, blocks=N_DEV, name=f"g_glu{i}")
            dq = mm_nt(dpg, s["w_glu"], name=f"d_q{i}")
            duu, dbr, dbi, dcr, dci, dar, dai, ddd = s5_bwd(
                s["uu"], s["y"], dq, pm, pmt, s["st_r"], s["st_i"], s["bd_r"], s["bd_i"], s["ct_r"], s["ct_i"],
                s["ab_r"], s["ab_i"], s["dd"], tb=tb, name=f"s5_bwd{i}")
            from_blocks = lambda t: _diag_blocks(t, KG, C, P, eye).transpose(2, 0, 1, 3).reshape(C, G, P)
            d_are, d_aim, d_ldt, d_bre, d_bim = s5_disc_bwd(
                *s["prm"], dar.reshape(G, P), dai.reshape(G, P), from_blocks(dbr), from_blocks(dbi),
                name=f"s5_disc_bwd{i}")
            gs["b_a_re"][j], gs["b_a_im"][j], gs["b_log_dt"][j] = d_are, d_aim, d_ldt[:, 0]
            gs["b_b_re"][j], gs["b_b_im"][j] = d_bre.transpose(1, 2, 0), d_bim.transpose(1, 2, 0)
            gs["b_c_re"][j] = _diag_blocks(dcr, KG, C, P, eye).reshape(G, C, P)
            gs["b_c_im"][j] = _diag_blocks(dci, KG, C, P, eye).reshape(G, C, P)
            gs["b_d"][j] = ddd.reshape(E_B)
            gbig["b_w_in"][j] = mm_tn(s["hn"], duu, blocks=1, name=f"g_bin{i}").reshape(N_DEV, D // N_DEV, E_B)
            dhn = mm_nt(duu, s["w_in"], name=f"d_hn_b{i}")
        mix = ("a_w_out", "a_w_in") if i % 2 == 0 else ("b_w_glu", "b_w_in")
        st = exchange_start([gbig[n][j] for n in mix], bcast=False, name=f"xchg_mix_start{i}")
        pending.append((st, [(n, j) for n in mix], f"mix{i}"))
        dh, dhb, dg = rms_bwd(s["h"], norm_mix_g[i][None], dhn, dh_mid, dep=st[-1], name=f"rms_mix_bwd{i}")
        gs["norm_mix_g"][i] = dg[0]
    grad_x = dh[None]

    grads, deltas, new_m, new_v = {}, {}, {}, {}
    small = [n for n in WEIGHTS if n not in BIG]
    full = {n: (dg_final[0] if n == "final_g" else jnp.stack(gs[n])) for n in small}
    flat = _flatten_pack([full[n] for n in small])
    rows = flat.shape[0] // N_DEV // LANES
    st_small = exchange_start([flat.reshape(N_DEV, rows, LANES)], bcast=False, name="xchg_small_start")

    done = {n: None for n in BIG}

    def finish(entries, after):
        for st, items, tag in entries:
            recvs = exchange_wait(st, after, bcast=False, name=f"xchg_wait_{tag}")
            for (n, l), recv in zip(items, recvs):
                done[n] = adam_reduce(recv, W[n], Mo[n], Vo[n], l, done[n], name=f"adam_{n}{l}")
                after = done[n][0]
        return after

    after = finish(pending[:-1], st_small[-1])
    recv = exchange_wait(st_small, after, bcast=False, name="xchg_small_wait")[0]
    part = sum_slots(recv, name="sum_small")
    st_tot = exchange_start([part], bcast=True, name="gather_small_start")
    after = finish(pending[-1:], st_tot[-1])
    tot = exchange_wait(st_tot, after, bcast=True, name="gather_small_wait")[0].reshape(-1)
    red = dict(zip(small, _unpack(tot, [full[n] for n in small])))
    red["f_conv_w"] = lax.dynamic_index_in_dim(red["f_conv_w"], me, axis=1, keepdims=False)
    red["b_d"] = lax.dynamic_slice_in_dim(red["b_d"], me * (E_B // N_DEV), E_B // N_DEV, axis=1)
    gflat = _flatten_pack([red[n] for n in small]).reshape(-1, LANES)
    d_f, m_f, v_f = adam_flat(
        gflat, _flatten_pack([W[n] for n in small]).reshape(-1, LANES),
        _flatten_pack([Mo[n] for n in small]).reshape(-1, LANES),
        _flatten_pack([Vo[n] for n in small]).reshape(-1, LANES), name="adam_small")
    like = [W[n] for n in small]
    for n, d_, m_, v_ in zip(small, _unpack(d_f.reshape(-1), like), _unpack(m_f.reshape(-1), like),
                             _unpack(v_f.reshape(-1), like)):
        grads[n], deltas[n], new_m[n], new_v[n] = red[n], d_, m_, v_

    for n in BIG:
        grads[n], deltas[n], new_m[n], new_v[n] = done[n]

    return (loss, grad_x, *[grads[n] for n in WEIGHTS], *[deltas[n] for n in WEIGHTS],
            *[new_m[n] for n in WEIGHTS], *[new_v[n] for n in WEIGHTS])
```
